```python
import math
import jax, jax.numpy as jnp
from jax import lax
import numpy as np

D_MODEL = 1024
BATCH = 32
SEQ = 256
DEPTH = 1
DEC_BATCH = 8
DEC_SEQ = 1024
PAST_LEN = 256

GRID_W = 64
ROPE_THETA = 10000.0
NORM_EPS = 1e-6
MLA_HEADS = 4
QK_NOPE = 128
QK_ROPE = 64
V_HEAD = 128
Q_LORA = 384
KV_LORA = 256
MLA_WIDTH = MLA_HEADS * V_HEAD
SSM_HEADS = 8
SSM_HEAD_DIM = 64
SSM_WIDTH = SSM_HEADS * SSM_HEAD_DIM
SSM_GROUPS = 2
D_STATE = 128
D_CONV = 3
CHUNK = 128
CONV_DIM = SSM_WIDTH + 2 * SSM_GROUPS * D_STATE
MIX_WIDTH = MLA_WIDTH + SSM_WIDTH
IN_WIDTH = Q_LORA + KV_LORA + QK_ROPE + SSM_WIDTH + CONV_DIM + 2 * SSM_HEADS
D_FF = -(-8 * D_MODEL // (3 * 256)) * 256
Q_BLOCK = 128

kernel_name = "hybrid_mla_ssd_prefix_dit_step"


def rms_norm(x, g):
    xf = x.astype(jnp.float32)
    y = xf * lax.rsqrt(jnp.mean(xf * xf, axis=-1, keepdims=True) + NORM_EPS)
    return (y * g.astype(jnp.float32)).astype(x.dtype)


def adaln(cond, w_mod, b_mod):
    m = jax.nn.silu(cond) @ w_mod + b_mod
    if m.ndim == 2:
        m = m[:, None, :]
    return jnp.split(m, 6, axis=-1)


def rope_2d(x, length):
    rows = length // GRID_W
    row = jnp.repeat(jnp.arange(rows), GRID_W)
    col = jnp.tile(jnp.arange(GRID_W), rows)
    half = QK_ROPE // 2
    quarter = half // 2
    inv_freq = ROPE_THETA ** (-jnp.arange(quarter, dtype=jnp.float32) / quarter)
    bshape = (1, length) + (1,) * (x.ndim - 3) + (quarter,)

    def rot(xa, pos):
        ang = pos.astype(jnp.float32)[:, None] * inv_freq[None, :]
        cos = jnp.cos(ang).reshape(bshape)
        sin = jnp.sin(ang).reshape(bshape)
        xa = xa.astype(jnp.float32)
        x1, x2 = xa[..., :quarter], xa[..., quarter:]
        return jnp.concatenate([x1 * cos - x2 * sin, x2 * cos + x1 * sin], axis=-1)

    out = jnp.concatenate([rot(x[..., :half], row), rot(x[..., half:], col)], axis=-1)
    return out.astype(x.dtype)


def mla_attention(q_nope, q_rope, k_nope, k_rope, v):
    b, sq, h, _ = q_nope.shape
    nb = sq // Q_BLOCK
    scale = (QK_NOPE + QK_ROPE) ** -0.5

    def block(qs):
        qn, qr = qs
        s = jnp.einsum("bqhd,bkhd->bhqk", qn, k_nope) + jnp.einsum("bqhr,bkr->bhqk", qr, k_rope)
        p = jax.nn.softmax(s.astype(jnp.float32) * scale, axis=-1).astype(v.dtype)
        return jnp.einsum("bhqk,bkhd->bqhd", p, v)

    qn_b = q_nope.reshape(b, nb, Q_BLOCK, h, QK_NOPE).transpose(1, 0, 2, 3, 4)
    qr_b = q_rope.reshape(b, nb, Q_BLOCK, h, QK_ROPE).transpose(1, 0, 2, 3, 4)
    out = lax.map(block, (qn_b, qr_b))
    return out.transpose(1, 0, 2, 3, 4).reshape(b, sq, h * V_HEAD)


def depthwise_conv_centred(x, w, bias):
    l = x.shape[1]
    pad = D_CONV // 2
    xp = jnp.pad(x, ((0, 0), (pad, pad), (0, 0)))
    y = bias
    for k in range(D_CONV):
        y = y + xp[:, k:k + l] * w[k]
    return y


def ssd_chunked(x, dt, a, bm, cm, init_state):
    b, l, h, p = x.shape
    g, n = bm.shape[-2:]
    r = h // g
    c = l // CHUNK
    f32 = jnp.float32
    dt = dt.astype(f32)
    xr = (x.astype(f32) * dt[..., None]).reshape(b, c, CHUNK, g, r, p)
    da = (dt * a).reshape(b, c, CHUNK, g, r).transpose(0, 3, 4, 1, 2)
    bc = bm.astype(f32).reshape(b, c, CHUNK, g, n)
    cc = cm.astype(f32).reshape(b, c, CHUNK, g, n)
    a_cs = jnp.cumsum(da, axis=-1)
    seg = a_cs[..., :, None] - a_cs[..., None, :]
    causal = jnp.tril(jnp.ones((CHUNK, CHUNK), dtype=bool))
    lmat = jnp.exp(jnp.where(causal, seg, -jnp.inf))
    cb = jnp.einsum("bclgn,bcsgn->bgcls", cc, bc)
    y_diag = jnp.einsum("bgcls,bgrcls,bcsgrp->bclgrp", cb, lmat, xr)
    decay_states = jnp.exp(a_cs[..., -1:] - a_cs)
    chunk_states = jnp.einsum("bclgn,bgrcl,bclgrp->cbgrpn", bc, decay_states, xr)
    chunk_decay = jnp.exp(a_cs[..., -1]).transpose(3, 0, 1, 2)

    def step(state, inp):
        dec, st = inp
        return state * dec[..., None, None] + st, state

    final, entering = lax.scan(step, init_state.astype(f32).reshape(b, g, r, p, n),
                               (chunk_decay, chunk_states))
    y_off = jnp.einsum("bclgn,cbgrpn,bgrcl->bclgrp", cc, entering, jnp.exp(a_cs))
    y = (y_diag + y_off).reshape(b, l, h, p)
    return y, final.reshape(b, h, p, n)


def ssm_mixer(z, xbc, dt_f_raw, dt_b_raw, conv_w, conv_b, dt_bias, a_log, d_skip, g_out,
              init_f, init_b):
    b, l, _ = xbc.shape
    xbc = jax.nn.silu(depthwise_conv_centred(xbc, conv_w, conv_b))
    xs, bm, cm = jnp.split(xbc, [SSM_WIDTH, SSM_WIDTH + SSM_GROUPS * D_STATE], axis=-1)
    xs = xs.reshape(b, l, SSM_HEADS, SSM_HEAD_DIM)
    bm = bm.reshape(b, l, SSM_GROUPS, D_STATE)
    cm = cm.reshape(b, l, SSM_GROUPS, D_STATE)
    f32 = jnp.float32
    dt_f = jax.nn.softplus(dt_f_raw.astype(f32) + dt_bias[0].astype(f32))
    dt_b = jax.nn.softplus(dt_b_raw.astype(f32) + dt_bias[1].astype(f32))
    a_f = -jnp.exp(a_log[0].astype(f32))
    a_b = -jnp.exp(a_log[1].astype(f32))
    flip = lambda t: jnp.flip(t, axis=1)
    y_f, fin_f = ssd_chunked(xs, dt_f, a_f, bm, cm, init_f)
    y_b, fin_b = ssd_chunked(flip(xs), flip(dt_b), a_b, flip(bm), flip(cm), init_b)
    y = y_f + flip(y_b) + d_skip.astype(f32)[:, None] * xs.astype(f32)
    y = y.reshape(b, l, SSM_WIDTH) * jax.nn.silu(z.astype(f32))
    return rms_norm(y, g_out).astype(z.dtype), fin_f, fin_b


def split_in(proj):
    idx = np.cumsum([Q_LORA, KV_LORA, QK_ROPE, SSM_WIDTH, CONV_DIM, SSM_HEADS]).tolist()
    return jnp.split(proj, idx, axis=-1)


def token_mixing(h, w_in, g_q, w_uq, g_kv, w_ukv, conv_w, conv_b, dt_bias, a_log, d_skip,
                 g_ssm_out, w_out, ctx_ckv, ctx_krope, init_f, init_b):
    b, l, _ = h.shape
    q_c, kv_c, k_rope, z, xbc, dt_f, dt_b = split_in(h @ w_in)
    ckv = rms_norm(kv_c, g_kv)
    q = (rms_norm(q_c, g_q) @ w_uq).reshape(b, l, MLA_HEADS, QK_NOPE + QK_ROPE)
    q_nope, q_rope = q[..., :QK_NOPE], q[..., QK_NOPE:]
    if ctx_ckv is None:
        ckv_all, krope_all = ckv, k_rope
    else:
        q_rope = rope_2d(q_rope, l)
        ckv_all = jnp.concatenate([ctx_ckv, ckv], axis=1)
        krope_all = jnp.concatenate([ctx_krope, rope_2d(k_rope, l)], axis=1)
    sk = ckv_all.shape[1]
    kv = (ckv_all @ w_ukv).reshape(b, sk, MLA_HEADS, QK_NOPE + V_HEAD)
    k_nope, v = kv[..., :QK_NOPE], kv[..., QK_NOPE:]
    attn = mla_attention(q_nope, q_rope, k_nope, krope_all, v)
    ssm_out, fin_f, fin_b = ssm_mixer(z, xbc, dt_f, dt_b, conv_w, conv_b, dt_bias, a_log,
                                      d_skip, g_ssm_out, init_f, init_b)
    out = jnp.concatenate([attn, ssm_out], axis=-1) @ w_out
    return out, ckv, k_rope, fin_f, fin_b


def ffn_sublayer(x, shift, scale, gate, g_pre, g_post, w_gate, w_up, w_down):
    h = rms_norm(x, g_pre) * (1 + scale) + shift
    f = (jax.nn.silu(h @ w_gate) * (h @ w_up)) @ w_down
    return x + gate * rms_norm(f, g_post)


def setup_inputs(seed: int = 0) -> dict:
    key = jax.random.key(seed)
    ks = iter(jax.random.split(key, 32))
    nrm = lambda shape, s=1.0: jax.random.normal(next(ks), shape, jnp.float32) * s
    gain = lambda shape: 1.0 + 0.1 * jax.random.normal(next(ks), shape, jnp.float32)
    u = jax.random.uniform(next(ks), (DEPTH, 2, SSM_HEADS), jnp.float32)
    dt0 = jnp.exp(u * (math.log(0.1) - math.log(0.001)) + math.log(0.001))
    dt_bias = dt0 + jnp.log(-jnp.expm1(-dt0))
    a_log = jnp.log(jax.random.uniform(next(ks), (DEPTH, 2, SSM_HEADS), jnp.float32, 1.0, 16.0))
    return {
        "x_prompt": nrm((BATCH, SEQ, D_MODEL)),
        "x_sample": nrm((DEC_BATCH, DEC_SEQ, D_MODEL)),
        "cache_ckv": nrm((DEC_BATCH, DEPTH, PAST_LEN, KV_LORA)),
        "cache_krope": nrm((DEC_BATCH, DEPTH, PAST_LEN, QK_ROPE)),
        "state_ssm": nrm((DEC_BATCH, DEPTH, 2, SSM_HEADS, SSM_HEAD_DIM, D_STATE), 0.1),
        "c": nrm((DEC_BATCH, D_MODEL)),
        "c_ctx": nrm((D_MODEL,)),
        "w_mod": nrm((DEPTH, D_MODEL, 6 * D_MODEL), D_MODEL ** -0.5),
        "b_mod": nrm((DEPTH, 6 * D_MODEL), 0.01),
        "g_pre_mix": gain((DEPTH, D_MODEL)),
        "g_post_mix": gain((DEPTH, D_MODEL)),
        "w_in": nrm((DEPTH, D_MODEL, IN_WIDTH), D_MODEL ** -0.5),
        "g_q": gain((DEPTH, Q_LORA)),
        "w_uq": nrm((DEPTH, Q_LORA, MLA_HEADS * (QK_NOPE + QK_ROPE)), Q_LORA ** -0.5),
        "g_kv": gain((DEPTH, KV_LORA)),
        "w_ukv": nrm((DEPTH, KV_LORA, MLA_HEADS * (QK_NOPE + V_HEAD)), KV_LORA ** -0.5),
        "conv_w": nrm((DEPTH, D_CONV, CONV_DIM), D_CONV ** -0.5),
        "conv_b": nrm((DEPTH, CONV_DIM), 0.01),
        "dt_bias": dt_bias,
        "a_log": a_log,
        "d_skip": gain((DEPTH, SSM_HEADS)),
        "g_ssm_out": gain((DEPTH, SSM_WIDTH)),
        "w_out": nrm((DEPTH, MIX_WIDTH, D_MODEL), MIX_WIDTH ** -0.5),
        "g_pre_ffn": gain((DEPTH, D_MODEL)),
        "g_post_ffn": gain((DEPTH, D_MODEL)),
        "w_gate": nrm((DEPTH, D_MODEL, D_FF), D_MODEL ** -0.5),
        "w_up": nrm((DEPTH, D_MODEL, D_FF), D_MODEL ** -0.5),
        "w_down": nrm((DEPTH, D_FF, D_MODEL), D_FF ** -0.5),
    }


def reference(x_prompt, x_sample, cache_ckv, cache_krope, state_ssm, c, c_ctx, w_mod, b_mod,
              g_pre_mix, g_post_mix, w_in, g_q, w_uq, g_kv, w_ukv, conv_w, conv_b, dt_bias,
              a_log, d_skip, g_ssm_out, w_out, g_pre_ffn, g_post_ffn, w_gate, w_up, w_down):
    y_p = x_prompt
    y_s = x_sample
    new_ckv, new_krope, new_ssm = [], [], []
    zero_state = jnp.zeros((x_prompt.shape[0], SSM_HEADS, SSM_HEAD_DIM, D_STATE), jnp.float32)
    for l in range(DEPTH):
        mix_args = (w_in[l], g_q[l], w_uq[l], g_kv[l], w_ukv[l], conv_w[l], conv_b[l],
                    dt_bias[l], a_log[l], d_skip[l], g_ssm_out[l], w_out[l])
        ffn_args = (g_pre_ffn[l], g_post_ffn[l], w_gate[l], w_up[l], w_down[l])

        sh1, sc1, gt1, sh2, sc2, gt2 = adaln(c_ctx, w_mod[l], b_mod[l])
        h = rms_norm(y_p, g_pre_mix[l]) * (1 + sc1) + sh1
        out, ckv, k_rope, fin_f, fin_b = token_mixing(h, *mix_args, None, None,
                                                      zero_state, zero_state)
        y_p = y_p + gt1 * rms_norm(out, g_post_mix[l])
        y_p = ffn_sublayer(y_p, sh2, sc2, gt2, *ffn_args)
        new_ckv.append(ckv)
        new_krope.append(k_rope)
        new_ssm.append(jnp.stack([fin_f, fin_b], axis=1).astype(x_prompt.dtype))

        sh1, sc1, gt1, sh2, sc2, gt2 = adaln(c, w_mod[l], b_mod[l])
        h = rms_norm(y_s, g_pre_mix[l]) * (1 + sc1) + sh1
        out, _, _, _, _ = token_mixing(h, *mix_args, cache_ckv[:, l], cache_krope[:, l],
                                       state_ssm[:, l, 0], state_ssm[:, l, 1])
        y_s = y_s + gt1 * rms_norm(out, g_post_mix[l])
        y_s = ffn_sublayer(y_s, sh2, sc2, gt2, *ffn_args)

    new_cache_ckv = jnp.stack(new_ckv, axis=1)
    new_cache_krope = jnp.stack(new_krope, axis=1)
    new_state_ssm = jnp.stack(new_ssm, axis=1)
    return (y_p, y_s, new_cache_ckv, new_cache_krope, new_state_ssm)
```

```python
import functools
import math

import numpy as np
import jax
import jax.numpy as jnp
from jax import lax
from jax.experimental import pallas as pl
from jax.experimental.pallas import tpu as pltpu

F32 = jnp.float32
BF16 = jnp.bfloat16

D_MODEL = 1024
GRID_W = 64
ROPE_THETA = 10000.0
NORM_EPS = 1e-6
MLA_HEADS = 4
QK_NOPE = 128
QK_ROPE = 64
V_HEAD = 128
Q_LORA = 384
KV_LORA = 256
SSM_HEADS = 8
SSM_HEAD_DIM = 64
SSM_WIDTH = SSM_HEADS * SSM_HEAD_DIM
SSM_GROUPS = 2
D_STATE = 128
CHUNK = 128
CONV_DIM = SSM_WIDTH + 2 * SSM_GROUPS * D_STATE
D_FF = 2816

LANES = 128
QK_PAD = 256
GROUP_W = SSM_WIDTH // SSM_GROUPS
VMEM_LIMIT = 56 * 1024 * 1024

_SEG_Q = (0, 384)
_SEG_KV = (384, 640)
_SEG_Z = (640, 1152)
_SEG_XBC = (1152, 2176)
_SEG_KR = (2176, 2304)
_SEG_DT = (2304, 2432)
_SEG_KRS = (2432, 2560)


def _rms(x, g):
    return x * lax.rsqrt(jnp.mean(x * x, axis=-1, keepdims=True) + NORM_EPS) * g


def _silu(x):
    return x * (1.0 / (1.0 + jnp.exp(-x)))


def _dot(a, b):
    return jnp.dot(a, b, preferred_element_type=F32)


def _dot_nt(a, b):
    return lax.dot_general(a, b, (((1,), (1,)), ((), ())), preferred_element_type=F32)


def _split3(x):
    hi = x.astype(BF16)
    r = x - hi.astype(F32)
    mid = r.astype(BF16)
    lo = (r - mid.astype(F32)).astype(BF16)
    return hi, mid, lo


def _dot3_rhs(m, x):
    hi, mid, lo = _split3(x)
    return _dot(m, hi) + _dot(m, mid) + _dot(m, lo)


def _dot3_lhs(x, m):
    hi, mid, lo = _split3(x)
    return _dot(hi, m) + _dot(mid, m) + _dot(lo, m)


def _const_spec(shape):
    nd = len(shape)
    return pl.BlockSpec(shape, lambda *_: (0,) * nd, pipeline_mode=pl.Buffered(1))


def _params(sem):
    return pltpu.CompilerParams(dimension_semantics=sem, vmem_limit_bytes=VMEM_LIMIT)


def _mod_kernel(c_ref, w_ref, b_ref, o_ref):
    c = c_ref[...]
    s = _silu(c)
    w = w_ref[...]
    s_hi = s.astype(BF16)
    s_lo = (s - s_hi.astype(F32)).astype(BF16)
    w_hi = w.astype(BF16)
    w_lo = (w - w_hi.astype(F32)).astype(BF16)
    o_ref[...] = _dot(s_hi, w_hi) + _dot(s_lo, w_hi) + _dot(s_hi, w_lo) + b_ref[...]


def _modulation(cond, w_mod, b_mod):
    rows, d = cond.shape
    n = w_mod.shape[1]
    tn = 768
    return pl.pallas_call(
        _mod_kernel,
        grid=(n // tn,),
        in_specs=[pl.BlockSpec((rows, d), lambda j: (0, 0)),
                  pl.BlockSpec((d, tn), lambda j: (0, j)),
                  pl.BlockSpec((1, tn), lambda j: (0, j))],
        out_specs=pl.BlockSpec((rows, tn), lambda j: (0, j)),
        out_shape=jax.ShapeDtypeStruct((rows, n), F32),
        compiler_params=_params(("arbitrary",)),
        name="modulation",
    )(cond, w_mod, b_mod)


def _inproj_kernel(*refs, latent):
    if latent:
        (x_ref, mod_ref, gpre_ref, win_ref, gq_ref, wuq_ref, gkv_ref, wukv_ref, cos_ref, sin_ref,
         q_ref, k_ref, v_ref, z_ref, xbc_ref, dt_ref) = refs
    else:
        (x_ref, mod_ref, gpre_ref, win_ref, gq_ref, wuq_ref, gkv_ref, wukv_ref,
         q_ref, k_ref, v_ref, z_ref, xbc_ref, dt_ref, ckv_ref, krope_ref) = refs
    scale = (QK_NOPE + QK_ROPE) ** -0.5
    x = x_ref[0]
    m = mod_ref[0]
    h = _rms(x, gpre_ref[...]) * (1.0 + m[1:2]) + m[0:1]
    proj = _dot(h.astype(BF16), win_ref[...])
    q_c = proj[:, _SEG_Q[0]:_SEG_Q[1]]
    kv_c = proj[:, _SEG_KV[0]:_SEG_KV[1]]
    z_ref[0] = proj[:, _SEG_Z[0]:_SEG_Z[1]].astype(BF16)
    xbc_ref[0] = proj[:, _SEG_XBC[0]:_SEG_XBC[1]].astype(BF16)
    dt_ref[0] = proj[:, _SEG_DT[0]:_SEG_DT[1]]
    kr = proj[:, _SEG_KR[0]:_SEG_KR[1]]
    ckv = _rms(kv_c, gkv_ref[...])
    qall = _dot(_rms(q_c, gq_ref[...]).astype(BF16), wuq_ref[...])
    kv = _dot(ckv.astype(BF16), wukv_ref[...])
    if latent:
        cos = cos_ref[...]
        sin = sin_ref[...]
        kr_rot = kr * cos + proj[:, _SEG_KRS[0]:_SEG_KRS[1]] * sin
    else:
        kr_rot = kr
        ckv_ref[0] = ckv
        krope_ref[0] = kr[:, :QK_ROPE]
    kr_b = kr_rot.astype(BF16)
    nw = MLA_HEADS * LANES
    for hd in range(MLA_HEADS):
        base = hd * QK_PAD
        q_ref[0, :, base:base + LANES] = (qall[:, hd * LANES:(hd + 1) * LANES] * scale).astype(BF16)
        qr = qall[:, nw + hd * LANES:nw + (hd + 1) * LANES]
        if latent:
            qr = qr * cos + qall[:, 2 * nw + hd * LANES:2 * nw + (hd + 1) * LANES] * sin
        q_ref[0, :, base + LANES:base + QK_PAD] = (qr * scale).astype(BF16)
        k_ref[0, :, base:base + LANES] = kv[:, base:base + LANES].astype(BF16)
        k_ref[0, :, base + LANES:base + QK_PAD] = kr_b
        v_ref[0, :, hd * LANES:(hd + 1) * LANES] = kv[:, base + LANES:base + QK_PAD].astype(BF16)


def _inproj(x, mod, mod_off, gpre, win, gq, wuq, gkv, wukv, rope, tm):
    b, s, d = x.shape
    latent = rope is not None
    grid = (b, s // tm)
    tok = lambda w: pl.BlockSpec((1, tm, w), lambda i, j: (i, j, 0))
    in_specs = [tok(d),
                pl.BlockSpec((1, 6, d), lambda i, j: (i + mod_off, 0, 0)),
                _const_spec(gpre.shape), _const_spec(win.shape), _const_spec(gq.shape),
                _const_spec(wuq.shape), _const_spec(gkv.shape), _const_spec(wukv.shape)]
    args = [x, mod, gpre, win, gq, wuq, gkv, wukv]
    out_shape = [jax.ShapeDtypeStruct((b, s, MLA_HEADS * QK_PAD), BF16),
                 jax.ShapeDtypeStruct((b, s, MLA_HEADS * QK_PAD), BF16),
                 jax.ShapeDtypeStruct((b, s, MLA_HEADS * V_HEAD), BF16),
                 jax.ShapeDtypeStruct((b, s, SSM_WIDTH), BF16),
                 jax.ShapeDtypeStruct((b, s, CONV_DIM), BF16),
                 jax.ShapeDtypeStruct((b, s, LANES), F32)]
    out_specs = [tok(MLA_HEADS * QK_PAD), tok(MLA_HEADS * QK_PAD), tok(MLA_HEADS * V_HEAD),
                 tok(SSM_WIDTH), tok(CONV_DIM), tok(LANES)]
    if latent:
        in_specs += [pl.BlockSpec((tm, LANES), lambda i, j: (j, 0))] * 2
        args += list(rope)
    else:
        out_shape += [jax.ShapeDtypeStruct((b, s, KV_LORA), F32),
                      jax.ShapeDtypeStruct((b, s, QK_ROPE), F32)]
        out_specs += [tok(KV_LORA), tok(QK_ROPE)]
    return pl.pallas_call(
        functools.partial(_inproj_kernel, latent=latent),
        grid=grid, in_specs=in_specs, out_specs=out_specs, out_shape=out_shape,
        compiler_params=_params(("parallel", "parallel")),
        name="inproj_latent" if latent else "inproj_ctx",
    )(*args)


def _ctxkv_kernel(ckv_ref, kr_ref, wukv_ref, k_ref, v_ref):
    kv = _dot(ckv_ref[0].astype(BF16), wukv_ref[...])
    kr_b = kr_ref[0].astype(BF16)
    for hd in range(MLA_HEADS):
        base = hd * QK_PAD
        k_ref[0, :, base:base + LANES] = kv[:, base:base + LANES].astype(BF16)
        k_ref[0, :, base + LANES:base + QK_PAD] = kr_b
        v_ref[0, :, hd * LANES:(hd + 1) * LANES] = kv[:, base + LANES:base + QK_PAD].astype(BF16)


def _ctxkv(ckv, kr_pad, wukv):
    b, s, _ = ckv.shape
    blk = lambda w: pl.BlockSpec((1, s, w), lambda i: (i, 0, 0))
    return pl.pallas_call(
        _ctxkv_kernel,
        grid=(b,),
        in_specs=[blk(KV_LORA), blk(LANES), _const_spec(wukv.shape)],
        out_specs=[blk(MLA_HEADS * QK_PAD), blk(MLA_HEADS * V_HEAD)],
        out_shape=[jax.ShapeDtypeStruct((b, s, MLA_HEADS * QK_PAD), BF16),
                   jax.ShapeDtypeStruct((b, s, MLA_HEADS * V_HEAD), BF16)],
        compiler_params=_params(("parallel",)),
        name="ctx_kv",
    )(ckv, kr_pad, wukv)


def _attn_kernel(*refs, latent):
    if latent:
        q_ref, k_ref, v_ref, kc_ref, vc_ref, o_ref = refs
    else:
        q_ref, k_ref, v_ref, o_ref = refs
    for hd in range(MLA_HEADS):
        qs = slice(hd * QK_PAD, (hd + 1) * QK_PAD)
        vs = slice(hd * V_HEAD, (hd + 1) * V_HEAD)
        q = q_ref[0, :, qs]
        s = _dot_nt(q, k_ref[0, :, qs])
        mx = jnp.max(s, axis=-1, keepdims=True)
        if latent:
            sc = _dot_nt(q, kc_ref[0, :, qs])
            mx = jnp.maximum(mx, jnp.max(sc, axis=-1, keepdims=True))
        p = jnp.exp(s - mx)
        den = jnp.sum(p, axis=-1, keepdims=True)
        acc = _dot(p.astype(BF16), v_ref[0, :, vs])
        if latent:
            pc = jnp.exp(sc - mx)
            den = den + jnp.sum(pc, axis=-1, keepdims=True)
            acc = acc + _dot(pc.astype(BF16), vc_ref[0, :, vs])
        o_ref[0, :, vs] = (acc / den).astype(BF16)


def _attention(q, k, v, ctx, tq):
    b, s, _ = q.shape
    latent = ctx is not None
    qw, vw = MLA_HEADS * QK_PAD, MLA_HEADS * V_HEAD
    in_specs = [pl.BlockSpec((1, tq, qw), lambda i, j: (i, j, 0)),
                pl.BlockSpec((1, s, qw), lambda i, j: (i, 0, 0)),
                pl.BlockSpec((1, s, vw), lambda i, j: (i, 0, 0))]
    args = [q, k, v]
    if latent:
        sc = ctx[0].shape[1]
        in_specs += [pl.BlockSpec((1, sc, qw), lambda i, j: (i, 0, 0)),
                     pl.BlockSpec((1, sc, vw), lambda i, j: (i, 0, 0))]
        args += list(ctx)
    return pl.pallas_call(
        functools.partial(_attn_kernel, latent=latent),
        grid=(b, s // tq), in_specs=in_specs,
        out_specs=pl.BlockSpec((1, tq, vw), lambda i, j: (i, j, 0)),
        out_shape=jax.ShapeDtypeStruct((b, s, vw), BF16),
        compiler_params=_params(("parallel", "parallel")),
        name="attn_latent" if latent else "attn_ctx",
    )(*args)


def _ssd_kernel(*refs, seq, has_init):
    if has_init:
        (xbc_ref, z_ref, dt_ref, init_ref, cw_ref, cb_ref, dtbias_ref, alog_ref, dskip_ref, gout_ref,
         tril_ref, triu_ref, e128_ref, e64_ref,
         out_ref, fin_ref,
         xpad_s, xs_s, c_s, bt_s, y_s, xwb_s, eg_s, dt_s, sf_s, sb_s) = refs
    else:
        (xbc_ref, z_ref, dt_ref, cw_ref, cb_ref, dtbias_ref, alog_ref, dskip_ref, gout_ref,
         tril_ref, triu_ref, e128_ref, e64_ref,
         out_ref, fin_ref,
         xpad_s, xs_s, c_s, bt_s, y_s, xwb_s, eg_s, dt_s, sf_s, sb_s) = refs
    L = CHUNK
    nchunk = seq // L
    halo = 8

    xpad_s[0:halo, :] = jnp.zeros((halo, CONV_DIM), F32)
    xpad_s[halo + seq:2 * halo + seq, :] = jnp.zeros((halo, CONV_DIM), F32)

    def stage(j, carry):
        r0 = pl.multiple_of(j * L, L)
        xpad_s[pl.ds(r0 + halo, L), :] = xbc_ref[0, pl.ds(r0, L), :].astype(F32)
        dt_s[pl.ds(r0, L), :] = jax.nn.softplus(dt_ref[0, pl.ds(r0, L), :] + dtbias_ref[...])
        return carry

    lax.fori_loop(0, nchunk, stage, 0)

    if has_init:
        sf_s[...] = jnp.transpose(init_ref[0, 0])
        sb_s[...] = jnp.transpose(init_ref[0, 1])
    else:
        sf_s[...] = jnp.zeros_like(sf_s)
        sb_s[...] = jnp.zeros_like(sb_s)

    col = lax.broadcasted_iota(jnp.int32, (L, LANES), 1)
    row_t = lax.broadcasted_iota(jnp.int32, (L, L), 0)
    col_s = lax.broadcasted_iota(jnp.int32, (L, L), 1)
    lower = col_s <= row_t
    upper = col_s >= row_t
    lane_lo = lax.broadcasted_iota(jnp.int32, (L, LANES), 1) < SSM_HEAD_DIM
    a_row = -jnp.exp(alog_ref[...])
    neg_inf = jnp.float32(-jnp.inf)

    def fwd(j, carry):
        r0 = pl.multiple_of(j * L, L)
        win = xpad_s[pl.ds(r0, L + 2 * halo), :]
        prev = pltpu.roll(win, 1, axis=0)[halo:halo + L]
        nxt = pltpu.roll(win, L + 2 * halo - 1, axis=0)[halo:halo + L]
        cur = win[halo:halo + L]
        conv = cb_ref[...] + prev * cw_ref[0:1, :] + cur * cw_ref[1:2, :] + nxt * cw_ref[2:3, :]
        act = _silu(conv)
        xs = act[:, :SSM_WIDTH]
        bm = act[:, SSM_WIDTH:SSM_WIDTH + SSM_GROUPS * D_STATE]
        cm = act[:, SSM_WIDTH + SSM_GROUPS * D_STATE:]
        xs_s[pl.ds(r0, L), :] = xs
        cm_b = cm.astype(BF16)
        c_s[pl.ds(r0, L), :] = cm_b
        xs_b = xs.astype(BF16)

        dtc = dt_s[pl.ds(r0, L), :]
        da = jnp.where(col < 2 * SSM_HEADS, dtc * a_row, 0.0)
        cum = _dot3_rhs(tril_ref[...], da)
        suf = _dot3_rhs(triu_ref[...], da)
        fg = jnp.where(col < SSM_HEADS, cum, suf)
        tot = cum[L - 1:L, :]
        w_small = dtc * jnp.exp(tot - fg)
        e_small = jnp.exp(fg)
        w_x = _dot3_lhs(w_small, e64_ref[...])
        e_x = _dot3_lhs(e_small, e64_ref[...])
        fg_x = _dot3_lhs(fg, e128_ref[...])
        fg_t = jnp.transpose(fg)
        dt_t = jnp.transpose(dtc)

        wf_x = w_x[:, :SSM_WIDTH]
        wb_x = w_x[:, SSM_WIDTH:]
        ef_x = e_x[:, :SSM_WIDTH]
        eg_x = e_x[:, SSM_WIDTH:]
        eg_s[pl.ds(r0, L), :] = eg_x
        xwf_b = (xs * wf_x).astype(BF16)
        xwb_s[pl.ds(r0, L), :] = (xs * wb_x).astype(BF16)

        y_parts = []
        new_states = []
        for g in range(SSM_GROUPS):
            gs = slice(g * GROUP_W, (g + 1) * GROUP_W)
            bm_g = bm[:, g * D_STATE:(g + 1) * D_STATE]
            cm_g = cm_b[:, g * D_STATE:(g + 1) * D_STATE]
            bt_g = jnp.transpose(bm_g).astype(BF16)
            bt_s[pl.ds(pl.multiple_of((j * SSM_GROUPS + g) * D_STATE, D_STATE), D_STATE), :] = bt_g
            cbm = _dot_nt(cm_g, bm_g.astype(BF16))
            heads_per_group = SSM_HEADS // SSM_GROUPS
            for pair in range(heads_per_group // 2):
                ws = []
                for k in range(2):
                    hd = g * heads_per_group + pair * 2 + k
                    seg_f = fg_x[:, hd * L:(hd + 1) * L] - fg_t[hd:hd + 1, :]
                    lf = jnp.exp(jnp.where(lower, seg_f, neg_inf)) * dt_t[hd:hd + 1, :]
                    hb = SSM_HEADS + hd
                    seg_b = fg_x[:, hb * L:(hb + 1) * L] - fg_t[hb:hb + 1, :]
                    ub = jnp.exp(jnp.where(upper, seg_b, neg_inf)) * dt_t[hb:hb + 1, :]
                    ws.append((cbm * (lf + ub)).astype(BF16))
                p0 = (g * heads_per_group + pair * 2) * SSM_HEAD_DIM
                xpair = xs_b[:, p0:p0 + LANES]
                zeros = jnp.zeros_like(xpair)
                rhs = jnp.concatenate([jnp.where(lane_lo, xpair, zeros),
                                       jnp.where(lane_lo, zeros, xpair)], axis=0)
                y_parts.append(_dot(jnp.concatenate(ws, axis=1), rhs))
            s_in = sf_s[:, gs]
            y_off = _dot(cm_g, s_in.astype(BF16)) * ef_x[:, gs]
            y_parts[-2] = y_parts[-2] + y_off[:, :LANES]
            y_parts[-1] = y_parts[-1] + y_off[:, LANES:]
            new_states.append(s_in * ef_x[L - 1:L, gs] + _dot(bt_g, xwf_b[:, gs]))
        for g in range(SSM_GROUPS):
            sf_s[:, g * GROUP_W:(g + 1) * GROUP_W] = new_states[g]
        for i, yp in enumerate(y_parts):
            y_s[pl.ds(r0, L), i * LANES:(i + 1) * LANES] = yp
        return carry

    lax.fori_loop(0, nchunk, fwd, 0)

    def bwd(jj, carry):
        j = nchunk - 1 - jj
        r0 = pl.multiple_of(j * L, L)
        xs = xs_s[pl.ds(r0, L), :]
        eg_x = eg_s[pl.ds(r0, L), :]
        xwb_b = xwb_s[pl.ds(r0, L), :]
        cm_b = c_s[pl.ds(r0, L), :]
        y_off = []
        for g in range(SSM_GROUPS):
            gs = slice(g * GROUP_W, (g + 1) * GROUP_W)
            s_in = sb_s[:, gs]
            y_off.append(_dot(cm_b[:, g * D_STATE:(g + 1) * D_STATE], s_in.astype(BF16)) * eg_x[:, gs])
            bt_g = bt_s[pl.ds(pl.multiple_of((j * SSM_GROUPS + g) * D_STATE, D_STATE), D_STATE), :]
            sb_s[:, gs] = s_in * eg_x[0:1, gs] + _dot(bt_g, xwb_b[:, gs])
        y = y_s[pl.ds(r0, L), :] + jnp.concatenate(y_off, axis=1) + dskip_ref[...] * xs
        y = y * _silu(z_ref[0, pl.ds(r0, L), :].astype(F32))
        out_ref[0, pl.ds(r0, L), :] = _rms(y, gout_ref[...]).astype(BF16)
        return carry

    lax.fori_loop(0, nchunk, bwd, 0)
    fin_ref[0, 0] = jnp.transpose(sf_s[...])
    fin_ref[0, 1] = jnp.transpose(sb_s[...])


def _ssd(xbc, z, dt, init, consts):
    b, s, _ = xbc.shape
    has_init = init is not None
    nchunk = s // CHUNK
    blk = lambda w: pl.BlockSpec((1, s, w), lambda i: (i, 0, 0))
    st_spec = pl.BlockSpec((1, 2, SSM_WIDTH, D_STATE), lambda i: (i, 0, 0, 0))
    in_specs = [blk(CONV_DIM), blk(SSM_WIDTH), blk(LANES)]
    args = [xbc, z, dt]
    if has_init:
        in_specs.append(st_spec)
        args.append(init)
    in_specs += [_const_spec(c.shape) for c in consts]
    args += list(consts)
    scratch = [pltpu.VMEM((s + 16, CONV_DIM), F32),
               pltpu.VMEM((s, SSM_WIDTH), F32),
               pltpu.VMEM((s, SSM_GROUPS * D_STATE), BF16),
               pltpu.VMEM((nchunk * SSM_GROUPS * D_STATE, CHUNK), BF16),
               pltpu.VMEM((s, SSM_WIDTH), F32),
               pltpu.VMEM((s, SSM_WIDTH), BF16),
               pltpu.VMEM((s, SSM_WIDTH), F32),
               pltpu.VMEM((s, LANES), F32),
               pltpu.VMEM((D_STATE, SSM_WIDTH), F32),
               pltpu.VMEM((D_STATE, SSM_WIDTH), F32)]
    return pl.pallas_call(
        functools.partial(_ssd_kernel, seq=s, has_init=has_init),
        grid=(b,), in_specs=in_specs,
        out_specs=[blk(SSM_WIDTH), st_spec],
        out_shape=[jax.ShapeDtypeStruct((b, s, SSM_WIDTH), BF16),
                   jax.ShapeDtypeStruct((b, 2, SSM_WIDTH, D_STATE), F32)],
        scratch_shapes=scratch,
        compiler_params=_params(("parallel",)),
        name="ssd_latent" if has_init else "ssd_ctx",
    )(*args)


FF_CHUNK = 256


def _outffn_kernel(x_ref, attn_ref, ssm_ref, mod_ref, wout_ref, gpost_ref, gpre_ref, gpostf_ref,
                   wg_ref, wu_ref, wd_ref, o_ref):
    x = x_ref[0]
    m = mod_ref[0]
    half = MLA_HEADS * V_HEAD
    mix = _dot(attn_ref[0], wout_ref[0:half, :]) + _dot(ssm_ref[0], wout_ref[half:, :])
    y = x + m[2:3] * _rms(mix, gpost_ref[...])
    h = (_rms(y, gpre_ref[...]) * (1.0 + m[4:5]) + m[3:4]).astype(BF16)
    acc = None
    for c in range(D_FF // FF_CHUNK):
        cs = slice(c * FF_CHUNK, (c + 1) * FF_CHUNK)
        gate = _dot(h, wg_ref[:, cs])
        up = _dot(h, wu_ref[:, cs])
        part = _dot((_silu(gate) * up).astype(BF16), wd_ref[cs, :])
        acc = part if acc is None else acc + part
    o_ref[0] = y + m[5:6] * _rms(acc, gpostf_ref[...])


def _outffn(x, attn, ssm, mod, mod_off, wout, gpost, gpre, gpostf, wg, wu, wd, tm):
    b, s, d = x.shape
    tok = lambda w: pl.BlockSpec((1, tm, w), lambda i, j: (i, j, 0))
    consts = [wout, gpost, gpre, gpostf, wg, wu, wd]
    return pl.pallas_call(
        _outffn_kernel,
        grid=(b, s // tm),
        in_specs=[tok(d), tok(MLA_HEADS * V_HEAD), tok(SSM_WIDTH),
                  pl.BlockSpec((1, 6, d), lambda i, j: (i + mod_off, 0, 0))]
                 + [_const_spec(c.shape) for c in consts],
        out_specs=tok(d),
        out_shape=jax.ShapeDtypeStruct((b, s, d), F32),
        compiler_params=_params(("parallel", "parallel")),
        name="out_ffn",
    )(x, attn, ssm, mod, *consts)


def _rope_tables(length):
    quarter = QK_ROPE // 4
    pos = np.arange(length)
    inv_freq = ROPE_THETA ** (-np.arange(quarter, dtype=np.float64) / quarter)
    ang_r = (pos // GRID_W)[:, None] * inv_freq[None, :]
    ang_c = (pos % GRID_W)[:, None] * inv_freq[None, :]
    zeros = np.zeros((length, LANES - QK_ROPE))
    cos = np.concatenate([np.cos(ang_r)] * 2 + [np.cos(ang_c)] * 2 + [zeros], axis=1)
    sin = np.concatenate([-np.sin(ang_r), np.sin(ang_r), -np.sin(ang_c), np.sin(ang_c), zeros], axis=1)
    return jnp.asarray(cos, F32), jnp.asarray(sin, F32)


def _swap_halves(w):
    q = QK_ROPE // 4
    return jnp.concatenate([w[..., q:2 * q], w[..., :q], w[..., 3 * q:], w[..., 2 * q:3 * q]], axis=-1)


def _pad_cols(w, width):
    return jnp.pad(w, ((0, 0), (0, width - w.shape[-1])))


def _ssd_constants():
    idx = np.arange(CHUNK)
    tril = (idx[None, :] <= idx[:, None]).astype(np.float32)
    triu = (idx[None, :] >= idx[:, None]).astype(np.float32)
    ncol = 2 * SSM_HEADS
    e128 = np.zeros((LANES, ncol * CHUNK), np.float32)
    e64 = np.zeros((LANES, ncol * SSM_HEAD_DIM), np.float32)
    for q in range(ncol):
        e128[q, q * CHUNK:(q + 1) * CHUNK] = 1.0
        e64[q, q * SSM_HEAD_DIM:(q + 1) * SSM_HEAD_DIM] = 1.0
    return [jnp.asarray(a, BF16) for a in (tril, triu, e128, e64)]


def kernel(x_prompt, x_sample, cache_ckv, cache_krope, state_ssm, c, c_ctx, w_mod, b_mod,
           g_pre_mix, g_post_mix, w_in, g_q, w_uq, g_kv, w_ukv, conv_w, conv_b, dt_bias,
           a_log, d_skip, g_ssm_out, w_out, g_pre_ffn, g_post_ffn, w_gate, w_up, w_down):
    depth = w_in.shape[0]
    assert depth == 1
    nb, seq, d = x_prompt.shape
    db, dseq, _ = x_sample.shape
    l = 0

    rows = 16
    cond = jnp.concatenate([c_ctx[None, :], c, jnp.zeros((rows - 1 - db, d), F32)], axis=0)
    mod = _modulation(cond, w_mod[l], b_mod[l][None, :]).reshape(rows, 6, d)

    wi = w_in[l]
    o_q, o_kv, o_kr = Q_LORA, Q_LORA + KV_LORA, Q_LORA + KV_LORA + QK_ROPE
    o_z, o_xbc = o_kr + SSM_WIDTH, o_kr + SSM_WIDTH + CONV_DIM
    w_kr = wi[:, o_kv:o_kr]
    segs = [wi[:, :o_q], wi[:, o_q:o_kv], wi[:, o_kr:o_z], wi[:, o_z:o_xbc],
            _pad_cols(w_kr, LANES), _pad_cols(wi[:, o_xbc:], LANES)]
    win_ctx = jnp.concatenate(segs, axis=1).astype(BF16)
    win_lat = jnp.concatenate(segs + [_pad_cols(_swap_halves(w_kr), LANES)], axis=1).astype(BF16)

    wq = w_uq[l].reshape(Q_LORA, MLA_HEADS, QK_NOPE + QK_ROPE)
    wq_n = wq[:, :, :QK_NOPE].reshape(Q_LORA, MLA_HEADS * QK_NOPE)
    wq_r = wq[:, :, QK_NOPE:]
    pad_r = lambda w: jnp.pad(w, ((0, 0), (0, 0), (0, LANES - QK_ROPE))).reshape(Q_LORA, MLA_HEADS * LANES)
    wuq_ctx = jnp.concatenate([wq_n, pad_r(wq_r)], axis=1).astype(BF16)
    wuq_lat = jnp.concatenate([wq_n, pad_r(wq_r), pad_r(_swap_halves(wq_r))], axis=1).astype(BF16)
    wukv = w_ukv[l].astype(BF16)
    wout = w_out[l].astype(BF16)
    wg, wu, wd = w_gate[l].astype(BF16), w_up[l].astype(BF16), w_down[l].astype(BF16)
    row = lambda v: v.reshape(1, -1)
    gpre, gq, gkv = row(g_pre_mix[l]), row(g_q[l]), row(g_kv[l])
    gpost, gpre_f, gpost_f = row(g_post_mix[l]), row(g_pre_ffn[l]), row(g_post_ffn[l])

    small = lambda v: _pad_cols(v.reshape(1, -1), LANES)
    ssd_consts = [conv_w[l], row(conv_b[l]), small(dt_bias[l]), small(a_log[l]),
                  row(jnp.repeat(d_skip[l], SSM_HEAD_DIM)), row(g_ssm_out[l])] + _ssd_constants()

    xp = x_prompt.reshape(1, nb * seq, d)
    q, k, v, z, xbc, dt, ckv, krope = _inproj(xp, mod, 0, gpre, win_ctx, gq, wuq_ctx, gkv, wukv,
                                              None, tm=512)
    per_seq = lambda a: a.reshape(nb, seq, a.shape[-1])
    attn = _attention(per_seq(q), per_seq(k), per_seq(v), None, tq=seq)
    ssm, fin = _ssd(per_seq(xbc), per_seq(z), per_seq(dt), None, ssd_consts)
    y_p = _outffn(xp, attn.reshape(1, nb * seq, -1), ssm.reshape(1, nb * seq, -1), mod, 0,
                  wout, gpost, gpre_f, gpost_f, wg, wu, wd, tm=512).reshape(nb, seq, d)
    new_ckv = ckv.reshape(nb, 1, seq, KV_LORA)
    new_krope = krope.reshape(nb, 1, seq, QK_ROPE)
    new_ssm = fin.reshape(nb, 1, 2, SSM_HEADS, SSM_HEAD_DIM, D_STATE)

    past = cache_ckv.shape[2]
    kr_cache = jnp.pad(cache_krope[:, l], ((0, 0), (0, 0), (0, LANES - QK_ROPE)))
    kc, vc = _ctxkv(cache_ckv[:, l], kr_cache, wukv)
    q, k, v, z, xbc, dt = _inproj(x_sample, mod, 1, gpre, win_lat, gq, wuq_lat, gkv, wukv,
                                  _rope_tables(dseq), tm=512)
    attn = _attention(q, k, v, (kc, vc), tq=512)
    init = state_ssm[:, l].reshape(db, 2, SSM_WIDTH, D_STATE)
    ssm, _ = _ssd(xbc, z, dt, init, ssd_consts)
    y_s = _outffn(x_sample, attn, ssm, mod, 1, wout, gpost, gpre_f, gpost_f, wg, wu, wd, tm=512)

    return (y_p, y_s, new_ckv, new_krope, new_ssm)
```

```python
import functools
import math

import numpy as np
import jax
import jax.numpy as jnp
from jax import lax
from jax.experimental import pallas as pl
from jax.experimental.pallas import tpu as pltpu

F32 = jnp.float32
BF16 = jnp.bfloat16

D_MODEL = 1024
GRID_W = 64
ROPE_THETA = 10000.0
NORM_EPS = 1e-6
MLA_HEADS = 4
QK_NOPE = 128
QK_ROPE = 64
V_HEAD = 128
Q_LORA = 384
KV_LORA = 256
SSM_HEADS = 8
SSM_HEAD_DIM = 64
SSM_WIDTH = SSM_HEADS * SSM_HEAD_DIM
SSM_GROUPS = 2
D_STATE = 128
CHUNK = 128
CONV_DIM = SSM_WIDTH + 2 * SSM_GROUPS * D_STATE
D_FF = 2816

LANES = 128
QK_PAD = 256
GROUP_W = SSM_WIDTH // SSM_GROUPS
VMEM_LIMIT = 56 * 1024 * 1024

_SEG_Q = (0, 384)
_SEG_KV = (384, 640)
_SEG_Z = (640, 1152)
_SEG_XBC = (1152, 2176)
_SEG_KR = (2176, 2304)
_SEG_DT = (2304, 2432)
_SEG_KRS = (2432, 2560)


def _rms(x, g):
    return x * lax.rsqrt(jnp.mean(x * x, axis=-1, keepdims=True) + NORM_EPS) * g


def _silu(x):
    return x * (0.5 * jnp.tanh(0.5 * x) + 0.5)


def _dot(a, b):
    return jnp.dot(a, b, preferred_element_type=F32)


def _dot_nt(a, b):
    return lax.dot_general(a, b, (((1,), (1,)), ((), ())), preferred_element_type=F32)


def _split3(x):
    hi = x.astype(BF16)
    r = x - hi.astype(F32)
    mid = r.astype(BF16)
    lo = (r - mid.astype(F32)).astype(BF16)
    return hi, mid, lo


def _dot3_rhs(m, x):
    hi, mid, lo = _split3(x)
    return _dot(m, hi) + _dot(m, mid) + _dot(m, lo)


def _dot3_lhs(x, m):
    hi, mid, lo = _split3(x)
    return _dot(hi, m) + _dot(mid, m) + _dot(lo, m)


def _const_spec(shape):
    nd = len(shape)
    return pl.BlockSpec(shape, lambda *_: (0,) * nd, pipeline_mode=pl.Buffered(1))


def _params(sem):
    return pltpu.CompilerParams(dimension_semantics=sem, vmem_limit_bytes=VMEM_LIMIT)


def _mod_kernel(c_ref, w_ref, b_ref, o_ref):
    c = c_ref[...]
    s = _silu(c)
    w = w_ref[...]
    s_hi = s.astype(BF16)
    s_lo = (s - s_hi.astype(F32)).astype(BF16)
    w_hi = w.astype(BF16)
    w_lo = (w - w_hi.astype(F32)).astype(BF16)
    o_ref[...] = _dot(s_hi, w_hi) + _dot(s_lo, w_hi) + _dot(s_hi, w_lo) + b_ref[...]


def _modulation(cond, w_mod, b_mod):
    rows, d = cond.shape
    n = w_mod.shape[1]
    tn = 768
    return pl.pallas_call(
        _mod_kernel,
        grid=(n // tn,),
        in_specs=[pl.BlockSpec((rows, d), lambda j: (0, 0)),
                  pl.BlockSpec((d, tn), lambda j: (0, j)),
                  pl.BlockSpec((1, tn), lambda j: (0, j))],
        out_specs=pl.BlockSpec((rows, tn), lambda j: (0, j)),
        out_shape=jax.ShapeDtypeStruct((rows, n), F32),
        compiler_params=_params(("arbitrary",)),
        name="modulation",
    )(cond, w_mod, b_mod)


def _inproj_kernel(*refs, latent):
    if latent:
        (x_ref, mod_ref, gpre_ref, win_ref, gq_ref, wuq_ref, gkv_ref, wukv_ref, cos_ref, sin_ref,
         q_ref, k_ref, v_ref, z_ref, xbc_ref, dt_ref) = refs
    else:
        (x_ref, mod_ref, gpre_ref, win_ref, gq_ref, wuq_ref, gkv_ref, wukv_ref,
         q_ref, k_ref, v_ref, z_ref, xbc_ref, dt_ref, ckv_ref, krope_ref) = refs
    scale = (QK_NOPE + QK_ROPE) ** -0.5
    x = x_ref[0]
    m = mod_ref[0]
    h = _rms(x, gpre_ref[...]) * (1.0 + m[1:2]) + m[0:1]
    proj = _dot(h.astype(BF16), win_ref[...])
    q_c = proj[:, _SEG_Q[0]:_SEG_Q[1]]
    kv_c = proj[:, _SEG_KV[0]:_SEG_KV[1]]
    z_ref[0] = proj[:, _SEG_Z[0]:_SEG_Z[1]].astype(BF16)
    xbc_ref[0] = proj[:, _SEG_XBC[0]:_SEG_XBC[1]].astype(BF16)
    dt_ref[0] = proj[:, _SEG_DT[0]:_SEG_DT[1]]
    kr = proj[:, _SEG_KR[0]:_SEG_KR[1]]
    ckv = _rms(kv_c, gkv_ref[...])
    qall = _dot(_rms(q_c, gq_ref[...]).astype(BF16), wuq_ref[...])
    kv = _dot(ckv.astype(BF16), wukv_ref[...])
    if latent:
        cos = cos_ref[...]
        sin = sin_ref[...]
        kr_rot = kr * cos + proj[:, _SEG_KRS[0]:_SEG_KRS[1]] * sin
    else:
        kr_rot = kr
        ckv_ref[0] = ckv
        krope_ref[0] = kr[:, :QK_ROPE]
    kr_b = kr_rot.astype(BF16)
    nw = MLA_HEADS * LANES
    for hd in range(MLA_HEADS):
        base = hd * QK_PAD
        q_ref[0, :, base:base + LANES] = (qall[:, hd * LANES:(hd + 1) * LANES] * scale).astype(BF16)
        qr = qall[:, nw + hd * LANES:nw + (hd + 1) * LANES]
        if latent:
            qr = qr * cos + qall[:, 2 * nw + hd * LANES:2 * nw + (hd + 1) * LANES] * sin
        q_ref[0, :, base + LANES:base + QK_PAD] = (qr * scale).astype(BF16)
        k_ref[0, :, base:base + LANES] = kv[:, base:base + LANES].astype(BF16)
        k_ref[0, :, base + LANES:base + QK_PAD] = kr_b
        v_ref[0, :, hd * LANES:(hd + 1) * LANES] = kv[:, base + LANES:base + QK_PAD].astype(BF16)


def _inproj(x, mod, mod_off, gpre, win, gq, wuq, gkv, wukv, rope, tm):
    b, s, d = x.shape
    latent = rope is not None
    grid = (b, s // tm)
    tok = lambda w: pl.BlockSpec((1, tm, w), lambda i, j: (i, j, 0))
    in_specs = [tok(d),
                pl.BlockSpec((1, 6, d), lambda i, j: (i + mod_off, 0, 0)),
                _const_spec(gpre.shape), _const_spec(win.shape), _const_spec(gq.shape),
                _const_spec(wuq.shape), _const_spec(gkv.shape), _const_spec(wukv.shape)]
    args = [x, mod, gpre, win, gq, wuq, gkv, wukv]
    out_shape = [jax.ShapeDtypeStruct((b, s, MLA_HEADS * QK_PAD), BF16),
                 jax.ShapeDtypeStruct((b, s, MLA_HEADS * QK_PAD), BF16),
                 jax.ShapeDtypeStruct((b, s, MLA_HEADS * V_HEAD), BF16),
                 jax.ShapeDtypeStruct((b, s, SSM_WIDTH), BF16),
                 jax.ShapeDtypeStruct((b, s, CONV_DIM), BF16),
                 jax.ShapeDtypeStruct((b, s, LANES), F32)]
    out_specs = [tok(MLA_HEADS * QK_PAD), tok(MLA_HEADS * QK_PAD), tok(MLA_HEADS * V_HEAD),
                 tok(SSM_WIDTH), tok(CONV_DIM), tok(LANES)]
    if latent:
        in_specs += [pl.BlockSpec((tm, LANES), lambda i, j: (j, 0))] * 2
        args += list(rope)
    else:
        out_shape += [jax.ShapeDtypeStruct((b, s, KV_LORA), F32),
                      jax.ShapeDtypeStruct((b, s, QK_ROPE), F32)]
        out_specs += [tok(KV_LORA), tok(QK_ROPE)]
    return pl.pallas_call(
        functools.partial(_inproj_kernel, latent=latent),
        grid=grid, in_specs=in_specs, out_specs=out_specs, out_shape=out_shape,
        compiler_params=_params(("parallel", "parallel")),
        name="inproj_latent" if latent else "inproj_ctx",
    )(*args)


def _ctxkv_kernel(ckv_ref, kr_ref, wukv_ref, k_ref, v_ref):
    kv = _dot(ckv_ref[0].astype(BF16), wukv_ref[...])
    kr_b = kr_ref[0].astype(BF16)
    for hd in range(MLA_HEADS):
        base = hd * QK_PAD
        k_ref[0, :, base:base + LANES] = kv[:, base:base + LANES].astype(BF16)
        k_ref[0, :, base + LANES:base + QK_PAD] = kr_b
        v_ref[0, :, hd * LANES:(hd + 1) * LANES] = kv[:, base + LANES:base + QK_PAD].astype(BF16)


def _ctxkv(ckv, kr_pad, wukv):
    b, s, _ = ckv.shape
    blk = lambda w: pl.BlockSpec((1, s, w), lambda i: (i, 0, 0))
    return pl.pallas_call(
        _ctxkv_kernel,
        grid=(b,),
        in_specs=[blk(KV_LORA), blk(LANES), _const_spec(wukv.shape)],
        out_specs=[blk(MLA_HEADS * QK_PAD), blk(MLA_HEADS * V_HEAD)],
        out_shape=[jax.ShapeDtypeStruct((b, s, MLA_HEADS * QK_PAD), BF16),
                   jax.ShapeDtypeStruct((b, s, MLA_HEADS * V_HEAD), BF16)],
        compiler_params=_params(("parallel",)),
        name="ctx_kv",
    )(ckv, kr_pad, wukv)


def _attn_kernel(*refs, latent):
    if latent:
        q_ref, k_ref, v_ref, kc_ref, vc_ref, o_ref = refs
    else:
        q_ref, k_ref, v_ref, o_ref = refs
    for hd in range(MLA_HEADS):
        qs = slice(hd * QK_PAD, (hd + 1) * QK_PAD)
        vs = slice(hd * V_HEAD, (hd + 1) * V_HEAD)
        q = q_ref[0, :, qs]
        s = _dot_nt(q, k_ref[0, :, qs])
        mx = jnp.max(s, axis=-1, keepdims=True)
        if latent:
            sc = _dot_nt(q, kc_ref[0, :, qs])
            mx = jnp.maximum(mx, jnp.max(sc, axis=-1, keepdims=True))
        p = jnp.exp(s - mx)
        den = jnp.sum(p, axis=-1, keepdims=True)
        acc = _dot(p.astype(BF16), v_ref[0, :, vs])
        if latent:
            pc = jnp.exp(sc - mx)
            den = den + jnp.sum(pc, axis=-1, keepdims=True)
            acc = acc + _dot(pc.astype(BF16), vc_ref[0, :, vs])
        o_ref[0, :, vs] = (acc / den).astype(BF16)


def _attention(q, k, v, ctx, tq):
    b, s, _ = q.shape
    latent = ctx is not None
    qw, vw = MLA_HEADS * QK_PAD, MLA_HEADS * V_HEAD
    in_specs = [pl.BlockSpec((1, tq, qw), lambda i, j: (i, j, 0)),
                pl.BlockSpec((1, s, qw), lambda i, j: (i, 0, 0)),
                pl.BlockSpec((1, s, vw), lambda i, j: (i, 0, 0))]
    args = [q, k, v]
    if latent:
        sc = ctx[0].shape[1]
        in_specs += [pl.BlockSpec((1, sc, qw), lambda i, j: (i, 0, 0)),
                     pl.BlockSpec((1, sc, vw), lambda i, j: (i, 0, 0))]
        args += list(ctx)
    return pl.pallas_call(
        functools.partial(_attn_kernel, latent=latent),
        grid=(b, s // tq), in_specs=in_specs,
        out_specs=pl.BlockSpec((1, tq, vw), lambda i, j: (i, j, 0)),
        out_shape=jax.ShapeDtypeStruct((b, s, vw), BF16),
        compiler_params=_params(("parallel", "parallel")),
        name="attn_latent" if latent else "attn_ctx",
    )(*args)


def _ssd_kernel(*refs, seq, has_init):
    if has_init:
        (xbc_ref, z_ref, dt_ref, init_ref, cw_ref, cb_ref, dtbias_ref, alog_ref, dskip_ref, gout_ref,
         tril_ref, e64_ref,
         out_ref, fin_ref,
         xpad_s, xs_s, c_s, bt_s, y_s, xwb_s, eg_s, dt_s, sf_s, sb_s) = refs
    else:
        (xbc_ref, z_ref, dt_ref, cw_ref, cb_ref, dtbias_ref, alog_ref, dskip_ref, gout_ref,
         tril_ref, e64_ref,
         out_ref, fin_ref,
         xpad_s, xs_s, c_s, bt_s, y_s, xwb_s, eg_s, dt_s, sf_s, sb_s) = refs
    L = CHUNK
    nchunk = seq // L
    halo = 8

    xpad_s[0:halo, :] = jnp.zeros((halo, CONV_DIM), F32)
    xpad_s[halo + seq:2 * halo + seq, :] = jnp.zeros((halo, CONV_DIM), F32)

    def stage(j, carry):
        r0 = pl.multiple_of(j * L, L)
        xpad_s[pl.ds(r0 + halo, L), :] = xbc_ref[0, pl.ds(r0, L), :].astype(F32)
        dt_s[pl.ds(r0, L), :] = jax.nn.softplus(dt_ref[0, pl.ds(r0, L), :] + dtbias_ref[...])
        return carry

    lax.fori_loop(0, nchunk, stage, 0)

    if has_init:
        sf_s[...] = jnp.transpose(init_ref[0, 0])
        sb_s[...] = jnp.transpose(init_ref[0, 1])
    else:
        sf_s[...] = jnp.zeros_like(sf_s)
        sb_s[...] = jnp.zeros_like(sb_s)

    col = lax.broadcasted_iota(jnp.int32, (L, LANES), 1)
    row_t = lax.broadcasted_iota(jnp.int32, (L, L), 0)
    col_s = lax.broadcasted_iota(jnp.int32, (L, L), 1)
    lower = col_s <= row_t
    upper = col_s >= row_t
    lane_lo = lax.broadcasted_iota(jnp.int32, (L, LANES), 1) < SSM_HEAD_DIM
    a_row = -jnp.exp(alog_ref[...])
    neg_inf = jnp.float32(-jnp.inf)

    def fwd(j, carry):
        r0 = pl.multiple_of(j * L, L)
        win = xpad_s[pl.ds(r0, L + 2 * halo), :]
        prev = pltpu.roll(win, 1, axis=0)[halo:halo + L]
        nxt = pltpu.roll(win, L + 2 * halo - 1, axis=0)[halo:halo + L]
        cur = win[halo:halo + L]
        conv = cb_ref[...] + prev * cw_ref[0:1, :] + cur * cw_ref[1:2, :] + nxt * cw_ref[2:3, :]
        act = _silu(conv)
        xs = act[:, :SSM_WIDTH]
        bm = act[:, SSM_WIDTH:SSM_WIDTH + SSM_GROUPS * D_STATE]
        cm = act[:, SSM_WIDTH + SSM_GROUPS * D_STATE:]
        xs_s[pl.ds(r0, L), :] = xs
        cm_b = cm.astype(BF16)
        c_s[pl.ds(r0, L), :] = cm_b
        xs_b = xs.astype(BF16)

        dtc = dt_s[pl.ds(r0, L), :]
        da = jnp.where(col < 2 * SSM_HEADS, dtc * a_row, 0.0)
        cum = _dot3_rhs(tril_ref[...], da)
        tot = cum[L - 1:L, :]
        suf = tot - cum + da
        fg = jnp.where(col < SSM_HEADS, cum, suf)
        w_small = dtc * jnp.exp(tot - fg)
        e_small = jnp.exp(fg)
        w_hi = w_small.astype(BF16)
        w_lo = (w_small - w_hi.astype(F32)).astype(BF16)
        ex = _dot(jnp.concatenate([w_hi, w_lo, e_small.astype(BF16)], axis=0), e64_ref[...])
        w_x = ex[0:L] + ex[L:2 * L]
        e_x = ex[2 * L:3 * L]
        r_t = jnp.transpose(fg - jnp.log(dtc))

        wf_x = w_x[:, :SSM_WIDTH]
        wb_x = w_x[:, SSM_WIDTH:]
        ef_x = e_x[:, :SSM_WIDTH]
        eg_x = e_x[:, SSM_WIDTH:]
        eg_s[pl.ds(r0, L), :] = eg_x
        xwf_b = (xs * wf_x).astype(BF16)
        xwb_s[pl.ds(r0, L), :] = (xs * wb_x).astype(BF16)

        y_parts = []
        new_states = []
        for g in range(SSM_GROUPS):
            gs = slice(g * GROUP_W, (g + 1) * GROUP_W)
            bm_g = bm[:, g * D_STATE:(g + 1) * D_STATE]
            cm_g = cm_b[:, g * D_STATE:(g + 1) * D_STATE]
            bt_g = jnp.transpose(bm_g).astype(BF16)
            bt_s[pl.ds(pl.multiple_of((j * SSM_GROUPS + g) * D_STATE, D_STATE), D_STATE), :] = bt_g
            cbm = _dot_nt(cm_g, bm_g.astype(BF16))
            heads_per_group = SSM_HEADS // SSM_GROUPS
            for pair in range(heads_per_group // 2):
                ws = []
                for k in range(2):
                    hd = g * heads_per_group + pair * 2 + k
                    seg_f = fg[:, hd:hd + 1] - r_t[hd:hd + 1, :]
                    lf = jnp.exp(jnp.where(lower, seg_f, neg_inf))
                    hb = SSM_HEADS + hd
                    seg_b = fg[:, hb:hb + 1] - r_t[hb:hb + 1, :]
                    ub = jnp.exp(jnp.where(upper, seg_b, neg_inf))
                    ws.append((cbm * (lf + ub)).astype(BF16))
                p0 = (g * heads_per_group + pair * 2) * SSM_HEAD_DIM
                xpair = xs_b[:, p0:p0 + LANES]
                zeros = jnp.zeros_like(xpair)
                rhs = jnp.concatenate([jnp.where(lane_lo, xpair, zeros),
                                       jnp.where(lane_lo, zeros, xpair)], axis=0)
                y_parts.append(_dot(jnp.concatenate(ws, axis=1), rhs))
            s_in = sf_s[:, gs]
            y_off = _dot(cm_g, s_in.astype(BF16)) * ef_x[:, gs]
            y_parts[-2] = y_parts[-2] + y_off[:, :LANES]
            y_parts[-1] = y_parts[-1] + y_off[:, LANES:]
            new_states.append(s_in * ef_x[L - 1:L, gs] + _dot(bt_g, xwf_b[:, gs]))
        for g in range(SSM_GROUPS):
            sf_s[:, g * GROUP_W:(g + 1) * GROUP_W] = new_states[g]
        for i, yp in enumerate(y_parts):
            y_s[pl.ds(r0, L), i * LANES:(i + 1) * LANES] = yp
        return carry

    lax.fori_loop(0, nchunk, fwd, 0)

    def bwd(jj, carry):
        j = nchunk - 1 - jj
        r0 = pl.multiple_of(j * L, L)
        xs = xs_s[pl.ds(r0, L), :]
        eg_x = eg_s[pl.ds(r0, L), :]
        xwb_b = xwb_s[pl.ds(r0, L), :]
        cm_b = c_s[pl.ds(r0, L), :]
        y_off = []
        for g in range(SSM_GROUPS):
            gs = slice(g * GROUP_W, (g + 1) * GROUP_W)
            s_in = sb_s[:, gs]
            y_off.append(_dot(cm_b[:, g * D_STATE:(g + 1) * D_STATE], s_in.astype(BF16)) * eg_x[:, gs])
            bt_g = bt_s[pl.ds(pl.multiple_of((j * SSM_GROUPS + g) * D_STATE, D_STATE), D_STATE), :]
            sb_s[:, gs] = s_in * eg_x[0:1, gs] + _dot(bt_g, xwb_b[:, gs])
        y = y_s[pl.ds(r0, L), :] + jnp.concatenate(y_off, axis=1) + dskip_ref[...] * xs
        y = y * _silu(z_ref[0, pl.ds(r0, L), :].astype(F32))
        out_ref[0, pl.ds(r0, L), :] = _rms(y, gout_ref[...]).astype(BF16)
        return carry

    lax.fori_loop(0, nchunk, bwd, 0)
    fin_ref[0, 0] = jnp.transpose(sf_s[...])
    fin_ref[0, 1] = jnp.transpose(sb_s[...])


def _ssd(xbc, z, dt, init, consts):
    b, s, _ = xbc.shape
    has_init = init is not None
    nchunk = s // CHUNK
    blk = lambda w: pl.BlockSpec((1, s, w), lambda i: (i, 0, 0))
    st_spec = pl.BlockSpec((1, 2, SSM_WIDTH, D_STATE), lambda i: (i, 0, 0, 0))
    in_specs = [blk(CONV_DIM), blk(SSM_WIDTH), blk(LANES)]
    args = [xbc, z, dt]
    if has_init:
        in_specs.append(st_spec)
        args.append(init)
    in_specs += [_const_spec(c.shape) for c in consts]
    args += list(consts)
    scratch = [pltpu.VMEM((s + 16, CONV_DIM), F32),
               pltpu.VMEM((s, SSM_WIDTH), F32),
               pltpu.VMEM((s, SSM_GROUPS * D_STATE), BF16),
               pltpu.VMEM((nchunk * SSM_GROUPS * D_STATE, CHUNK), BF16),
               pltpu.VMEM((s, SSM_WIDTH), F32),
               pltpu.VMEM((s, SSM_WIDTH), BF16),
               pltpu.VMEM((s, SSM_WIDTH), F32),
               pltpu.VMEM((s, LANES), F32),
               pltpu.VMEM((D_STATE, SSM_WIDTH), F32),
               pltpu.VMEM((D_STATE, SSM_WIDTH), F32)]
    return pl.pallas_call(
        functools.partial(_ssd_kernel, seq=s, has_init=has_init),
        grid=(b,), in_specs=in_specs,
        out_specs=[blk(SSM_WIDTH), st_spec],
        out_shape=[jax.ShapeDtypeStruct((b, s, SSM_WIDTH), BF16),
                   jax.ShapeDtypeStruct((b, 2, SSM_WIDTH, D_STATE), F32)],
        scratch_shapes=scratch,
        compiler_params=_params(("parallel",)),
        name="ssd_latent" if has_init else "ssd_ctx",
    )(*args)


FF_CHUNK = 256


def _outffn_kernel(x_ref, attn_ref, ssm_ref, mod_ref, wout_ref, gpost_ref, gpre_ref, gpostf_ref,
                   wg_ref, wu_ref, wd_ref, o_ref):
    x = x_ref[0]
    m = mod_ref[0]
    half = MLA_HEADS * V_HEAD
    mix = _dot(attn_ref[0], wout_ref[0:half, :]) + _dot(ssm_ref[0], wout_ref[half:, :])
    y = x + m[2:3] * _rms(mix, gpost_ref[...])
    h = (_rms(y, gpre_ref[...]) * (1.0 + m[4:5]) + m[3:4]).astype(BF16)
    acc = None
    for c in range(D_FF // FF_CHUNK):
        cs = slice(c * FF_CHUNK, (c + 1) * FF_CHUNK)
        gate = _dot(h, wg_ref[:, cs])
        up = _dot(h, wu_ref[:, cs])
        part = _dot((_silu(gate) * up).astype(BF16), wd_ref[cs, :])
        acc = part if acc is None else acc + part
    o_ref[0] = y + m[5:6] * _rms(acc, gpostf_ref[...])


def _outffn(x, attn, ssm, mod, mod_off, wout, gpost, gpre, gpostf, wg, wu, wd, tm):
    b, s, d = x.shape
    tok = lambda w: pl.BlockSpec((1, tm, w), lambda i, j: (i, j, 0))
    consts = [wout, gpost, gpre, gpostf, wg, wu, wd]
    return pl.pallas_call(
        _outffn_kernel,
        grid=(b, s // tm),
        in_specs=[tok(d), tok(MLA_HEADS * V_HEAD), tok(SSM_WIDTH),
                  pl.BlockSpec((1, 6, d), lambda i, j: (i + mod_off, 0, 0))]
                 + [_const_spec(c.shape) for c in consts],
        out_specs=tok(d),
        out_shape=jax.ShapeDtypeStruct((b, s, d), F32),
        compiler_params=_params(("parallel", "parallel")),
        name="out_ffn",
    )(x, attn, ssm, mod, *consts)


def _rope_tables(length):
    quarter = QK_ROPE // 4
    pos = np.arange(length)
    inv_freq = ROPE_THETA ** (-np.arange(quarter, dtype=np.float64) / quarter)
    ang_r = (pos // GRID_W)[:, None] * inv_freq[None, :]
    ang_c = (pos % GRID_W)[:, None] * inv_freq[None, :]
    zeros = np.zeros((length, LANES - QK_ROPE))
    cos = np.concatenate([np.cos(ang_r)] * 2 + [np.cos(ang_c)] * 2 + [zeros], axis=1)
    sin = np.concatenate([-np.sin(ang_r), np.sin(ang_r), -np.sin(ang_c), np.sin(ang_c), zeros], axis=1)
    return jnp.asarray(cos, F32), jnp.asarray(sin, F32)


def _swap_halves(w):
    q = QK_ROPE // 4
    return jnp.concatenate([w[..., q:2 * q], w[..., :q], w[..., 3 * q:], w[..., 2 * q:3 * q]], axis=-1)


def _pad_cols(w, width):
    return jnp.pad(w, ((0, 0), (0, width - w.shape[-1])))


def _ssd_constants():
    idx = np.arange(CHUNK)
    tril = (idx[None, :] <= idx[:, None]).astype(np.float32)
    ncol = 2 * SSM_HEADS
    e64 = np.zeros((LANES, ncol * SSM_HEAD_DIM), np.float32)
    for q in range(ncol):
        e64[q, q * SSM_HEAD_DIM:(q + 1) * SSM_HEAD_DIM] = 1.0
    return [jnp.asarray(a, BF16) for a in (tril, e64)]


def kernel(x_prompt, x_sample, cache_ckv, cache_krope, state_ssm, c, c_ctx, w_mod, b_mod,
           g_pre_mix, g_post_mix, w_in, g_q, w_uq, g_kv, w_ukv, conv_w, conv_b, dt_bias,
           a_log, d_skip, g_ssm_out, w_out, g_pre_ffn, g_post_ffn, w_gate, w_up, w_down):
    depth = w_in.shape[0]
    assert depth == 1
    nb, seq, d = x_prompt.shape
    db, dseq, _ = x_sample.shape
    l = 0

    rows = 16
    cond = jnp.concatenate([c_ctx[None, :], c, jnp.zeros((rows - 1 - db, d), F32)], axis=0)
    mod = _modulation(cond, w_mod[l], b_mod[l][None, :]).reshape(rows, 6, d)

    wi = w_in[l]
    o_q, o_kv, o_kr = Q_LORA, Q_LORA + KV_LORA, Q_LORA + KV_LORA + QK_ROPE
    o_z, o_xbc = o_kr + SSM_WIDTH, o_kr + SSM_WIDTH + CONV_DIM
    w_kr = wi[:, o_kv:o_kr]
    segs = [wi[:, :o_q], wi[:, o_q:o_kv], wi[:, o_kr:o_z], wi[:, o_z:o_xbc],
            _pad_cols(w_kr, LANES), _pad_cols(wi[:, o_xbc:], LANES)]
    win_ctx = jnp.concatenate(segs, axis=1).astype(BF16)
    win_lat = jnp.concatenate(segs + [_pad_cols(_swap_halves(w_kr), LANES)], axis=1).astype(BF16)

    wq = w_uq[l].reshape(Q_LORA, MLA_HEADS, QK_NOPE + QK_ROPE)
    wq_n = wq[:, :, :QK_NOPE].reshape(Q_LORA, MLA_HEADS * QK_NOPE)
    wq_r = wq[:, :, QK_NOPE:]
    pad_r = lambda w: jnp.pad(w, ((0, 0), (0, 0), (0, LANES - QK_ROPE))).reshape(Q_LORA, MLA_HEADS * LANES)
    wuq_ctx = jnp.concatenate([wq_n, pad_r(wq_r)], axis=1).astype(BF16)
    wuq_lat = jnp.concatenate([wq_n, pad_r(wq_r), pad_r(_swap_halves(wq_r))], axis=1).astype(BF16)
    wukv = w_ukv[l].astype(BF16)
    wout = w_out[l].astype(BF16)
    wg, wu, wd = w_gate[l].astype(BF16), w_up[l].astype(BF16), w_down[l].astype(BF16)
    row = lambda v: v.reshape(1, -1)
    gpre, gq, gkv = row(g_pre_mix[l]), row(g_q[l]), row(g_kv[l])
    gpost, gpre_f, gpost_f = row(g_post_mix[l]), row(g_pre_ffn[l]), row(g_post_ffn[l])

    small = lambda v: _pad_cols(v.reshape(1, -1), LANES)
    ssd_consts = [conv_w[l], row(conv_b[l]), small(dt_bias[l]), small(a_log[l]),
                  row(jnp.repeat(d_skip[l], SSM_HEAD_DIM)), row(g_ssm_out[l])] + _ssd_constants()

    xp = x_prompt.reshape(1, nb * seq, d)
    q, k, v, z, xbc, dt, ckv, krope = _inproj(xp, mod, 0, gpre, win_ctx, gq, wuq_ctx, gkv, wukv,
                                              None, tm=512)
    per_seq = lambda a: a.reshape(nb, seq, a.shape[-1])
    attn = _attention(per_seq(q), per_seq(k), per_seq(v), None, tq=seq)
    ssm, fin = _ssd(per_seq(xbc), per_seq(z), per_seq(dt), None, ssd_consts)
    y_p = _outffn(xp, attn.reshape(1, nb * seq, -1), ssm.reshape(1, nb * seq, -1), mod, 0,
                  wout, gpost, gpre_f, gpost_f, wg, wu, wd, tm=512).reshape(nb, seq, d)
    new_ckv = ckv.reshape(nb, 1, seq, KV_LORA)
    new_krope = krope.reshape(nb, 1, seq, QK_ROPE)
    new_ssm = fin.reshape(nb, 1, 2, SSM_HEADS, SSM_HEAD_DIM, D_STATE)

    past = cache_ckv.shape[2]
    kr_cache = jnp.pad(cache_krope[:, l], ((0, 0), (0, 0), (0, LANES - QK_ROPE)))
    kc, vc = _ctxkv(cache_ckv[:, l], kr_cache, wukv)
    q, k, v, z, xbc, dt = _inproj(x_sample, mod, 1, gpre, win_lat, gq, wuq_lat, gkv, wukv,
                                  _rope_tables(dseq), tm=512)
    attn = _attention(q, k, v, (kc, vc), tq=512)
    init = state_ssm[:, l].reshape(db, 2, SSM_WIDTH, D_STATE)
    ssm, _ = _ssd(xbc, z, dt, init, ssd_consts)
    y_s = _outffn(x_sample, attn, ssm, mod, 1, wout, gpost, gpre_f, gpost_f, wg, wu, wd, tm=512)

    return (y_p, y_s, new_ckv, new_krope, new_ssm)
```

```python
import functools

import numpy as np
import jax
import jax.numpy as jnp
from jax import lax
from jax.experimental import pallas as pl
from jax.experimental.pallas import tpu as pltpu

F32 = jnp.float32
BF16 = jnp.bfloat16

D_MODEL = 1024
GRID_W = 64
ROPE_THETA = 10000.0
NORM_EPS = 1e-6
MLA_HEADS = 4
QK_NOPE = 128
QK_ROPE = 64
V_HEAD = 128
Q_LORA = 384
KV_LORA = 256
SSM_HEADS = 8
SSM_HEAD_DIM = 64
SSM_WIDTH = SSM_HEADS * SSM_HEAD_DIM
SSM_GROUPS = 2
D_STATE = 128
D_CONV = 3
CHUNK = 128
CONV_DIM = SSM_WIDTH + 2 * SSM_GROUPS * D_STATE
D_FF = 2816

LANES = 128
SUBLANES = 8
QK_PAD = 256
GROUP_W = SSM_WIDTH // SSM_GROUPS
VMEM_LIMIT = 56 * 1024 * 1024

_SEG_Q = (0, 384)
_SEG_KV = (384, 640)
_SEG_Z = (640, 1152)
_SEG_XBC = (1152, 2176)
_SEG_KR = (2176, 2304)
DT_OFF = QK_ROPE
N_DT = 2 * SSM_HEADS


def _rms(x, g):
    return x * lax.rsqrt(jnp.mean(x * x, axis=-1, keepdims=True) + NORM_EPS) * g


def _silu(x):
    u = 0.5 * x
    return u * jnp.tanh(u) + u


def _dot(a, b):
    return jnp.dot(a, b, preferred_element_type=F32)


def _dot_nt(a, b):
    return lax.dot_general(a, b, (((1,), (1,)), ((), ())), preferred_element_type=F32)


def _split3(x):
    hi = x.astype(BF16)
    r = x - hi.astype(F32)
    mid = r.astype(BF16)
    lo = (r - mid.astype(F32)).astype(BF16)
    return hi, mid, lo


def _dot3_rhs(m, x):
    hi, mid, lo = _split3(x)
    return _dot(m, hi) + _dot(m, mid) + _dot(m, lo)


def _const_spec(shape):
    nd = len(shape)
    return pl.BlockSpec(shape, lambda *_: (0,) * nd, pipeline_mode=pl.Buffered(1))


def _params(sem):
    return pltpu.CompilerParams(dimension_semantics=sem, vmem_limit_bytes=VMEM_LIMIT)


def _mod_kernel(c_ref, w_ref, b_ref, o_ref):
    c = c_ref[...]
    s = _silu(c)
    w = w_ref[...]
    s_hi = s.astype(BF16)
    s_lo = (s - s_hi.astype(F32)).astype(BF16)
    w_hi = w.astype(BF16)
    w_lo = (w - w_hi.astype(F32)).astype(BF16)
    o_ref[...] = _dot(s_hi, w_hi) + _dot(s_lo, w_hi) + _dot(s_hi, w_lo) + b_ref[...]


def _modulation(cond, w_mod, b_mod):
    rows, d = cond.shape
    n = w_mod.shape[1]
    tn = 768
    return pl.pallas_call(
        _mod_kernel,
        grid=(n // tn,),
        in_specs=[pl.BlockSpec((rows, d), lambda j: (0, 0)),
                  pl.BlockSpec((d, tn), lambda j: (0, j)),
                  pl.BlockSpec((1, tn), lambda j: (0, j))],
        out_specs=pl.BlockSpec((rows, tn), lambda j: (0, j)),
        out_shape=jax.ShapeDtypeStruct((rows, n), F32),
        compiler_params=_params(("arbitrary",)),
        name="modulation",
    )(cond, w_mod, b_mod)


def _conv_silu(xb, prev_rows, next_rows, cw_ref, cb_ref):
    n = xb.shape[0]
    win = jnp.concatenate([prev_rows, xb, next_rows], axis=0)
    total = n + 2 * SUBLANES
    prev = pltpu.roll(win, 1, axis=0)[SUBLANES:SUBLANES + n]
    nxt = pltpu.roll(win, total - 1, axis=0)[SUBLANES:SUBLANES + n]
    conv = cb_ref[...] + prev * cw_ref[0:1, :] + xb * cw_ref[1:2, :] + nxt * cw_ref[2:3, :]
    return _silu(conv)


def _inproj_kernel(*refs, latent, seq_rows):
    if latent:
        (x_ref, xprev_ref, xnext_ref, mod_ref, gpre_ref, win_ref, gq_ref, wuq_ref, gkv_ref, wukv_ref,
         cw_ref, cb_ref, cos_ref, sin_ref,
         q_ref, k_ref, v_ref, z_ref, act_ref, small_ref) = refs
    else:
        (x_ref, mod_ref, gpre_ref, win_ref, gq_ref, wuq_ref, gkv_ref, wukv_ref, cw_ref, cb_ref,
         q_ref, k_ref, v_ref, z_ref, act_ref, small_ref, ckv_ref, krope_ref) = refs
    scale = (QK_NOPE + QK_ROPE) ** -0.5
    m = mod_ref[0]

    def pre(xv):
        return (_rms(xv, gpre_ref[...]) * (1.0 + m[1:2]) + m[0:1]).astype(BF16)

    tm = x_ref.shape[1]
    proj = _dot(pre(x_ref[0]), win_ref[...])
    z_ref[0] = proj[:, _SEG_Z[0]:_SEG_Z[1]].astype(BF16)
    xbc = proj[:, _SEG_XBC[0]:_SEG_XBC[1]]
    if latent:
        j = pl.program_id(1)
        xh = jnp.concatenate([xprev_ref[0], xnext_ref[0]], axis=0)
        ph = _dot(pre(xh), win_ref[:, _SEG_XBC[0]:_SEG_XBC[1]])
        prev_rows = ph[0:SUBLANES] * jnp.where(j > 0, 1.0, 0.0)
        next_rows = ph[SUBLANES:] * jnp.where(j < pl.num_programs(1) - 1, 1.0, 0.0)
        act_ref[0] = _conv_silu(xbc, prev_rows, next_rows, cw_ref, cb_ref).astype(BF16)
    else:
        zero_rows = jnp.zeros((SUBLANES, CONV_DIM), F32)
        for s in range(tm // seq_rows):
            rs = slice(s * seq_rows, (s + 1) * seq_rows)
            act_ref[0, rs, :] = _conv_silu(xbc[rs], zero_rows, zero_rows, cw_ref, cb_ref).astype(BF16)

    krb = proj[:, _SEG_KR[0]:_SEG_KR[1]]
    small_ref[0] = krb
    ckv = _rms(proj[:, _SEG_KV[0]:_SEG_KV[1]], gkv_ref[...])
    qall = _dot(_rms(proj[:, _SEG_Q[0]:_SEG_Q[1]], gq_ref[...]).astype(BF16), wuq_ref[...])
    kv = _dot(ckv.astype(BF16), wukv_ref[...])

    lane = lax.broadcasted_iota(jnp.int32, (tm, LANES), 1)
    rope_lanes = lane < QK_ROPE
    if latent:
        cos = cos_ref[...]
        sin = sin_ref[...]
        first_quarter = jnp.bitwise_and(lane, QK_ROPE // 4) == 0

        def rot(t):
            swapped = jnp.where(first_quarter, pltpu.roll(t, LANES - QK_ROPE // 4, axis=1),
                                pltpu.roll(t, QK_ROPE // 4, axis=1))
            return t * cos + swapped * sin
    else:
        rot = lambda t: t
        ckv_ref[0] = ckv
        krope_ref[0] = krb[:, :QK_ROPE]
    kr_b = jnp.where(rope_lanes, rot(krb), 0.0).astype(BF16)
    nw = MLA_HEADS * QK_NOPE
    for pr in range(MLA_HEADS // 2):
        pair = rot(qall[:, nw + pr * LANES:nw + (pr + 1) * LANES]) * scale
        for k in range(2):
            hd = 2 * pr + k
            base = hd * QK_PAD
            qr = pair if k == 0 else pltpu.roll(pair, QK_ROPE, axis=1)
            q_ref[0, :, base:base + LANES] = (qall[:, hd * LANES:(hd + 1) * LANES] * scale).astype(BF16)
            q_ref[0, :, base + LANES:base + QK_PAD] = jnp.where(rope_lanes, qr, 0.0).astype(BF16)
            k_ref[0, :, base:base + LANES] = kv[:, base:base + LANES].astype(BF16)
            k_ref[0, :, base + LANES:base + QK_PAD] = kr_b
            v_ref[0, :, hd * LANES:(hd + 1) * LANES] = kv[:, base + LANES:base + QK_PAD].astype(BF16)


def _inproj(x, mod, mod_off, gpre, win, gq, wuq, gkv, wukv, cw, cb, rope, tm, seq_rows):
    b, s, d = x.shape
    latent = rope is not None
    grid = (b, s // tm)
    tok = lambda w: pl.BlockSpec((1, tm, w), lambda i, j: (i, j, 0))
    mod_spec = pl.BlockSpec((1, 6, d), lambda i, j: (i + mod_off, 0, 0))
    consts = [gpre, win, gq, wuq, gkv, wukv, cw, cb]
    if latent:
        per_tile = tm // SUBLANES
        last = s // SUBLANES - 1
        in_specs = [tok(d),
                    pl.BlockSpec((1, SUBLANES, d), lambda i, j: (i, jnp.maximum(j * per_tile - 1, 0), 0)),
                    pl.BlockSpec((1, SUBLANES, d), lambda i, j: (i, jnp.minimum((j + 1) * per_tile, last), 0)),
                    mod_spec]
        args = [x, x, x, mod]
    else:
        in_specs = [tok(d), mod_spec]
        args = [x, mod]
    in_specs += [_const_spec(c.shape) for c in consts]
    args += consts
    out_shape = [jax.ShapeDtypeStruct((b, s, MLA_HEADS * QK_PAD), BF16),
                 jax.ShapeDtypeStruct((b, s, MLA_HEADS * QK_PAD), BF16),
                 jax.ShapeDtypeStruct((b, s, MLA_HEADS * V_HEAD), BF16),
                 jax.ShapeDtypeStruct((b, s, SSM_WIDTH), BF16),
                 jax.ShapeDtypeStruct((b, s, CONV_DIM), BF16),
                 jax.ShapeDtypeStruct((b, s, LANES), F32)]
    out_specs = [tok(MLA_HEADS * QK_PAD), tok(MLA_HEADS * QK_PAD), tok(MLA_HEADS * V_HEAD),
                 tok(SSM_WIDTH), tok(CONV_DIM), tok(LANES)]
    if latent:
        in_specs += [pl.BlockSpec((tm, LANES), lambda i, j: (j, 0))] * 2
        args += list(rope)
    else:
        out_shape += [jax.ShapeDtypeStruct((b, s, KV_LORA), F32),
                      jax.ShapeDtypeStruct((b, s, QK_ROPE), F32)]
        out_specs += [tok(KV_LORA), tok(QK_ROPE)]
    return pl.pallas_call(
        functools.partial(_inproj_kernel, latent=latent, seq_rows=seq_rows),
        grid=grid, in_specs=in_specs, out_specs=out_specs, out_shape=out_shape,
        compiler_params=_params(("parallel", "parallel")),
        name="inproj_latent" if latent else "inproj_ctx",
    )(*args)


def _ctxkv_kernel(ckv_ref, kr_ref, wukv_ref, k_ref, v_ref):
    kv = _dot(ckv_ref[0].astype(BF16), wukv_ref[...])
    kr_b = kr_ref[0].astype(BF16)
    for hd in range(MLA_HEADS):
        base = hd * QK_PAD
        k_ref[0, :, base:base + LANES] = kv[:, base:base + LANES].astype(BF16)
        k_ref[0, :, base + LANES:base + QK_PAD] = kr_b
        v_ref[0, :, hd * LANES:(hd + 1) * LANES] = kv[:, base + LANES:base + QK_PAD].astype(BF16)


def _ctxkv(ckv, kr_pad, wukv):
    b, s, _ = ckv.shape
    blk = lambda w: pl.BlockSpec((1, s, w), lambda i: (i, 0, 0))
    return pl.pallas_call(
        _ctxkv_kernel,
        grid=(b,),
        in_specs=[blk(KV_LORA), blk(LANES), _const_spec(wukv.shape)],
        out_specs=[blk(MLA_HEADS * QK_PAD), blk(MLA_HEADS * V_HEAD)],
        out_shape=[jax.ShapeDtypeStruct((b, s, MLA_HEADS * QK_PAD), BF16),
                   jax.ShapeDtypeStruct((b, s, MLA_HEADS * V_HEAD), BF16)],
        compiler_params=_params(("parallel",)),
        name="ctx_kv",
    )(ckv, kr_pad, wukv)


def _attn_kernel(*refs, latent):
    if latent:
        q_ref, k_ref, v_ref, kc_ref, vc_ref, o_ref = refs
    else:
        q_ref, k_ref, v_ref, o_ref = refs
    for sq in range(q_ref.shape[0]):
        for hd in range(MLA_HEADS):
            qs = slice(hd * QK_PAD, (hd + 1) * QK_PAD)
            vs = slice(hd * V_HEAD, (hd + 1) * V_HEAD)
            q = q_ref[sq, :, qs]
            s = _dot_nt(q, k_ref[sq, :, qs])
            mx = jnp.max(s, axis=-1, keepdims=True)
            if latent:
                sc = _dot_nt(q, kc_ref[sq, :, qs])
                mx = jnp.maximum(mx, jnp.max(sc, axis=-1, keepdims=True))
            p = jnp.exp(s - mx)
            den = jnp.sum(p, axis=-1, keepdims=True)
            acc = _dot(p.astype(BF16), v_ref[sq, :, vs])
            if latent:
                pc = jnp.exp(sc - mx)
                den = den + jnp.sum(pc, axis=-1, keepdims=True)
                acc = acc + _dot(pc.astype(BF16), vc_ref[sq, :, vs])
            o_ref[sq, :, vs] = (acc / den).astype(BF16)


def _attention(q, k, v, ctx, tq, bs):
    b, s, _ = q.shape
    latent = ctx is not None
    qw, vw = MLA_HEADS * QK_PAD, MLA_HEADS * V_HEAD
    in_specs = [pl.BlockSpec((bs, tq, qw), lambda i, j: (i, j, 0)),
                pl.BlockSpec((bs, s, qw), lambda i, j: (i, 0, 0)),
                pl.BlockSpec((bs, s, vw), lambda i, j: (i, 0, 0))]
    args = [q, k, v]
    if latent:
        sc = ctx[0].shape[1]
        in_specs += [pl.BlockSpec((bs, sc, qw), lambda i, j: (i, 0, 0)),
                     pl.BlockSpec((bs, sc, vw), lambda i, j: (i, 0, 0))]
        args += list(ctx)
    return pl.pallas_call(
        functools.partial(_attn_kernel, latent=latent),
        grid=(b // bs, s // tq), in_specs=in_specs,
        out_specs=pl.BlockSpec((bs, tq, vw), lambda i, j: (i, j, 0)),
        out_shape=jax.ShapeDtypeStruct((b, s, vw), BF16),
        compiler_params=_params(("parallel", "parallel")),
        name="attn_latent" if latent else "attn_ctx",
    )(*args)


def _ssd_kernel(*refs, seq, has_init):
    if has_init:
        (act_ref, z_ref, small_ref, init_ref, dtbias_ref, alog_ref, dskip_ref, gout_ref,
         tril_ref, e64_ref, out_ref, fin_ref, bt_s, y_s, xwb_s, eg_s, sf_s, sb_s) = refs
    else:
        (act_ref, z_ref, small_ref, dtbias_ref, alog_ref, dskip_ref, gout_ref,
         tril_ref, e64_ref, out_ref, fin_ref, bt_s, y_s, xwb_s, eg_s, sf_s, sb_s) = refs
    L = CHUNK
    nchunk = seq // L
    if has_init:
        sf_s[...] = jnp.transpose(init_ref[0, 0])
        sb_s[...] = jnp.transpose(init_ref[0, 1])
    else:
        sf_s[...] = jnp.zeros_like(sf_s)
        sb_s[...] = jnp.zeros_like(sb_s)

    col = lax.broadcasted_iota(jnp.int32, (L, LANES), 1)
    dt_cols = (col >= DT_OFF) & (col < DT_OFF + N_DT)
    fwd_cols = col < DT_OFF + SSM_HEADS
    row_t = lax.broadcasted_iota(jnp.int32, (L, L), 0)
    col_s = lax.broadcasted_iota(jnp.int32, (L, L), 1)
    lower = col_s <= row_t
    upper = col_s >= row_t
    lane_lo = col < SSM_HEAD_DIM
    a_row = -jnp.exp(alog_ref[...])
    neg_inf = jnp.float32(-jnp.inf)
    b_off = SSM_WIDTH
    c_off = SSM_WIDTH + SSM_GROUPS * D_STATE

    def fwd(j, carry):
        r0 = pl.multiple_of(j * L, L)
        xs_b = act_ref[0, pl.ds(r0, L), 0:SSM_WIDTH]
        xs = xs_b.astype(F32)

        dtc = jax.nn.softplus(small_ref[0, pl.ds(r0, L), :] + dtbias_ref[...])
        da = jnp.where(dt_cols, dtc * a_row, 0.0)
        cum = _dot3_rhs(tril_ref[...], da)
        tot = cum[L - 1:L, :]
        suf = tot - cum + da
        fg = jnp.where(fwd_cols, cum, suf)
        w_small = jnp.where(dt_cols, dtc * jnp.exp(tot - fg), 0.0)
        e_small = jnp.where(dt_cols, jnp.exp(fg), 0.0)
        w_hi = w_small.astype(BF16)
        w_lo = (w_small - w_hi.astype(F32)).astype(BF16)
        ex = _dot(jnp.concatenate([w_hi, w_lo, e_small.astype(BF16)], axis=0), e64_ref[...])
        w_x = ex[0:L] + ex[L:2 * L]
        e_x = ex[2 * L:3 * L]
        r_t = jnp.transpose(fg - jnp.log(dtc))

        wf_x = w_x[:, :SSM_WIDTH]
        wb_x = w_x[:, SSM_WIDTH:]
        ef_x = e_x[:, :SSM_WIDTH]
        eg_x = e_x[:, SSM_WIDTH:]
        eg_s[pl.ds(r0, L), :] = eg_x
        xwf_b = (xs * wf_x).astype(BF16)
        xwb_s[pl.ds(r0, L), :] = (xs * wb_x).astype(BF16)

        y_parts = []
        new_states = []
        heads_per_group = SSM_HEADS // SSM_GROUPS
        for g in range(SSM_GROUPS):
            gs = slice(g * GROUP_W, (g + 1) * GROUP_W)
            bm_g = act_ref[0, pl.ds(r0, L), b_off + g * D_STATE:b_off + (g + 1) * D_STATE]
            cm_g = act_ref[0, pl.ds(r0, L), c_off + g * D_STATE:c_off + (g + 1) * D_STATE]
            bt_g = jnp.transpose(bm_g.astype(F32)).astype(BF16)
            bt_s[pl.ds(pl.multiple_of((j * SSM_GROUPS + g) * D_STATE, D_STATE), D_STATE), :] = bt_g
            cbm = _dot_nt(cm_g, bm_g)
            for pair in range(heads_per_group // 2):
                ws = []
                for k in range(2):
                    cf = DT_OFF + g * heads_per_group + pair * 2 + k
                    seg_f = fg[:, cf:cf + 1] - r_t[cf:cf + 1, :]
                    lf = jnp.exp(jnp.where(lower, seg_f, neg_inf))
                    cg = cf + SSM_HEADS
                    seg_b = fg[:, cg:cg + 1] - r_t[cg:cg + 1, :]
                    ub = jnp.exp(jnp.where(upper, seg_b, neg_inf))
                    ws.append((cbm * (lf + ub)).astype(BF16))
                p0 = (g * heads_per_group + pair * 2) * SSM_HEAD_DIM
                xpair = xs_b[:, p0:p0 + LANES]
                zeros = jnp.zeros_like(xpair)
                rhs = jnp.concatenate([jnp.where(lane_lo, xpair, zeros),
                                       jnp.where(lane_lo, zeros, xpair)], axis=0)
                y_parts.append(_dot(jnp.concatenate(ws, axis=1), rhs))
            s_in = sf_s[:, gs]
            y_off = _dot(cm_g, s_in.astype(BF16)) * ef_x[:, gs]
            y_parts[-2] = y_parts[-2] + y_off[:, :LANES]
            y_parts[-1] = y_parts[-1] + y_off[:, LANES:]
            new_states.append(s_in * ef_x[L - 1:L, gs] + _dot(bt_g, xwf_b[:, gs]))
        for g in range(SSM_GROUPS):
            sf_s[:, g * GROUP_W:(g + 1) * GROUP_W] = new_states[g]
        for i, yp in enumerate(y_parts):
            y_s[pl.ds(r0, L), i * LANES:(i + 1) * LANES] = yp
        return carry

    lax.fori_loop(0, nchunk, fwd, 0, unroll=2)

    def bwd(jj, carry):
        j = nchunk - 1 - jj
        r0 = pl.multiple_of(j * L, L)
        xs = act_ref[0, pl.ds(r0, L), 0:SSM_WIDTH].astype(F32)
        eg_x = eg_s[pl.ds(r0, L), :]
        xwb_b = xwb_s[pl.ds(r0, L), :]
        y_off = []
        for g in range(SSM_GROUPS):
            gs = slice(g * GROUP_W, (g + 1) * GROUP_W)
            s_in = sb_s[:, gs]
            cm_g = act_ref[0, pl.ds(r0, L), c_off + g * D_STATE:c_off + (g + 1) * D_STATE]
            y_off.append(_dot(cm_g, s_in.astype(BF16)) * eg_x[:, gs])
            bt_g = bt_s[pl.ds(pl.multiple_of((j * SSM_GROUPS + g) * D_STATE, D_STATE), D_STATE), :]
            sb_s[:, gs] = s_in * eg_x[0:1, gs] + _dot(bt_g, xwb_b[:, gs])
        y = y_s[pl.ds(r0, L), :] + jnp.concatenate(y_off, axis=1) + dskip_ref[...] * xs
        y = y * _silu(z_ref[0, pl.ds(r0, L), :].astype(F32))
        out_ref[0, pl.ds(r0, L), :] = _rms(y, gout_ref[...]).astype(BF16)
        return carry

    lax.fori_loop(0, nchunk, bwd, 0, unroll=2)
    fin_ref[0, 0] = jnp.transpose(sf_s[...])
    fin_ref[0, 1] = jnp.transpose(sb_s[...])


def _ssd(act, z, small, init, consts):
    b, s, _ = act.shape
    has_init = init is not None
    nchunk = s // CHUNK
    blk = lambda w: pl.BlockSpec((1, s, w), lambda i: (i, 0, 0))
    st_spec = pl.BlockSpec((1, 2, SSM_WIDTH, D_STATE), lambda i: (i, 0, 0, 0))
    in_specs = [blk(CONV_DIM), blk(SSM_WIDTH), blk(LANES)]
    args = [act, z, small]
    if has_init:
        in_specs.append(st_spec)
        args.append(init)
    in_specs += [_const_spec(c.shape) for c in consts]
    args += list(consts)
    scratch = [pltpu.VMEM((nchunk * SSM_GROUPS * D_STATE, CHUNK), BF16),
               pltpu.VMEM((s, SSM_WIDTH), F32),
               pltpu.VMEM((s, SSM_WIDTH), BF16),
               pltpu.VMEM((s, SSM_WIDTH), F32),
               pltpu.VMEM((D_STATE, SSM_WIDTH), F32),
               pltpu.VMEM((D_STATE, SSM_WIDTH), F32)]
    return pl.pallas_call(
        functools.partial(_ssd_kernel, seq=s, has_init=has_init),
        grid=(b,), in_specs=in_specs,
        out_specs=[blk(SSM_WIDTH), st_spec],
        out_shape=[jax.ShapeDtypeStruct((b, s, SSM_WIDTH), BF16),
                   jax.ShapeDtypeStruct((b, 2, SSM_WIDTH, D_STATE), F32)],
        scratch_shapes=scratch,
        compiler_params=_params(("parallel",)),
        name="ssd_latent" if has_init else "ssd_ctx",
    )(*args)


FF_CHUNK = 256


def _outffn_kernel(x_ref, attn_ref, ssm_ref, mod_ref, wout_ref, gpost_ref, gpre_ref, gpostf_ref,
                   wg_ref, wu_ref, wd_ref, o_ref):
    x = x_ref[0]
    m = mod_ref[0]
    half = MLA_HEADS * V_HEAD
    mix = _dot(attn_ref[0], wout_ref[0:half, :]) + _dot(ssm_ref[0], wout_ref[half:, :])
    y = x + m[2:3] * _rms(mix, gpost_ref[...])
    h = (_rms(y, gpre_ref[...]) * (1.0 + m[4:5]) + m[3:4]).astype(BF16)
    acc = None
    for c in range(D_FF // FF_CHUNK):
        cs = slice(c * FF_CHUNK, (c + 1) * FF_CHUNK)
        gate = _dot(h, wg_ref[:, cs])
        up = _dot(h, wu_ref[:, cs])
        part = _dot((_silu(gate) * up).astype(BF16), wd_ref[cs, :])
        acc = part if acc is None else acc + part
    o_ref[0] = y + m[5:6] * _rms(acc, gpostf_ref[...])


def _outffn(x, attn, ssm, mod, mod_off, wout, gpost, gpre, gpostf, wg, wu, wd, tm):
    b, s, d = x.shape
    tok = lambda w: pl.BlockSpec((1, tm, w), lambda i, j: (i, j, 0))
    consts = [wout, gpost, gpre, gpostf, wg, wu, wd]
    return pl.pallas_call(
        _outffn_kernel,
        grid=(b, s // tm),
        in_specs=[tok(d), tok(MLA_HEADS * V_HEAD), tok(SSM_WIDTH),
                  pl.BlockSpec((1, 6, d), lambda i, j: (i + mod_off, 0, 0))]
                 + [_const_spec(c.shape) for c in consts],
        out_specs=tok(d),
        out_shape=jax.ShapeDtypeStruct((b, s, d), F32),
        compiler_params=_params(("parallel", "parallel")),
        name="out_ffn",
    )(x, attn, ssm, mod, *consts)


def _rope_tables(length):
    quarter = QK_ROPE // 4
    pos = np.arange(length)
    inv_freq = ROPE_THETA ** (-np.arange(quarter, dtype=np.float64) / quarter)
    ang_r = (pos // GRID_W)[:, None] * inv_freq[None, :]
    ang_c = (pos % GRID_W)[:, None] * inv_freq[None, :]
    cos = np.concatenate([np.cos(ang_r)] * 2 + [np.cos(ang_c)] * 2, axis=1)
    sin = np.concatenate([-np.sin(ang_r), np.sin(ang_r), -np.sin(ang_c), np.sin(ang_c)], axis=1)
    return jnp.asarray(np.tile(cos, (1, 2)), F32), jnp.asarray(np.tile(sin, (1, 2)), F32)


def _pad_cols(w, width, left=0):
    return jnp.pad(w, ((0, 0), (left, width - left - w.shape[-1])))


def _ssd_constants():
    idx = np.arange(CHUNK)
    tril = (idx[None, :] <= idx[:, None]).astype(np.float32)
    e64 = np.zeros((LANES, N_DT * SSM_HEAD_DIM), np.float32)
    for q in range(N_DT):
        e64[DT_OFF + q, q * SSM_HEAD_DIM:(q + 1) * SSM_HEAD_DIM] = 1.0
    return [jnp.asarray(a, BF16) for a in (tril, e64)]


def kernel(x_prompt, x_sample, cache_ckv, cache_krope, state_ssm, c, c_ctx, w_mod, b_mod,
           g_pre_mix, g_post_mix, w_in, g_q, w_uq, g_kv, w_ukv, conv_w, conv_b, dt_bias,
           a_log, d_skip, g_ssm_out, w_out, g_pre_ffn, g_post_ffn, w_gate, w_up, w_down):
    depth = w_in.shape[0]
    assert depth == 1
    nb, seq, d = x_prompt.shape
    db, dseq, _ = x_sample.shape
    l = 0

    rows = 16
    cond = jnp.concatenate([c_ctx[None, :], c, jnp.zeros((rows - 1 - db, d), F32)], axis=0)
    mod = _modulation(cond, w_mod[l], b_mod[l][None, :]).reshape(rows, 6, d)

    wi = w_in[l]
    o_q, o_kv, o_kr = Q_LORA, Q_LORA + KV_LORA, Q_LORA + KV_LORA + QK_ROPE
    o_z, o_xbc = o_kr + SSM_WIDTH, o_kr + SSM_WIDTH + CONV_DIM
    kr_dt = _pad_cols(jnp.concatenate([wi[:, o_kv:o_kr], wi[:, o_xbc:]], axis=1), LANES)
    win = jnp.concatenate([wi[:, :o_q], wi[:, o_q:o_kv], wi[:, o_kr:o_z], wi[:, o_z:o_xbc], kr_dt],
                          axis=1).astype(BF16)
    wq = w_uq[l].reshape(Q_LORA, MLA_HEADS, QK_NOPE + QK_ROPE)
    wuq = jnp.concatenate([wq[:, :, :QK_NOPE].reshape(Q_LORA, MLA_HEADS * QK_NOPE),
                           wq[:, :, QK_NOPE:].reshape(Q_LORA, MLA_HEADS * QK_ROPE)], axis=1).astype(BF16)
    wukv = w_ukv[l].astype(BF16)
    wout = w_out[l].astype(BF16)
    wg, wu, wd = w_gate[l].astype(BF16), w_up[l].astype(BF16), w_down[l].astype(BF16)
    row = lambda v: v.reshape(1, -1)
    gpre, gq, gkv = row(g_pre_mix[l]), row(g_q[l]), row(g_kv[l])
    gpost, gpre_f, gpost_f = row(g_post_mix[l]), row(g_pre_ffn[l]), row(g_post_ffn[l])
    cw, cb = conv_w[l], row(conv_b[l])

    small = lambda v: _pad_cols(v.reshape(1, -1), LANES, left=DT_OFF)
    ssd_consts = [small(dt_bias[l]), small(a_log[l]),
                  row(jnp.repeat(d_skip[l], SSM_HEAD_DIM)), row(g_ssm_out[l])] + _ssd_constants()

    xp = x_prompt.reshape(1, nb * seq, d)
    q, k, v, z, act, sm, ckv, krope = _inproj(xp, mod, 0, gpre, win, gq, wuq, gkv, wukv, cw, cb,
                                              None, tm=512, seq_rows=seq)
    per_seq = lambda a: a.reshape(nb, seq, a.shape[-1])
    attn = _attention(per_seq(q), per_seq(k), per_seq(v), None, tq=seq, bs=4)
    ssm, fin = _ssd(per_seq(act), per_seq(z), per_seq(sm), None, ssd_consts)
    y_p = _outffn(xp, attn.reshape(1, nb * seq, -1), ssm.reshape(1, nb * seq, -1), mod, 0,
                  wout, gpost, gpre_f, gpost_f, wg, wu, wd, tm=512).reshape(nb, seq, d)
    new_ckv = ckv.reshape(nb, 1, seq, KV_LORA)
    new_krope = krope.reshape(nb, 1, seq, QK_ROPE)
    new_ssm = fin.reshape(nb, 1, 2, SSM_HEADS, SSM_HEAD_DIM, D_STATE)

    kr_cache = jnp.pad(cache_krope[:, l], ((0, 0), (0, 0), (0, LANES - QK_ROPE)))
    kc, vc = _ctxkv(cache_ckv[:, l], kr_cache, wukv)
    q, k, v, z, act, sm = _inproj(x_sample, mod, 1, gpre, win, gq, wuq, gkv, wukv, cw, cb,
                                  _rope_tables(dseq), tm=512, seq_rows=dseq)
    attn = _attention(q, k, v, (kc, vc), tq=512, bs=1)
    init = state_ssm[:, l].reshape(db, 2, SSM_WIDTH, D_STATE)
    ssm, _ = _ssd(act, z, sm, init, ssd_consts)
    y_s = _outffn(x_sample, attn, ssm, mod, 1, wout, gpost, gpre_f, gpost_f, wg, wu, wd, tm=512)

    return (y_p, y_s, new_ckv, new_krope, new_ssm)
```

```python
import functools

import numpy as np
import jax
import jax.numpy as jnp
from jax import lax
from jax.experimental import pallas as pl
from jax.experimental.pallas import tpu as pltpu

F32 = jnp.float32
BF16 = jnp.bfloat16

D_MODEL = 1024
GRID_W = 64
ROPE_THETA = 10000.0
NORM_EPS = 1e-6
MLA_HEADS = 4
QK_NOPE = 128
QK_ROPE = 64
V_HEAD = 128
Q_LORA = 384
KV_LORA = 256
SSM_HEADS = 8
SSM_HEAD_DIM = 64
SSM_WIDTH = SSM_HEADS * SSM_HEAD_DIM
SSM_GROUPS = 2
D_STATE = 128
D_CONV = 3
CHUNK = 128
CONV_DIM = SSM_WIDTH + 2 * SSM_GROUPS * D_STATE
D_FF = 2816

LANES = 128
SUBLANES = 8
QK_PAD = 256
GROUP_W = SSM_WIDTH // SSM_GROUPS
VMEM_LIMIT = 56 * 1024 * 1024

_SEG_Q = (0, 384)
_SEG_KV = (384, 640)
_SEG_Z = (640, 1152)
_SEG_XBC = (1152, 2176)
_SEG_KR = (2176, 2304)
DT_OFF = QK_ROPE
N_DT = 2 * SSM_HEADS


def _rms(x, g):
    return x * lax.rsqrt(jnp.mean(x * x, axis=-1, keepdims=True) + NORM_EPS) * g


def _silu(x):
    u = 0.5 * x
    return u * jnp.tanh(u) + u


def _dot(a, b):
    return jnp.dot(a, b, preferred_element_type=F32)


def _dot_nt(a, b):
    return lax.dot_general(a, b, (((1,), (1,)), ((), ())), preferred_element_type=F32)


def _split3(x):
    hi = x.astype(BF16)
    r = x - hi.astype(F32)
    mid = r.astype(BF16)
    lo = (r - mid.astype(F32)).astype(BF16)
    return hi, mid, lo


def _dot3_rhs(m, x):
    hi, mid, lo = _split3(x)
    return _dot(m, hi) + _dot(m, mid) + _dot(m, lo)


def _const_spec(shape):
    nd = len(shape)
    return pl.BlockSpec(shape, lambda *_: (0,) * nd, pipeline_mode=pl.Buffered(1))


def _params(sem):
    return pltpu.CompilerParams(dimension_semantics=sem, vmem_limit_bytes=VMEM_LIMIT)


def _mod_kernel(c_ref, w_ref, b_ref, o_ref):
    c = c_ref[...]
    s = _silu(c)
    w = w_ref[...]
    s_hi = s.astype(BF16)
    s_lo = (s - s_hi.astype(F32)).astype(BF16)
    w_hi = w.astype(BF16)
    w_lo = (w - w_hi.astype(F32)).astype(BF16)
    o_ref[...] = _dot(s_hi, w_hi) + _dot(s_lo, w_hi) + _dot(s_hi, w_lo) + b_ref[...]


def _modulation(cond, w_mod, b_mod):
    rows, d = cond.shape
    n = w_mod.shape[1]
    tn = 768
    return pl.pallas_call(
        _mod_kernel,
        grid=(n // tn,),
        in_specs=[pl.BlockSpec((rows, d), lambda j: (0, 0)),
                  pl.BlockSpec((d, tn), lambda j: (0, j)),
                  pl.BlockSpec((1, tn), lambda j: (0, j))],
        out_specs=pl.BlockSpec((rows, tn), lambda j: (0, j)),
        out_shape=jax.ShapeDtypeStruct((rows, n), F32),
        compiler_params=_params(("arbitrary",)),
        name="modulation",
    )(cond, w_mod, b_mod)


def _conv_silu(xb, prev_rows, next_rows, cw_ref, cb_ref):
    n = xb.shape[0]
    win = jnp.concatenate([prev_rows, xb, next_rows], axis=0)
    total = n + 2 * SUBLANES
    prev = pltpu.roll(win, 1, axis=0)[SUBLANES:SUBLANES + n]
    nxt = pltpu.roll(win, total - 1, axis=0)[SUBLANES:SUBLANES + n]
    conv = cb_ref[...] + prev * cw_ref[0:1, :] + xb * cw_ref[1:2, :] + nxt * cw_ref[2:3, :]
    return _silu(conv)


def _inproj_kernel(*refs, latent, seq_rows):
    if latent:
        (x_ref, xprev_ref, xnext_ref, mod_ref, gpre_ref, win_ref, gq_ref, wuq_ref, gkv_ref, wukv_ref,
         cw_ref, cb_ref, cos_ref, sin_ref,
         q_ref, k_ref, v_ref, z_ref, act_ref, small_ref) = refs
    else:
        (x_ref, mod_ref, gpre_ref, win_ref, gq_ref, wuq_ref, gkv_ref, wukv_ref, cw_ref, cb_ref,
         q_ref, k_ref, v_ref, z_ref, act_ref, small_ref, ckv_ref, krope_ref) = refs
    scale = (QK_NOPE + QK_ROPE) ** -0.5
    m = mod_ref[0]

    def pre(xv):
        return (_rms(xv, gpre_ref[...]) * (1.0 + m[1:2]) + m[0:1]).astype(BF16)

    tm = x_ref.shape[1]
    proj = _dot(pre(x_ref[0]), win_ref[...])
    seg = lambda s: proj[:, s[0]:s[1]]
    xbc = seg(_SEG_XBC)
    q_c = seg(_SEG_Q)
    kv_c = seg(_SEG_KV)
    krb = seg(_SEG_KR)
    if latent:
        j = pl.program_id(1)
        xh = jnp.concatenate([xprev_ref[0], xnext_ref[0]], axis=0)
        ph = _dot(pre(xh), win_ref[:, _SEG_XBC[0]:_SEG_XBC[1]])
        prev_rows = ph[0:SUBLANES] * jnp.where(j > 0, 1.0, 0.0)
        next_rows = ph[SUBLANES:] * jnp.where(j < pl.num_programs(1) - 1, 1.0, 0.0)
        act_ref[0] = _conv_silu(xbc, prev_rows, next_rows, cw_ref, cb_ref).astype(BF16)
    else:
        zero_rows = jnp.zeros((SUBLANES, CONV_DIM), F32)
        for s in range(tm // seq_rows):
            rs = slice(s * seq_rows, (s + 1) * seq_rows)
            act_ref[0, rs, :] = _conv_silu(xbc[rs], zero_rows, zero_rows, cw_ref, cb_ref).astype(BF16)

    small_ref[0] = krb
    ckv = _rms(kv_c, gkv_ref[...])
    qall = _dot(_rms(q_c, gq_ref[...]).astype(BF16), wuq_ref[...])
    kv = _dot(ckv.astype(BF16), wukv_ref[...])
    z_ref[0] = seg(_SEG_Z).astype(BF16)

    lane = lax.broadcasted_iota(jnp.int32, (tm, LANES), 1)
    rope_lanes = lane < QK_ROPE
    if latent:
        cos = cos_ref[...]
        sin = sin_ref[...]
        first_quarter = jnp.bitwise_and(lane, QK_ROPE // 4) == 0

        def rot(t):
            swapped = jnp.where(first_quarter, pltpu.roll(t, LANES - QK_ROPE // 4, axis=1),
                                pltpu.roll(t, QK_ROPE // 4, axis=1))
            return t * cos + swapped * sin
    else:
        rot = lambda t: t
        ckv_ref[0] = ckv
        krope_ref[0] = krb[:, :QK_ROPE]
    kr_b = jnp.where(rope_lanes, rot(krb), 0.0).astype(BF16)
    nw = MLA_HEADS * QK_NOPE
    for pr in range(MLA_HEADS // 2):
        pair = rot(qall[:, nw + pr * LANES:nw + (pr + 1) * LANES]) * scale
        for k in range(2):
            hd = 2 * pr + k
            base = hd * QK_PAD
            qr = pair if k == 0 else pltpu.roll(pair, QK_ROPE, axis=1)
            q_ref[0, :, base:base + LANES] = (qall[:, hd * LANES:(hd + 1) * LANES] * scale).astype(BF16)
            q_ref[0, :, base + LANES:base + QK_PAD] = jnp.where(rope_lanes, qr, 0.0).astype(BF16)
            k_ref[0, :, base:base + LANES] = kv[:, base:base + LANES].astype(BF16)
            k_ref[0, :, base + LANES:base + QK_PAD] = kr_b
            v_ref[0, :, hd * LANES:(hd + 1) * LANES] = kv[:, base + LANES:base + QK_PAD].astype(BF16)


def _inproj(x, mod, mod_off, gpre, win, gq, wuq, gkv, wukv, cw, cb, rope, tm, seq_rows):
    b, s, d = x.shape
    latent = rope is not None
    grid = (b, s // tm)
    tok = lambda w: pl.BlockSpec((1, tm, w), lambda i, j: (i, j, 0))
    mod_spec = pl.BlockSpec((1, 6, d), lambda i, j: (i + mod_off, 0, 0))
    consts = [gpre, win, gq, wuq, gkv, wukv, cw, cb]
    if latent:
        per_tile = tm // SUBLANES
        last = s // SUBLANES - 1
        in_specs = [tok(d),
                    pl.BlockSpec((1, SUBLANES, d), lambda i, j: (i, jnp.maximum(j * per_tile - 1, 0), 0)),
                    pl.BlockSpec((1, SUBLANES, d), lambda i, j: (i, jnp.minimum((j + 1) * per_tile, last), 0)),
                    mod_spec]
        args = [x, x, x, mod]
    else:
        in_specs = [tok(d), mod_spec]
        args = [x, mod]
    in_specs += [_const_spec(c.shape) for c in consts]
    args += consts
    out_shape = [jax.ShapeDtypeStruct((b, s, MLA_HEADS * QK_PAD), BF16),
                 jax.ShapeDtypeStruct((b, s, MLA_HEADS * QK_PAD), BF16),
                 jax.ShapeDtypeStruct((b, s, MLA_HEADS * V_HEAD), BF16),
                 jax.ShapeDtypeStruct((b, s, SSM_WIDTH), BF16),
                 jax.ShapeDtypeStruct((b, s, CONV_DIM), BF16),
                 jax.ShapeDtypeStruct((b, s, LANES), F32)]
    out_specs = [tok(MLA_HEADS * QK_PAD), tok(MLA_HEADS * QK_PAD), tok(MLA_HEADS * V_HEAD),
                 tok(SSM_WIDTH), tok(CONV_DIM), tok(LANES)]
    if latent:
        in_specs += [pl.BlockSpec((tm, LANES), lambda i, j: (j, 0))] * 2
        args += list(rope)
    else:
        out_shape += [jax.ShapeDtypeStruct((b, s, KV_LORA), F32),
                      jax.ShapeDtypeStruct((b, s, QK_ROPE), F32)]
        out_specs += [tok(KV_LORA), tok(QK_ROPE)]
    return pl.pallas_call(
        functools.partial(_inproj_kernel, latent=latent, seq_rows=seq_rows),
        grid=grid, in_specs=in_specs, out_specs=out_specs, out_shape=out_shape,
        compiler_params=_params(("parallel", "parallel")),
        name="inproj_latent" if latent else "inproj_ctx",
    )(*args)


def _ctxkv_kernel(ckv_ref, kr_ref, wukv_ref, k_ref, v_ref):
    kv = _dot(ckv_ref[0].astype(BF16), wukv_ref[...])
    kr_b = kr_ref[0].astype(BF16)
    for hd in range(MLA_HEADS):
        base = hd * QK_PAD
        k_ref[0, :, base:base + LANES] = kv[:, base:base + LANES].astype(BF16)
        k_ref[0, :, base + LANES:base + QK_PAD] = kr_b
        v_ref[0, :, hd * LANES:(hd + 1) * LANES] = kv[:, base + LANES:base + QK_PAD].astype(BF16)


def _ctxkv(ckv, kr_pad, wukv):
    b, s, _ = ckv.shape
    blk = lambda w: pl.BlockSpec((1, s, w), lambda i: (i, 0, 0))
    return pl.pallas_call(
        _ctxkv_kernel,
        grid=(b,),
        in_specs=[blk(KV_LORA), blk(LANES), _const_spec(wukv.shape)],
        out_specs=[blk(MLA_HEADS * QK_PAD), blk(MLA_HEADS * V_HEAD)],
        out_shape=[jax.ShapeDtypeStruct((b, s, MLA_HEADS * QK_PAD), BF16),
                   jax.ShapeDtypeStruct((b, s, MLA_HEADS * V_HEAD), BF16)],
        compiler_params=_params(("parallel",)),
        name="ctx_kv",
    )(ckv, kr_pad, wukv)


def _attn_kernel(*refs, latent):
    if latent:
        q_ref, k_ref, v_ref, kc_ref, vc_ref, o_ref = refs
    else:
        q_ref, k_ref, v_ref, o_ref = refs
    for sq in range(q_ref.shape[0]):
        for hd in range(MLA_HEADS):
            qs = slice(hd * QK_PAD, (hd + 1) * QK_PAD)
            vs = slice(hd * V_HEAD, (hd + 1) * V_HEAD)
            q = q_ref[sq, :, qs]
            s = _dot_nt(q, k_ref[sq, :, qs])
            mx = jnp.max(s, axis=-1, keepdims=True)
            if latent:
                sc = _dot_nt(q, kc_ref[sq, :, qs])
                mx = jnp.maximum(mx, jnp.max(sc, axis=-1, keepdims=True))
            p = jnp.exp(s - mx)
            den = jnp.sum(p, axis=-1, keepdims=True)
            acc = _dot(p.astype(BF16), v_ref[sq, :, vs])
            if latent:
                pc = jnp.exp(sc - mx)
                den = den + jnp.sum(pc, axis=-1, keepdims=True)
                acc = acc + _dot(pc.astype(BF16), vc_ref[sq, :, vs])
            o_ref[sq, :, vs] = (acc / den).astype(BF16)


def _attention(q, k, v, ctx, tq, bs):
    b, s, _ = q.shape
    latent = ctx is not None
    qw, vw = MLA_HEADS * QK_PAD, MLA_HEADS * V_HEAD
    in_specs = [pl.BlockSpec((bs, tq, qw), lambda i, j: (i, j, 0)),
                pl.BlockSpec((bs, s, qw), lambda i, j: (i, 0, 0)),
                pl.BlockSpec((bs, s, vw), lambda i, j: (i, 0, 0))]
    args = [q, k, v]
    if latent:
        sc = ctx[0].shape[1]
        in_specs += [pl.BlockSpec((bs, sc, qw), lambda i, j: (i, 0, 0)),
                     pl.BlockSpec((bs, sc, vw), lambda i, j: (i, 0, 0))]
        args += list(ctx)
    return pl.pallas_call(
        functools.partial(_attn_kernel, latent=latent),
        grid=(b // bs, s // tq), in_specs=in_specs,
        out_specs=pl.BlockSpec((bs, tq, vw), lambda i, j: (i, j, 0)),
        out_shape=jax.ShapeDtypeStruct((b, s, vw), BF16),
        compiler_params=_params(("parallel", "parallel")),
        name="attn_latent" if latent else "attn_ctx",
    )(*args)


def _ssd_kernel(*refs, seq, has_init):
    if has_init:
        (act_ref, z_ref, small_ref, init_ref, dtbias_ref, alog_ref, dskip_ref, gout_ref,
         tril_ref, e64_ref, out_ref, fin_ref, bt_s, y_s, xwb_s, eg_s, sf_s, sb_s) = refs
    else:
        (act_ref, z_ref, small_ref, dtbias_ref, alog_ref, dskip_ref, gout_ref,
         tril_ref, e64_ref, out_ref, fin_ref, bt_s, y_s, xwb_s, eg_s, sf_s, sb_s) = refs
    L = CHUNK
    nchunk = seq // L
    if has_init:
        sf_s[...] = jnp.transpose(init_ref[0, 0])
        sb_s[...] = jnp.transpose(init_ref[0, 1])
    else:
        sf_s[...] = jnp.zeros_like(sf_s)
        sb_s[...] = jnp.zeros_like(sb_s)

    col = lax.broadcasted_iota(jnp.int32, (L, LANES), 1)
    dt_cols = (col >= DT_OFF) & (col < DT_OFF + N_DT)
    fwd_cols = col < DT_OFF + SSM_HEADS
    row_t = lax.broadcasted_iota(jnp.int32, (L, L), 0)
    col_s = lax.broadcasted_iota(jnp.int32, (L, L), 1)
    lower = col_s <= row_t
    upper = col_s >= row_t
    lane_lo = col < SSM_HEAD_DIM
    a_row = -jnp.exp(alog_ref[...])
    neg_inf = jnp.float32(-jnp.inf)
    b_off = SSM_WIDTH
    c_off = SSM_WIDTH + SSM_GROUPS * D_STATE

    def fwd(j, carry):
        r0 = pl.multiple_of(j * L, L)
        xs_b = act_ref[0, pl.ds(r0, L), 0:SSM_WIDTH]
        xs = xs_b.astype(F32)

        dtc = jax.nn.softplus(small_ref[0, pl.ds(r0, L), :] + dtbias_ref[...])
        da = jnp.where(dt_cols, dtc * a_row, 0.0)
        cum = _dot3_rhs(tril_ref[...], da)
        tot = cum[L - 1:L, :]
        suf = tot - cum + da
        fg = jnp.where(fwd_cols, cum, suf)
        w_small = jnp.where(dt_cols, dtc * jnp.exp(tot - fg), 0.0)
        e_small = jnp.where(dt_cols, jnp.exp(fg), 0.0)
        w_hi = w_small.astype(BF16)
        w_lo = (w_small - w_hi.astype(F32)).astype(BF16)
        ex = _dot(jnp.concatenate([w_hi, w_lo, e_small.astype(BF16)], axis=0), e64_ref[...])
        w_x = ex[0:L] + ex[L:2 * L]
        e_x = ex[2 * L:3 * L]
        r_t = jnp.transpose(fg - jnp.log(dtc))

        wf_x = w_x[:, :SSM_WIDTH]
        wb_x = w_x[:, SSM_WIDTH:]
        ef_x = e_x[:, :SSM_WIDTH]
        eg_x = e_x[:, SSM_WIDTH:]
        eg_s[pl.ds(r0, L), :] = eg_x
        xwf_b = (xs * wf_x).astype(BF16)
        xwb_s[pl.ds(r0, L), :] = (xs * wb_x).astype(BF16)

        y_parts = []
        new_states = []
        heads_per_group = SSM_HEADS // SSM_GROUPS
        for g in range(SSM_GROUPS):
            gs = slice(g * GROUP_W, (g + 1) * GROUP_W)
            bm_g = act_ref[0, pl.ds(r0, L), b_off + g * D_STATE:b_off + (g + 1) * D_STATE]
            cm_g = act_ref[0, pl.ds(r0, L), c_off + g * D_STATE:c_off + (g + 1) * D_STATE]
            bt_g = jnp.transpose(bm_g.astype(F32)).astype(BF16)
            bt_s[pl.ds(pl.multiple_of((j * SSM_GROUPS + g) * D_STATE, D_STATE), D_STATE), :] = bt_g
            cbm = _dot_nt(cm_g, bm_g)
            for pair in range(heads_per_group // 2):
                ws = []
                for k in range(2):
                    cf = DT_OFF + g * heads_per_group + pair * 2 + k
                    seg_f = fg[:, cf:cf + 1] - r_t[cf:cf + 1, :]
                    lf = jnp.exp(jnp.where(lower, seg_f, neg_inf))
                    cg = cf + SSM_HEADS
                    seg_b = fg[:, cg:cg + 1] - r_t[cg:cg + 1, :]
                    ub = jnp.exp(jnp.where(upper, seg_b, neg_inf))
                    ws.append((cbm * (lf + ub)).astype(BF16))
                p0 = (g * heads_per_group + pair * 2) * SSM_HEAD_DIM
                xpair = xs_b[:, p0:p0 + LANES]
                zeros = jnp.zeros_like(xpair)
                rhs = jnp.concatenate([jnp.where(lane_lo, xpair, zeros),
                                       jnp.where(lane_lo, zeros, xpair)], axis=0)
                y_parts.append(_dot(jnp.concatenate(ws, axis=1), rhs))
            s_in = sf_s[:, gs]
            y_off = _dot(cm_g, s_in.astype(BF16)) * ef_x[:, gs]
            y_parts[-2] = y_parts[-2] + y_off[:, :LANES]
            y_parts[-1] = y_parts[-1] + y_off[:, LANES:]
            new_states.append(s_in * ef_x[L - 1:L, gs] + _dot(bt_g, xwf_b[:, gs]))
        for g in range(SSM_GROUPS):
            sf_s[:, g * GROUP_W:(g + 1) * GROUP_W] = new_states[g]
        for i, yp in enumerate(y_parts):
            y_s[pl.ds(r0, L), i * LANES:(i + 1) * LANES] = yp
        return carry

    lax.fori_loop(0, nchunk, fwd, 0, unroll=2)

    def bwd(jj, carry):
        j = nchunk - 1 - jj
        r0 = pl.multiple_of(j * L, L)
        xs = act_ref[0, pl.ds(r0, L), 0:SSM_WIDTH].astype(F32)
        eg_x = eg_s[pl.ds(r0, L), :]
        xwb_b = xwb_s[pl.ds(r0, L), :]
        y_off = []
        for g in range(SSM_GROUPS):
            gs = slice(g * GROUP_W, (g + 1) * GROUP_W)
            s_in = sb_s[:, gs]
            cm_g = act_ref[0, pl.ds(r0, L), c_off + g * D_STATE:c_off + (g + 1) * D_STATE]
            y_off.append(_dot(cm_g, s_in.astype(BF16)) * eg_x[:, gs])
            bt_g = bt_s[pl.ds(pl.multiple_of((j * SSM_GROUPS + g) * D_STATE, D_STATE), D_STATE), :]
            sb_s[:, gs] = s_in * eg_x[0:1, gs] + _dot(bt_g, xwb_b[:, gs])
        y = y_s[pl.ds(r0, L), :] + jnp.concatenate(y_off, axis=1) + dskip_ref[...] * xs
        y = y * _silu(z_ref[0, pl.ds(r0, L), :].astype(F32))
        out_ref[0, pl.ds(r0, L), :] = _rms(y, gout_ref[...]).astype(BF16)
        return carry

    lax.fori_loop(0, nchunk, bwd, 0, unroll=2)
    fin_ref[0, 0] = jnp.transpose(sf_s[...])
    fin_ref[0, 1] = jnp.transpose(sb_s[...])


def _ssd(act, z, small, init, consts):
    b, s, _ = act.shape
    has_init = init is not None
    nchunk = s // CHUNK
    blk = lambda w: pl.BlockSpec((1, s, w), lambda i: (i, 0, 0))
    st_spec = pl.BlockSpec((1, 2, SSM_WIDTH, D_STATE), lambda i: (i, 0, 0, 0))
    in_specs = [blk(CONV_DIM), blk(SSM_WIDTH), blk(LANES)]
    args = [act, z, small]
    if has_init:
        in_specs.append(st_spec)
        args.append(init)
    in_specs += [_const_spec(c.shape) for c in consts]
    args += list(consts)
    scratch = [pltpu.VMEM((nchunk * SSM_GROUPS * D_STATE, CHUNK), BF16),
               pltpu.VMEM((s, SSM_WIDTH), F32),
               pltpu.VMEM((s, SSM_WIDTH), BF16),
               pltpu.VMEM((s, SSM_WIDTH), F32),
               pltpu.VMEM((D_STATE, SSM_WIDTH), F32),
               pltpu.VMEM((D_STATE, SSM_WIDTH), F32)]
    return pl.pallas_call(
        functools.partial(_ssd_kernel, seq=s, has_init=has_init),
        grid=(b,), in_specs=in_specs,
        out_specs=[blk(SSM_WIDTH), st_spec],
        out_shape=[jax.ShapeDtypeStruct((b, s, SSM_WIDTH), BF16),
                   jax.ShapeDtypeStruct((b, 2, SSM_WIDTH, D_STATE), F32)],
        scratch_shapes=scratch,
        compiler_params=_params(("parallel",)),
        name="ssd_latent" if has_init else "ssd_ctx",
    )(*args)


SSD_GROUP = LANES // N_DT


def _ssd2_kernel(*refs, nseq, cps, has_init):
    if has_init:
        (act_ref, z_ref, small_ref, init_ref, dtbias_ref, alog_ref, dskip_ref, gout_ref,
         tril_ref, e64_ref, out_ref, fin_ref,
         fg_s, wsm_s, rt_s, bt_s, y_s, xwb_s, eg_s, sf_s, sb_s) = refs
    else:
        (act_ref, z_ref, small_ref, dtbias_ref, alog_ref, dskip_ref, gout_ref,
         tril_ref, e64_ref, out_ref, fin_ref,
         fg_s, wsm_s, rt_s, bt_s, y_s, xwb_s, eg_s, sf_s, sb_s) = refs
    L = CHUNK
    G = nseq * cps
    heads_per_group = SSM_HEADS // SSM_GROUPS
    b_off = SSM_WIDTH
    c_off = SSM_WIDTH + SSM_GROUPS * D_STATE

    lane = lax.broadcasted_iota(jnp.int32, (L, LANES), 1)
    packed = jnp.zeros((L, LANES), F32)
    for c in range(G):
        raw = small_ref[c // cps, (c % cps) * L:(c % cps + 1) * L, :]
        shifted = pltpu.roll(raw, (c * N_DT - DT_OFF) % LANES, axis=1)
        packed = jnp.where((lane >= c * N_DT) & (lane < (c + 1) * N_DT), shifted, packed)
    dtc = jax.nn.softplus(packed + dtbias_ref[...])
    da = dtc * (-jnp.exp(alog_ref[...]))
    cum = _dot3_rhs(tril_ref[...], da)
    tot = cum[L - 1:L, :]
    suf = tot - cum + da
    fg = jnp.where(jnp.bitwise_and(lane, SSM_HEADS) == 0, cum, suf)
    w_small = dtc * jnp.exp(tot - fg)
    e_small = jnp.exp(fg)
    rt_s[...] = jnp.transpose(fg - jnp.log(dtc))
    for c in range(G):
        back = (LANES - c * N_DT) % LANES
        unroll = lambda t: t if back == 0 else pltpu.roll(t, back, axis=1)
        fg_s[c] = unroll(fg)
        wsm_s[c, 0:L, :] = unroll(w_small).astype(BF16)
        wsm_s[c, L:2 * L, :] = unroll(e_small).astype(BF16)

    for s in range(nseq):
        if has_init:
            sf_s[s] = jnp.transpose(init_ref[s, 0])
            sb_s[s] = jnp.transpose(init_ref[s, 1])
        else:
            sf_s[s] = jnp.zeros((D_STATE, SSM_WIDTH), F32)
            sb_s[s] = jnp.zeros((D_STATE, SSM_WIDTH), F32)

    row_t = lax.broadcasted_iota(jnp.int32, (L, L), 0)
    col_s = lax.broadcasted_iota(jnp.int32, (L, L), 1)
    lower = col_s <= row_t
    upper = col_s >= row_t
    lane_lo = lane < SSM_HEAD_DIM
    neg_inf = jnp.float32(-jnp.inf)

    def locate(c):
        sq = c // cps
        r0 = pl.multiple_of((c - sq * cps) * L, L)
        return sq, r0

    def fwd(c, carry):
        sq, r0 = locate(c)
        f0 = pl.multiple_of(c * L, L)
        xs_b = act_ref[sq, pl.ds(r0, L), 0:SSM_WIDTH]
        xs = xs_b.astype(F32)
        fgc = fg_s[c]
        rtc = rt_s[pl.ds(pl.multiple_of(c * N_DT, N_DT), N_DT), :]
        ex = _dot(wsm_s[c], e64_ref[...])
        wf_x = ex[0:L, :SSM_WIDTH]
        wb_x = ex[0:L, SSM_WIDTH:]
        ef_x = ex[L:2 * L, :SSM_WIDTH]
        eg_x = ex[L:2 * L, SSM_WIDTH:]
        eg_s[pl.ds(f0, L), :] = eg_x
        xwf_b = (xs * wf_x).astype(BF16)
        xwb_s[pl.ds(f0, L), :] = (xs * wb_x).astype(BF16)

        y_parts = []
        new_states = []
        for g in range(SSM_GROUPS):
            gs = slice(g * GROUP_W, (g + 1) * GROUP_W)
            bm_g = act_ref[sq, pl.ds(r0, L), b_off + g * D_STATE:b_off + (g + 1) * D_STATE]
            cm_g = act_ref[sq, pl.ds(r0, L), c_off + g * D_STATE:c_off + (g + 1) * D_STATE]
            bt_g = jnp.transpose(bm_g.astype(F32)).astype(BF16)
            bt_s[pl.ds(pl.multiple_of((c * SSM_GROUPS + g) * D_STATE, D_STATE), D_STATE), :] = bt_g
            cbm = _dot_nt(cm_g, bm_g)
            for pair in range(heads_per_group // 2):
                ws = []
                for k in range(2):
                    cf = g * heads_per_group + pair * 2 + k
                    seg_f = fgc[:, cf:cf + 1] - rtc[cf:cf + 1, :]
                    lf = jnp.exp(jnp.where(lower, seg_f, neg_inf))
                    cg = cf + SSM_HEADS
                    seg_b = fgc[:, cg:cg + 1] - rtc[cg:cg + 1, :]
                    ub = jnp.exp(jnp.where(upper, seg_b, neg_inf))
                    ws.append((cbm * (lf + ub)).astype(BF16))
                p0 = (g * heads_per_group + pair * 2) * SSM_HEAD_DIM
                xpair = xs_b[:, p0:p0 + LANES]
                zeros = jnp.zeros_like(xpair)
                rhs = jnp.concatenate([jnp.where(lane_lo, xpair, zeros),
                                       jnp.where(lane_lo, zeros, xpair)], axis=0)
                y_parts.append(_dot(jnp.concatenate(ws, axis=1), rhs))
            s_in = sf_s[sq, :, gs]
            y_off = _dot(cm_g, s_in.astype(BF16)) * ef_x[:, gs]
            y_parts[-2] = y_parts[-2] + y_off[:, :LANES]
            y_parts[-1] = y_parts[-1] + y_off[:, LANES:]
            new_states.append(s_in * ef_x[L - 1:L, gs] + _dot(bt_g, xwf_b[:, gs]))
        for g in range(SSM_GROUPS):
            sf_s[sq, :, g * GROUP_W:(g + 1) * GROUP_W] = new_states[g]
        for i, yp in enumerate(y_parts):
            y_s[pl.ds(f0, L), i * LANES:(i + 1) * LANES] = yp
        return carry

    lax.fori_loop(0, G, fwd, 0, unroll=2)

    def bwd(cc, carry):
        c = G - 1 - cc
        sq, r0 = locate(c)
        f0 = pl.multiple_of(c * L, L)
        xs = act_ref[sq, pl.ds(r0, L), 0:SSM_WIDTH].astype(F32)
        eg_x = eg_s[pl.ds(f0, L), :]
        xwb_b = xwb_s[pl.ds(f0, L), :]
        y_off = []
        for g in range(SSM_GROUPS):
            gs = slice(g * GROUP_W, (g + 1) * GROUP_W)
            s_in = sb_s[sq, :, gs]
            cm_g = act_ref[sq, pl.ds(r0, L), c_off + g * D_STATE:c_off + (g + 1) * D_STATE]
            y_off.append(_dot(cm_g, s_in.astype(BF16)) * eg_x[:, gs])
            bt_g = bt_s[pl.ds(pl.multiple_of((c * SSM_GROUPS + g) * D_STATE, D_STATE), D_STATE), :]
            sb_s[sq, :, gs] = s_in * eg_x[0:1, gs] + _dot(bt_g, xwb_b[:, gs])
        y = y_s[pl.ds(f0, L), :] + jnp.concatenate(y_off, axis=1) + dskip_ref[...] * xs
        y = y * _silu(z_ref[sq, pl.ds(r0, L), :].astype(F32))
        out_ref[sq, pl.ds(r0, L), :] = _rms(y, gout_ref[...]).astype(BF16)
        return carry

    lax.fori_loop(0, G, bwd, 0, unroll=2)
    for s in range(nseq):
        fin_ref[s, 0] = jnp.transpose(sf_s[s])
        fin_ref[s, 1] = jnp.transpose(sb_s[s])


def _ssd2(act, z, small, init, consts):
    b, s, _ = act.shape
    has_init = init is not None
    cps = s // CHUNK
    assert SSD_GROUP % cps == 0
    nseq = SSD_GROUP // cps
    assert b % nseq == 0
    blk = lambda w: pl.BlockSpec((nseq, s, w), lambda i: (i, 0, 0))
    st_spec = pl.BlockSpec((nseq, 2, SSM_WIDTH, D_STATE), lambda i: (i, 0, 0, 0))
    in_specs = [blk(CONV_DIM), blk(SSM_WIDTH), blk(LANES)]
    args = [act, z, small]
    if has_init:
        in_specs.append(st_spec)
        args.append(init)
    in_specs += [_const_spec(c.shape) for c in consts]
    args += list(consts)
    rows = SSD_GROUP * CHUNK
    scratch = [pltpu.VMEM((SSD_GROUP, CHUNK, LANES), F32),
               pltpu.VMEM((SSD_GROUP, 2 * CHUNK, LANES), BF16),
               pltpu.VMEM((LANES, CHUNK), F32),
               pltpu.VMEM((SSD_GROUP * SSM_GROUPS * D_STATE, CHUNK), BF16),
               pltpu.VMEM((rows, SSM_WIDTH), F32),
               pltpu.VMEM((rows, SSM_WIDTH), BF16),
               pltpu.VMEM((rows, SSM_WIDTH), F32),
               pltpu.VMEM((nseq, D_STATE, SSM_WIDTH), F32),
               pltpu.VMEM((nseq, D_STATE, SSM_WIDTH), F32)]
    return pl.pallas_call(
        functools.partial(_ssd2_kernel, nseq=nseq, cps=cps, has_init=has_init),
        grid=(b // nseq,), in_specs=in_specs,
        out_specs=[blk(SSM_WIDTH), st_spec],
        out_shape=[jax.ShapeDtypeStruct((b, s, SSM_WIDTH), BF16),
                   jax.ShapeDtypeStruct((b, 2, SSM_WIDTH, D_STATE), F32)],
        scratch_shapes=scratch,
        compiler_params=_params(("parallel",)),
        name="ssd_latent" if has_init else "ssd_ctx",
    )(*args)


FF_CHUNK = 256
ROW_SPLIT = 1


def _outffn_kernel(x_ref, attn_ref, ssm_ref, mod_ref, wout_ref, gpost_ref, gpre_ref, gpostf_ref,
                   wg_ref, wu_ref, wd_ref, o_ref):
    m = mod_ref[0]
    half = MLA_HEADS * V_HEAD
    rows = x_ref.shape[1] // ROW_SPLIT
    for s in range(ROW_SPLIT):
        rs = slice(s * rows, (s + 1) * rows)
        mix = _dot(attn_ref[0, rs, :], wout_ref[0:half, :]) + _dot(ssm_ref[0, rs, :], wout_ref[half:, :])
        y = x_ref[0, rs, :] + m[2:3] * _rms(mix, gpost_ref[...])
        h = (_rms(y, gpre_ref[...]) * (1.0 + m[4:5]) + m[3:4]).astype(BF16)
        acc = None
        for c in range(D_FF // FF_CHUNK):
            cs = slice(c * FF_CHUNK, (c + 1) * FF_CHUNK)
            gate = _dot(h, wg_ref[:, cs])
            up = _dot(h, wu_ref[:, cs])
            part = _dot((_silu(gate) * up).astype(BF16), wd_ref[cs, :])
            acc = part if acc is None else acc + part
        o_ref[0, rs, :] = y + m[5:6] * _rms(acc, gpostf_ref[...])


def _outffn(x, attn, ssm, mod, mod_off, wout, gpost, gpre, gpostf, wg, wu, wd, tm):
    b, s, d = x.shape
    tok = lambda w: pl.BlockSpec((1, tm, w), lambda i, j: (i, j, 0))
    consts = [wout, gpost, gpre, gpostf, wg, wu, wd]
    return pl.pallas_call(
        _outffn_kernel,
        grid=(b, s // tm),
        in_specs=[tok(d), tok(MLA_HEADS * V_HEAD), tok(SSM_WIDTH),
                  pl.BlockSpec((1, 6, d), lambda i, j: (i + mod_off, 0, 0))]
                 + [_const_spec(c.shape) for c in consts],
        out_specs=tok(d),
        out_shape=jax.ShapeDtypeStruct((b, s, d), F32),
        compiler_params=_params(("parallel", "parallel")),
        name="out_ffn",
    )(x, attn, ssm, mod, *consts)


def _rope_tables(length):
    quarter = QK_ROPE // 4
    pos = np.arange(length)
    inv_freq = ROPE_THETA ** (-np.arange(quarter, dtype=np.float64) / quarter)
    ang_r = (pos // GRID_W)[:, None] * inv_freq[None, :]
    ang_c = (pos % GRID_W)[:, None] * inv_freq[None, :]
    cos = np.concatenate([np.cos(ang_r)] * 2 + [np.cos(ang_c)] * 2, axis=1)
    sin = np.concatenate([-np.sin(ang_r), np.sin(ang_r), -np.sin(ang_c), np.sin(ang_c)], axis=1)
    return jnp.asarray(np.tile(cos, (1, 2)), F32), jnp.asarray(np.tile(sin, (1, 2)), F32)


def _pad_cols(w, width, left=0):
    return jnp.pad(w, ((0, 0), (left, width - left - w.shape[-1])))


def _ssd_constants():
    idx = np.arange(CHUNK)
    tril = (idx[None, :] <= idx[:, None]).astype(np.float32)
    e64 = np.zeros((LANES, N_DT * SSM_HEAD_DIM), np.float32)
    for q in range(N_DT):
        e64[q, q * SSM_HEAD_DIM:(q + 1) * SSM_HEAD_DIM] = 1.0
    return [jnp.asarray(a, BF16) for a in (tril, e64)]


def kernel(x_prompt, x_sample, cache_ckv, cache_krope, state_ssm, c, c_ctx, w_mod, b_mod,
           g_pre_mix, g_post_mix, w_in, g_q, w_uq, g_kv, w_ukv, conv_w, conv_b, dt_bias,
           a_log, d_skip, g_ssm_out, w_out, g_pre_ffn, g_post_ffn, w_gate, w_up, w_down):
    depth = w_in.shape[0]
    assert depth == 1
    nb, seq, d = x_prompt.shape
    db, dseq, _ = x_sample.shape
    l = 0

    rows = 16
    cond = jnp.concatenate([c_ctx[None, :], c, jnp.zeros((rows - 1 - db, d), F32)], axis=0)
    mod = _modulation(cond, w_mod[l], b_mod[l][None, :]).reshape(rows, 6, d)

    wi = w_in[l]
    o_q, o_kv, o_kr = Q_LORA, Q_LORA + KV_LORA, Q_LORA + KV_LORA + QK_ROPE
    o_z, o_xbc = o_kr + SSM_WIDTH, o_kr + SSM_WIDTH + CONV_DIM
    kr_dt = _pad_cols(jnp.concatenate([wi[:, o_kv:o_kr], wi[:, o_xbc:]], axis=1), LANES)
    win = jnp.concatenate([wi[:, :o_q], wi[:, o_q:o_kv], wi[:, o_kr:o_z], wi[:, o_z:o_xbc], kr_dt],
                          axis=1).astype(BF16)
    wq = w_uq[l].reshape(Q_LORA, MLA_HEADS, QK_NOPE + QK_ROPE)
    wuq = jnp.concatenate([wq[:, :, :QK_NOPE].reshape(Q_LORA, MLA_HEADS * QK_NOPE),
                           wq[:, :, QK_NOPE:].reshape(Q_LORA, MLA_HEADS * QK_ROPE)], axis=1).astype(BF16)
    wukv = w_ukv[l].astype(BF16)
    wout = w_out[l].astype(BF16)
    wg, wu, wd = w_gate[l].astype(BF16), w_up[l].astype(BF16), w_down[l].astype(BF16)
    row = lambda v: v.reshape(1, -1)
    gpre, gq, gkv = row(g_pre_mix[l]), row(g_q[l]), row(g_kv[l])
    gpost, gpre_f, gpost_f = row(g_post_mix[l]), row(g_pre_ffn[l]), row(g_post_ffn[l])
    cw, cb = conv_w[l], row(conv_b[l])

    small = lambda v: jnp.tile(v.reshape(1, -1), (1, SSD_GROUP))
    ssd_consts = [small(dt_bias[l]), small(a_log[l]),
                  row(jnp.repeat(d_skip[l], SSM_HEAD_DIM)), row(g_ssm_out[l])] + _ssd_constants()

    xp = x_prompt.reshape(1, nb * seq, d)
    q, k, v, z, act, sm, ckv, krope = _inproj(xp, mod, 0, gpre, win, gq, wuq, gkv, wukv, cw, cb,
                                              None, tm=512, seq_rows=seq)
    per_seq = lambda a: a.reshape(nb, seq, a.shape[-1])
    attn = _attention(per_seq(q), per_seq(k), per_seq(v), None, tq=seq, bs=4)
    ssm, fin = _ssd2(per_seq(act), per_seq(z), per_seq(sm), None, ssd_consts)
    y_p = _outffn(xp, attn.reshape(1, nb * seq, -1), ssm.reshape(1, nb * seq, -1), mod, 0,
                  wout, gpost, gpre_f, gpost_f, wg, wu, wd, tm=512).reshape(nb, seq, d)
    new_ckv = ckv.reshape(nb, 1, seq, KV_LORA)
    new_krope = krope.reshape(nb, 1, seq, QK_ROPE)
    new_ssm = fin.reshape(nb, 1, 2, SSM_HEADS, SSM_HEAD_DIM, D_STATE)

    kr_cache = jnp.pad(cache_krope[:, l], ((0, 0), (0, 0), (0, LANES - QK_ROPE)))
    kc, vc = _ctxkv(cache_ckv[:, l], kr_cache, wukv)
    q, k, v, z, act, sm = _inproj(x_sample, mod, 1, gpre, win, gq, wuq, gkv, wukv, cw, cb,
                                  _rope_tables(dseq), tm=512, seq_rows=dseq)
    attn = _attention(q, k, v, (kc, vc), tq=512, bs=1)
    init = state_ssm[:, l].reshape(db, 2, SSM_WIDTH, D_STATE)
    ssm, _ = _ssd2(act, z, sm, init, ssd_consts)
    y_s = _outffn(x_sample, attn, ssm, mod, 1, wout, gpost, gpre_f, gpost_f, wg, wu, wd, tm=512)

    return (y_p, y_s, new_ckv, new_krope, new_ssm)
```

```python
import functools

import numpy as np
import jax
import jax.numpy as jnp
from jax import lax
from jax.experimental import pallas as pl
from jax.experimental.pallas import tpu as pltpu

F32 = jnp.float32
BF16 = jnp.bfloat16

D_MODEL = 1024
GRID_W = 64
ROPE_THETA = 10000.0
NORM_EPS = 1e-6
MLA_HEADS = 4
QK_NOPE = 128
QK_ROPE = 64
V_HEAD = 128
Q_LORA = 384
KV_LORA = 256
SSM_HEADS = 8
SSM_HEAD_DIM = 64
SSM_WIDTH = SSM_HEADS * SSM_HEAD_DIM
SSM_GROUPS = 2
D_STATE = 128
D_CONV = 3
CHUNK = 128
CONV_DIM = SSM_WIDTH + 2 * SSM_GROUPS * D_STATE
D_FF = 2816

LOG2E = 1.4426950408889634
LANES = 128
SUBLANES = 8
QK_PAD = 256
GROUP_W = SSM_WIDTH // SSM_GROUPS
VMEM_LIMIT = 56 * 1024 * 1024

_SEG_Q = (0, 384)
_SEG_KV = (384, 640)
_SEG_Z = (640, 1152)
_SEG_XBC = (1152, 2176)
_SEG_KR = (2176, 2304)
DT_OFF = QK_ROPE
N_DT = 2 * SSM_HEADS


def _rms(x, g):
    return x * lax.rsqrt(jnp.mean(x * x, axis=-1, keepdims=True) + NORM_EPS) * g


def _silu(x):
    u = 0.5 * x
    return u * jnp.tanh(u) + u


def _dot(a, b):
    return jnp.dot(a, b, preferred_element_type=F32)


def _dot_nt(a, b):
    return lax.dot_general(a, b, (((1,), (1,)), ((), ())), preferred_element_type=F32)


def _split3(x):
    hi = x.astype(BF16)
    r = x - hi.astype(F32)
    mid = r.astype(BF16)
    lo = (r - mid.astype(F32)).astype(BF16)
    return hi, mid, lo


def _dot3_rhs(m, x):
    hi, mid, lo = _split3(x)
    return _dot(m, hi) + _dot(m, mid) + _dot(m, lo)


def _const_spec(shape):
    nd = len(shape)
    return pl.BlockSpec(shape, lambda *_: (0,) * nd, pipeline_mode=pl.Buffered(1))


def _params(sem):
    return pltpu.CompilerParams(dimension_semantics=sem, vmem_limit_bytes=VMEM_LIMIT)


def _mod_kernel(c_ref, w_ref, b_ref, o_ref):
    c = c_ref[...]
    s = _silu(c)
    w = w_ref[...]
    s_hi = s.astype(BF16)
    s_lo = (s - s_hi.astype(F32)).astype(BF16)
    w_hi = w.astype(BF16)
    w_lo = (w - w_hi.astype(F32)).astype(BF16)
    o_ref[...] = _dot(s_hi, w_hi) + _dot(s_lo, w_hi) + _dot(s_hi, w_lo) + b_ref[...]


def _modulation(cond, w_mod, b_mod):
    rows, d = cond.shape
    n = w_mod.shape[1]
    tn = 768
    return pl.pallas_call(
        _mod_kernel,
        grid=(n // tn,),
        in_specs=[pl.BlockSpec((rows, d), lambda j: (0, 0)),
                  pl.BlockSpec((d, tn), lambda j: (0, j)),
                  pl.BlockSpec((1, tn), lambda j: (0, j))],
        out_specs=pl.BlockSpec((rows, tn), lambda j: (0, j)),
        out_shape=jax.ShapeDtypeStruct((rows, n), F32),
        compiler_params=_params(("arbitrary",)),
        name="modulation",
    )(cond, w_mod, b_mod)


def _conv_silu(xb, prev_rows, next_rows, cw_ref, cb_ref):
    n = xb.shape[0]
    win = jnp.concatenate([prev_rows, xb, next_rows], axis=0)
    total = n + 2 * SUBLANES
    prev = pltpu.roll(win, 1, axis=0)[SUBLANES:SUBLANES + n]
    nxt = pltpu.roll(win, total - 1, axis=0)[SUBLANES:SUBLANES + n]
    conv = cb_ref[...] + prev * cw_ref[0:1, :] + xb * cw_ref[1:2, :] + nxt * cw_ref[2:3, :]
    return _silu(conv)


def _inproj_kernel(*refs, latent, seq_rows):
    if latent:
        (x_ref, xprev_ref, xnext_ref, mod_ref, gpre_ref, win_ref, gq_ref, wuq_ref, gkv_ref, wukv_ref,
         cw_ref, cb_ref, cos_ref, sin_ref,
         q_ref, k_ref, v_ref, z_ref, act_ref, small_ref) = refs
    else:
        (x_ref, mod_ref, gpre_ref, win_ref, gq_ref, wuq_ref, gkv_ref, wukv_ref, cw_ref, cb_ref,
         q_ref, k_ref, v_ref, z_ref, act_ref, small_ref, ckv_ref, krope_ref) = refs
    scale = (QK_NOPE + QK_ROPE) ** -0.5 * LOG2E
    m = mod_ref[0]

    g_mod = gpre_ref[...] * (1.0 + m[1:2])

    def pre(xv):
        return (_rms(xv, g_mod) + m[0:1]).astype(BF16)

    tm = x_ref.shape[1]
    proj = _dot(pre(x_ref[0]), win_ref[...])
    seg = lambda s: proj[:, s[0]:s[1]]
    xbc = seg(_SEG_XBC)
    q_c = seg(_SEG_Q)
    kv_c = seg(_SEG_KV)
    krb = seg(_SEG_KR)
    if latent:
        j = pl.program_id(1)
        xh = jnp.concatenate([xprev_ref[0], xnext_ref[0]], axis=0)
        ph = _dot(pre(xh), win_ref[:, _SEG_XBC[0]:_SEG_XBC[1]])
        prev_rows = ph[0:SUBLANES] * jnp.where(j > 0, 1.0, 0.0)
        next_rows = ph[SUBLANES:] * jnp.where(j < pl.num_programs(1) - 1, 1.0, 0.0)
        act_ref[0] = _conv_silu(xbc, prev_rows, next_rows, cw_ref, cb_ref).astype(BF16)
    else:
        zero_rows = jnp.zeros((SUBLANES, CONV_DIM), F32)
        for s in range(tm // seq_rows):
            rs = slice(s * seq_rows, (s + 1) * seq_rows)
            act_ref[0, rs, :] = _conv_silu(xbc[rs], zero_rows, zero_rows, cw_ref, cb_ref).astype(BF16)

    small_ref[0] = krb
    ckv = _rms(kv_c, gkv_ref[...])
    qall = _dot(_rms(q_c, gq_ref[...]).astype(BF16), wuq_ref[...])
    kv = _dot(ckv.astype(BF16), wukv_ref[...])
    z_ref[0] = seg(_SEG_Z).astype(BF16)

    lane = lax.broadcasted_iota(jnp.int32, (tm, LANES), 1)
    rope_lanes = lane < QK_ROPE
    if latent:
        cos = cos_ref[...]
        sin = sin_ref[...]
        first_quarter = jnp.bitwise_and(lane, QK_ROPE // 4) == 0

        def rot(t):
            swapped = jnp.where(first_quarter, pltpu.roll(t, LANES - QK_ROPE // 4, axis=1),
                                pltpu.roll(t, QK_ROPE // 4, axis=1))
            return t * cos + swapped * sin
    else:
        rot = lambda t: t
        ckv_ref[0] = ckv
        krope_ref[0] = krb[:, :QK_ROPE]
    kr_b = jnp.where(rope_lanes, rot(krb), 0.0).astype(BF16)
    nw = MLA_HEADS * QK_NOPE
    for pr in range(MLA_HEADS // 2):
        pair = rot(qall[:, nw + pr * LANES:nw + (pr + 1) * LANES]) * scale
        for k in range(2):
            hd = 2 * pr + k
            base = hd * QK_PAD
            qr = pair if k == 0 else pltpu.roll(pair, QK_ROPE, axis=1)
            q_ref[0, :, base:base + LANES] = (qall[:, hd * LANES:(hd + 1) * LANES] * scale).astype(BF16)
            q_ref[0, :, base + LANES:base + QK_PAD] = jnp.where(rope_lanes, qr, 0.0).astype(BF16)
            k_ref[0, :, base:base + LANES] = kv[:, base:base + LANES].astype(BF16)
            k_ref[0, :, base + LANES:base + QK_PAD] = kr_b
            v_ref[0, :, hd * LANES:(hd + 1) * LANES] = kv[:, base + LANES:base + QK_PAD].astype(BF16)


def _inproj(x, mod, mod_off, gpre, win, gq, wuq, gkv, wukv, cw, cb, rope, tm, seq_rows):
    b, s, d = x.shape
    latent = rope is not None
    grid = (b, s // tm)
    tok = lambda w: pl.BlockSpec((1, tm, w), lambda i, j: (i, j, 0))
    mod_spec = pl.BlockSpec((1, 6, d), lambda i, j: (i + mod_off, 0, 0))
    consts = [gpre, win, gq, wuq, gkv, wukv, cw, cb]
    if latent:
        per_tile = tm // SUBLANES
        last = s // SUBLANES - 1
        in_specs = [tok(d),
                    pl.BlockSpec((1, SUBLANES, d), lambda i, j: (i, jnp.maximum(j * per_tile - 1, 0), 0)),
                    pl.BlockSpec((1, SUBLANES, d), lambda i, j: (i, jnp.minimum((j + 1) * per_tile, last), 0)),
                    mod_spec]
        args = [x, x, x, mod]
    else:
        in_specs = [tok(d), mod_spec]
        args = [x, mod]
    in_specs += [_const_spec(c.shape) for c in consts]
    args += consts
    out_shape = [jax.ShapeDtypeStruct((b, s, MLA_HEADS * QK_PAD), BF16),
                 jax.ShapeDtypeStruct((b, s, MLA_HEADS * QK_PAD), BF16),
                 jax.ShapeDtypeStruct((b, s, MLA_HEADS * V_HEAD), BF16),
                 jax.ShapeDtypeStruct((b, s, SSM_WIDTH), BF16),
                 jax.ShapeDtypeStruct((b, s, CONV_DIM), BF16),
                 jax.ShapeDtypeStruct((b, s, LANES), F32)]
    out_specs = [tok(MLA_HEADS * QK_PAD), tok(MLA_HEADS * QK_PAD), tok(MLA_HEADS * V_HEAD),
                 tok(SSM_WIDTH), tok(CONV_DIM), tok(LANES)]
    if latent:
        in_specs += [pl.BlockSpec((tm, LANES), lambda i, j: (j, 0))] * 2
        args += list(rope)
    else:
        out_shape += [jax.ShapeDtypeStruct((b, s, KV_LORA), F32),
                      jax.ShapeDtypeStruct((b, s, QK_ROPE), F32)]
        out_specs += [tok(KV_LORA), tok(QK_ROPE)]
    return pl.pallas_call(
        functools.partial(_inproj_kernel, latent=latent, seq_rows=seq_rows),
        grid=grid, in_specs=in_specs, out_specs=out_specs, out_shape=out_shape,
        compiler_params=_params(("parallel", "parallel")),
        name="inproj_latent" if latent else "inproj_ctx",
    )(*args)


def _ctxkv_kernel(ckv_ref, kr_ref, wukv_ref, k_ref, v_ref):
    kv = _dot(ckv_ref[0].astype(BF16), wukv_ref[...])
    kr_b = kr_ref[0].astype(BF16)
    for hd in range(MLA_HEADS):
        base = hd * QK_PAD
        k_ref[0, :, base:base + LANES] = kv[:, base:base + LANES].astype(BF16)
        k_ref[0, :, base + LANES:base + QK_PAD] = kr_b
        v_ref[0, :, hd * LANES:(hd + 1) * LANES] = kv[:, base + LANES:base + QK_PAD].astype(BF16)


def _ctxkv(ckv, kr_pad, wukv):
    b, s, _ = ckv.shape
    blk = lambda w: pl.BlockSpec((1, s, w), lambda i: (i, 0, 0))
    return pl.pallas_call(
        _ctxkv_kernel,
        grid=(b,),
        in_specs=[blk(KV_LORA), blk(LANES), _const_spec(wukv.shape)],
        out_specs=[blk(MLA_HEADS * QK_PAD), blk(MLA_HEADS * V_HEAD)],
        out_shape=[jax.ShapeDtypeStruct((b, s, MLA_HEADS * QK_PAD), BF16),
                   jax.ShapeDtypeStruct((b, s, MLA_HEADS * V_HEAD), BF16)],
        compiler_params=_params(("parallel",)),
        name="ctx_kv",
    )(ckv, kr_pad, wukv)


def _attn_kernel(*refs, latent):
    if latent:
        q_ref, k_ref, v_ref, kc_ref, vc_ref, o_ref = refs
    else:
        q_ref, k_ref, v_ref, o_ref = refs
    for sq in range(q_ref.shape[0]):
        for hd in range(MLA_HEADS):
            qs = slice(hd * QK_PAD, (hd + 1) * QK_PAD)
            vs = slice(hd * V_HEAD, (hd + 1) * V_HEAD)
            q = q_ref[sq, :, qs]
            s = _dot_nt(q, k_ref[sq, :, qs])
            mx = jnp.max(s, axis=-1, keepdims=True)
            if latent:
                sc = _dot_nt(q, kc_ref[sq, :, qs])
                mx = jnp.maximum(mx, jnp.max(sc, axis=-1, keepdims=True))
            p = jnp.exp2(s - mx)
            den = jnp.sum(p, axis=-1, keepdims=True)
            acc = _dot(p.astype(BF16), v_ref[sq, :, vs])
            if latent:
                pc = jnp.exp2(sc - mx)
                den = den + jnp.sum(pc, axis=-1, keepdims=True)
                acc = acc + _dot(pc.astype(BF16), vc_ref[sq, :, vs])
            o_ref[sq, :, vs] = (acc / den).astype(BF16)


def _attention(q, k, v, ctx, tq, bs):
    b, s, _ = q.shape
    latent = ctx is not None
    qw, vw = MLA_HEADS * QK_PAD, MLA_HEADS * V_HEAD
    in_specs = [pl.BlockSpec((bs, tq, qw), lambda i, j: (i, j, 0)),
                pl.BlockSpec((bs, s, qw), lambda i, j: (i, 0, 0)),
                pl.BlockSpec((bs, s, vw), lambda i, j: (i, 0, 0))]
    args = [q, k, v]
    if latent:
        sc = ctx[0].shape[1]
        in_specs += [pl.BlockSpec((bs, sc, qw), lambda i, j: (i, 0, 0)),
                     pl.BlockSpec((bs, sc, vw), lambda i, j: (i, 0, 0))]
        args += list(ctx)
    return pl.pallas_call(
        functools.partial(_attn_kernel, latent=latent),
        grid=(b // bs, s // tq), in_specs=in_specs,
        out_specs=pl.BlockSpec((bs, tq, vw), lambda i, j: (i, j, 0)),
        out_shape=jax.ShapeDtypeStruct((b, s, vw), BF16),
        compiler_params=_params(("parallel", "parallel")),
        name="attn_latent" if latent else "attn_ctx",
    )(*args)


def _ssd_kernel(*refs, seq, has_init):
    if has_init:
        (act_ref, z_ref, small_ref, init_ref, dtbias_ref, alog_ref, dskip_ref, gout_ref,
         tril_ref, e64_ref, out_ref, fin_ref, bt_s, y_s, xwb_s, eg_s, sf_s, sb_s) = refs
    else:
        (act_ref, z_ref, small_ref, dtbias_ref, alog_ref, dskip_ref, gout_ref,
         tril_ref, e64_ref, out_ref, fin_ref, bt_s, y_s, xwb_s, eg_s, sf_s, sb_s) = refs
    L = CHUNK
    nchunk = seq // L
    if has_init:
        sf_s[...] = jnp.transpose(init_ref[0, 0])
        sb_s[...] = jnp.transpose(init_ref[0, 1])
    else:
        sf_s[...] = jnp.zeros_like(sf_s)
        sb_s[...] = jnp.zeros_like(sb_s)

    col = lax.broadcasted_iota(jnp.int32, (L, LANES), 1)
    dt_cols = (col >= DT_OFF) & (col < DT_OFF + N_DT)
    fwd_cols = col < DT_OFF + SSM_HEADS
    row_t = lax.broadcasted_iota(jnp.int32, (L, L), 0)
    col_s = lax.broadcasted_iota(jnp.int32, (L, L), 1)
    lower = col_s <= row_t
    upper = col_s >= row_t
    lane_lo = col < SSM_HEAD_DIM
    a_row = -jnp.exp(alog_ref[...])
    neg_inf = jnp.float32(-jnp.inf)
    b_off = SSM_WIDTH
    c_off = SSM_WIDTH + SSM_GROUPS * D_STATE

    def fwd(j, carry):
        r0 = pl.multiple_of(j * L, L)
        xs_b = act_ref[0, pl.ds(r0, L), 0:SSM_WIDTH]
        xs = xs_b.astype(F32)

        dtc = jax.nn.softplus(small_ref[0, pl.ds(r0, L), :] + dtbias_ref[...])
        da = jnp.where(dt_cols, dtc * a_row, 0.0)
        cum = _dot3_rhs(tril_ref[...], da)
        tot = cum[L - 1:L, :]
        suf = tot - cum + da
        fg = jnp.where(fwd_cols, cum, suf)
        w_small = jnp.where(dt_cols, dtc * jnp.exp(tot - fg), 0.0)
        e_small = jnp.where(dt_cols, jnp.exp(fg), 0.0)
        w_hi = w_small.astype(BF16)
        w_lo = (w_small - w_hi.astype(F32)).astype(BF16)
        ex = _dot(jnp.concatenate([w_hi, w_lo, e_small.astype(BF16)], axis=0), e64_ref[...])
        w_x = ex[0:L] + ex[L:2 * L]
        e_x = ex[2 * L:3 * L]
        r_t = jnp.transpose(fg - jnp.log(dtc))

        wf_x = w_x[:, :SSM_WIDTH]
        wb_x = w_x[:, SSM_WIDTH:]
        ef_x = e_x[:, :SSM_WIDTH]
        eg_x = e_x[:, SSM_WIDTH:]
        eg_s[pl.ds(r0, L), :] = eg_x
        xwf_b = (xs * wf_x).astype(BF16)
        xwb_s[pl.ds(r0, L), :] = (xs * wb_x).astype(BF16)

        y_parts = []
        new_states = []
        heads_per_group = SSM_HEADS // SSM_GROUPS
        for g in range(SSM_GROUPS):
            gs = slice(g * GROUP_W, (g + 1) * GROUP_W)
            bm_g = act_ref[0, pl.ds(r0, L), b_off + g * D_STATE:b_off + (g + 1) * D_STATE]
            cm_g = act_ref[0, pl.ds(r0, L), c_off + g * D_STATE:c_off + (g + 1) * D_STATE]
            bt_g = jnp.transpose(bm_g.astype(F32)).astype(BF16)
            bt_s[pl.ds(pl.multiple_of((j * SSM_GROUPS + g) * D_STATE, D_STATE), D_STATE), :] = bt_g
            cbm = _dot_nt(cm_g, bm_g)
            for pair in range(heads_per_group // 2):
                ws = []
                for k in range(2):
                    cf = DT_OFF + g * heads_per_group + pair * 2 + k
                    seg_f = fg[:, cf:cf + 1] - r_t[cf:cf + 1, :]
                    lf = jnp.exp(jnp.where(lower, seg_f, neg_inf))
                    cg = cf + SSM_HEADS
                    seg_b = fg[:, cg:cg + 1] - r_t[cg:cg + 1, :]
                    ub = jnp.exp(jnp.where(upper, seg_b, neg_inf))
                    ws.append((cbm * (lf + ub)).astype(BF16))
                p0 = (g * heads_per_group + pair * 2) * SSM_HEAD_DIM
                xpair = xs_b[:, p0:p0 + LANES]
                zeros = jnp.zeros_like(xpair)
                rhs = jnp.concatenate([jnp.where(lane_lo, xpair, zeros),
                                       jnp.where(lane_lo, zeros, xpair)], axis=0)
                y_parts.append(_dot(jnp.concatenate(ws, axis=1), rhs))
            s_in = sf_s[:, gs]
            y_off = _dot(cm_g, s_in.astype(BF16)) * ef_x[:, gs]
            y_parts[-2] = y_parts[-2] + y_off[:, :LANES]
            y_parts[-1] = y_parts[-1] + y_off[:, LANES:]
            new_states.append(s_in * ef_x[L - 1:L, gs] + _dot(bt_g, xwf_b[:, gs]))
        for g in range(SSM_GROUPS):
            sf_s[:, g * GROUP_W:(g + 1) * GROUP_W] = new_states[g]
        for i, yp in enumerate(y_parts):
            y_s[pl.ds(r0, L), i * LANES:(i + 1) * LANES] = yp
        return carry

    lax.fori_loop(0, nchunk, fwd, 0, unroll=2)

    def bwd(jj, carry):
        j = nchunk - 1 - jj
        r0 = pl.multiple_of(j * L, L)
        xs = act_ref[0, pl.ds(r0, L), 0:SSM_WIDTH].astype(F32)
        eg_x = eg_s[pl.ds(r0, L), :]
        xwb_b = xwb_s[pl.ds(r0, L), :]
        y_off = []
        for g in range(SSM_GROUPS):
            gs = slice(g * GROUP_W, (g + 1) * GROUP_W)
            s_in = sb_s[:, gs]
            cm_g = act_ref[0, pl.ds(r0, L), c_off + g * D_STATE:c_off + (g + 1) * D_STATE]
            y_off.append(_dot(cm_g, s_in.astype(BF16)) * eg_x[:, gs])
            bt_g = bt_s[pl.ds(pl.multiple_of((j * SSM_GROUPS + g) * D_STATE, D_STATE), D_STATE), :]
            sb_s[:, gs] = s_in * eg_x[0:1, gs] + _dot(bt_g, xwb_b[:, gs])
        y = y_s[pl.ds(r0, L), :] + jnp.concatenate(y_off, axis=1) + dskip_ref[...] * xs
        y = y * _silu(z_ref[0, pl.ds(r0, L), :].astype(F32))
        out_ref[0, pl.ds(r0, L), :] = _rms(y, gout_ref[...]).astype(BF16)
        return carry

    lax.fori_loop(0, nchunk, bwd, 0, unroll=2)
    fin_ref[0, 0] = jnp.transpose(sf_s[...])
    fin_ref[0, 1] = jnp.transpose(sb_s[...])


def _ssd(act, z, small, init, consts):
    b, s, _ = act.shape
    has_init = init is not None
    nchunk = s // CHUNK
    blk = lambda w: pl.BlockSpec((1, s, w), lambda i: (i, 0, 0))
    st_spec = pl.BlockSpec((1, 2, SSM_WIDTH, D_STATE), lambda i: (i, 0, 0, 0))
    in_specs = [blk(CONV_DIM), blk(SSM_WIDTH), blk(LANES)]
    args = [act, z, small]
    if has_init:
        in_specs.append(st_spec)
        args.append(init)
    in_specs += [_const_spec(c.shape) for c in consts]
    args += list(consts)
    scratch = [pltpu.VMEM((nchunk * SSM_GROUPS * D_STATE, CHUNK), BF16),
               pltpu.VMEM((s, SSM_WIDTH), F32),
               pltpu.VMEM((s, SSM_WIDTH), BF16),
               pltpu.VMEM((s, SSM_WIDTH), F32),
               pltpu.VMEM((D_STATE, SSM_WIDTH), F32),
               pltpu.VMEM((D_STATE, SSM_WIDTH), F32)]
    return pl.pallas_call(
        functools.partial(_ssd_kernel, seq=s, has_init=has_init),
        grid=(b,), in_specs=in_specs,
        out_specs=[blk(SSM_WIDTH), st_spec],
        out_shape=[jax.ShapeDtypeStruct((b, s, SSM_WIDTH), BF16),
                   jax.ShapeDtypeStruct((b, 2, SSM_WIDTH, D_STATE), F32)],
        scratch_shapes=scratch,
        compiler_params=_params(("parallel",)),
        name="ssd_latent" if has_init else "ssd_ctx",
    )(*args)


SSD_GROUP = LANES // N_DT


def _ssd2_kernel(*refs, nseq, cps, has_init):
    if has_init:
        (act_ref, z_ref, small_ref, init_ref, dtbias_ref, alog_ref, dskip_ref, gout_ref,
         tril_ref, e64_ref, out_ref, fin_ref,
         fg_s, wsm_s, rt_s, bt_s, y_s, xwb_s, eg_s, sf_s, sb_s) = refs
    else:
        (act_ref, z_ref, small_ref, dtbias_ref, alog_ref, dskip_ref, gout_ref,
         tril_ref, e64_ref, out_ref, fin_ref,
         fg_s, wsm_s, rt_s, bt_s, y_s, xwb_s, eg_s, sf_s, sb_s) = refs
    L = CHUNK
    G = nseq * cps
    heads_per_group = SSM_HEADS // SSM_GROUPS
    b_off = SSM_WIDTH
    c_off = SSM_WIDTH + SSM_GROUPS * D_STATE

    lane = lax.broadcasted_iota(jnp.int32, (L, LANES), 1)
    packed = jnp.zeros((L, LANES), F32)
    for c in range(G):
        raw = small_ref[c // cps, (c % cps) * L:(c % cps + 1) * L, :]
        shifted = pltpu.roll(raw, (c * N_DT - DT_OFF) % LANES, axis=1)
        packed = jnp.where((lane >= c * N_DT) & (lane < (c + 1) * N_DT), shifted, packed)
    dtc = jax.nn.softplus(packed + dtbias_ref[...])
    da = dtc * (-jnp.exp(alog_ref[...]))
    cum = _dot3_rhs(tril_ref[...], da)
    tot = cum[L - 1:L, :]
    suf = tot - cum + da
    fg = jnp.where(jnp.bitwise_and(lane, SSM_HEADS) == 0, cum, suf)
    w_small = dtc * jnp.exp(tot - fg)
    e_small = jnp.exp(fg)
    rt_s[...] = jnp.transpose(fg - jnp.log(dtc))
    for c in range(G):
        back = (LANES - c * N_DT) % LANES
        unroll = lambda t: t if back == 0 else pltpu.roll(t, back, axis=1)
        fg_s[c] = unroll(fg)
        wsm_s[c, 0:L, :] = unroll(w_small).astype(BF16)
        wsm_s[c, L:2 * L, :] = unroll(e_small).astype(BF16)

    for s in range(nseq):
        if has_init:
            sf_s[s] = jnp.transpose(init_ref[s, 0])
            sb_s[s] = jnp.transpose(init_ref[s, 1])
        else:
            sf_s[s] = jnp.zeros((D_STATE, SSM_WIDTH), F32)
            sb_s[s] = jnp.zeros((D_STATE, SSM_WIDTH), F32)

    row_t = lax.broadcasted_iota(jnp.int32, (L, L), 0)
    col_s = lax.broadcasted_iota(jnp.int32, (L, L), 1)
    lower = col_s <= row_t
    upper = col_s >= row_t
    lane_lo = lane < SSM_HEAD_DIM
    neg_inf = jnp.float32(-jnp.inf)

    def locate(c):
        sq = c // cps
        r0 = pl.multiple_of((c - sq * cps) * L, L)
        return sq, r0

    def fwd(c, carry):
        sq, r0 = locate(c)
        f0 = pl.multiple_of(c * L, L)
        xs_b = act_ref[sq, pl.ds(r0, L), 0:SSM_WIDTH]
        xs = xs_b.astype(F32)
        fgc = fg_s[c]
        rtc = rt_s[pl.ds(pl.multiple_of(c * N_DT, N_DT), N_DT), :]
        ex = _dot(wsm_s[c], e64_ref[...])
        wf_x = ex[0:L, :SSM_WIDTH]
        wb_x = ex[0:L, SSM_WIDTH:]
        ef_x = ex[L:2 * L, :SSM_WIDTH]
        eg_x = ex[L:2 * L, SSM_WIDTH:]
        eg_s[pl.ds(f0, L), :] = eg_x
        xwf_b = (xs * wf_x).astype(BF16)
        xwb_s[pl.ds(f0, L), :] = (xs * wb_x).astype(BF16)

        y_parts = []
        new_states = []
        for g in range(SSM_GROUPS):
            gs = slice(g * GROUP_W, (g + 1) * GROUP_W)
            bm_g = act_ref[sq, pl.ds(r0, L), b_off + g * D_STATE:b_off + (g + 1) * D_STATE]
            cm_g = act_ref[sq, pl.ds(r0, L), c_off + g * D_STATE:c_off + (g + 1) * D_STATE]
            bt_g = jnp.transpose(bm_g.astype(F32)).astype(BF16)
            bt_s[pl.ds(pl.multiple_of((c * SSM_GROUPS + g) * D_STATE, D_STATE), D_STATE), :] = bt_g
            cbm = _dot_nt(cm_g, bm_g)
            for pair in range(heads_per_group // 2):
                ws = []
                for k in range(2):
                    cf = g * heads_per_group + pair * 2 + k
                    seg_f = fgc[:, cf:cf + 1] - rtc[cf:cf + 1, :]
                    lf = jnp.exp(jnp.where(lower, seg_f, neg_inf))
                    cg = cf + SSM_HEADS
                    seg_b = fgc[:, cg:cg + 1] - rtc[cg:cg + 1, :]
                    ub = jnp.exp(jnp.where(upper, seg_b, neg_inf))
                    ws.append((cbm * (lf + ub)).astype(BF16))
                p0 = (g * heads_per_group + pair * 2) * SSM_HEAD_DIM
                xpair = xs_b[:, p0:p0 + LANES]
                zeros = jnp.zeros_like(xpair)
                rhs = jnp.concatenate([jnp.where(lane_lo, xpair, zeros),
                                       jnp.where(lane_lo, zeros, xpair)], axis=0)
                y_parts.append(_dot(jnp.concatenate(ws, axis=1), rhs))
            s_in = sf_s[sq, :, gs]
            y_off = _dot(cm_g, s_in.astype(BF16)) * ef_x[:, gs]
            y_parts[-2] = y_parts[-2] + y_off[:, :LANES]
            y_parts[-1] = y_parts[-1] + y_off[:, LANES:]
            new_states.append(s_in * ef_x[L - 1:L, gs] + _dot(bt_g, xwf_b[:, gs]))
        for g in range(SSM_GROUPS):
            sf_s[sq, :, g * GROUP_W:(g + 1) * GROUP_W] = new_states[g]
        for i, yp in enumerate(y_parts):
            y_s[pl.ds(f0, L), i * LANES:(i + 1) * LANES] = yp
        return carry

    lax.fori_loop(0, G, fwd, 0, unroll=2)

    def bwd(cc, carry):
        c = G - 1 - cc
        sq, r0 = locate(c)
        f0 = pl.multiple_of(c * L, L)
        xs = act_ref[sq, pl.ds(r0, L), 0:SSM_WIDTH].astype(F32)
        eg_x = eg_s[pl.ds(f0, L), :]
        xwb_b = xwb_s[pl.ds(f0, L), :]
        y_off = []
        for g in range(SSM_GROUPS):
            gs = slice(g * GROUP_W, (g + 1) * GROUP_W)
            s_in = sb_s[sq, :, gs]
            cm_g = act_ref[sq, pl.ds(r0, L), c_off + g * D_STATE:c_off + (g + 1) * D_STATE]
            y_off.append(_dot(cm_g, s_in.astype(BF16)) * eg_x[:, gs])
            bt_g = bt_s[pl.ds(pl.multiple_of((c * SSM_GROUPS + g) * D_STATE, D_STATE), D_STATE), :]
            sb_s[sq, :, gs] = s_in * eg_x[0:1, gs] + _dot(bt_g, xwb_b[:, gs])
        y = y_s[pl.ds(f0, L), :] + jnp.concatenate(y_off, axis=1) + dskip_ref[...] * xs
        y = y * _silu(z_ref[sq, pl.ds(r0, L), :].astype(F32))
        out_ref[sq, pl.ds(r0, L), :] = _rms(y, gout_ref[...]).astype(BF16)
        return carry

    lax.fori_loop(0, G, bwd, 0, unroll=2)
    for s in range(nseq):
        fin_ref[s, 0] = jnp.transpose(sf_s[s])
        fin_ref[s, 1] = jnp.transpose(sb_s[s])


def _ssd2(act, z, small, init, consts):
    b, s, _ = act.shape
    has_init = init is not None
    cps = s // CHUNK
    assert SSD_GROUP % cps == 0
    nseq = SSD_GROUP // cps
    assert b % nseq == 0
    blk = lambda w: pl.BlockSpec((nseq, s, w), lambda i: (i, 0, 0))
    st_spec = pl.BlockSpec((nseq, 2, SSM_WIDTH, D_STATE), lambda i: (i, 0, 0, 0))
    in_specs = [blk(CONV_DIM), blk(SSM_WIDTH), blk(LANES)]
    args = [act, z, small]
    if has_init:
        in_specs.append(st_spec)
        args.append(init)
    in_specs += [_const_spec(c.shape) for c in consts]
    args += list(consts)
    rows = SSD_GROUP * CHUNK
    scratch = [pltpu.VMEM((SSD_GROUP, CHUNK, LANES), F32),
               pltpu.VMEM((SSD_GROUP, 2 * CHUNK, LANES), BF16),
               pltpu.VMEM((LANES, CHUNK), F32),
               pltpu.VMEM((SSD_GROUP * SSM_GROUPS * D_STATE, CHUNK), BF16),
               pltpu.VMEM((rows, SSM_WIDTH), F32),
               pltpu.VMEM((rows, SSM_WIDTH), BF16),
               pltpu.VMEM((rows, SSM_WIDTH), F32),
               pltpu.VMEM((nseq, D_STATE, SSM_WIDTH), F32),
               pltpu.VMEM((nseq, D_STATE, SSM_WIDTH), F32)]
    return pl.pallas_call(
        functools.partial(_ssd2_kernel, nseq=nseq, cps=cps, has_init=has_init),
        grid=(b // nseq,), in_specs=in_specs,
        out_specs=[blk(SSM_WIDTH), st_spec],
        out_shape=[jax.ShapeDtypeStruct((b, s, SSM_WIDTH), BF16),
                   jax.ShapeDtypeStruct((b, 2, SSM_WIDTH, D_STATE), F32)],
        scratch_shapes=scratch,
        compiler_params=_params(("parallel",)),
        name="ssd_latent" if has_init else "ssd_ctx",
    )(*args)


FF_CHUNK = 256
ROW_SPLIT = 1


def _outffn_kernel(x_ref, attn_ref, ssm_ref, mod_ref, wout_ref, gpost_ref, gpre_ref, gpostf_ref,
                   wg_ref, wu_ref, wd_ref, o_ref, mix_s, y_s, h_s):
    m = mod_ref[0]
    half = MLA_HEADS * V_HEAD
    rows = x_ref.shape[1] // 2
    nslice = 8
    srows = rows // nslice
    nchunks = D_FF // FF_CHUNK
    g1 = gpost_ref[...] * m[2:3]
    g2 = gpre_ref[...] * (1.0 + m[4:5])
    sh2 = m[3:4]
    g3 = gpostf_ref[...] * m[5:6]

    def out_proj(rs):
        return _dot(attn_ref[0, rs, :], wout_ref[0:half, :]) + _dot(ssm_ref[0, rs, :], wout_ref[half:, :])

    def pre_ffn(mix, xr):
        y = xr + _rms(mix, g1)
        return y, (_rms(y, g2) + sh2).astype(BF16)

    def ffn_chunk(h, c, acc):
        cs = slice(c * FF_CHUNK, (c + 1) * FF_CHUNK)
        part = _dot((_silu(_dot(h, wg_ref[:, cs])) * _dot(h, wu_ref[:, cs])).astype(BF16), wd_ref[cs, :])
        return part if acc is None else acc + part

    y_a, h_a = pre_ffn(out_proj(slice(0, rows)), x_ref[0, 0:rows, :])
    mix_s[...] = out_proj(slice(rows, 2 * rows))
    acc_a = None
    for c in range(nchunks):
        acc_a = ffn_chunk(h_a, c, acc_a)
        if c < nslice:
            rs = slice(c * srows, (c + 1) * srows)
            y_b, h_b = pre_ffn(mix_s[rs, :], x_ref[0, rows + c * srows:rows + (c + 1) * srows, :])
            y_s[rs, :] = y_b
            h_s[rs, :] = h_b
    h_bb = h_s[...]
    acc_b = None
    for c in range(nchunks):
        acc_b = ffn_chunk(h_bb, c, acc_b)
        if c < nslice:
            rs = slice(c * srows, (c + 1) * srows)
            o_ref[0, rs, :] = y_a[rs] + _rms(acc_a[rs], g3)
    o_ref[0, rows:2 * rows, :] = y_s[...] + _rms(acc_b, g3)


def _outffn(x, attn, ssm, mod, mod_off, wout, gpost, gpre, gpostf, wg, wu, wd, tm):
    b, s, d = x.shape
    tok = lambda w: pl.BlockSpec((1, tm, w), lambda i, j: (i, j, 0))
    consts = [wout, gpost, gpre, gpostf, wg, wu, wd]
    return pl.pallas_call(
        _outffn_kernel,
        grid=(b, s // tm),
        in_specs=[tok(d), tok(MLA_HEADS * V_HEAD), tok(SSM_WIDTH),
                  pl.BlockSpec((1, 6, d), lambda i, j: (i + mod_off, 0, 0))]
                 + [_const_spec(c.shape) for c in consts],
        out_specs=tok(d),
        out_shape=jax.ShapeDtypeStruct((b, s, d), F32),
        scratch_shapes=[pltpu.VMEM((tm // 2, d), F32),
                        pltpu.VMEM((tm // 2, d), F32),
                        pltpu.VMEM((tm // 2, d), BF16)],
        compiler_params=_params(("parallel", "parallel")),
        name="out_ffn",
    )(x, attn, ssm, mod, *consts)


def _rope_tables(length):
    quarter = QK_ROPE // 4
    pos = np.arange(length)
    inv_freq = ROPE_THETA ** (-np.arange(quarter, dtype=np.float64) / quarter)
    ang_r = (pos // GRID_W)[:, None] * inv_freq[None, :]
    ang_c = (pos % GRID_W)[:, None] * inv_freq[None, :]
    cos = np.concatenate([np.cos(ang_r)] * 2 + [np.cos(ang_c)] * 2, axis=1)
    sin = np.concatenate([-np.sin(ang_r), np.sin(ang_r), -np.sin(ang_c), np.sin(ang_c)], axis=1)
    return jnp.asarray(np.tile(cos, (1, 2)), F32), jnp.asarray(np.tile(sin, (1, 2)), F32)


def _pad_cols(w, width, left=0):
    return jnp.pad(w, ((0, 0), (left, width - left - w.shape[-1])))


def _ssd_constants():
    idx = np.arange(CHUNK)
    tril = (idx[None, :] <= idx[:, None]).astype(np.float32)
    e64 = np.zeros((LANES, N_DT * SSM_HEAD_DIM), np.float32)
    for q in range(N_DT):
        e64[q, q * SSM_HEAD_DIM:(q + 1) * SSM_HEAD_DIM] = 1.0
    return [jnp.asarray(a, BF16) for a in (tril, e64)]


def kernel(x_prompt, x_sample, cache_ckv, cache_krope, state_ssm, c, c_ctx, w_mod, b_mod,
           g_pre_mix, g_post_mix, w_in, g_q, w_uq, g_kv, w_ukv, conv_w, conv_b, dt_bias,
           a_log, d_skip, g_ssm_out, w_out, g_pre_ffn, g_post_ffn, w_gate, w_up, w_down):
    depth = w_in.shape[0]
    assert depth == 1
    nb, seq, d = x_prompt.shape
    db, dseq, _ = x_sample.shape
    l = 0

    rows = 16
    cond = jnp.concatenate([c_ctx[None, :], c, jnp.zeros((rows - 1 - db, d), F32)], axis=0)
    mod = _modulation(cond, w_mod[l], b_mod[l][None, :]).reshape(rows, 6, d)

    wi = w_in[l]
    o_q, o_kv, o_kr = Q_LORA, Q_LORA + KV_LORA, Q_LORA + KV_LORA + QK_ROPE
    o_z, o_xbc = o_kr + SSM_WIDTH, o_kr + SSM_WIDTH + CONV_DIM
    kr_dt = _pad_cols(jnp.concatenate([wi[:, o_kv:o_kr], wi[:, o_xbc:]], axis=1), LANES)
    win = jnp.concatenate([wi[:, :o_q], wi[:, o_q:o_kv], wi[:, o_kr:o_z], wi[:, o_z:o_xbc], kr_dt],
                          axis=1).astype(BF16)
    wq = w_uq[l].reshape(Q_LORA, MLA_HEADS, QK_NOPE + QK_ROPE)
    wuq = jnp.concatenate([wq[:, :, :QK_NOPE].reshape(Q_LORA, MLA_HEADS * QK_NOPE),
                           wq[:, :, QK_NOPE:].reshape(Q_LORA, MLA_HEADS * QK_ROPE)], axis=1).astype(BF16)
    wukv = w_ukv[l].astype(BF16)
    wout = w_out[l].astype(BF16)
    wg, wu, wd = w_gate[l].astype(BF16), w_up[l].astype(BF16), w_down[l].astype(BF16)
    row = lambda v: v.reshape(1, -1)
    gpre, gq, gkv = row(g_pre_mix[l]), row(g_q[l]), row(g_kv[l])
    gpost, gpre_f, gpost_f = row(g_post_mix[l]), row(g_pre_ffn[l]), row(g_post_ffn[l])
    cw, cb = conv_w[l], row(conv_b[l])

    small = lambda v: jnp.tile(v.reshape(1, -1), (1, SSD_GROUP))
    ssd_consts = [small(dt_bias[l]), small(a_log[l]),
                  row(jnp.repeat(d_skip[l], SSM_HEAD_DIM)), row(g_ssm_out[l])] + _ssd_constants()

    xp = x_prompt.reshape(1, nb * seq, d)
    q, k, v, z, act, sm, ckv, krope = _inproj(xp, mod, 0, gpre, win, gq, wuq, gkv, wukv, cw, cb,
                                              None, tm=512, seq_rows=seq)
    per_seq = lambda a: a.reshape(nb, seq, a.shape[-1])
    attn = _attention(per_seq(q), per_seq(k), per_seq(v), None, tq=seq, bs=4)
    ssm, fin = _ssd2(per_seq(act), per_seq(z), per_seq(sm), None, ssd_consts)
    y_p = _outffn(xp, attn.reshape(1, nb * seq, -1), ssm.reshape(1, nb * seq, -1), mod, 0,
                  wout, gpost, gpre_f, gpost_f, wg, wu, wd, tm=1024).reshape(nb, seq, d)
    new_ckv = ckv.reshape(nb, 1, seq, KV_LORA)
    new_krope = krope.reshape(nb, 1, seq, QK_ROPE)
    new_ssm = fin.reshape(nb, 1, 2, SSM_HEADS, SSM_HEAD_DIM, D_STATE)

    kr_cache = jnp.pad(cache_krope[:, l], ((0, 0), (0, 0), (0, LANES - QK_ROPE)))
    kc, vc = _ctxkv(cache_ckv[:, l], kr_cache, wukv)
    q, k, v, z, act, sm = _inproj(x_sample, mod, 1, gpre, win, gq, wuq, gkv, wukv, cw, cb,
                                  _rope_tables(dseq), tm=512, seq_rows=dseq)
    attn = _attention(q, k, v, (kc, vc), tq=512, bs=1)
    init = state_ssm[:, l].reshape(db, 2, SSM_WIDTH, D_STATE)
    ssm, _ = _ssd2(act, z, sm, init, ssd_consts)
    y_s = _outffn(x_sample, attn, ssm, mod, 1, wout, gpost, gpre_f, gpost_f, wg, wu, wd, tm=1024)

    return (y_p, y_s, new_ckv, new_krope, new_ssm)
```

```python
import functools

import numpy as np
import jax
import jax.numpy as jnp
from jax import lax
from jax.experimental import pallas as pl
from jax.experimental.pallas import tpu as pltpu

F32 = jnp.float32
BF16 = jnp.bfloat16

D_MODEL = 1024
GRID_W = 64
ROPE_THETA = 10000.0
NORM_EPS = 1e-6
MLA_HEADS = 4
QK_NOPE = 128
QK_ROPE = 64
V_HEAD = 128
Q_LORA = 384
KV_LORA = 256
SSM_HEADS = 8
SSM_HEAD_DIM = 64
SSM_WIDTH = SSM_HEADS * SSM_HEAD_DIM
SSM_GROUPS = 2
D_STATE = 128
D_CONV = 3
CHUNK = 128
CONV_DIM = SSM_WIDTH + 2 * SSM_GROUPS * D_STATE
D_FF = 2816

LOG2E = 1.4426950408889634
LANES = 128
SUBLANES = 8
QK_PAD = 256
GROUP_W = SSM_WIDTH // SSM_GROUPS
VMEM_LIMIT = 56 * 1024 * 1024

_SEG_Q = (0, 384)
_SEG_KV = (384, 640)
_SEG_Z = (640, 1152)
_SEG_XBC = (1152, 2176)
_SEG_KR = (2176, 2304)
DT_OFF = QK_ROPE
N_DT = 2 * SSM_HEADS


def _rms(x, g):
    return x * lax.rsqrt(jnp.mean(x * x, axis=-1, keepdims=True) + NORM_EPS) * g


def _silu(x):
    u = 0.5 * x
    return u * jnp.tanh(u) + u


def _dot(a, b):
    return jnp.dot(a, b, preferred_element_type=F32)


def _dot_nt(a, b):
    return lax.dot_general(a, b, (((1,), (1,)), ((), ())), preferred_element_type=F32)


def _split3(x):
    hi = x.astype(BF16)
    r = x - hi.astype(F32)
    mid = r.astype(BF16)
    lo = (r - mid.astype(F32)).astype(BF16)
    return hi, mid, lo


def _dot3_rhs(m, x):
    hi, mid, lo = _split3(x)
    return _dot(m, hi) + _dot(m, mid) + _dot(m, lo)


def _const_spec(shape):
    nd = len(shape)
    return pl.BlockSpec(shape, lambda *_: (0,) * nd, pipeline_mode=pl.Buffered(1))


def _params(sem):
    return pltpu.CompilerParams(dimension_semantics=sem, vmem_limit_bytes=VMEM_LIMIT)


def _mod_kernel(c_ref, w_ref, b_ref, o_ref):
    c = c_ref[...]
    s = _silu(c)
    w = w_ref[...]
    s_hi = s.astype(BF16)
    s_lo = (s - s_hi.astype(F32)).astype(BF16)
    w_hi = w.astype(BF16)
    w_lo = (w - w_hi.astype(F32)).astype(BF16)
    o_ref[...] = _dot(s_hi, w_hi) + _dot(s_lo, w_hi) + _dot(s_hi, w_lo) + b_ref[...]


def _modulation(cond, w_mod, b_mod):
    rows, d = cond.shape
    n = w_mod.shape[1]
    tn = 768
    return pl.pallas_call(
        _mod_kernel,
        grid=(n // tn,),
        in_specs=[pl.BlockSpec((rows, d), lambda j: (0, 0)),
                  pl.BlockSpec((d, tn), lambda j: (0, j)),
                  pl.BlockSpec((1, tn), lambda j: (0, j))],
        out_specs=pl.BlockSpec((rows, tn), lambda j: (0, j)),
        out_shape=jax.ShapeDtypeStruct((rows, n), F32),
        compiler_params=_params(("arbitrary",)),
        name="modulation",
    )(cond, w_mod, b_mod)


def _conv_silu(xb, prev_rows, next_rows, cw_ref, cb_ref):
    n = xb.shape[0]
    win = jnp.concatenate([prev_rows, xb, next_rows], axis=0)
    total = n + 2 * SUBLANES
    prev = pltpu.roll(win, 1, axis=0)[SUBLANES:SUBLANES + n]
    nxt = pltpu.roll(win, total - 1, axis=0)[SUBLANES:SUBLANES + n]
    conv = cb_ref[...] + prev * cw_ref[0:1, :] + xb * cw_ref[1:2, :] + nxt * cw_ref[2:3, :]
    return _silu(conv)


def _inproj_kernel(*refs, latent, seq_rows):
    if latent:
        (x_ref, mod_ref, gpre_ref, win_ref, gq_ref, wuq_ref, gkv_ref, wukv_ref, cw_ref, cb_ref,
         cos_ref, sin_ref, q_ref, k_ref, v_ref, z_ref, act_ref, small_ref) = refs
    else:
        (x_ref, mod_ref, gpre_ref, win_ref, gq_ref, wuq_ref, gkv_ref, wukv_ref, cw_ref, cb_ref,
         q_ref, k_ref, v_ref, z_ref, act_ref, small_ref, ckv_ref, krope_ref) = refs
    scale = (QK_NOPE + QK_ROPE) ** -0.5 * LOG2E
    m = mod_ref[0]

    g_mod = gpre_ref[...] * (1.0 + m[1:2])

    def pre(xv):
        return (_rms(xv, g_mod) + m[0:1]).astype(BF16)

    tm = x_ref.shape[1]
    rows = tm // 2
    halves = (slice(0, rows), slice(rows, tm))
    assert seq_rows == tm or rows % seq_rows == 0
    zero_rows = jnp.zeros((SUBLANES, CONV_DIM), F32)
    projs = [_dot(pre(x_ref[0, halves[0], :]), win_ref[...])]
    if seq_rows == tm:
        ph = _dot(pre(x_ref[0, rows - SUBLANES:rows + SUBLANES, :]), win_ref[:, _SEG_XBC[0]:_SEG_XBC[1]])
        halo = ((zero_rows, ph[SUBLANES:]), (ph[:SUBLANES], zero_rows))
    projs.append(_dot(pre(x_ref[0, halves[1], :]), win_ref[...]))

    lane = lax.broadcasted_iota(jnp.int32, (rows, LANES), 1)
    rope_lanes = lane < QK_ROPE
    first_quarter = jnp.bitwise_and(lane, QK_ROPE // 4) == 0
    nw = MLA_HEADS * QK_NOPE
    for hf, rs in enumerate(halves):
        proj = projs[hf]
        seg = lambda s: proj[:, s[0]:s[1]]
        xbc = seg(_SEG_XBC)
        if seq_rows == tm:
            act_ref[0, rs, :] = _conv_silu(xbc, halo[hf][0], halo[hf][1], cw_ref, cb_ref).astype(BF16)
        else:
            for s in range(rows // seq_rows):
                sub = slice(s * seq_rows, (s + 1) * seq_rows)
                dst = slice(rs.start + sub.start, rs.start + sub.stop)
                act_ref[0, dst, :] = _conv_silu(xbc[sub], zero_rows, zero_rows, cw_ref, cb_ref).astype(BF16)
        krb = seg(_SEG_KR)
        small_ref[0, rs, :] = krb
        ckv = _rms(seg(_SEG_KV), gkv_ref[...])
        qall = _dot(_rms(seg(_SEG_Q), gq_ref[...]).astype(BF16), wuq_ref[...])
        kv = _dot(ckv.astype(BF16), wukv_ref[...])
        z_ref[0, rs, :] = seg(_SEG_Z).astype(BF16)
        if latent:
            cos = cos_ref[rs, :]
            sin = sin_ref[rs, :]

            def rot(t):
                swapped = jnp.where(first_quarter, pltpu.roll(t, LANES - QK_ROPE // 4, axis=1),
                                    pltpu.roll(t, QK_ROPE // 4, axis=1))
                return t * cos + swapped * sin
        else:
            rot = lambda t: t
            ckv_ref[0, rs, :] = ckv
            krope_ref[0, rs, :] = krb[:, :QK_ROPE]
        kr_b = jnp.where(rope_lanes, rot(krb), 0.0).astype(BF16)
        for pr in range(MLA_HEADS // 2):
            pair = rot(qall[:, nw + pr * LANES:nw + (pr + 1) * LANES]) * scale
            for k in range(2):
                hd = 2 * pr + k
                base = hd * QK_PAD
                qr = pair if k == 0 else pltpu.roll(pair, QK_ROPE, axis=1)
                q_ref[0, rs, base:base + LANES] = (qall[:, hd * LANES:(hd + 1) * LANES] * scale).astype(BF16)
                q_ref[0, rs, base + LANES:base + QK_PAD] = jnp.where(rope_lanes, qr, 0.0).astype(BF16)
                k_ref[0, rs, base:base + LANES] = kv[:, base:base + LANES].astype(BF16)
                k_ref[0, rs, base + LANES:base + QK_PAD] = kr_b
                v_ref[0, rs, hd * LANES:(hd + 1) * LANES] = kv[:, base + LANES:base + QK_PAD].astype(BF16)


def _inproj(x, mod, mod_off, gpre, win, gq, wuq, gkv, wukv, cw, cb, rope, tm, seq_rows):
    b, s, d = x.shape
    latent = rope is not None
    grid = (b, s // tm)
    tok = lambda w: pl.BlockSpec((1, tm, w), lambda i, j: (i, j, 0))
    mod_spec = pl.BlockSpec((1, 6, d), lambda i, j: (i + mod_off, 0, 0))
    consts = [gpre, win, gq, wuq, gkv, wukv, cw, cb]
    in_specs = [tok(d), mod_spec]
    args = [x, mod]
    in_specs += [_const_spec(c.shape) for c in consts]
    args += consts
    out_shape = [jax.ShapeDtypeStruct((b, s, MLA_HEADS * QK_PAD), BF16),
                 jax.ShapeDtypeStruct((b, s, MLA_HEADS * QK_PAD), BF16),
                 jax.ShapeDtypeStruct((b, s, MLA_HEADS * V_HEAD), BF16),
                 jax.ShapeDtypeStruct((b, s, SSM_WIDTH), BF16),
                 jax.ShapeDtypeStruct((b, s, CONV_DIM), BF16),
                 jax.ShapeDtypeStruct((b, s, LANES), F32)]
    out_specs = [tok(MLA_HEADS * QK_PAD), tok(MLA_HEADS * QK_PAD), tok(MLA_HEADS * V_HEAD),
                 tok(SSM_WIDTH), tok(CONV_DIM), tok(LANES)]
    if latent:
        in_specs += [pl.BlockSpec((tm, LANES), lambda i, j: (j, 0))] * 2
        args += list(rope)
    else:
        out_shape += [jax.ShapeDtypeStruct((b, s, KV_LORA), F32),
                      jax.ShapeDtypeStruct((b, s, QK_ROPE), F32)]
        out_specs += [tok(KV_LORA), tok(QK_ROPE)]
    return pl.pallas_call(
        functools.partial(_inproj_kernel, latent=latent, seq_rows=seq_rows),
        grid=grid, in_specs=in_specs, out_specs=out_specs, out_shape=out_shape,
        compiler_params=_params(("parallel", "parallel")),
        name="inproj_latent" if latent else "inproj_ctx",
    )(*args)


def _ctxkv_kernel(ckv_ref, kr_ref, wukv_ref, k_ref, v_ref):
    kv = _dot(ckv_ref[0].astype(BF16), wukv_ref[...])
    kr_b = kr_ref[0].astype(BF16)
    for hd in range(MLA_HEADS):
        base = hd * QK_PAD
        k_ref[0, :, base:base + LANES] = kv[:, base:base + LANES].astype(BF16)
        k_ref[0, :, base + LANES:base + QK_PAD] = kr_b
        v_ref[0, :, hd * LANES:(hd + 1) * LANES] = kv[:, base + LANES:base + QK_PAD].astype(BF16)


def _ctxkv(ckv, kr_pad, wukv):
    b, s, _ = ckv.shape
    blk = lambda w: pl.BlockSpec((1, s, w), lambda i: (i, 0, 0))
    return pl.pallas_call(
        _ctxkv_kernel,
        grid=(b,),
        in_specs=[blk(KV_LORA), blk(LANES), _const_spec(wukv.shape)],
        out_specs=[blk(MLA_HEADS * QK_PAD), blk(MLA_HEADS * V_HEAD)],
        out_shape=[jax.ShapeDtypeStruct((b, s, MLA_HEADS * QK_PAD), BF16),
                   jax.ShapeDtypeStruct((b, s, MLA_HEADS * V_HEAD), BF16)],
        compiler_params=_params(("parallel",)),
        name="ctx_kv",
    )(ckv, kr_pad, wukv)


def _attn_kernel(*refs, latent):
    if latent:
        q_ref, k_ref, v_ref, kc_ref, vc_ref, o_ref = refs
    else:
        q_ref, k_ref, v_ref, o_ref = refs
    for sq in range(q_ref.shape[0]):
        for hd in range(MLA_HEADS):
            qs = slice(hd * QK_PAD, (hd + 1) * QK_PAD)
            vs = slice(hd * V_HEAD, (hd + 1) * V_HEAD)
            q = q_ref[sq, :, qs]
            s = _dot_nt(q, k_ref[sq, :, qs])
            mx = jnp.max(s, axis=-1, keepdims=True)
            if latent:
                sc = _dot_nt(q, kc_ref[sq, :, qs])
                mx = jnp.maximum(mx, jnp.max(sc, axis=-1, keepdims=True))
            p = jnp.exp2(s - mx)
            den = jnp.sum(p, axis=-1, keepdims=True)
            acc = _dot(p.astype(BF16), v_ref[sq, :, vs])
            if latent:
                pc = jnp.exp2(sc - mx)
                den = den + jnp.sum(pc, axis=-1, keepdims=True)
                acc = acc + _dot(pc.astype(BF16), vc_ref[sq, :, vs])
            o_ref[sq, :, vs] = (acc / den).astype(BF16)


def _attention(q, k, v, ctx, tq, bs):
    b, s, _ = q.shape
    latent = ctx is not None
    qw, vw = MLA_HEADS * QK_PAD, MLA_HEADS * V_HEAD
    in_specs = [pl.BlockSpec((bs, tq, qw), lambda i, j: (i, j, 0)),
                pl.BlockSpec((bs, s, qw), lambda i, j: (i, 0, 0)),
                pl.BlockSpec((bs, s, vw), lambda i, j: (i, 0, 0))]
    args = [q, k, v]
    if latent:
        sc = ctx[0].shape[1]
        in_specs += [pl.BlockSpec((bs, sc, qw), lambda i, j: (i, 0, 0)),
                     pl.BlockSpec((bs, sc, vw), lambda i, j: (i, 0, 0))]
        args += list(ctx)
    return pl.pallas_call(
        functools.partial(_attn_kernel, latent=latent),
        grid=(b // bs, s // tq), in_specs=in_specs,
        out_specs=pl.BlockSpec((bs, tq, vw), lambda i, j: (i, j, 0)),
        out_shape=jax.ShapeDtypeStruct((b, s, vw), BF16),
        compiler_params=_params(("parallel", "parallel")),
        name="attn_latent" if latent else "attn_ctx",
    )(*args)


def _ssd_kernel(*refs, seq, has_init):
    if has_init:
        (act_ref, z_ref, small_ref, init_ref, dtbias_ref, alog_ref, dskip_ref, gout_ref,
         tril_ref, e64_ref, out_ref, fin_ref, bt_s, y_s, xwb_s, eg_s, sf_s, sb_s) = refs
    else:
        (act_ref, z_ref, small_ref, dtbias_ref, alog_ref, dskip_ref, gout_ref,
         tril_ref, e64_ref, out_ref, fin_ref, bt_s, y_s, xwb_s, eg_s, sf_s, sb_s) = refs
    L = CHUNK
    nchunk = seq // L
    if has_init:
        sf_s[...] = jnp.transpose(init_ref[0, 0])
        sb_s[...] = jnp.transpose(init_ref[0, 1])
    else:
        sf_s[...] = jnp.zeros_like(sf_s)
        sb_s[...] = jnp.zeros_like(sb_s)

    col = lax.broadcasted_iota(jnp.int32, (L, LANES), 1)
    dt_cols = (col >= DT_OFF) & (col < DT_OFF + N_DT)
    fwd_cols = col < DT_OFF + SSM_HEADS
    row_t = lax.broadcasted_iota(jnp.int32, (L, L), 0)
    col_s = lax.broadcasted_iota(jnp.int32, (L, L), 1)
    lower = col_s <= row_t
    upper = col_s >= row_t
    lane_lo = col < SSM_HEAD_DIM
    a_row = -jnp.exp(alog_ref[...])
    neg_inf = jnp.float32(-jnp.inf)
    b_off = SSM_WIDTH
    c_off = SSM_WIDTH + SSM_GROUPS * D_STATE

    def fwd(j, carry):
        r0 = pl.multiple_of(j * L, L)
        xs_b = act_ref[0, pl.ds(r0, L), 0:SSM_WIDTH]
        xs = xs_b.astype(F32)

        dtc = jax.nn.softplus(small_ref[0, pl.ds(r0, L), :] + dtbias_ref[...])
        da = jnp.where(dt_cols, dtc * a_row, 0.0)
        cum = _dot3_rhs(tril_ref[...], da)
        tot = cum[L - 1:L, :]
        suf = tot - cum + da
        fg = jnp.where(fwd_cols, cum, suf)
        w_small = jnp.where(dt_cols, dtc * jnp.exp(tot - fg), 0.0)
        e_small = jnp.where(dt_cols, jnp.exp(fg), 0.0)
        w_hi = w_small.astype(BF16)
        w_lo = (w_small - w_hi.astype(F32)).astype(BF16)
        ex = _dot(jnp.concatenate([w_hi, w_lo, e_small.astype(BF16)], axis=0), e64_ref[...])
        w_x = ex[0:L] + ex[L:2 * L]
        e_x = ex[2 * L:3 * L]
        r_t = jnp.transpose(fg - jnp.log(dtc))

        wf_x = w_x[:, :SSM_WIDTH]
        wb_x = w_x[:, SSM_WIDTH:]
        ef_x = e_x[:, :SSM_WIDTH]
        eg_x = e_x[:, SSM_WIDTH:]
        eg_s[pl.ds(r0, L), :] = eg_x
        xwf_b = (xs * wf_x).astype(BF16)
        xwb_s[pl.ds(r0, L), :] = (xs * wb_x).astype(BF16)

        y_parts = []
        new_states = []
        heads_per_group = SSM_HEADS // SSM_GROUPS
        for g in range(SSM_GROUPS):
            gs = slice(g * GROUP_W, (g + 1) * GROUP_W)
            bm_g = act_ref[0, pl.ds(r0, L), b_off + g * D_STATE:b_off + (g + 1) * D_STATE]
            cm_g = act_ref[0, pl.ds(r0, L), c_off + g * D_STATE:c_off + (g + 1) * D_STATE]
            bt_g = jnp.transpose(bm_g.astype(F32)).astype(BF16)
            bt_s[pl.ds(pl.multiple_of((j * SSM_GROUPS + g) * D_STATE, D_STATE), D_STATE), :] = bt_g
            cbm = _dot_nt(cm_g, bm_g)
            for pair in range(heads_per_group // 2):
                ws = []
                for k in range(2):
                    cf = DT_OFF + g * heads_per_group + pair * 2 + k
                    seg_f = fg[:, cf:cf + 1] - r_t[cf:cf + 1, :]
                    lf = jnp.exp(jnp.where(lower, seg_f, neg_inf))
                    cg = cf + SSM_HEADS
                    seg_b = fg[:, cg:cg + 1] - r_t[cg:cg + 1, :]
                    ub = jnp.exp(jnp.where(upper, seg_b, neg_inf))
                    ws.append((cbm * (lf + ub)).astype(BF16))
                p0 = (g * heads_per_group + pair * 2) * SSM_HEAD_DIM
                xpair = xs_b[:, p0:p0 + LANES]
                zeros = jnp.zeros_like(xpair)
                rhs = jnp.concatenate([jnp.where(lane_lo, xpair, zeros),
                                       jnp.where(lane_lo, zeros, xpair)], axis=0)
                y_parts.append(_dot(jnp.concatenate(ws, axis=1), rhs))
            s_in = sf_s[:, gs]
            y_off = _dot(cm_g, s_in.astype(BF16)) * ef_x[:, gs]
            y_parts[-2] = y_parts[-2] + y_off[:, :LANES]
            y_parts[-1] = y_parts[-1] + y_off[:, LANES:]
            new_states.append(s_in * ef_x[L - 1:L, gs] + _dot(bt_g, xwf_b[:, gs]))
        for g in range(SSM_GROUPS):
            sf_s[:, g * GROUP_W:(g + 1) * GROUP_W] = new_states[g]
        for i, yp in enumerate(y_parts):
            y_s[pl.ds(r0, L), i * LANES:(i + 1) * LANES] = yp
        return carry

    lax.fori_loop(0, nchunk, fwd, 0, unroll=2)

    def bwd(jj, carry):
        j = nchunk - 1 - jj
        r0 = pl.multiple_of(j * L, L)
        xs = act_ref[0, pl.ds(r0, L), 0:SSM_WIDTH].astype(F32)
        eg_x = eg_s[pl.ds(r0, L), :]
        xwb_b = xwb_s[pl.ds(r0, L), :]
        y_off = []
        for g in range(SSM_GROUPS):
            gs = slice(g * GROUP_W, (g + 1) * GROUP_W)
            s_in = sb_s[:, gs]
            cm_g = act_ref[0, pl.ds(r0, L), c_off + g * D_STATE:c_off + (g + 1) * D_STATE]
            y_off.append(_dot(cm_g, s_in.astype(BF16)) * eg_x[:, gs])
            bt_g = bt_s[pl.ds(pl.multiple_of((j * SSM_GROUPS + g) * D_STATE, D_STATE), D_STATE), :]
            sb_s[:, gs] = s_in * eg_x[0:1, gs] + _dot(bt_g, xwb_b[:, gs])
        y = y_s[pl.ds(r0, L), :] + jnp.concatenate(y_off, axis=1) + dskip_ref[...] * xs
        y = y * _silu(z_ref[0, pl.ds(r0, L), :].astype(F32))
        out_ref[0, pl.ds(r0, L), :] = _rms(y, gout_ref[...]).astype(BF16)
        return carry

    lax.fori_loop(0, nchunk, bwd, 0, unroll=2)
    fin_ref[0, 0] = jnp.transpose(sf_s[...])
    fin_ref[0, 1] = jnp.transpose(sb_s[...])


def _ssd(act, z, small, init, consts):
    b, s, _ = act.shape
    has_init = init is not None
    nchunk = s // CHUNK
    blk = lambda w: pl.BlockSpec((1, s, w), lambda i: (i, 0, 0))
    st_spec = pl.BlockSpec((1, 2, SSM_WIDTH, D_STATE), lambda i: (i, 0, 0, 0))
    in_specs = [blk(CONV_DIM), blk(SSM_WIDTH), blk(LANES)]
    args = [act, z, small]
    if has_init:
        in_specs.append(st_spec)
        args.append(init)
    in_specs += [_const_spec(c.shape) for c in consts]
    args += list(consts)
    scratch = [pltpu.VMEM((nchunk * SSM_GROUPS * D_STATE, CHUNK), BF16),
               pltpu.VMEM((s, SSM_WIDTH), F32),
               pltpu.VMEM((s, SSM_WIDTH), BF16),
               pltpu.VMEM((s, SSM_WIDTH), F32),
               pltpu.VMEM((D_STATE, SSM_WIDTH), F32),
               pltpu.VMEM((D_STATE, SSM_WIDTH), F32)]
    return pl.pallas_call(
        functools.partial(_ssd_kernel, seq=s, has_init=has_init),
        grid=(b,), in_specs=in_specs,
        out_specs=[blk(SSM_WIDTH), st_spec],
        out_shape=[jax.ShapeDtypeStruct((b, s, SSM_WIDTH), BF16),
                   jax.ShapeDtypeStruct((b, 2, SSM_WIDTH, D_STATE), F32)],
        scratch_shapes=scratch,
        compiler_params=_params(("parallel",)),
        name="ssd_latent" if has_init else "ssd_ctx",
    )(*args)


SSD_GROUP = LANES // N_DT


def _ssd2_kernel(*refs, nseq, cps, has_init):
    if has_init:
        (act_ref, z_ref, small_ref, init_ref, dtbias_ref, alog_ref, dskip_ref, gout_ref,
         tril_ref, e64_ref, out_ref, fin_ref,
         fg_s, wsm_s, rt_s, bt_s, y_s, xwb_s, eg_s, sf_s, sb_s) = refs
    else:
        (act_ref, z_ref, small_ref, dtbias_ref, alog_ref, dskip_ref, gout_ref,
         tril_ref, e64_ref, out_ref, fin_ref,
         fg_s, wsm_s, rt_s, bt_s, y_s, xwb_s, eg_s, sf_s, sb_s) = refs
    L = CHUNK
    G = nseq * cps
    heads_per_group = SSM_HEADS // SSM_GROUPS
    b_off = SSM_WIDTH
    c_off = SSM_WIDTH + SSM_GROUPS * D_STATE

    lane = lax.broadcasted_iota(jnp.int32, (L, LANES), 1)
    packed = jnp.zeros((L, LANES), F32)
    for c in range(G):
        raw = small_ref[c // cps, (c % cps) * L:(c % cps + 1) * L, :]
        shifted = pltpu.roll(raw, (c * N_DT - DT_OFF) % LANES, axis=1)
        packed = jnp.where((lane >= c * N_DT) & (lane < (c + 1) * N_DT), shifted, packed)
    dtc = jax.nn.softplus(packed + dtbias_ref[...])
    da = dtc * (-jnp.exp(alog_ref[...]))
    cum = _dot3_rhs(tril_ref[...], da)
    tot = cum[L - 1:L, :]
    suf = tot - cum + da
    fg = jnp.where(jnp.bitwise_and(lane, SSM_HEADS) == 0, cum, suf)
    w_small = dtc * jnp.exp(tot - fg)
    e_small = jnp.exp(fg)
    rt_s[...] = jnp.transpose(fg - jnp.log(dtc))
    for c in range(G):
        back = (LANES - c * N_DT) % LANES
        unroll = lambda t: t if back == 0 else pltpu.roll(t, back, axis=1)
        fg_s[c] = unroll(fg)
        wsm_s[c, 0:L, :] = unroll(w_small).astype(BF16)
        wsm_s[c, L:2 * L, :] = unroll(e_small).astype(BF16)

    for s in range(nseq):
        if has_init:
            sf_s[s] = jnp.transpose(init_ref[s, 0])
            sb_s[s] = jnp.transpose(init_ref[s, 1])
        else:
            sf_s[s] = jnp.zeros((D_STATE, SSM_WIDTH), F32)
            sb_s[s] = jnp.zeros((D_STATE, SSM_WIDTH), F32)

    row_t = lax.broadcasted_iota(jnp.int32, (L, L), 0)
    col_s = lax.broadcasted_iota(jnp.int32, (L, L), 1)
    lower = col_s <= row_t
    upper = col_s >= row_t
    lane_lo = lane < SSM_HEAD_DIM
    neg_inf = jnp.float32(-jnp.inf)

    def locate(c):
        sq = c // cps
        r0 = pl.multiple_of((c - sq * cps) * L, L)
        return sq, r0

    def fwd(c, carry):
        sq, r0 = locate(c)
        f0 = pl.multiple_of(c * L, L)
        xs_b = act_ref[sq, pl.ds(r0, L), 0:SSM_WIDTH]
        xs = xs_b.astype(F32)
        fgc = fg_s[c]
        rtc = rt_s[pl.ds(pl.multiple_of(c * N_DT, N_DT), N_DT), :]
        ex = _dot(wsm_s[c], e64_ref[...])
        wf_x = ex[0:L, :SSM_WIDTH]
        wb_x = ex[0:L, SSM_WIDTH:]
        ef_x = ex[L:2 * L, :SSM_WIDTH]
        eg_x = ex[L:2 * L, SSM_WIDTH:]
        eg_s[pl.ds(f0, L), :] = eg_x
        xwf_b = (xs * wf_x).astype(BF16)
        xwb_s[pl.ds(f0, L), :] = (xs * wb_x).astype(BF16)

        y_parts = []
        new_states = []
        for g in range(SSM_GROUPS):
            gs = slice(g * GROUP_W, (g + 1) * GROUP_W)
            bm_g = act_ref[sq, pl.ds(r0, L), b_off + g * D_STATE:b_off + (g + 1) * D_STATE]
            cm_g = act_ref[sq, pl.ds(r0, L), c_off + g * D_STATE:c_off + (g + 1) * D_STATE]
            bt_g = jnp.transpose(bm_g.astype(F32)).astype(BF16)
            bt_s[pl.ds(pl.multiple_of((c * SSM_GROUPS + g) * D_STATE, D_STATE), D_STATE), :] = bt_g
            cbm = _dot_nt(cm_g, bm_g)
            for pair in range(heads_per_group // 2):
                ws = []
                for k in range(2):
                    cf = g * heads_per_group + pair * 2 + k
                    seg_f = fgc[:, cf:cf + 1] - rtc[cf:cf + 1, :]
                    lf = jnp.exp(jnp.where(lower, seg_f, neg_inf))
                    cg = cf + SSM_HEADS
                    seg_b = fgc[:, cg:cg + 1] - rtc[cg:cg + 1, :]
                    ub = jnp.exp(jnp.where(upper, seg_b, neg_inf))
                    ws.append((cbm * (lf + ub)).astype(BF16))
                p0 = (g * heads_per_group + pair * 2) * SSM_HEAD_DIM
                xpair = xs_b[:, p0:p0 + LANES]
                zeros = jnp.zeros_like(xpair)
                rhs = jnp.concatenate([jnp.where(lane_lo, xpair, zeros),
                                       jnp.where(lane_lo, zeros, xpair)], axis=0)
                y_parts.append(_dot(jnp.concatenate(ws, axis=1), rhs))
            s_in = sf_s[sq, :, gs]
            y_off = _dot(cm_g, s_in.astype(BF16)) * ef_x[:, gs]
            y_parts[-2] = y_parts[-2] + y_off[:, :LANES]
            y_parts[-1] = y_parts[-1] + y_off[:, LANES:]
            new_states.append(s_in * ef_x[L - 1:L, gs] + _dot(bt_g, xwf_b[:, gs]))
        for g in range(SSM_GROUPS):
            sf_s[sq, :, g * GROUP_W:(g + 1) * GROUP_W] = new_states[g]
        for i, yp in enumerate(y_parts):
            y_s[pl.ds(f0, L), i * LANES:(i + 1) * LANES] = yp
        return carry

    lax.fori_loop(0, G, fwd, 0, unroll=2)

    def bwd(cc, carry):
        c = G - 1 - cc
        sq, r0 = locate(c)
        f0 = pl.multiple_of(c * L, L)
        xs = act_ref[sq, pl.ds(r0, L), 0:SSM_WIDTH].astype(F32)
        eg_x = eg_s[pl.ds(f0, L), :]
        xwb_b = xwb_s[pl.ds(f0, L), :]
        y_off = []
        for g in range(SSM_GROUPS):
            gs = slice(g * GROUP_W, (g + 1) * GROUP_W)
            s_in = sb_s[sq, :, gs]
            cm_g = act_ref[sq, pl.ds(r0, L), c_off + g * D_STATE:c_off + (g + 1) * D_STATE]
            y_off.append(_dot(cm_g, s_in.astype(BF16)) * eg_x[:, gs])
            bt_g = bt_s[pl.ds(pl.multiple_of((c * SSM_GROUPS + g) * D_STATE, D_STATE), D_STATE), :]
            sb_s[sq, :, gs] = s_in * eg_x[0:1, gs] + _dot(bt_g, xwb_b[:, gs])
        y = y_s[pl.ds(f0, L), :] + jnp.concatenate(y_off, axis=1) + dskip_ref[...] * xs
        y = y * _silu(z_ref[sq, pl.ds(r0, L), :].astype(F32))
        out_ref[sq, pl.ds(r0, L), :] = _rms(y, gout_ref[...]).astype(BF16)
        return carry

    lax.fori_loop(0, G, bwd, 0, unroll=2)
    for s in range(nseq):
        fin_ref[s, 0] = jnp.transpose(sf_s[s])
        fin_ref[s, 1] = jnp.transpose(sb_s[s])


def _ssd2(act, z, small, init, consts):
    b, s, _ = act.shape
    has_init = init is not None
    cps = s // CHUNK
    assert SSD_GROUP % cps == 0
    nseq = SSD_GROUP // cps
    assert b % nseq == 0
    blk = lambda w: pl.BlockSpec((nseq, s, w), lambda i: (i, 0, 0))
    st_spec = pl.BlockSpec((nseq, 2, SSM_WIDTH, D_STATE), lambda i: (i, 0, 0, 0))
    in_specs = [blk(CONV_DIM), blk(SSM_WIDTH), blk(LANES)]
    args = [act, z, small]
    if has_init:
        in_specs.append(st_spec)
        args.append(init)
    in_specs += [_const_spec(c.shape) for c in consts]
    args += list(consts)
    rows = SSD_GROUP * CHUNK
    scratch = [pltpu.VMEM((SSD_GROUP, CHUNK, LANES), F32),
               pltpu.VMEM((SSD_GROUP, 2 * CHUNK, LANES), BF16),
               pltpu.VMEM((LANES, CHUNK), F32),
               pltpu.VMEM((SSD_GROUP * SSM_GROUPS * D_STATE, CHUNK), BF16),
               pltpu.VMEM((rows, SSM_WIDTH), F32),
               pltpu.VMEM((rows, SSM_WIDTH), BF16),
               pltpu.VMEM((rows, SSM_WIDTH), F32),
               pltpu.VMEM((nseq, D_STATE, SSM_WIDTH), F32),
               pltpu.VMEM((nseq, D_STATE, SSM_WIDTH), F32)]
    return pl.pallas_call(
        functools.partial(_ssd2_kernel, nseq=nseq, cps=cps, has_init=has_init),
        grid=(b // nseq,), in_specs=in_specs,
        out_specs=[blk(SSM_WIDTH), st_spec],
        out_shape=[jax.ShapeDtypeStruct((b, s, SSM_WIDTH), BF16),
                   jax.ShapeDtypeStruct((b, 2, SSM_WIDTH, D_STATE), F32)],
        scratch_shapes=scratch,
        compiler_params=_params(("parallel",)),
        name="ssd_latent" if has_init else "ssd_ctx",
    )(*args)


FF_CHUNK = 256
ROW_SPLIT = 1


def _outffn_kernel(x_ref, attn_ref, ssm_ref, mod_ref, wout_ref, gpost_ref, gpre_ref, gpostf_ref,
                   wg_ref, wu_ref, wd_ref, o_ref, mix_s, y_s, h_s):
    m = mod_ref[0]
    half = MLA_HEADS * V_HEAD
    rows = x_ref.shape[1] // 2
    nslice = 8
    srows = rows // nslice
    nchunks = D_FF // FF_CHUNK
    g1 = gpost_ref[...] * m[2:3]
    g2 = gpre_ref[...] * (1.0 + m[4:5])
    sh2 = m[3:4]
    g3 = gpostf_ref[...] * m[5:6]

    def out_proj(rs):
        return _dot(attn_ref[0, rs, :], wout_ref[0:half, :]) + _dot(ssm_ref[0, rs, :], wout_ref[half:, :])

    def pre_ffn(mix, xr):
        y = xr + _rms(mix, g1)
        return y, (_rms(y, g2) + sh2).astype(BF16)

    def ffn_chunk(h, c, acc):
        cs = slice(c * FF_CHUNK, (c + 1) * FF_CHUNK)
        part = _dot((_silu(_dot(h, wg_ref[:, cs])) * _dot(h, wu_ref[:, cs])).astype(BF16), wd_ref[cs, :])
        return part if acc is None else acc + part

    y_a, h_a = pre_ffn(out_proj(slice(0, rows)), x_ref[0, 0:rows, :])
    mix_s[...] = out_proj(slice(rows, 2 * rows))
    acc_a = None
    for c in range(nchunks):
        acc_a = ffn_chunk(h_a, c, acc_a)
        if c < nslice:
            rs = slice(c * srows, (c + 1) * srows)
            y_b, h_b = pre_ffn(mix_s[rs, :], x_ref[0, rows + c * srows:rows + (c + 1) * srows, :])
            y_s[rs, :] = y_b
            h_s[rs, :] = h_b
    h_bb = h_s[...]
    acc_b = None
    for c in range(nchunks):
        acc_b = ffn_chunk(h_bb, c, acc_b)
        if c < nslice:
            rs = slice(c * srows, (c + 1) * srows)
            o_ref[0, rs, :] = y_a[rs] + _rms(acc_a[rs], g3)
    o_ref[0, rows:2 * rows, :] = y_s[...] + _rms(acc_b, g3)


def _outffn(x, attn, ssm, mod, mod_off, wout, gpost, gpre, gpostf, wg, wu, wd, tm):
    b, s, d = x.shape
    tok = lambda w: pl.BlockSpec((1, tm, w), lambda i, j: (i, j, 0))
    consts = [wout, gpost, gpre, gpostf, wg, wu, wd]
    return pl.pallas_call(
        _outffn_kernel,
        grid=(b, s // tm),
        in_specs=[tok(d), tok(MLA_HEADS * V_HEAD), tok(SSM_WIDTH),
                  pl.BlockSpec((1, 6, d), lambda i, j: (i + mod_off, 0, 0))]
                 + [_const_spec(c.shape) for c in consts],
        out_specs=tok(d),
        out_shape=jax.ShapeDtypeStruct((b, s, d), F32),
        scratch_shapes=[pltpu.VMEM((tm // 2, d), F32),
                        pltpu.VMEM((tm // 2, d), F32),
                        pltpu.VMEM((tm // 2, d), BF16)],
        compiler_params=_params(("parallel", "parallel")),
        name="out_ffn",
    )(x, attn, ssm, mod, *consts)


def _rope_tables(length):
    quarter = QK_ROPE // 4
    pos = np.arange(length)
    inv_freq = ROPE_THETA ** (-np.arange(quarter, dtype=np.float64) / quarter)
    ang_r = (pos // GRID_W)[:, None] * inv_freq[None, :]
    ang_c = (pos % GRID_W)[:, None] * inv_freq[None, :]
    cos = np.concatenate([np.cos(ang_r)] * 2 + [np.cos(ang_c)] * 2, axis=1)
    sin = np.concatenate([-np.sin(ang_r), np.sin(ang_r), -np.sin(ang_c), np.sin(ang_c)], axis=1)
    return jnp.asarray(np.tile(cos, (1, 2)), F32), jnp.asarray(np.tile(sin, (1, 2)), F32)


def _pad_cols(w, width, left=0):
    return jnp.pad(w, ((0, 0), (left, width - left - w.shape[-1])))


def _ssd_constants():
    idx = np.arange(CHUNK)
    tril = (idx[None, :] <= idx[:, None]).astype(np.float32)
    e64 = np.zeros((LANES, N_DT * SSM_HEAD_DIM), np.float32)
    for q in range(N_DT):
        e64[q, q * SSM_HEAD_DIM:(q + 1) * SSM_HEAD_DIM] = 1.0
    return [jnp.asarray(a, BF16) for a in (tril, e64)]


def kernel(x_prompt, x_sample, cache_ckv, cache_krope, state_ssm, c, c_ctx, w_mod, b_mod,
           g_pre_mix, g_post_mix, w_in, g_q, w_uq, g_kv, w_ukv, conv_w, conv_b, dt_bias,
           a_log, d_skip, g_ssm_out, w_out, g_pre_ffn, g_post_ffn, w_gate, w_up, w_down):
    depth = w_in.shape[0]
    assert depth == 1
    nb, seq, d = x_prompt.shape
    db, dseq, _ = x_sample.shape
    l = 0

    rows = 16
    cond = jnp.concatenate([c_ctx[None, :], c, jnp.zeros((rows - 1 - db, d), F32)], axis=0)
    mod = _modulation(cond, w_mod[l], b_mod[l][None, :]).reshape(rows, 6, d)

    wi = w_in[l]
    o_q, o_kv, o_kr = Q_LORA, Q_LORA + KV_LORA, Q_LORA + KV_LORA + QK_ROPE
    o_z, o_xbc = o_kr + SSM_WIDTH, o_kr + SSM_WIDTH + CONV_DIM
    kr_dt = _pad_cols(jnp.concatenate([wi[:, o_kv:o_kr], wi[:, o_xbc:]], axis=1), LANES)
    win = jnp.concatenate([wi[:, :o_q], wi[:, o_q:o_kv], wi[:, o_kr:o_z], wi[:, o_z:o_xbc], kr_dt],
                          axis=1).astype(BF16)
    wq = w_uq[l].reshape(Q_LORA, MLA_HEADS, QK_NOPE + QK_ROPE)
    wuq = jnp.concatenate([wq[:, :, :QK_NOPE].reshape(Q_LORA, MLA_HEADS * QK_NOPE),
                           wq[:, :, QK_NOPE:].reshape(Q_LORA, MLA_HEADS * QK_ROPE)], axis=1).astype(BF16)
    wukv = w_ukv[l].astype(BF16)
    wout = w_out[l].astype(BF16)
    wg, wu, wd = w_gate[l].astype(BF16), w_up[l].astype(BF16), w_down[l].astype(BF16)
    row = lambda v: v.reshape(1, -1)
    gpre, gq, gkv = row(g_pre_mix[l]), row(g_q[l]), row(g_kv[l])
    gpost, gpre_f, gpost_f = row(g_post_mix[l]), row(g_pre_ffn[l]), row(g_post_ffn[l])
    cw, cb = conv_w[l], row(conv_b[l])

    small = lambda v: jnp.tile(v.reshape(1, -1), (1, SSD_GROUP))
    ssd_consts = [small(dt_bias[l]), small(a_log[l]),
                  row(jnp.repeat(d_skip[l], SSM_HEAD_DIM)), row(g_ssm_out[l])] + _ssd_constants()

    xp = x_prompt.reshape(1, nb * seq, d)
    q, k, v, z, act, sm, ckv, krope = _inproj(xp, mod, 0, gpre, win, gq, wuq, gkv, wukv, cw, cb,
                                              None, tm=1024, seq_rows=seq)
    per_seq = lambda a: a.reshape(nb, seq, a.shape[-1])
    attn = _attention(per_seq(q), per_seq(k), per_seq(v), None, tq=seq, bs=4)
    ssm, fin = _ssd2(per_seq(act), per_seq(z), per_seq(sm), None, ssd_consts)
    y_p = _outffn(xp, attn.reshape(1, nb * seq, -1), ssm.reshape(1, nb * seq, -1), mod, 0,
                  wout, gpost, gpre_f, gpost_f, wg, wu, wd, tm=1024).reshape(nb, seq, d)
    new_ckv = ckv.reshape(nb, 1, seq, KV_LORA)
    new_krope = krope.reshape(nb, 1, seq, QK_ROPE)
    new_ssm = fin.reshape(nb, 1, 2, SSM_HEADS, SSM_HEAD_DIM, D_STATE)

    kr_cache = jnp.pad(cache_krope[:, l], ((0, 0), (0, 0), (0, LANES - QK_ROPE)))
    kc, vc = _ctxkv(cache_ckv[:, l], kr_cache, wukv)
    q, k, v, z, act, sm = _inproj(x_sample, mod, 1, gpre, win, gq, wuq, gkv, wukv, cw, cb,
                                  _rope_tables(dseq), tm=1024, seq_rows=dseq)
    attn = _attention(q, k, v, (kc, vc), tq=1024, bs=1)
    init = state_ssm[:, l].reshape(db, 2, SSM_WIDTH, D_STATE)
    ssm, _ = _ssd2(act, z, sm, init, ssd_consts)
    y_s = _outffn(x_sample, attn, ssm, mod, 1, wout, gpost, gpre_f, gpost_f, wg, wu, wd, tm=1024)

    return (y_p, y_s, new_ckv, new_krope, new_ssm)
```

```python
import functools

import numpy as np
import jax
import jax.numpy as jnp
from jax import lax
from jax.experimental import pallas as pl
from jax.experimental.pallas import tpu as pltpu

F32 = jnp.float32
BF16 = jnp.bfloat16

D_MODEL = 1024
GRID_W = 64
ROPE_THETA = 10000.0
NORM_EPS = 1e-6
MLA_HEADS = 4
QK_NOPE = 128
QK_ROPE = 64
V_HEAD = 128
Q_LORA = 384
KV_LORA = 256
SSM_HEADS = 8
SSM_HEAD_DIM = 64
SSM_WIDTH = SSM_HEADS * SSM_HEAD_DIM
SSM_GROUPS = 2
D_STATE = 128
D_CONV = 3
CHUNK = 128
CONV_DIM = SSM_WIDTH + 2 * SSM_GROUPS * D_STATE
D_FF = 2816

LOG2E = 1.4426950408889634
LANES = 128
SUBLANES = 8
QK_PAD = 256
GROUP_W = SSM_WIDTH // SSM_GROUPS
VMEM_LIMIT = 56 * 1024 * 1024

_SEG_Q = (0, 384)
_SEG_KV = (384, 640)
_SEG_Z = (640, 1152)
_SEG_XBC = (1152, 2176)
_SEG_KR = (2176, 2304)
DT_OFF = QK_ROPE
N_DT = 2 * SSM_HEADS


def _rms(x, g):
    return x * lax.rsqrt(jnp.mean(x * x, axis=-1, keepdims=True) + NORM_EPS) * g


def _silu(x):
    u = 0.5 * x
    return u * jnp.tanh(u) + u


def _dot(a, b):
    return jnp.dot(a, b, preferred_element_type=F32)


def _dot_nt(a, b):
    return lax.dot_general(a, b, (((1,), (1,)), ((), ())), preferred_element_type=F32)


def _split3(x):
    hi = x.astype(BF16)
    r = x - hi.astype(F32)
    mid = r.astype(BF16)
    lo = (r - mid.astype(F32)).astype(BF16)
    return hi, mid, lo


def _dot3_rhs(m, x):
    hi, mid, lo = _split3(x)
    return _dot(m, hi) + _dot(m, mid) + _dot(m, lo)


def _const_spec(shape):
    nd = len(shape)
    return pl.BlockSpec(shape, lambda *_: (0,) * nd, pipeline_mode=pl.Buffered(1))


def _params(sem):
    return pltpu.CompilerParams(dimension_semantics=sem, vmem_limit_bytes=VMEM_LIMIT)


def _mod_kernel(c_ref, w_ref, b_ref, o_ref):
    c = c_ref[...]
    s = _silu(c)
    w = w_ref[...]
    s_hi = s.astype(BF16)
    s_lo = (s - s_hi.astype(F32)).astype(BF16)
    w_hi = w.astype(BF16)
    w_lo = (w - w_hi.astype(F32)).astype(BF16)
    o_ref[...] = _dot(s_hi, w_hi) + _dot(s_lo, w_hi) + _dot(s_hi, w_lo) + b_ref[...]


def _modulation(cond, w_mod, b_mod):
    rows, d = cond.shape
    n = w_mod.shape[1]
    tn = 1536
    return pl.pallas_call(
        _mod_kernel,
        grid=(n // tn,),
        in_specs=[pl.BlockSpec((rows, d), lambda j: (0, 0)),
                  pl.BlockSpec((d, tn), lambda j: (0, j)),
                  pl.BlockSpec((1, tn), lambda j: (0, j))],
        out_specs=pl.BlockSpec((rows, tn), lambda j: (0, j)),
        out_shape=jax.ShapeDtypeStruct((rows, n), F32),
        compiler_params=_params(("arbitrary",)),
        name="modulation",
    )(cond, w_mod, b_mod)


def _conv_silu(xb, prev_rows, next_rows, cw_ref, cb_ref):
    n = xb.shape[0]
    win = jnp.concatenate([prev_rows, xb, next_rows], axis=0)
    total = n + 2 * SUBLANES
    prev = pltpu.roll(win, 1, axis=0)[SUBLANES:SUBLANES + n]
    nxt = pltpu.roll(win, total - 1, axis=0)[SUBLANES:SUBLANES + n]
    conv = cb_ref[...] + prev * cw_ref[0:1, :] + xb * cw_ref[1:2, :] + nxt * cw_ref[2:3, :]
    return _silu(conv)


def _inproj_kernel(*refs, latent, seq_rows):
    if latent:
        (x_ref, mod_ref, gpre_ref, win_ref, gq_ref, wuq_ref, gkv_ref, wukv_ref, cw_ref, cb_ref,
         cos_ref, sin_ref, q_ref, k_ref, v_ref, z_ref, act_ref, small_ref) = refs
    else:
        (x_ref, mod_ref, gpre_ref, win_ref, gq_ref, wuq_ref, gkv_ref, wukv_ref, cw_ref, cb_ref,
         q_ref, k_ref, v_ref, z_ref, act_ref, small_ref, ckv_ref, krope_ref) = refs
    scale = (QK_NOPE + QK_ROPE) ** -0.5 * LOG2E
    m = mod_ref[0]

    g_mod = gpre_ref[...] * (1.0 + m[1:2])

    def pre(xv):
        return (_rms(xv, g_mod) + m[0:1]).astype(BF16)

    tm = x_ref.shape[1]
    rows = tm // 2
    halves = (slice(0, rows), slice(rows, tm))
    assert seq_rows == tm or rows % seq_rows == 0
    zero_rows = jnp.zeros((SUBLANES, CONV_DIM), F32)
    projs = [_dot(pre(x_ref[0, halves[0], :]), win_ref[...])]
    if seq_rows == tm:
        ph = _dot(pre(x_ref[0, rows - SUBLANES:rows + SUBLANES, :]), win_ref[:, _SEG_XBC[0]:_SEG_XBC[1]])
        halo = ((zero_rows, ph[SUBLANES:]), (ph[:SUBLANES], zero_rows))
    projs.append(_dot(pre(x_ref[0, halves[1], :]), win_ref[...]))

    lane = lax.broadcasted_iota(jnp.int32, (rows, LANES), 1)
    rope_lanes = lane < QK_ROPE
    first_quarter = jnp.bitwise_and(lane, QK_ROPE // 4) == 0
    nw = MLA_HEADS * QK_NOPE
    for hf, rs in enumerate(halves):
        proj = projs[hf]
        seg = lambda s: proj[:, s[0]:s[1]]
        xbc = seg(_SEG_XBC)
        if seq_rows == tm:
            act_ref[0, rs, :] = _conv_silu(xbc, halo[hf][0], halo[hf][1], cw_ref, cb_ref).astype(BF16)
        else:
            for s in range(rows // seq_rows):
                sub = slice(s * seq_rows, (s + 1) * seq_rows)
                dst = slice(rs.start + sub.start, rs.start + sub.stop)
                act_ref[0, dst, :] = _conv_silu(xbc[sub], zero_rows, zero_rows, cw_ref, cb_ref).astype(BF16)
        krb = seg(_SEG_KR)
        small_ref[0, rs, :] = krb
        ckv = _rms(seg(_SEG_KV), gkv_ref[...])
        qall = _dot(_rms(seg(_SEG_Q), gq_ref[...]).astype(BF16), wuq_ref[...])
        kv = _dot(ckv.astype(BF16), wukv_ref[...])
        z_ref[0, rs, :] = seg(_SEG_Z).astype(BF16)
        if latent:
            cos = cos_ref[rs, :]
            sin = sin_ref[rs, :]

            def rot(t):
                swapped = jnp.where(first_quarter, pltpu.roll(t, LANES - QK_ROPE // 4, axis=1),
                                    pltpu.roll(t, QK_ROPE // 4, axis=1))
                return t * cos + swapped * sin
        else:
            rot = lambda t: t
            ckv_ref[0, rs, :] = ckv
            krope_ref[0, rs, :] = krb[:, :QK_ROPE]
        kr_b = jnp.where(rope_lanes, rot(krb), 0.0).astype(BF16)
        for pr in range(MLA_HEADS // 2):
            pair = rot(qall[:, nw + pr * LANES:nw + (pr + 1) * LANES]) * scale
            for k in range(2):
                hd = 2 * pr + k
                base = hd * QK_PAD
                qr = pair if k == 0 else pltpu.roll(pair, QK_ROPE, axis=1)
                q_ref[0, rs, base:base + LANES] = (qall[:, hd * LANES:(hd + 1) * LANES] * scale).astype(BF16)
                q_ref[0, rs, base + LANES:base + QK_PAD] = jnp.where(rope_lanes, qr, 0.0).astype(BF16)
                k_ref[0, rs, base:base + LANES] = kv[:, base:base + LANES].astype(BF16)
                k_ref[0, rs, base + LANES:base + QK_PAD] = kr_b
                v_ref[0, rs, hd * LANES:(hd + 1) * LANES] = kv[:, base + LANES:base + QK_PAD].astype(BF16)


def _inproj(x, mod, mod_off, gpre, win, gq, wuq, gkv, wukv, cw, cb, rope, tm, seq_rows):
    b, s, d = x.shape
    latent = rope is not None
    grid = (b, s // tm)
    tok = lambda w: pl.BlockSpec((1, tm, w), lambda i, j: (i, j, 0))
    mod_spec = pl.BlockSpec((1, 6, d), lambda i, j: (i + mod_off, 0, 0))
    consts = [gpre, win, gq, wuq, gkv, wukv, cw, cb]
    in_specs = [tok(d), mod_spec]
    args = [x, mod]
    in_specs += [_const_spec(c.shape) for c in consts]
    args += consts
    out_shape = [jax.ShapeDtypeStruct((b, s, MLA_HEADS * QK_PAD), BF16),
                 jax.ShapeDtypeStruct((b, s, MLA_HEADS * QK_PAD), BF16),
                 jax.ShapeDtypeStruct((b, s, MLA_HEADS * V_HEAD), BF16),
                 jax.ShapeDtypeStruct((b, s, SSM_WIDTH), BF16),
                 jax.ShapeDtypeStruct((b, s, CONV_DIM), BF16),
                 jax.ShapeDtypeStruct((b, s, LANES), F32)]
    out_specs = [tok(MLA_HEADS * QK_PAD), tok(MLA_HEADS * QK_PAD), tok(MLA_HEADS * V_HEAD),
                 tok(SSM_WIDTH), tok(CONV_DIM), tok(LANES)]
    if latent:
        in_specs += [pl.BlockSpec((tm, LANES), lambda i, j: (j, 0))] * 2
        args += list(rope)
    else:
        out_shape += [jax.ShapeDtypeStruct((b, s, KV_LORA), F32),
                      jax.ShapeDtypeStruct((b, s, QK_ROPE), F32)]
        out_specs += [tok(KV_LORA), tok(QK_ROPE)]
    return pl.pallas_call(
        functools.partial(_inproj_kernel, latent=latent, seq_rows=seq_rows),
        grid=grid, in_specs=in_specs, out_specs=out_specs, out_shape=out_shape,
        compiler_params=_params(("parallel", "parallel")),
        name="inproj_latent" if latent else "inproj_ctx",
    )(*args)


def _ctxkv_kernel(ckv_ref, kr_ref, wukv_ref, k_ref, v_ref):
    kv = _dot(ckv_ref[0].astype(BF16), wukv_ref[...])
    kr_b = kr_ref[0].astype(BF16)
    for hd in range(MLA_HEADS):
        base = hd * QK_PAD
        k_ref[0, :, base:base + LANES] = kv[:, base:base + LANES].astype(BF16)
        k_ref[0, :, base + LANES:base + QK_PAD] = kr_b
        v_ref[0, :, hd * LANES:(hd + 1) * LANES] = kv[:, base + LANES:base + QK_PAD].astype(BF16)


def _ctxkv(ckv, kr_pad, wukv):
    b, s, _ = ckv.shape
    n = b * s
    tm = min(n, 1024)
    blk = lambda w: pl.BlockSpec((1, tm, w), lambda i: (0, i, 0))
    k, v = pl.pallas_call(
        _ctxkv_kernel,
        grid=(n // tm,),
        in_specs=[blk(KV_LORA), blk(LANES), _const_spec(wukv.shape)],
        out_specs=[blk(MLA_HEADS * QK_PAD), blk(MLA_HEADS * V_HEAD)],
        out_shape=[jax.ShapeDtypeStruct((1, n, MLA_HEADS * QK_PAD), BF16),
                   jax.ShapeDtypeStruct((1, n, MLA_HEADS * V_HEAD), BF16)],
        compiler_params=_params(("parallel",)),
        name="ctx_kv",
    )(ckv.reshape(1, n, KV_LORA), kr_pad.reshape(1, n, LANES), wukv)
    return k.reshape(b, s, -1), v.reshape(b, s, -1)


def _attn_kernel(*refs, latent):
    if latent:
        q_ref, k_ref, v_ref, kc_ref, vc_ref, o_ref = refs
    else:
        q_ref, k_ref, v_ref, o_ref = refs
    items = [(sq, hd) for sq in range(q_ref.shape[0]) for hd in range(MLA_HEADS)]

    def scores(item):
        sq, hd = item
        qs = slice(hd * QK_PAD, (hd + 1) * QK_PAD)
        q = q_ref[sq, :, qs]
        if latent:
            return _dot_nt(k_ref[sq, :, qs], q), _dot_nt(kc_ref[sq, :, qs], q)
        return (_dot_nt(q, k_ref[sq, :, qs]),)

    nxt = scores(items[0])
    for n, (sq, hd) in enumerate(items):
        cur = nxt
        if n + 1 < len(items):
            nxt = scores(items[n + 1])
        vs = slice(hd * V_HEAD, (hd + 1) * V_HEAD)
        if latent:
            s, sc = cur
            mx = jnp.maximum(jnp.max(s, axis=0, keepdims=True), jnp.max(sc, axis=0, keepdims=True))
            p = jnp.exp2(s - mx)
            pc = jnp.exp2(sc - mx)
            den = jnp.sum(p, axis=0, keepdims=True) + jnp.sum(pc, axis=0, keepdims=True)
            v_t = jnp.transpose(v_ref[sq, :, vs].astype(F32)).astype(BF16)
            vc_t = jnp.transpose(vc_ref[sq, :, vs].astype(F32)).astype(BF16)
            acc = _dot(v_t, p.astype(BF16)) + _dot(vc_t, pc.astype(BF16))
            o_ref[sq, :, vs] = jnp.transpose(acc / den).astype(BF16)
        else:
            s, = cur
            p = jnp.exp2(s - jnp.max(s, axis=-1, keepdims=True))
            den = jnp.sum(p, axis=-1, keepdims=True)
            o_ref[sq, :, vs] = (_dot(p.astype(BF16), v_ref[sq, :, vs]) / den).astype(BF16)


def _attention(q, k, v, ctx, tq, bs):
    b, s, _ = q.shape
    latent = ctx is not None
    qw, vw = MLA_HEADS * QK_PAD, MLA_HEADS * V_HEAD
    in_specs = [pl.BlockSpec((bs, tq, qw), lambda i, j: (i, j, 0)),
                pl.BlockSpec((bs, s, qw), lambda i, j: (i, 0, 0)),
                pl.BlockSpec((bs, s, vw), lambda i, j: (i, 0, 0))]
    args = [q, k, v]
    if latent:
        sc = ctx[0].shape[1]
        in_specs += [pl.BlockSpec((bs, sc, qw), lambda i, j: (i, 0, 0)),
                     pl.BlockSpec((bs, sc, vw), lambda i, j: (i, 0, 0))]
        args += list(ctx)
    return pl.pallas_call(
        functools.partial(_attn_kernel, latent=latent),
        grid=(b // bs, s // tq), in_specs=in_specs,
        out_specs=pl.BlockSpec((bs, tq, vw), lambda i, j: (i, j, 0)),
        out_shape=jax.ShapeDtypeStruct((b, s, vw), BF16),
        compiler_params=_params(("parallel", "parallel")),
        name="attn_latent" if latent else "attn_ctx",
    )(*args)


def _ssd_kernel(*refs, seq, has_init):
    if has_init:
        (act_ref, z_ref, small_ref, init_ref, dtbias_ref, alog_ref, dskip_ref, gout_ref,
         tril_ref, e64_ref, out_ref, fin_ref, bt_s, y_s, xwb_s, eg_s, sf_s, sb_s) = refs
    else:
        (act_ref, z_ref, small_ref, dtbias_ref, alog_ref, dskip_ref, gout_ref,
         tril_ref, e64_ref, out_ref, fin_ref, bt_s, y_s, xwb_s, eg_s, sf_s, sb_s) = refs
    L = CHUNK
    nchunk = seq // L
    if has_init:
        sf_s[...] = jnp.transpose(init_ref[0, 0])
        sb_s[...] = jnp.transpose(init_ref[0, 1])
    else:
        sf_s[...] = jnp.zeros_like(sf_s)
        sb_s[...] = jnp.zeros_like(sb_s)

    col = lax.broadcasted_iota(jnp.int32, (L, LANES), 1)
    dt_cols = (col >= DT_OFF) & (col < DT_OFF + N_DT)
    fwd_cols = col < DT_OFF + SSM_HEADS
    row_t = lax.broadcasted_iota(jnp.int32, (L, L), 0)
    col_s = lax.broadcasted_iota(jnp.int32, (L, L), 1)
    lower = col_s <= row_t
    upper = col_s >= row_t
    lane_lo = col < SSM_HEAD_DIM
    a_row = -jnp.exp(alog_ref[...])
    neg_inf = jnp.float32(-jnp.inf)
    b_off = SSM_WIDTH
    c_off = SSM_WIDTH + SSM_GROUPS * D_STATE

    def fwd(j, carry):
        r0 = pl.multiple_of(j * L, L)
        xs_b = act_ref[0, pl.ds(r0, L), 0:SSM_WIDTH]
        xs = xs_b.astype(F32)

        dtc = jax.nn.softplus(small_ref[0, pl.ds(r0, L), :] + dtbias_ref[...])
        da = jnp.where(dt_cols, dtc * a_row, 0.0)
        cum = _dot3_rhs(tril_ref[...], da)
        tot = cum[L - 1:L, :]
        suf = tot - cum + da
        fg = jnp.where(fwd_cols, cum, suf)
        w_small = jnp.where(dt_cols, dtc * jnp.exp(tot - fg), 0.0)
        e_small = jnp.where(dt_cols, jnp.exp(fg), 0.0)
        w_hi = w_small.astype(BF16)
        w_lo = (w_small - w_hi.astype(F32)).astype(BF16)
        ex = _dot(jnp.concatenate([w_hi, w_lo, e_small.astype(BF16)], axis=0), e64_ref[...])
        w_x = ex[0:L] + ex[L:2 * L]
        e_x = ex[2 * L:3 * L]
        r_t = jnp.transpose(fg - jnp.log(dtc))

        wf_x = w_x[:, :SSM_WIDTH]
        wb_x = w_x[:, SSM_WIDTH:]
        ef_x = e_x[:, :SSM_WIDTH]
        eg_x = e_x[:, SSM_WIDTH:]
        eg_s[pl.ds(r0, L), :] = eg_x
        xwf_b = (xs * wf_x).astype(BF16)
        xwb_s[pl.ds(r0, L), :] = (xs * wb_x).astype(BF16)

        y_parts = []
        new_states = []
        heads_per_group = SSM_HEADS // SSM_GROUPS
        for g in range(SSM_GROUPS):
            gs = slice(g * GROUP_W, (g + 1) * GROUP_W)
            bm_g = act_ref[0, pl.ds(r0, L), b_off + g * D_STATE:b_off + (g + 1) * D_STATE]
            cm_g = act_ref[0, pl.ds(r0, L), c_off + g * D_STATE:c_off + (g + 1) * D_STATE]
            bt_g = jnp.transpose(bm_g.astype(F32)).astype(BF16)
            bt_s[pl.ds(pl.multiple_of((j * SSM_GROUPS + g) * D_STATE, D_STATE), D_STATE), :] = bt_g
            cbm = _dot_nt(cm_g, bm_g)
            for pair in range(heads_per_group // 2):
                ws = []
                for k in range(2):
                    cf = DT_OFF + g * heads_per_group + pair * 2 + k
                    seg_f = fg[:, cf:cf + 1] - r_t[cf:cf + 1, :]
                    lf = jnp.exp(jnp.where(lower, seg_f, neg_inf))
                    cg = cf + SSM_HEADS
                    seg_b = fg[:, cg:cg + 1] - r_t[cg:cg + 1, :]
                    ub = jnp.exp(jnp.where(upper, seg_b, neg_inf))
                    ws.append((cbm * (lf + ub)).astype(BF16))
                p0 = (g * heads_per_group + pair * 2) * SSM_HEAD_DIM
                xpair = xs_b[:, p0:p0 + LANES]
                zeros = jnp.zeros_like(xpair)
                rhs = jnp.concatenate([jnp.where(lane_lo, xpair, zeros),
                                       jnp.where(lane_lo, zeros, xpair)], axis=0)
                y_parts.append(_dot(jnp.concatenate(ws, axis=1), rhs))
            s_in = sf_s[:, gs]
            y_off = _dot(cm_g, s_in.astype(BF16)) * ef_x[:, gs]
            y_parts[-2] = y_parts[-2] + y_off[:, :LANES]
            y_parts[-1] = y_parts[-1] + y_off[:, LANES:]
            new_states.append(s_in * ef_x[L - 1:L, gs] + _dot(bt_g, xwf_b[:, gs]))
        for g in range(SSM_GROUPS):
            sf_s[:, g * GROUP_W:(g + 1) * GROUP_W] = new_states[g]
        for i, yp in enumerate(y_parts):
            y_s[pl.ds(r0, L), i * LANES:(i + 1) * LANES] = yp
        return carry

    lax.fori_loop(0, nchunk, fwd, 0, unroll=2)

    def bwd(jj, carry):
        j = nchunk - 1 - jj
        r0 = pl.multiple_of(j * L, L)
        xs = act_ref[0, pl.ds(r0, L), 0:SSM_WIDTH].astype(F32)
        eg_x = eg_s[pl.ds(r0, L), :]
        xwb_b = xwb_s[pl.ds(r0, L), :]
        y_off = []
        for g in range(SSM_GROUPS):
            gs = slice(g * GROUP_W, (g + 1) * GROUP_W)
            s_in = sb_s[:, gs]
            cm_g = act_ref[0, pl.ds(r0, L), c_off + g * D_STATE:c_off + (g + 1) * D_STATE]
            y_off.append(_dot(cm_g, s_in.astype(BF16)) * eg_x[:, gs])
            bt_g = bt_s[pl.ds(pl.multiple_of((j * SSM_GROUPS + g) * D_STATE, D_STATE), D_STATE), :]
            sb_s[:, gs] = s_in * eg_x[0:1, gs] + _dot(bt_g, xwb_b[:, gs])
        y = y_s[pl.ds(r0, L), :] + jnp.concatenate(y_off, axis=1) + dskip_ref[...] * xs
        y = y * _silu(z_ref[0, pl.ds(r0, L), :].astype(F32))
        out_ref[0, pl.ds(r0, L), :] = _rms(y, gout_ref[...]).astype(BF16)
        return carry

    lax.fori_loop(0, nchunk, bwd, 0, unroll=2)
    fin_ref[0, 0] = jnp.transpose(sf_s[...])
    fin_ref[0, 1] = jnp.transpose(sb_s[...])


def _ssd(act, z, small, init, consts):
    b, s, _ = act.shape
    has_init = init is not None
    nchunk = s // CHUNK
    blk = lambda w: pl.BlockSpec((1, s, w), lambda i: (i, 0, 0))
    st_spec = pl.BlockSpec((1, 2, SSM_WIDTH, D_STATE), lambda i: (i, 0, 0, 0))
    in_specs = [blk(CONV_DIM), blk(SSM_WIDTH), blk(LANES)]
    args = [act, z, small]
    if has_init:
        in_specs.append(st_spec)
        args.append(init)
    in_specs += [_const_spec(c.shape) for c in consts]
    args += list(consts)
    scratch = [pltpu.VMEM((nchunk * SSM_GROUPS * D_STATE, CHUNK), BF16),
               pltpu.VMEM((s, SSM_WIDTH), F32),
               pltpu.VMEM((s, SSM_WIDTH), BF16),
               pltpu.VMEM((s, SSM_WIDTH), F32),
               pltpu.VMEM((D_STATE, SSM_WIDTH), F32),
               pltpu.VMEM((D_STATE, SSM_WIDTH), F32)]
    return pl.pallas_call(
        functools.partial(_ssd_kernel, seq=s, has_init=has_init),
        grid=(b,), in_specs=in_specs,
        out_specs=[blk(SSM_WIDTH), st_spec],
        out_shape=[jax.ShapeDtypeStruct((b, s, SSM_WIDTH), BF16),
                   jax.ShapeDtypeStruct((b, 2, SSM_WIDTH, D_STATE), F32)],
        scratch_shapes=scratch,
        compiler_params=_params(("parallel",)),
        name="ssd_latent" if has_init else "ssd_ctx",
    )(*args)


SSD_GROUP = LANES // N_DT


def _ssd2_kernel(*refs, nseq, cps, has_init):
    if has_init:
        (act_ref, z_ref, small_ref, init_ref, dtbias_ref, alog_ref, dskip_ref, gout_ref,
         tril_ref, e64_ref, out_ref, fin_ref,
         fg_s, wsm_s, rt_s, bt_s, y_s, xwb_s, eg_s, sf_s, sb_s) = refs
    else:
        (act_ref, z_ref, small_ref, dtbias_ref, alog_ref, dskip_ref, gout_ref,
         tril_ref, e64_ref, out_ref, fin_ref,
         fg_s, wsm_s, rt_s, bt_s, y_s, xwb_s, eg_s, sf_s, sb_s) = refs
    L = CHUNK
    G = nseq * cps
    heads_per_group = SSM_HEADS // SSM_GROUPS
    b_off = SSM_WIDTH
    c_off = SSM_WIDTH + SSM_GROUPS * D_STATE

    lane = lax.broadcasted_iota(jnp.int32, (L, LANES), 1)
    packed = jnp.zeros((L, LANES), F32)
    for c in range(G):
        raw = small_ref[c // cps, (c % cps) * L:(c % cps + 1) * L, :]
        shifted = pltpu.roll(raw, (c * N_DT - DT_OFF) % LANES, axis=1)
        packed = jnp.where((lane >= c * N_DT) & (lane < (c + 1) * N_DT), shifted, packed)
    dtc = jax.nn.softplus(packed + dtbias_ref[...])
    da = dtc * (-jnp.exp(alog_ref[...]))
    cum = _dot3_rhs(tril_ref[...], da)
    tot = cum[L - 1:L, :]
    suf = tot - cum + da
    fg = jnp.where(jnp.bitwise_and(lane, SSM_HEADS) == 0, cum, suf)
    w_small = dtc * jnp.exp(tot - fg)
    e_small = jnp.exp(fg)
    rt_s[...] = jnp.transpose(fg - jnp.log(dtc))
    for c in range(G):
        back = (LANES - c * N_DT) % LANES
        unroll = lambda t: t if back == 0 else pltpu.roll(t, back, axis=1)
        fg_s[c] = unroll(fg)
        wsm_s[c, 0:L, :] = unroll(w_small).astype(BF16)
        wsm_s[c, L:2 * L, :] = unroll(e_small).astype(BF16)

    for s in range(nseq):
        if has_init:
            sf_s[s] = jnp.transpose(init_ref[s, 0])
            sb_s[s] = jnp.transpose(init_ref[s, 1])
        else:
            sf_s[s] = jnp.zeros((D_STATE, SSM_WIDTH), F32)
            sb_s[s] = jnp.zeros((D_STATE, SSM_WIDTH), F32)

    row_t = lax.broadcasted_iota(jnp.int32, (L, L), 0)
    col_s = lax.broadcasted_iota(jnp.int32, (L, L), 1)
    lower = col_s <= row_t
    upper = col_s >= row_t
    lane_lo = lane < SSM_HEAD_DIM
    neg_inf = jnp.float32(-jnp.inf)

    def locate(c):
        sq = c // cps
        r0 = pl.multiple_of((c - sq * cps) * L, L)
        return sq, r0

    def fwd(c, carry):
        sq, r0 = locate(c)
        f0 = pl.multiple_of(c * L, L)
        xs_b = act_ref[sq, pl.ds(r0, L), 0:SSM_WIDTH]
        fgc = fg_s[c]
        rtc = rt_s[pl.ds(pl.multiple_of(c * N_DT, N_DT), N_DT), :]
        ex = _dot(wsm_s[c], e64_ref[...])
        w_b = ex[0:L, :].astype(BF16)
        ef_x = ex[L:2 * L, :SSM_WIDTH]
        eg_x = ex[L:2 * L, SSM_WIDTH:]
        eg_s[pl.ds(f0, L), :] = eg_x
        xwf_b = xs_b * w_b[:, :SSM_WIDTH]
        xwb_s[pl.ds(f0, L), :] = xs_b * w_b[:, SSM_WIDTH:]

        y_parts = []
        new_states = []
        for g in range(SSM_GROUPS):
            gs = slice(g * GROUP_W, (g + 1) * GROUP_W)
            bm_g = act_ref[sq, pl.ds(r0, L), b_off + g * D_STATE:b_off + (g + 1) * D_STATE]
            cm_g = act_ref[sq, pl.ds(r0, L), c_off + g * D_STATE:c_off + (g + 1) * D_STATE]
            bt_g = jnp.transpose(bm_g.astype(F32)).astype(BF16)
            bt_s[pl.ds(pl.multiple_of((c * SSM_GROUPS + g) * D_STATE, D_STATE), D_STATE), :] = bt_g
            cbm = _dot_nt(cm_g, bm_g)
            for pair in range(heads_per_group // 2):
                ws = []
                for k in range(2):
                    cf = g * heads_per_group + pair * 2 + k
                    seg_f = fgc[:, cf:cf + 1] - rtc[cf:cf + 1, :]
                    lf = jnp.exp(jnp.where(lower, seg_f, neg_inf))
                    cg = cf + SSM_HEADS
                    seg_b = fgc[:, cg:cg + 1] - rtc[cg:cg + 1, :]
                    ub = jnp.exp(jnp.where(upper, seg_b, neg_inf))
                    ws.append((cbm * (lf + ub)).astype(BF16))
                p0 = (g * heads_per_group + pair * 2) * SSM_HEAD_DIM
                xpair = xs_b[:, p0:p0 + LANES]
                zeros = jnp.zeros_like(xpair)
                rhs = jnp.concatenate([jnp.where(lane_lo, xpair, zeros),
                                       jnp.where(lane_lo, zeros, xpair)], axis=0)
                y_parts.append(_dot(jnp.concatenate(ws, axis=1), rhs))
            s_in = sf_s[sq, :, gs]
            y_off = _dot(cm_g, s_in.astype(BF16)) * ef_x[:, gs]
            y_parts[-2] = y_parts[-2] + y_off[:, :LANES]
            y_parts[-1] = y_parts[-1] + y_off[:, LANES:]
            new_states.append(s_in * ef_x[L - 1:L, gs] + _dot(bt_g, xwf_b[:, gs]))
        for g in range(SSM_GROUPS):
            sf_s[sq, :, g * GROUP_W:(g + 1) * GROUP_W] = new_states[g]
        for i, yp in enumerate(y_parts):
            y_s[pl.ds(f0, L), i * LANES:(i + 1) * LANES] = yp
        return carry

    lax.fori_loop(0, G, fwd, 0, unroll=2)

    def bwd(cc, carry):
        c = G - 1 - cc
        sq, r0 = locate(c)
        f0 = pl.multiple_of(c * L, L)
        xs = act_ref[sq, pl.ds(r0, L), 0:SSM_WIDTH].astype(F32)
        eg_x = eg_s[pl.ds(f0, L), :]
        xwb_b = xwb_s[pl.ds(f0, L), :]
        y_off = []
        for g in range(SSM_GROUPS):
            gs = slice(g * GROUP_W, (g + 1) * GROUP_W)
            s_in = sb_s[sq, :, gs]
            cm_g = act_ref[sq, pl.ds(r0, L), c_off + g * D_STATE:c_off + (g + 1) * D_STATE]
            y_off.append(_dot(cm_g, s_in.astype(BF16)) * eg_x[:, gs])
            bt_g = bt_s[pl.ds(pl.multiple_of((c * SSM_GROUPS + g) * D_STATE, D_STATE), D_STATE), :]
            sb_s[sq, :, gs] = s_in * eg_x[0:1, gs] + _dot(bt_g, xwb_b[:, gs])
        y = y_s[pl.ds(f0, L), :] + jnp.concatenate(y_off, axis=1) + dskip_ref[...] * xs
        y = y * _silu(z_ref[sq, pl.ds(r0, L), :].astype(F32))
        out_ref[sq, pl.ds(r0, L), :] = _rms(y, gout_ref[...]).astype(BF16)
        return carry

    lax.fori_loop(0, G, bwd, 0, unroll=2)
    for s in range(nseq):
        fin_ref[s, 0] = jnp.transpose(sf_s[s])
        fin_ref[s, 1] = jnp.transpose(sb_s[s])


def _ssd2(act, z, small, init, consts):
    b, s, _ = act.shape
    has_init = init is not None
    cps = s // CHUNK
    assert SSD_GROUP % cps == 0
    nseq = SSD_GROUP // cps
    assert b % nseq == 0
    blk = lambda w: pl.BlockSpec((nseq, s, w), lambda i: (i, 0, 0))
    st_spec = pl.BlockSpec((nseq, 2, SSM_WIDTH, D_STATE), lambda i: (i, 0, 0, 0))
    in_specs = [blk(CONV_DIM), blk(SSM_WIDTH), blk(LANES)]
    args = [act, z, small]
    if has_init:
        in_specs.append(st_spec)
        args.append(init)
    in_specs += [_const_spec(c.shape) for c in consts]
    args += list(consts)
    rows = SSD_GROUP * CHUNK
    scratch = [pltpu.VMEM((SSD_GROUP, CHUNK, LANES), F32),
               pltpu.VMEM((SSD_GROUP, 2 * CHUNK, LANES), BF16),
               pltpu.VMEM((LANES, CHUNK), F32),
               pltpu.VMEM((SSD_GROUP * SSM_GROUPS * D_STATE, CHUNK), BF16),
               pltpu.VMEM((rows, SSM_WIDTH), F32),
               pltpu.VMEM((rows, SSM_WIDTH), BF16),
               pltpu.VMEM((rows, SSM_WIDTH), F32),
               pltpu.VMEM((nseq, D_STATE, SSM_WIDTH), F32),
               pltpu.VMEM((nseq, D_STATE, SSM_WIDTH), F32)]
    return pl.pallas_call(
        functools.partial(_ssd2_kernel, nseq=nseq, cps=cps, has_init=has_init),
        grid=(b // nseq,), in_specs=in_specs,
        out_specs=[blk(SSM_WIDTH), st_spec],
        out_shape=[jax.ShapeDtypeStruct((b, s, SSM_WIDTH), BF16),
                   jax.ShapeDtypeStruct((b, 2, SSM_WIDTH, D_STATE), F32)],
        scratch_shapes=scratch,
        compiler_params=_params(("parallel",)),
        name="ssd_latent" if has_init else "ssd_ctx",
    )(*args)


FF_CHUNK = 256
ROW_SPLIT = 1


def _outffn_kernel(x_ref, attn_ref, ssm_ref, mod_ref, wout_ref, gpost_ref, gpre_ref, gpostf_ref,
                   wg_ref, wu_ref, wd_ref, o_ref, mix_s, y_s, h_s):
    m = mod_ref[0]
    half = MLA_HEADS * V_HEAD
    rows = x_ref.shape[1] // 2
    nslice = 8
    srows = rows // nslice
    nchunks = D_FF // FF_CHUNK
    g1 = gpost_ref[...] * m[2:3]
    g2 = gpre_ref[...] * (1.0 + m[4:5])
    sh2 = m[3:4]
    g3 = gpostf_ref[...] * m[5:6]

    def out_proj(rs):
        return _dot(attn_ref[0, rs, :], wout_ref[0:half, :]) + _dot(ssm_ref[0, rs, :], wout_ref[half:, :])

    def pre_ffn(mix, xr):
        y = xr + _rms(mix, g1)
        return y, (_rms(y, g2) + sh2).astype(BF16)

    def ffn_chunk(h, c, acc):
        cs = slice(c * FF_CHUNK, (c + 1) * FF_CHUNK)
        part = _dot((_silu(_dot(h, wg_ref[:, cs])) * _dot(h, wu_ref[:, cs])).astype(BF16), wd_ref[cs, :])
        return part if acc is None else acc + part

    y_a, h_a = pre_ffn(out_proj(slice(0, rows)), x_ref[0, 0:rows, :])
    mix_s[...] = out_proj(slice(rows, 2 * rows))
    acc_a = None
    for c in range(nchunks):
        acc_a = ffn_chunk(h_a, c, acc_a)
        if c < nslice:
            rs = slice(c * srows, (c + 1) * srows)
            y_b, h_b = pre_ffn(mix_s[rs, :], x_ref[0, rows + c * srows:rows + (c + 1) * srows, :])
            y_s[rs, :] = y_b
            h_s[rs, :] = h_b
    h_bb = h_s[...]
    acc_b = None
    for c in range(nchunks):
        acc_b = ffn_chunk(h_bb, c, acc_b)
        if c < nslice:
            rs = slice(c * srows, (c + 1) * srows)
            o_ref[0, rs, :] = y_a[rs] + _rms(acc_a[rs], g3)
    o_ref[0, rows:2 * rows, :] = y_s[...] + _rms(acc_b, g3)


def _outffn(x, attn, ssm, mod, mod_off, wout, gpost, gpre, gpostf, wg, wu, wd, tm):
    b, s, d = x.shape
    tok = lambda w: pl.BlockSpec((1, tm, w), lambda i, j: (i, j, 0))
    consts = [wout, gpost, gpre, gpostf, wg, wu, wd]
    return pl.pallas_call(
        _outffn_kernel,
        grid=(b, s // tm),
        in_specs=[tok(d), tok(MLA_HEADS * V_HEAD), tok(SSM_WIDTH),
                  pl.BlockSpec((1, 6, d), lambda i, j: (i + mod_off, 0, 0))]
                 + [_const_spec(c.shape) for c in consts],
        out_specs=tok(d),
        out_shape=jax.ShapeDtypeStruct((b, s, d), F32),
        scratch_shapes=[pltpu.VMEM((tm // 2, d), F32),
                        pltpu.VMEM((tm // 2, d), F32),
                        pltpu.VMEM((tm // 2, d), BF16)],
        compiler_params=_params(("parallel", "parallel")),
        name="out_ffn",
    )(x, attn, ssm, mod, *consts)


def _rope_tables(length):
    quarter = QK_ROPE // 4
    pos = np.arange(length)
    inv_freq = ROPE_THETA ** (-np.arange(quarter, dtype=np.float64) / quarter)
    ang_r = (pos // GRID_W)[:, None] * inv_freq[None, :]
    ang_c = (pos % GRID_W)[:, None] * inv_freq[None, :]
    cos = np.concatenate([np.cos(ang_r)] * 2 + [np.cos(ang_c)] * 2, axis=1)
    sin = np.concatenate([-np.sin(ang_r), np.sin(ang_r), -np.sin(ang_c), np.sin(ang_c)], axis=1)
    return jnp.asarray(np.tile(cos, (1, 2)), F32), jnp.asarray(np.tile(sin, (1, 2)), F32)


def _pad_cols(w, width, left=0):
    return jnp.pad(w, ((0, 0), (left, width - left - w.shape[-1])))


def _ssd_constants():
    idx = np.arange(CHUNK)
    tril = (idx[None, :] <= idx[:, None]).astype(np.float32)
    e64 = np.zeros((LANES, N_DT * SSM_HEAD_DIM), np.float32)
    for q in range(N_DT):
        e64[q, q * SSM_HEAD_DIM:(q + 1) * SSM_HEAD_DIM] = 1.0
    return [jnp.asarray(a, BF16) for a in (tril, e64)]


def kernel(x_prompt, x_sample, cache_ckv, cache_krope, state_ssm, c, c_ctx, w_mod, b_mod,
           g_pre_mix, g_post_mix, w_in, g_q, w_uq, g_kv, w_ukv, conv_w, conv_b, dt_bias,
           a_log, d_skip, g_ssm_out, w_out, g_pre_ffn, g_post_ffn, w_gate, w_up, w_down):
    depth = w_in.shape[0]
    assert depth == 1
    nb, seq, d = x_prompt.shape
    db, dseq, _ = x_sample.shape
    l = 0

    rows = 16
    cond = jnp.concatenate([c_ctx[None, :], c, jnp.zeros((rows - 1 - db, d), F32)], axis=0)
    mod = _modulation(cond, w_mod[l], b_mod[l][None, :]).reshape(rows, 6, d)

    wi = w_in[l]
    o_q, o_kv, o_kr = Q_LORA, Q_LORA + KV_LORA, Q_LORA + KV_LORA + QK_ROPE
    o_z, o_xbc = o_kr + SSM_WIDTH, o_kr + SSM_WIDTH + CONV_DIM
    kr_dt = _pad_cols(jnp.concatenate([wi[:, o_kv:o_kr], wi[:, o_xbc:]], axis=1), LANES)
    win = jnp.concatenate([wi[:, :o_q], wi[:, o_q:o_kv], wi[:, o_kr:o_z], wi[:, o_z:o_xbc], kr_dt],
                          axis=1).astype(BF16)
    wq = w_uq[l].reshape(Q_LORA, MLA_HEADS, QK_NOPE + QK_ROPE)
    wuq = jnp.concatenate([wq[:, :, :QK_NOPE].reshape(Q_LORA, MLA_HEADS * QK_NOPE),
                           wq[:, :, QK_NOPE:].reshape(Q_LORA, MLA_HEADS * QK_ROPE)], axis=1).astype(BF16)
    wukv = w_ukv[l].astype(BF16)
    wout = w_out[l].astype(BF16)
    wg, wu, wd = w_gate[l].astype(BF16), w_up[l].astype(BF16), w_down[l].astype(BF16)
    row = lambda v: v.reshape(1, -1)
    gpre, gq, gkv = row(g_pre_mix[l]), row(g_q[l]), row(g_kv[l])
    gpost, gpre_f, gpost_f = row(g_post_mix[l]), row(g_pre_ffn[l]), row(g_post_ffn[l])
    cw, cb = conv_w[l], row(conv_b[l])

    small = lambda v: jnp.tile(v.reshape(1, -1), (1, SSD_GROUP))
    ssd_consts = [small(dt_bias[l]), small(a_log[l]),
                  row(jnp.repeat(d_skip[l], SSM_HEAD_DIM)), row(g_ssm_out[l])] + _ssd_constants()

    xp = x_prompt.reshape(1, nb * seq, d)
    q, k, v, z, act, sm, ckv, krope = _inproj(xp, mod, 0, gpre, win, gq, wuq, gkv, wukv, cw, cb,
                                              None, tm=1024, seq_rows=seq)
    per_seq = lambda a: a.reshape(nb, seq, a.shape[-1])
    attn = _attention(per_seq(q), per_seq(k), per_seq(v), None, tq=seq, bs=4)
    ssm, fin = _ssd2(per_seq(act), per_seq(z), per_seq(sm), None, ssd_consts)
    y_p = _outffn(xp, attn.reshape(1, nb * seq, -1), ssm.reshape(1, nb * seq, -1), mod, 0,
                  wout, gpost, gpre_f, gpost_f, wg, wu, wd, tm=1024).reshape(nb, seq, d)
    new_ckv = ckv.reshape(nb, 1, seq, KV_LORA)
    new_krope = krope.reshape(nb, 1, seq, QK_ROPE)
    new_ssm = fin.reshape(nb, 1, 2, SSM_HEADS, SSM_HEAD_DIM, D_STATE)

    kr_cache = jnp.pad(cache_krope[:, l], ((0, 0), (0, 0), (0, LANES - QK_ROPE)))
    kc, vc = _ctxkv(cache_ckv[:, l], kr_cache, wukv)
    q, k, v, z, act, sm = _inproj(x_sample, mod, 1, gpre, win, gq, wuq, gkv, wukv, cw, cb,
                                  _rope_tables(dseq), tm=1024, seq_rows=dseq)
    attn = _attention(q, k, v, (kc, vc), tq=1024, bs=1)
    init = state_ssm[:, l].reshape(db, 2, SSM_WIDTH, D_STATE)
    ssm, _ = _ssd2(act, z, sm, init, ssd_consts)
    y_s = _outffn(x_sample, attn, ssm, mod, 1, wout, gpost, gpre_f, gpost_f, wg, wu, wd, tm=1024)

    return (y_p, y_s, new_ckv, new_krope, new_ssm)
```

```python
import functools

import numpy as np
import jax
import jax.numpy as jnp
from jax import lax
from jax.experimental import pallas as pl
from jax.experimental.pallas import tpu as pltpu

F32 = jnp.float32
BF16 = jnp.bfloat16

D_MODEL = 1024
GRID_W = 64
ROPE_THETA = 10000.0
NORM_EPS = 1e-6
MLA_HEADS = 4
QK_NOPE = 128
QK_ROPE = 64
V_HEAD = 128
Q_LORA = 384
KV_LORA = 256
SSM_HEADS = 8
SSM_HEAD_DIM = 64
SSM_WIDTH = SSM_HEADS * SSM_HEAD_DIM
SSM_GROUPS = 2
D_STATE = 128
D_CONV = 3
CHUNK = 128
CONV_DIM = SSM_WIDTH + 2 * SSM_GROUPS * D_STATE
D_FF = 2816

LOG2E = 1.4426950408889634
LANES = 128
SUBLANES = 8
QK_PAD = 256
GROUP_W = SSM_WIDTH // SSM_GROUPS
VMEM_LIMIT = 56 * 1024 * 1024

_SEG_Q = (0, 384)
_SEG_KV = (384, 640)
_SEG_Z = (640, 1152)
_SEG_XBC = (1152, 2176)
_SEG_KR = (2176, 2304)
DT_OFF = QK_ROPE
N_DT = 2 * SSM_HEADS


def _rms(x, g):
    return x * lax.rsqrt(jnp.mean(x * x, axis=-1, keepdims=True) + NORM_EPS) * g


def _silu(x):
    u = 0.5 * x
    return u * jnp.tanh(u) + u


def _dot(a, b):
    return jnp.dot(a, b, preferred_element_type=F32)


def _dot_nt(a, b):
    return lax.dot_general(a, b, (((1,), (1,)), ((), ())), preferred_element_type=F32)


def _split3(x):
    hi = x.astype(BF16)
    r = x - hi.astype(F32)
    mid = r.astype(BF16)
    lo = (r - mid.astype(F32)).astype(BF16)
    return hi, mid, lo


def _dot3_rhs(m, x):
    hi, mid, lo = _split3(x)
    return _dot(m, hi) + _dot(m, mid) + _dot(m, lo)


def _const_spec(shape):
    nd = len(shape)
    return pl.BlockSpec(shape, lambda *_: (0,) * nd, pipeline_mode=pl.Buffered(1))


def _params(sem):
    return pltpu.CompilerParams(dimension_semantics=sem, vmem_limit_bytes=VMEM_LIMIT)


def _mod_kernel(c_ref, w_ref, b_ref, o_ref):
    c = c_ref[...]
    s = _silu(c)
    w = w_ref[...]
    s_hi = s.astype(BF16)
    s_lo = (s - s_hi.astype(F32)).astype(BF16)
    w_hi = w.astype(BF16)
    w_lo = (w - w_hi.astype(F32)).astype(BF16)
    o_ref[...] = _dot(s_hi, w_hi) + _dot(s_lo, w_hi) + _dot(s_hi, w_lo) + b_ref[...]


def _modulation(cond, w_mod, b_mod):
    rows, d = cond.shape
    n = w_mod.shape[1]
    tn = 1536
    return pl.pallas_call(
        _mod_kernel,
        grid=(n // tn,),
        in_specs=[pl.BlockSpec((rows, d), lambda j: (0, 0)),
                  pl.BlockSpec((d, tn), lambda j: (0, j)),
                  pl.BlockSpec((1, tn), lambda j: (0, j))],
        out_specs=pl.BlockSpec((rows, tn), lambda j: (0, j)),
        out_shape=jax.ShapeDtypeStruct((rows, n), F32),
        compiler_params=_params(("arbitrary",)),
        name="modulation",
    )(cond, w_mod, b_mod)


def _conv_silu(xb, prev_rows, next_rows, cw_ref, cb_ref):
    n = xb.shape[0]
    win = jnp.concatenate([prev_rows, xb, next_rows], axis=0)
    total = n + 2 * SUBLANES
    prev = pltpu.roll(win, 1, axis=0)[SUBLANES:SUBLANES + n]
    nxt = pltpu.roll(win, total - 1, axis=0)[SUBLANES:SUBLANES + n]
    conv = cb_ref[...] + prev * cw_ref[0:1, :] + xb * cw_ref[1:2, :] + nxt * cw_ref[2:3, :]
    return _silu(conv)


def _inproj_kernel(*refs, latent, seq_rows):
    if latent:
        (x_ref, mod_ref, gpre_ref, win_ref, gq_ref, wuq_ref, gkv_ref, wukv_ref, cw_ref, cb_ref,
         cos_ref, sin_ref, q_ref, k_ref, v_ref, z_ref, act_ref, small_ref) = refs
    else:
        (x_ref, mod_ref, gpre_ref, win_ref, gq_ref, wuq_ref, gkv_ref, wukv_ref, cw_ref, cb_ref,
         q_ref, k_ref, v_ref, z_ref, act_ref, small_ref, ckv_ref, krope_ref) = refs
    scale = (QK_NOPE + QK_ROPE) ** -0.5 * LOG2E
    m = mod_ref[0]

    g_mod = gpre_ref[...] * (1.0 + m[1:2])

    def pre(xv):
        return (_rms(xv, g_mod) + m[0:1]).astype(BF16)

    tm = x_ref.shape[1]
    rows = tm // 2
    halves = (slice(0, rows), slice(rows, tm))
    assert seq_rows == tm or rows % seq_rows == 0
    zero_rows = jnp.zeros((SUBLANES, CONV_DIM), F32)
    projs = [_dot(pre(x_ref[0, halves[0], :]), win_ref[...])]
    if seq_rows == tm:
        ph = _dot(pre(x_ref[0, rows - SUBLANES:rows + SUBLANES, :]), win_ref[:, _SEG_XBC[0]:_SEG_XBC[1]])
        halo = ((zero_rows, ph[SUBLANES:]), (ph[:SUBLANES], zero_rows))
    projs.append(_dot(pre(x_ref[0, halves[1], :]), win_ref[...]))

    lane = lax.broadcasted_iota(jnp.int32, (rows, LANES), 1)
    rope_lanes = lane < QK_ROPE
    first_quarter = jnp.bitwise_and(lane, QK_ROPE // 4) == 0
    nw = MLA_HEADS * QK_NOPE
    for hf, rs in enumerate(halves):
        proj = projs[hf]
        seg = lambda s: proj[:, s[0]:s[1]]
        xbc = seg(_SEG_XBC)
        if seq_rows == tm:
            act_ref[0, rs, :] = _conv_silu(xbc, halo[hf][0], halo[hf][1], cw_ref, cb_ref).astype(BF16)
        else:
            for s in range(rows // seq_rows):
                sub = slice(s * seq_rows, (s + 1) * seq_rows)
                dst = slice(rs.start + sub.start, rs.start + sub.stop)
                act_ref[0, dst, :] = _conv_silu(xbc[sub], zero_rows, zero_rows, cw_ref, cb_ref).astype(BF16)
        krb = seg(_SEG_KR)
        small_ref[0, rs, :] = krb
        ckv = _rms(seg(_SEG_KV), gkv_ref[...])
        qall = _dot(_rms(seg(_SEG_Q), gq_ref[...]).astype(BF16), wuq_ref[...])
        kv = _dot(ckv.astype(BF16), wukv_ref[...])
        z_ref[0, rs, :] = seg(_SEG_Z).astype(BF16)
        if latent:
            cos = cos_ref[rs, :]
            sin = sin_ref[rs, :]

            def rot(t):
                swapped = jnp.where(first_quarter, pltpu.roll(t, LANES - QK_ROPE // 4, axis=1),
                                    pltpu.roll(t, QK_ROPE // 4, axis=1))
                return t * cos + swapped * sin
        else:
            rot = lambda t: t
            ckv_ref[0, rs, :] = ckv
            krope_ref[0, rs, :] = krb[:, :QK_ROPE]
        kr_b = jnp.where(rope_lanes, rot(krb), 0.0).astype(BF16)
        for pr in range(MLA_HEADS // 2):
            pair = rot(qall[:, nw + pr * LANES:nw + (pr + 1) * LANES]) * scale
            for k in range(2):
                hd = 2 * pr + k
                base = hd * QK_PAD
                qr = pair if k == 0 else pltpu.roll(pair, QK_ROPE, axis=1)
                q_ref[0, rs, base:base + LANES] = (qall[:, hd * LANES:(hd + 1) * LANES] * scale).astype(BF16)
                q_ref[0, rs, base + LANES:base + QK_PAD] = jnp.where(rope_lanes, qr, 0.0).astype(BF16)
                k_ref[0, rs, base:base + LANES] = kv[:, base:base + LANES].astype(BF16)
                k_ref[0, rs, base + LANES:base + QK_PAD] = kr_b
                v_ref[0, rs, hd * LANES:(hd + 1) * LANES] = kv[:, base + LANES:base + QK_PAD].astype(BF16)


def _inproj(x, mod, mod_off, gpre, win, gq, wuq, gkv, wukv, cw, cb, rope, tm, seq_rows):
    b, s, d = x.shape
    latent = rope is not None
    grid = (b, s // tm)
    tok = lambda w: pl.BlockSpec((1, tm, w), lambda i, j: (i, j, 0))
    mod_spec = pl.BlockSpec((1, 6, d), lambda i, j: (i + mod_off, 0, 0))
    consts = [gpre, win, gq, wuq, gkv, wukv, cw, cb]
    in_specs = [tok(d), mod_spec]
    args = [x, mod]
    in_specs += [_const_spec(c.shape) for c in consts]
    args += consts
    out_shape = [jax.ShapeDtypeStruct((b, s, MLA_HEADS * QK_PAD), BF16),
                 jax.ShapeDtypeStruct((b, s, MLA_HEADS * QK_PAD), BF16),
                 jax.ShapeDtypeStruct((b, s, MLA_HEADS * V_HEAD), BF16),
                 jax.ShapeDtypeStruct((b, s, SSM_WIDTH), BF16),
                 jax.ShapeDtypeStruct((b, s, CONV_DIM), BF16),
                 jax.ShapeDtypeStruct((b, s, LANES), F32)]
    out_specs = [tok(MLA_HEADS * QK_PAD), tok(MLA_HEADS * QK_PAD), tok(MLA_HEADS * V_HEAD),
                 tok(SSM_WIDTH), tok(CONV_DIM), tok(LANES)]
    if latent:
        in_specs += [pl.BlockSpec((tm, LANES), lambda i, j: (j, 0))] * 2
        args += list(rope)
    else:
        out_shape += [jax.ShapeDtypeStruct((b, s, KV_LORA), F32),
                      jax.ShapeDtypeStruct((b, s, QK_ROPE), F32)]
        out_specs += [tok(KV_LORA), tok(QK_ROPE)]
    return pl.pallas_call(
        functools.partial(_inproj_kernel, latent=latent, seq_rows=seq_rows),
        grid=grid, in_specs=in_specs, out_specs=out_specs, out_shape=out_shape,
        compiler_params=_params(("parallel", "parallel")),
        name="inproj_latent" if latent else "inproj_ctx",
    )(*args)


def _ctxkv_kernel(ckv_ref, kr_ref, wukv_ref, k_ref, v_ref):
    kv = _dot(ckv_ref[0].astype(BF16), wukv_ref[...])
    kr_b = kr_ref[0].astype(BF16)
    for hd in range(MLA_HEADS):
        base = hd * QK_PAD
        k_ref[0, :, base:base + LANES] = kv[:, base:base + LANES].astype(BF16)
        k_ref[0, :, base + LANES:base + QK_PAD] = kr_b
        v_ref[0, :, hd * LANES:(hd + 1) * LANES] = kv[:, base + LANES:base + QK_PAD].astype(BF16)


def _ctxkv(ckv, kr_pad, wukv):
    b, s, _ = ckv.shape
    n = b * s
    tm = min(n, 1024)
    blk = lambda w: pl.BlockSpec((1, tm, w), lambda i: (0, i, 0))
    k, v = pl.pallas_call(
        _ctxkv_kernel,
        grid=(n // tm,),
        in_specs=[blk(KV_LORA), blk(LANES), _const_spec(wukv.shape)],
        out_specs=[blk(MLA_HEADS * QK_PAD), blk(MLA_HEADS * V_HEAD)],
        out_shape=[jax.ShapeDtypeStruct((1, n, MLA_HEADS * QK_PAD), BF16),
                   jax.ShapeDtypeStruct((1, n, MLA_HEADS * V_HEAD), BF16)],
        compiler_params=_params(("parallel",)),
        name="ctx_kv",
    )(ckv.reshape(1, n, KV_LORA), kr_pad.reshape(1, n, LANES), wukv)
    return k.reshape(b, s, -1), v.reshape(b, s, -1)


def _attn_kernel(*refs, latent):
    if latent:
        q_ref, k_ref, v_ref, kc_ref, vc_ref, o_ref = refs
    else:
        q_ref, k_ref, v_ref, o_ref = refs
    items = [(sq, hd) for sq in range(q_ref.shape[0]) for hd in range(MLA_HEADS)]

    def scores(item):
        sq, hd = item
        qs = slice(hd * QK_PAD, (hd + 1) * QK_PAD)
        q = q_ref[sq, :, qs]
        if latent:
            return _dot_nt(k_ref[sq, :, qs], q), _dot_nt(kc_ref[sq, :, qs], q)
        return (_dot_nt(q, k_ref[sq, :, qs]),)

    nxt = scores(items[0])
    for n, (sq, hd) in enumerate(items):
        cur = nxt
        if n + 1 < len(items):
            nxt = scores(items[n + 1])
        vs = slice(hd * V_HEAD, (hd + 1) * V_HEAD)
        if latent:
            s, sc = cur
            mx = jnp.maximum(jnp.max(s, axis=0, keepdims=True), jnp.max(sc, axis=0, keepdims=True))
            p = jnp.exp2(s - mx)
            pc = jnp.exp2(sc - mx)
            den = jnp.sum(p, axis=0, keepdims=True) + jnp.sum(pc, axis=0, keepdims=True)
            v_t = jnp.transpose(v_ref[sq, :, vs].astype(F32)).astype(BF16)
            vc_t = jnp.transpose(vc_ref[sq, :, vs].astype(F32)).astype(BF16)
            acc = _dot(v_t, p.astype(BF16)) + _dot(vc_t, pc.astype(BF16))
            o_ref[sq, :, vs] = jnp.transpose(acc / den).astype(BF16)
        else:
            s, = cur
            p = jnp.exp2(s - jnp.max(s, axis=-1, keepdims=True))
            den = jnp.sum(p, axis=-1, keepdims=True)
            o_ref[sq, :, vs] = (_dot(p.astype(BF16), v_ref[sq, :, vs]) / den).astype(BF16)


def _attention(q, k, v, ctx, tq, bs):
    b, s, _ = q.shape
    latent = ctx is not None
    qw, vw = MLA_HEADS * QK_PAD, MLA_HEADS * V_HEAD
    in_specs = [pl.BlockSpec((bs, tq, qw), lambda i, j: (i, j, 0)),
                pl.BlockSpec((bs, s, qw), lambda i, j: (i, 0, 0)),
                pl.BlockSpec((bs, s, vw), lambda i, j: (i, 0, 0))]
    args = [q, k, v]
    if latent:
        sc = ctx[0].shape[1]
        in_specs += [pl.BlockSpec((bs, sc, qw), lambda i, j: (i, 0, 0)),
                     pl.BlockSpec((bs, sc, vw), lambda i, j: (i, 0, 0))]
        args += list(ctx)
    return pl.pallas_call(
        functools.partial(_attn_kernel, latent=latent),
        grid=(b // bs, s // tq), in_specs=in_specs,
        out_specs=pl.BlockSpec((bs, tq, vw), lambda i, j: (i, j, 0)),
        out_shape=jax.ShapeDtypeStruct((b, s, vw), BF16),
        compiler_params=_params(("parallel", "parallel")),
        name="attn_latent" if latent else "attn_ctx",
    )(*args)


def _ssd_kernel(*refs, seq, has_init):
    if has_init:
        (act_ref, z_ref, small_ref, init_ref, dtbias_ref, alog_ref, dskip_ref, gout_ref,
         tril_ref, e64_ref, out_ref, fin_ref, bt_s, y_s, xwb_s, eg_s, sf_s, sb_s) = refs
    else:
        (act_ref, z_ref, small_ref, dtbias_ref, alog_ref, dskip_ref, gout_ref,
         tril_ref, e64_ref, out_ref, fin_ref, bt_s, y_s, xwb_s, eg_s, sf_s, sb_s) = refs
    L = CHUNK
    nchunk = seq // L
    if has_init:
        sf_s[...] = jnp.transpose(init_ref[0, 0])
        sb_s[...] = jnp.transpose(init_ref[0, 1])
    else:
        sf_s[...] = jnp.zeros_like(sf_s)
        sb_s[...] = jnp.zeros_like(sb_s)

    col = lax.broadcasted_iota(jnp.int32, (L, LANES), 1)
    dt_cols = (col >= DT_OFF) & (col < DT_OFF + N_DT)
    fwd_cols = col < DT_OFF + SSM_HEADS
    row_t = lax.broadcasted_iota(jnp.int32, (L, L), 0)
    col_s = lax.broadcasted_iota(jnp.int32, (L, L), 1)
    lower = col_s <= row_t
    upper = col_s >= row_t
    lane_lo = col < SSM_HEAD_DIM
    a_row = -jnp.exp(alog_ref[...])
    neg_inf = jnp.float32(-jnp.inf)
    b_off = SSM_WIDTH
    c_off = SSM_WIDTH + SSM_GROUPS * D_STATE

    def fwd(j, carry):
        r0 = pl.multiple_of(j * L, L)
        xs_b = act_ref[0, pl.ds(r0, L), 0:SSM_WIDTH]
        xs = xs_b.astype(F32)

        dtc = jax.nn.softplus(small_ref[0, pl.ds(r0, L), :] + dtbias_ref[...])
        da = jnp.where(dt_cols, dtc * a_row, 0.0)
        cum = _dot3_rhs(tril_ref[...], da)
        tot = cum[L - 1:L, :]
        suf = tot - cum + da
        fg = jnp.where(fwd_cols, cum, suf)
        w_small = jnp.where(dt_cols, dtc * jnp.exp(tot - fg), 0.0)
        e_small = jnp.where(dt_cols, jnp.exp(fg), 0.0)
        w_hi = w_small.astype(BF16)
        w_lo = (w_small - w_hi.astype(F32)).astype(BF16)
        ex = _dot(jnp.concatenate([w_hi, w_lo, e_small.astype(BF16)], axis=0), e64_ref[...])
        w_x = ex[0:L] + ex[L:2 * L]
        e_x = ex[2 * L:3 * L]
        r_t = jnp.transpose(fg - jnp.log(dtc))

        wf_x = w_x[:, :SSM_WIDTH]
        wb_x = w_x[:, SSM_WIDTH:]
        ef_x = e_x[:, :SSM_WIDTH]
        eg_x = e_x[:, SSM_WIDTH:]
        eg_s[pl.ds(r0, L), :] = eg_x
        xwf_b = (xs * wf_x).astype(BF16)
        xwb_s[pl.ds(r0, L), :] = (xs * wb_x).astype(BF16)

        y_parts = []
        new_states = []
        heads_per_group = SSM_HEADS // SSM_GROUPS
        for g in range(SSM_GROUPS):
            gs = slice(g * GROUP_W, (g + 1) * GROUP_W)
            bm_g = act_ref[0, pl.ds(r0, L), b_off + g * D_STATE:b_off + (g + 1) * D_STATE]
            cm_g = act_ref[0, pl.ds(r0, L), c_off + g * D_STATE:c_off + (g + 1) * D_STATE]
            bt_g = jnp.transpose(bm_g.astype(F32)).astype(BF16)
            bt_s[pl.ds(pl.multiple_of((j * SSM_GROUPS + g) * D_STATE, D_STATE), D_STATE), :] = bt_g
            cbm = _dot_nt(cm_g, bm_g)
            for pair in range(heads_per_group // 2):
                ws = []
                for k in range(2):
                    cf = DT_OFF + g * heads_per_group + pair * 2 + k
                    seg_f = fg[:, cf:cf + 1] - r_t[cf:cf + 1, :]
                    lf = jnp.exp(jnp.where(lower, seg_f, neg_inf))
                    cg = cf + SSM_HEADS
                    seg_b = fg[:, cg:cg + 1] - r_t[cg:cg + 1, :]
                    ub = jnp.exp(jnp.where(upper, seg_b, neg_inf))
                    ws.append((cbm * (lf + ub)).astype(BF16))
                p0 = (g * heads_per_group + pair * 2) * SSM_HEAD_DIM
                xpair = xs_b[:, p0:p0 + LANES]
                zeros = jnp.zeros_like(xpair)
                rhs = jnp.concatenate([jnp.where(lane_lo, xpair, zeros),
                                       jnp.where(lane_lo, zeros, xpair)], axis=0)
                y_parts.append(_dot(jnp.concatenate(ws, axis=1), rhs))
            s_in = sf_s[:, gs]
            y_off = _dot(cm_g, s_in.astype(BF16)) * ef_x[:, gs]
            y_parts[-2] = y_parts[-2] + y_off[:, :LANES]
            y_parts[-1] = y_parts[-1] + y_off[:, LANES:]
            new_states.append(s_in * ef_x[L - 1:L, gs] + _dot(bt_g, xwf_b[:, gs]))
        for g in range(SSM_GROUPS):
            sf_s[:, g * GROUP_W:(g + 1) * GROUP_W] = new_states[g]
        for i, yp in enumerate(y_parts):
            y_s[pl.ds(r0, L), i * LANES:(i + 1) * LANES] = yp
        return carry

    lax.fori_loop(0, nchunk, fwd, 0, unroll=2)

    def bwd(jj, carry):
        j = nchunk - 1 - jj
        r0 = pl.multiple_of(j * L, L)
        xs = act_ref[0, pl.ds(r0, L), 0:SSM_WIDTH].astype(F32)
        eg_x = eg_s[pl.ds(r0, L), :]
        xwb_b = xwb_s[pl.ds(r0, L), :]
        y_off = []
        for g in range(SSM_GROUPS):
            gs = slice(g * GROUP_W, (g + 1) * GROUP_W)
            s_in = sb_s[:, gs]
            cm_g = act_ref[0, pl.ds(r0, L), c_off + g * D_STATE:c_off + (g + 1) * D_STATE]
            y_off.append(_dot(cm_g, s_in.astype(BF16)) * eg_x[:, gs])
            bt_g = bt_s[pl.ds(pl.multiple_of((j * SSM_GROUPS + g) * D_STATE, D_STATE), D_STATE), :]
            sb_s[:, gs] = s_in * eg_x[0:1, gs] + _dot(bt_g, xwb_b[:, gs])
        y = y_s[pl.ds(r0, L), :] + jnp.concatenate(y_off, axis=1) + dskip_ref[...] * xs
        y = y * _silu(z_ref[0, pl.ds(r0, L), :].astype(F32))
        out_ref[0, pl.ds(r0, L), :] = _rms(y, gout_ref[...]).astype(BF16)
        return carry

    lax.fori_loop(0, nchunk, bwd, 0, unroll=2)
    fin_ref[0, 0] = jnp.transpose(sf_s[...])
    fin_ref[0, 1] = jnp.transpose(sb_s[...])


def _ssd(act, z, small, init, consts):
    b, s, _ = act.shape
    has_init = init is not None
    nchunk = s // CHUNK
    blk = lambda w: pl.BlockSpec((1, s, w), lambda i: (i, 0, 0))
    st_spec = pl.BlockSpec((1, 2, SSM_WIDTH, D_STATE), lambda i: (i, 0, 0, 0))
    in_specs = [blk(CONV_DIM), blk(SSM_WIDTH), blk(LANES)]
    args = [act, z, small]
    if has_init:
        in_specs.append(st_spec)
        args.append(init)
    in_specs += [_const_spec(c.shape) for c in consts]
    args += list(consts)
    scratch = [pltpu.VMEM((nchunk * SSM_GROUPS * D_STATE, CHUNK), BF16),
               pltpu.VMEM((s, SSM_WIDTH), F32),
               pltpu.VMEM((s, SSM_WIDTH), BF16),
               pltpu.VMEM((s, SSM_WIDTH), F32),
               pltpu.VMEM((D_STATE, SSM_WIDTH), F32),
               pltpu.VMEM((D_STATE, SSM_WIDTH), F32)]
    return pl.pallas_call(
        functools.partial(_ssd_kernel, seq=s, has_init=has_init),
        grid=(b,), in_specs=in_specs,
        out_specs=[blk(SSM_WIDTH), st_spec],
        out_shape=[jax.ShapeDtypeStruct((b, s, SSM_WIDTH), BF16),
                   jax.ShapeDtypeStruct((b, 2, SSM_WIDTH, D_STATE), F32)],
        scratch_shapes=scratch,
        compiler_params=_params(("parallel",)),
        name="ssd_latent" if has_init else "ssd_ctx",
    )(*args)


SSD_GROUP = LANES // N_DT


def _ssd2_kernel(*refs, nseq, cps, has_init, n_cast):
    refs = list(refs)
    n_in = 3 + int(has_init) + 6
    cast_in = refs[n_in:n_in + n_cast]
    cast_out = refs[n_in + n_cast + 2:n_in + 2 * n_cast + 2]
    for src, dst in zip(cast_in, cast_out):
        dst[...] = src[...].astype(BF16)
    refs = refs[:n_in] + refs[n_in + n_cast:n_in + n_cast + 2] + refs[n_in + 2 * n_cast + 2:]
    if has_init:
        (act_ref, z_ref, small_ref, init_ref, dtbias_ref, alog_ref, dskip_ref, gout_ref,
         tril_ref, e64_ref, out_ref, fin_ref,
         fg_s, wsm_s, rt_s, bt_s, y_s, xwb_s, eg_s, sf_s, sb_s) = refs
    else:
        (act_ref, z_ref, small_ref, dtbias_ref, alog_ref, dskip_ref, gout_ref,
         tril_ref, e64_ref, out_ref, fin_ref,
         fg_s, wsm_s, rt_s, bt_s, y_s, xwb_s, eg_s, sf_s, sb_s) = refs
    L = CHUNK
    G = nseq * cps
    heads_per_group = SSM_HEADS // SSM_GROUPS
    b_off = SSM_WIDTH
    c_off = SSM_WIDTH + SSM_GROUPS * D_STATE

    lane = lax.broadcasted_iota(jnp.int32, (L, LANES), 1)
    packed = jnp.zeros((L, LANES), F32)
    for c in range(G):
        raw = small_ref[c // cps, (c % cps) * L:(c % cps + 1) * L, :]
        shifted = pltpu.roll(raw, (c * N_DT - DT_OFF) % LANES, axis=1)
        packed = jnp.where((lane >= c * N_DT) & (lane < (c + 1) * N_DT), shifted, packed)
    dtc = jax.nn.softplus(packed + dtbias_ref[...])
    da = dtc * (-jnp.exp(alog_ref[...]))
    cum = _dot3_rhs(tril_ref[...], da)
    tot = cum[L - 1:L, :]
    suf = tot - cum + da
    fg = jnp.where(jnp.bitwise_and(lane, SSM_HEADS) == 0, cum, suf)
    w_small = dtc * jnp.exp(tot - fg)
    e_small = jnp.exp(fg)
    rt_s[...] = jnp.transpose(fg - jnp.log(dtc))
    for c in range(G):
        back = (LANES - c * N_DT) % LANES
        unroll = lambda t: t if back == 0 else pltpu.roll(t, back, axis=1)
        fg_s[c] = unroll(fg)
        wsm_s[c, 0:L, :] = unroll(w_small).astype(BF16)
        wsm_s[c, L:2 * L, :] = unroll(e_small).astype(BF16)

    for s in range(nseq):
        if has_init:
            sf_s[s] = jnp.transpose(init_ref[s, 0])
            sb_s[s] = jnp.transpose(init_ref[s, 1])
        else:
            sf_s[s] = jnp.zeros((D_STATE, SSM_WIDTH), F32)
            sb_s[s] = jnp.zeros((D_STATE, SSM_WIDTH), F32)

    row_t = lax.broadcasted_iota(jnp.int32, (L, L), 0)
    col_s = lax.broadcasted_iota(jnp.int32, (L, L), 1)
    lower = col_s <= row_t
    upper = col_s >= row_t
    lane_lo = lane < SSM_HEAD_DIM
    neg_inf = jnp.float32(-jnp.inf)

    def locate(c):
        sq = c // cps
        r0 = pl.multiple_of((c - sq * cps) * L, L)
        return sq, r0

    def fwd(c, carry):
        sq, r0 = locate(c)
        f0 = pl.multiple_of(c * L, L)
        xs_b = act_ref[sq, pl.ds(r0, L), 0:SSM_WIDTH]
        fgc = fg_s[c]
        rtc = rt_s[pl.ds(pl.multiple_of(c * N_DT, N_DT), N_DT), :]
        ex = _dot(wsm_s[c], e64_ref[...])
        w_b = ex[0:L, :].astype(BF16)
        ef_x = ex[L:2 * L, :SSM_WIDTH]
        eg_x = ex[L:2 * L, SSM_WIDTH:]
        eg_s[pl.ds(f0, L), :] = eg_x
        xwf_b = xs_b * w_b[:, :SSM_WIDTH]
        xwb_s[pl.ds(f0, L), :] = xs_b * w_b[:, SSM_WIDTH:]

        y_parts = []
        new_states = []
        for g in range(SSM_GROUPS):
            gs = slice(g * GROUP_W, (g + 1) * GROUP_W)
            bm_g = act_ref[sq, pl.ds(r0, L), b_off + g * D_STATE:b_off + (g + 1) * D_STATE]
            cm_g = act_ref[sq, pl.ds(r0, L), c_off + g * D_STATE:c_off + (g + 1) * D_STATE]
            bt_g = jnp.transpose(bm_g.astype(F32)).astype(BF16)
            bt_s[pl.ds(pl.multiple_of((c * SSM_GROUPS + g) * D_STATE, D_STATE), D_STATE), :] = bt_g
            cbm = _dot_nt(cm_g, bm_g)
            for pair in range(heads_per_group // 2):
                ws = []
                for k in range(2):
                    cf = g * heads_per_group + pair * 2 + k
                    seg_f = fgc[:, cf:cf + 1] - rtc[cf:cf + 1, :]
                    lf = jnp.exp(jnp.where(lower, seg_f, neg_inf))
                    cg = cf + SSM_HEADS
                    seg_b = fgc[:, cg:cg + 1] - rtc[cg:cg + 1, :]
                    ub = jnp.exp(jnp.where(upper, seg_b, neg_inf))
                    ws.append((cbm * (lf + ub)).astype(BF16))
                p0 = (g * heads_per_group + pair * 2) * SSM_HEAD_DIM
                xpair = xs_b[:, p0:p0 + LANES]
                zeros = jnp.zeros_like(xpair)
                rhs = jnp.concatenate([jnp.where(lane_lo, xpair, zeros),
                                       jnp.where(lane_lo, zeros, xpair)], axis=0)
                y_parts.append(_dot(jnp.concatenate(ws, axis=1), rhs))
            s_in = sf_s[sq, :, gs]
            y_off = _dot(cm_g, s_in.astype(BF16)) * ef_x[:, gs]
            y_parts[-2] = y_parts[-2] + y_off[:, :LANES]
            y_parts[-1] = y_parts[-1] + y_off[:, LANES:]
            new_states.append(s_in * ef_x[L - 1:L, gs] + _dot(bt_g, xwf_b[:, gs]))
        for g in range(SSM_GROUPS):
            sf_s[sq, :, g * GROUP_W:(g + 1) * GROUP_W] = new_states[g]
        for i, yp in enumerate(y_parts):
            y_s[pl.ds(f0, L), i * LANES:(i + 1) * LANES] = yp
        return carry

    lax.fori_loop(0, G, fwd, 0, unroll=2)

    def bwd(cc, carry):
        c = G - 1 - cc
        sq, r0 = locate(c)
        f0 = pl.multiple_of(c * L, L)
        xs = act_ref[sq, pl.ds(r0, L), 0:SSM_WIDTH].astype(F32)
        eg_x = eg_s[pl.ds(f0, L), :]
        xwb_b = xwb_s[pl.ds(f0, L), :]
        y_off = []
        for g in range(SSM_GROUPS):
            gs = slice(g * GROUP_W, (g + 1) * GROUP_W)
            s_in = sb_s[sq, :, gs]
            cm_g = act_ref[sq, pl.ds(r0, L), c_off + g * D_STATE:c_off + (g + 1) * D_STATE]
            y_off.append(_dot(cm_g, s_in.astype(BF16)) * eg_x[:, gs])
            bt_g = bt_s[pl.ds(pl.multiple_of((c * SSM_GROUPS + g) * D_STATE, D_STATE), D_STATE), :]
            sb_s[sq, :, gs] = s_in * eg_x[0:1, gs] + _dot(bt_g, xwb_b[:, gs])
        y = y_s[pl.ds(f0, L), :] + jnp.concatenate(y_off, axis=1) + dskip_ref[...] * xs
        y = y * _silu(z_ref[sq, pl.ds(r0, L), :].astype(F32))
        out_ref[sq, pl.ds(r0, L), :] = _rms(y, gout_ref[...]).astype(BF16)
        return carry

    lax.fori_loop(0, G, bwd, 0, unroll=2)
    for s in range(nseq):
        fin_ref[s, 0] = jnp.transpose(sf_s[s])
        fin_ref[s, 1] = jnp.transpose(sb_s[s])


def _ssd2(act, z, small, init, consts, cast=()):
    b, s, _ = act.shape
    has_init = init is not None
    cps = s // CHUNK
    assert SSD_GROUP % cps == 0
    nseq = SSD_GROUP // cps
    assert b % nseq == 0
    steps = b // nseq
    blk = lambda w: pl.BlockSpec((nseq, s, w), lambda i: (i, 0, 0))
    st_spec = pl.BlockSpec((nseq, 2, SSM_WIDTH, D_STATE), lambda i: (i, 0, 0, 0))
    in_specs = [blk(CONV_DIM), blk(SSM_WIDTH), blk(LANES)]
    args = [act, z, small]
    if has_init:
        in_specs.append(st_spec)
        args.append(init)
    in_specs += [_const_spec(c.shape) for c in consts]
    args += list(consts)
    cast_specs = [pl.BlockSpec((w.shape[0] // steps, w.shape[1]), lambda i: (i, 0)) for w in cast]
    in_specs += cast_specs
    args += list(cast)
    rows = SSD_GROUP * CHUNK
    scratch = [pltpu.VMEM((SSD_GROUP, CHUNK, LANES), F32),
               pltpu.VMEM((SSD_GROUP, 2 * CHUNK, LANES), BF16),
               pltpu.VMEM((LANES, CHUNK), F32),
               pltpu.VMEM((SSD_GROUP * SSM_GROUPS * D_STATE, CHUNK), BF16),
               pltpu.VMEM((rows, SSM_WIDTH), F32),
               pltpu.VMEM((rows, SSM_WIDTH), BF16),
               pltpu.VMEM((rows, SSM_WIDTH), F32),
               pltpu.VMEM((nseq, D_STATE, SSM_WIDTH), F32),
               pltpu.VMEM((nseq, D_STATE, SSM_WIDTH), F32)]
    return pl.pallas_call(
        functools.partial(_ssd2_kernel, nseq=nseq, cps=cps, has_init=has_init, n_cast=len(cast)),
        grid=(steps,), in_specs=in_specs,
        out_specs=[blk(SSM_WIDTH), st_spec] + cast_specs,
        out_shape=[jax.ShapeDtypeStruct((b, s, SSM_WIDTH), BF16),
                   jax.ShapeDtypeStruct((b, 2, SSM_WIDTH, D_STATE), F32)]
                  + [jax.ShapeDtypeStruct(w.shape, BF16) for w in cast],
        scratch_shapes=scratch,
        compiler_params=_params(("parallel",)),
        name="ssd_latent" if has_init else "ssd_ctx",
    )(*args)


FF_CHUNK = 256
ROW_SPLIT = 1


def _outffn_kernel(x_ref, attn_ref, ssm_ref, mod_ref, wout_ref, gpost_ref, gpre_ref, gpostf_ref,
                   wg_ref, wu_ref, wd_ref, o_ref, mix_s, y_s, h_s):
    m = mod_ref[0]
    half = MLA_HEADS * V_HEAD
    rows = x_ref.shape[1] // 2
    nslice = 8
    srows = rows // nslice
    nchunks = D_FF // FF_CHUNK
    g1 = gpost_ref[...] * m[2:3]
    g2 = gpre_ref[...] * (1.0 + m[4:5])
    sh2 = m[3:4]
    g3 = gpostf_ref[...] * m[5:6]

    def out_proj(rs):
        return _dot(attn_ref[0, rs, :], wout_ref[0:half, :]) + _dot(ssm_ref[0, rs, :], wout_ref[half:, :])

    def pre_ffn(mix, xr):
        y = xr + _rms(mix, g1)
        return y, (_rms(y, g2) + sh2).astype(BF16)

    def ffn_chunk(h, c, acc):
        cs = slice(c * FF_CHUNK, (c + 1) * FF_CHUNK)
        part = _dot((_silu(_dot(h, wg_ref[:, cs])) * _dot(h, wu_ref[:, cs])).astype(BF16), wd_ref[cs, :])
        return part if acc is None else acc + part

    y_a, h_a = pre_ffn(out_proj(slice(0, rows)), x_ref[0, 0:rows, :])
    mix_s[...] = out_proj(slice(rows, 2 * rows))
    acc_a = None
    for c in range(nchunks):
        acc_a = ffn_chunk(h_a, c, acc_a)
        if c < nslice:
            rs = slice(c * srows, (c + 1) * srows)
            y_b, h_b = pre_ffn(mix_s[rs, :], x_ref[0, rows + c * srows:rows + (c + 1) * srows, :])
            y_s[rs, :] = y_b
            h_s[rs, :] = h_b
    h_bb = h_s[...]
    acc_b = None
    for c in range(nchunks):
        acc_b = ffn_chunk(h_bb, c, acc_b)
        if c < nslice:
            rs = slice(c * srows, (c + 1) * srows)
            o_ref[0, rs, :] = y_a[rs] + _rms(acc_a[rs], g3)
    o_ref[0, rows:2 * rows, :] = y_s[...] + _rms(acc_b, g3)


def _outffn(x, attn, ssm, mod, mod_off, wout, gpost, gpre, gpostf, wg, wu, wd, tm):
    b, s, d = x.shape
    tok = lambda w: pl.BlockSpec((1, tm, w), lambda i, j: (i, j, 0))
    consts = [wout, gpost, gpre, gpostf, wg, wu, wd]
    return pl.pallas_call(
        _outffn_kernel,
        grid=(b, s // tm),
        in_specs=[tok(d), tok(MLA_HEADS * V_HEAD), tok(SSM_WIDTH),
                  pl.BlockSpec((1, 6, d), lambda i, j: (i + mod_off, 0, 0))]
                 + [_const_spec(c.shape) for c in consts],
        out_specs=tok(d),
        out_shape=jax.ShapeDtypeStruct((b, s, d), F32),
        scratch_shapes=[pltpu.VMEM((tm // 2, d), F32),
                        pltpu.VMEM((tm // 2, d), F32),
                        pltpu.VMEM((tm // 2, d), BF16)],
        compiler_params=_params(("parallel", "parallel")),
        name="out_ffn",
    )(x, attn, ssm, mod, *consts)


def _rope_tables(length):
    quarter = QK_ROPE // 4
    pos = np.arange(length)
    inv_freq = ROPE_THETA ** (-np.arange(quarter, dtype=np.float64) / quarter)
    ang_r = (pos // GRID_W)[:, None] * inv_freq[None, :]
    ang_c = (pos % GRID_W)[:, None] * inv_freq[None, :]
    cos = np.concatenate([np.cos(ang_r)] * 2 + [np.cos(ang_c)] * 2, axis=1)
    sin = np.concatenate([-np.sin(ang_r), np.sin(ang_r), -np.sin(ang_c), np.sin(ang_c)], axis=1)
    return jnp.asarray(np.tile(cos, (1, 2)), F32), jnp.asarray(np.tile(sin, (1, 2)), F32)


def _pad_cols(w, width, left=0):
    return jnp.pad(w, ((0, 0), (left, width - left - w.shape[-1])))


def _ssd_constants():
    idx = np.arange(CHUNK)
    tril = (idx[None, :] <= idx[:, None]).astype(np.float32)
    e64 = np.zeros((LANES, N_DT * SSM_HEAD_DIM), np.float32)
    for q in range(N_DT):
        e64[q, q * SSM_HEAD_DIM:(q + 1) * SSM_HEAD_DIM] = 1.0
    return [jnp.asarray(a, BF16) for a in (tril, e64)]


def kernel(x_prompt, x_sample, cache_ckv, cache_krope, state_ssm, c, c_ctx, w_mod, b_mod,
           g_pre_mix, g_post_mix, w_in, g_q, w_uq, g_kv, w_ukv, conv_w, conv_b, dt_bias,
           a_log, d_skip, g_ssm_out, w_out, g_pre_ffn, g_post_ffn, w_gate, w_up, w_down):
    depth = w_in.shape[0]
    assert depth == 1
    nb, seq, d = x_prompt.shape
    db, dseq, _ = x_sample.shape
    l = 0

    rows = 16
    cond = jnp.concatenate([c_ctx[None, :], c, jnp.zeros((rows - 1 - db, d), F32)], axis=0)
    mod = _modulation(cond, w_mod[l], b_mod[l][None, :]).reshape(rows, 6, d)

    wi = w_in[l]
    o_q, o_kv, o_kr = Q_LORA, Q_LORA + KV_LORA, Q_LORA + KV_LORA + QK_ROPE
    o_z, o_xbc = o_kr + SSM_WIDTH, o_kr + SSM_WIDTH + CONV_DIM
    kr_dt = _pad_cols(jnp.concatenate([wi[:, o_kv:o_kr], wi[:, o_xbc:]], axis=1), LANES)
    win = jnp.concatenate([wi[:, :o_q], wi[:, o_q:o_kv], wi[:, o_kr:o_z], wi[:, o_z:o_xbc], kr_dt],
                          axis=1).astype(BF16)
    wq = w_uq[l].reshape(Q_LORA, MLA_HEADS, QK_NOPE + QK_ROPE)
    wuq = jnp.concatenate([wq[:, :, :QK_NOPE].reshape(Q_LORA, MLA_HEADS * QK_NOPE),
                           wq[:, :, QK_NOPE:].reshape(Q_LORA, MLA_HEADS * QK_ROPE)], axis=1).astype(BF16)
    wukv = w_ukv[l].astype(BF16)
    wout = w_out[l].astype(BF16)
    row = lambda v: v.reshape(1, -1)
    gpre, gq, gkv = row(g_pre_mix[l]), row(g_q[l]), row(g_kv[l])
    gpost, gpre_f, gpost_f = row(g_post_mix[l]), row(g_pre_ffn[l]), row(g_post_ffn[l])
    cw, cb = conv_w[l], row(conv_b[l])

    small = lambda v: jnp.tile(v.reshape(1, -1), (1, SSD_GROUP))
    ssd_consts = [small(dt_bias[l]), small(a_log[l]),
                  row(jnp.repeat(d_skip[l], SSM_HEAD_DIM)), row(g_ssm_out[l])] + _ssd_constants()

    xp = x_prompt.reshape(1, nb * seq, d)
    q, k, v, z, act, sm, ckv, krope = _inproj(xp, mod, 0, gpre, win, gq, wuq, gkv, wukv, cw, cb,
                                              None, tm=1024, seq_rows=seq)
    per_seq = lambda a: a.reshape(nb, seq, a.shape[-1])
    attn = _attention(per_seq(q), per_seq(k), per_seq(v), None, tq=seq, bs=4)
    ssm, fin, wg, wu, wd = _ssd2(per_seq(act), per_seq(z), per_seq(sm), None, ssd_consts,
                                 cast=(w_gate[l], w_up[l], w_down[l]))
    y_p = _outffn(xp, attn.reshape(1, nb * seq, -1), ssm.reshape(1, nb * seq, -1), mod, 0,
                  wout, gpost, gpre_f, gpost_f, wg, wu, wd, tm=1024).reshape(nb, seq, d)
    new_ckv = ckv.reshape(nb, 1, seq, KV_LORA)
    new_krope = krope.reshape(nb, 1, seq, QK_ROPE)
    new_ssm = fin.reshape(nb, 1, 2, SSM_HEADS, SSM_HEAD_DIM, D_STATE)

    kr_cache = jnp.pad(cache_krope[:, l], ((0, 0), (0, 0), (0, LANES - QK_ROPE)))
    kc, vc = _ctxkv(cache_ckv[:, l], kr_cache, wukv)
    q, k, v, z, act, sm = _inproj(x_sample, mod, 1, gpre, win, gq, wuq, gkv, wukv, cw, cb,
                                  _rope_tables(dseq), tm=1024, seq_rows=dseq)
    attn = _attention(q, k, v, (kc, vc), tq=1024, bs=1)
    init = state_ssm[:, l].reshape(db, 2, SSM_WIDTH, D_STATE)
    ssm, _ = _ssd2(act, z, sm, init, ssd_consts)
    y_s = _outffn(x_sample, attn, ssm, mod, 1, wout, gpost, gpre_f, gpost_f, wg, wu, wd, tm=1024)

    return (y_p, y_s, new_ckv, new_krope, new_ssm)
```

```python
import functools

import numpy as np
import jax
import jax.numpy as jnp
from jax import lax
from jax.experimental import pallas as pl
from jax.experimental.pallas import tpu as pltpu

F32 = jnp.float32
BF16 = jnp.bfloat16

D_MODEL = 1024
GRID_W = 64
ROPE_THETA = 10000.0
NORM_EPS = 1e-6
MLA_HEADS = 4
QK_NOPE = 128
QK_ROPE = 64
V_HEAD = 128
Q_LORA = 384
KV_LORA = 256
SSM_HEADS = 8
SSM_HEAD_DIM = 64
SSM_WIDTH = SSM_HEADS * SSM_HEAD_DIM
SSM_GROUPS = 2
D_STATE = 128
D_CONV = 3
CHUNK = 128
CONV_DIM = SSM_WIDTH + 2 * SSM_GROUPS * D_STATE
D_FF = 2816

LOG2E = 1.4426950408889634
LANES = 128
SUBLANES = 8
BF16_ROWS = 16
QK_PAD = 256
GROUP_W = SSM_WIDTH // SSM_GROUPS
VMEM_LIMIT = 56 * 1024 * 1024

_SEG_Q = (0, 384)
_SEG_KV = (384, 640)
_SEG_Z = (640, 1152)
_SEG_XBC = (1152, 2176)
_SEG_KR = (2176, 2304)
DT_OFF = QK_ROPE
N_DT = 2 * SSM_HEADS


def _rms(x, g):
    return x * lax.rsqrt(jnp.mean(x * x, axis=-1, keepdims=True) + NORM_EPS) * g


def _silu(x):
    u = 0.5 * x
    return u * jnp.tanh(u) + u


def _dot(a, b):
    return jnp.dot(a, b, preferred_element_type=F32)


def _dot_nt(a, b):
    return lax.dot_general(a, b, (((1,), (1,)), ((), ())), preferred_element_type=F32)


def _split3(x):
    hi = x.astype(BF16)
    r = x - hi.astype(F32)
    mid = r.astype(BF16)
    lo = (r - mid.astype(F32)).astype(BF16)
    return hi, mid, lo


def _dot3_rhs(m, x):
    hi, mid, lo = _split3(x)
    return _dot(m, hi) + _dot(m, mid) + _dot(m, lo)


def _const_spec(shape):
    nd = len(shape)
    return pl.BlockSpec(shape, lambda *_: (0,) * nd, pipeline_mode=pl.Buffered(1))


def _params(sem):
    return pltpu.CompilerParams(dimension_semantics=sem, vmem_limit_bytes=VMEM_LIMIT)


def _mod_kernel(c_ref, w_ref, b_ref, o_ref):
    c = c_ref[...]
    s = _silu(c)
    w = w_ref[...]
    s_hi = s.astype(BF16)
    s_lo = (s - s_hi.astype(F32)).astype(BF16)
    w_hi = w.astype(BF16)
    w_lo = (w - w_hi.astype(F32)).astype(BF16)
    o_ref[...] = _dot(s_hi, w_hi) + _dot(s_lo, w_hi) + _dot(s_hi, w_lo) + b_ref[...]


def _modulation(cond, w_mod, b_mod):
    rows, d = cond.shape
    n = w_mod.shape[1]
    tn = 1536
    return pl.pallas_call(
        _mod_kernel,
        grid=(n // tn,),
        in_specs=[pl.BlockSpec((rows, d), lambda j: (0, 0)),
                  pl.BlockSpec((d, tn), lambda j: (0, j)),
                  pl.BlockSpec((1, tn), lambda j: (0, j))],
        out_specs=pl.BlockSpec((rows, tn), lambda j: (0, j)),
        out_shape=jax.ShapeDtypeStruct((rows, n), F32),
        compiler_params=_params(("arbitrary",)),
        name="modulation",
    )(cond, w_mod, b_mod)


def _conv_silu(xb, prev_rows, next_rows, cw_ref, cb_ref):
    n = xb.shape[0]
    win = jnp.concatenate([prev_rows, xb, next_rows], axis=0)
    total = n + 2 * SUBLANES
    prev = pltpu.roll(win, 1, axis=0)[SUBLANES:SUBLANES + n]
    nxt = pltpu.roll(win, total - 1, axis=0)[SUBLANES:SUBLANES + n]
    conv = cb_ref[...] + prev * cw_ref[0:1, :] + xb * cw_ref[1:2, :] + nxt * cw_ref[2:3, :]
    return _silu(conv)


def _inproj_kernel(*refs, latent, seq_rows):
    if latent:
        (x_ref, mod_ref, gpre_ref, win_ref, gq_ref, wuq_ref, gkv_ref, wukv_ref, cw_ref, cb_ref,
         cos_ref, sin_ref, q_ref, k_ref, v_ref, z_ref, act_ref, small_ref) = refs
    else:
        (x_ref, mod_ref, gpre_ref, win_ref, gq_ref, wuq_ref, gkv_ref, wukv_ref, cw_ref, cb_ref,
         q_ref, k_ref, v_ref, z_ref, act_ref, small_ref, ckv_ref, krope_ref) = refs
    scale = (QK_NOPE + QK_ROPE) ** -0.5 * LOG2E
    m = mod_ref[0]

    g_mod = gpre_ref[...] * (1.0 + m[1:2])

    def pre(xv):
        return (_rms(xv, g_mod) + m[0:1]).astype(BF16)

    tm = x_ref.shape[1]
    rows = tm // 2
    halves = (slice(0, rows), slice(rows, tm))
    assert seq_rows == tm or rows % seq_rows == 0
    zero_rows = jnp.zeros((SUBLANES, CONV_DIM), F32)
    projs = [_dot(pre(x_ref[0, halves[0], :]), win_ref[...])]
    if seq_rows == tm:
        ph = _dot(pre(x_ref[0, rows - SUBLANES:rows + SUBLANES, :]), win_ref[:, _SEG_XBC[0]:_SEG_XBC[1]])
        halo = ((zero_rows, ph[SUBLANES:]), (ph[:SUBLANES], zero_rows))
    projs.append(_dot(pre(x_ref[0, halves[1], :]), win_ref[...]))

    lane = lax.broadcasted_iota(jnp.int32, (rows, LANES), 1)
    rope_lanes = lane < QK_ROPE
    first_quarter = jnp.bitwise_and(lane, QK_ROPE // 4) == 0
    nw = MLA_HEADS * QK_NOPE
    for hf, rs in enumerate(halves):
        proj = projs[hf]
        seg = lambda s: proj[:, s[0]:s[1]]
        xbc = seg(_SEG_XBC)
        if seq_rows == tm:
            act_ref[0, rs, :] = _conv_silu(xbc, halo[hf][0], halo[hf][1], cw_ref, cb_ref).astype(BF16)
        else:
            for s in range(rows // seq_rows):
                sub = slice(s * seq_rows, (s + 1) * seq_rows)
                dst = slice(rs.start + sub.start, rs.start + sub.stop)
                act_ref[0, dst, :] = _conv_silu(xbc[sub], zero_rows, zero_rows, cw_ref, cb_ref).astype(BF16)
        krb = seg(_SEG_KR)
        small_ref[0, rs, :] = krb
        ckv = _rms(seg(_SEG_KV), gkv_ref[...])
        qall = _dot(_rms(seg(_SEG_Q), gq_ref[...]).astype(BF16), wuq_ref[...])
        kv = _dot(ckv.astype(BF16), wukv_ref[...])
        z_ref[0, rs, :] = seg(_SEG_Z).astype(BF16)
        if latent:
            cos = cos_ref[rs, :]
            sin = sin_ref[rs, :]

            def rot(t):
                swapped = jnp.where(first_quarter, pltpu.roll(t, LANES - QK_ROPE // 4, axis=1),
                                    pltpu.roll(t, QK_ROPE // 4, axis=1))
                return t * cos + swapped * sin
        else:
            rot = lambda t: t
            ckv_ref[0, rs, :] = ckv
            krope_ref[0, rs, :] = krb[:, :QK_ROPE]
        kr_b = jnp.where(rope_lanes, rot(krb), 0.0).astype(BF16)
        for pr in range(MLA_HEADS // 2):
            pair = rot(qall[:, nw + pr * LANES:nw + (pr + 1) * LANES]) * scale
            for k in range(2):
                hd = 2 * pr + k
                base = hd * QK_PAD
                qr = pair if k == 0 else pltpu.roll(pair, QK_ROPE, axis=1)
                q_ref[0, rs, base:base + LANES] = (qall[:, hd * LANES:(hd + 1) * LANES] * scale).astype(BF16)
                q_ref[0, rs, base + LANES:base + QK_PAD] = jnp.where(rope_lanes, qr, 0.0).astype(BF16)
                k_ref[0, rs, base:base + LANES] = kv[:, base:base + LANES].astype(BF16)
                k_ref[0, rs, base + LANES:base + QK_PAD] = kr_b
                v_ref[0, rs, hd * LANES:(hd + 1) * LANES] = kv[:, base + LANES:base + QK_PAD].astype(BF16)


def _inproj(x, mod, mod_off, gpre, win, gq, wuq, gkv, wukv, cw, cb, rope, tm, seq_rows):
    b, s, d = x.shape
    latent = rope is not None
    grid = (b, s // tm)
    tok = lambda w: pl.BlockSpec((1, tm, w), lambda i, j: (i, j, 0))
    mod_spec = pl.BlockSpec((1, 6, d), lambda i, j: (i + mod_off, 0, 0))
    consts = [gpre, win, gq, wuq, gkv, wukv, cw, cb]
    in_specs = [tok(d), mod_spec]
    args = [x, mod]
    in_specs += [_const_spec(c.shape) for c in consts]
    args += consts
    out_shape = [jax.ShapeDtypeStruct((b, s, MLA_HEADS * QK_PAD), BF16),
                 jax.ShapeDtypeStruct((b, s, MLA_HEADS * QK_PAD), BF16),
                 jax.ShapeDtypeStruct((b, s, MLA_HEADS * V_HEAD), BF16),
                 jax.ShapeDtypeStruct((b, s, SSM_WIDTH), BF16),
                 jax.ShapeDtypeStruct((b, s, CONV_DIM), BF16),
                 jax.ShapeDtypeStruct((b, s, LANES), F32)]
    out_specs = [tok(MLA_HEADS * QK_PAD), tok(MLA_HEADS * QK_PAD), tok(MLA_HEADS * V_HEAD),
                 tok(SSM_WIDTH), tok(CONV_DIM), tok(LANES)]
    if latent:
        in_specs += [pl.BlockSpec((tm, LANES), lambda i, j: (j, 0))] * 2
        args += list(rope)
    else:
        out_shape += [jax.ShapeDtypeStruct((b, s, KV_LORA), F32),
                      jax.ShapeDtypeStruct((b, s, QK_ROPE), F32)]
        out_specs += [tok(KV_LORA), tok(QK_ROPE)]
    return pl.pallas_call(
        functools.partial(_inproj_kernel, latent=latent, seq_rows=seq_rows),
        grid=grid, in_specs=in_specs, out_specs=out_specs, out_shape=out_shape,
        compiler_params=_params(("parallel", "parallel")),
        name="inproj_latent" if latent else "inproj_ctx",
    )(*args)


def _ctxkv_kernel(ckv_ref, kr_ref, wukv_ref, k_ref, v_ref):
    kv = _dot(ckv_ref[0].astype(BF16), wukv_ref[...])
    kr_b = kr_ref[0].astype(BF16)
    for hd in range(MLA_HEADS):
        base = hd * QK_PAD
        k_ref[0, :, base:base + LANES] = kv[:, base:base + LANES].astype(BF16)
        k_ref[0, :, base + LANES:base + QK_PAD] = kr_b
        v_ref[0, :, hd * LANES:(hd + 1) * LANES] = kv[:, base + LANES:base + QK_PAD].astype(BF16)


def _ctxkv(ckv, kr_pad, wukv):
    b, s, _ = ckv.shape
    n = b * s
    tm = min(n, 1024)
    blk = lambda w: pl.BlockSpec((1, tm, w), lambda i: (0, i, 0))
    k, v = pl.pallas_call(
        _ctxkv_kernel,
        grid=(n // tm,),
        in_specs=[blk(KV_LORA), blk(LANES), _const_spec(wukv.shape)],
        out_specs=[blk(MLA_HEADS * QK_PAD), blk(MLA_HEADS * V_HEAD)],
        out_shape=[jax.ShapeDtypeStruct((1, n, MLA_HEADS * QK_PAD), BF16),
                   jax.ShapeDtypeStruct((1, n, MLA_HEADS * V_HEAD), BF16)],
        compiler_params=_params(("parallel",)),
        name="ctx_kv",
    )(ckv.reshape(1, n, KV_LORA), kr_pad.reshape(1, n, LANES), wukv)
    return k.reshape(b, s, -1), v.reshape(b, s, -1)


Q_SPLIT = 1


def _attn_kernel(*refs, latent):
    if latent:
        q_ref, k_ref, v_ref, kc_ref, vc_ref, o_ref = refs
    else:
        q_ref, k_ref, v_ref, o_ref = refs
    tq = q_ref.shape[1]
    qsplit = Q_SPLIT if latent else 1
    qrows = tq // qsplit
    items = [(sq, hd, qi) for sq in range(q_ref.shape[0]) for hd in range(MLA_HEADS)
             for qi in range(qsplit)]

    def scores(item):
        sq, hd, qi = item
        qs = slice(hd * QK_PAD, (hd + 1) * QK_PAD)
        q = q_ref[sq, qi * qrows:(qi + 1) * qrows, qs]
        if latent:
            return _dot_nt(k_ref[sq, :, qs], q), _dot_nt(kc_ref[sq, :, qs], q)
        return (_dot_nt(q, k_ref[sq, :, qs]),)

    nxt = scores(items[0])
    v_t = vc_t = None
    for n, (sq, hd, qi) in enumerate(items):
        cur = nxt
        if n + 1 < len(items):
            nxt = scores(items[n + 1])
        vs = slice(hd * V_HEAD, (hd + 1) * V_HEAD)
        if latent:
            s, sc = cur
            mx = jnp.maximum(jnp.max(s, axis=0, keepdims=True), jnp.max(sc, axis=0, keepdims=True))
            p = jnp.exp2(s - mx)
            pc = jnp.exp2(sc - mx)
            if qi == 0:
                ones = lambda n: jnp.ones((BF16_ROWS, n), BF16)
                v_t = jnp.concatenate([jnp.transpose(v_ref[sq, :, vs].astype(F32)).astype(BF16),
                                       ones(v_ref.shape[1])], axis=0)
                vc_t = jnp.concatenate([jnp.transpose(vc_ref[sq, :, vs].astype(F32)).astype(BF16),
                                        ones(vc_ref.shape[1])], axis=0)
            acc = _dot(v_t, p.astype(BF16)) + _dot(vc_t, pc.astype(BF16))
            out = acc[:V_HEAD] / acc[V_HEAD:V_HEAD + 1]
            o_ref[sq, qi * qrows:(qi + 1) * qrows, vs] = jnp.transpose(out).astype(BF16)
        else:
            s, = cur
            p = jnp.exp2(s - jnp.max(s, axis=-1, keepdims=True))
            den = jnp.sum(p, axis=-1, keepdims=True)
            o_ref[sq, :, vs] = (_dot(p.astype(BF16), v_ref[sq, :, vs]) / den).astype(BF16)


def _attention(q, k, v, ctx, tq, bs):
    b, s, _ = q.shape
    latent = ctx is not None
    qw, vw = MLA_HEADS * QK_PAD, MLA_HEADS * V_HEAD
    in_specs = [pl.BlockSpec((bs, tq, qw), lambda i, j: (i, j, 0)),
                pl.BlockSpec((bs, s, qw), lambda i, j: (i, 0, 0)),
                pl.BlockSpec((bs, s, vw), lambda i, j: (i, 0, 0))]
    args = [q, k, v]
    if latent:
        sc = ctx[0].shape[1]
        in_specs += [pl.BlockSpec((bs, sc, qw), lambda i, j: (i, 0, 0)),
                     pl.BlockSpec((bs, sc, vw), lambda i, j: (i, 0, 0))]
        args += list(ctx)
    return pl.pallas_call(
        functools.partial(_attn_kernel, latent=latent),
        grid=(b // bs, s // tq), in_specs=in_specs,
        out_specs=pl.BlockSpec((bs, tq, vw), lambda i, j: (i, j, 0)),
        out_shape=jax.ShapeDtypeStruct((b, s, vw), BF16),
        compiler_params=_params(("parallel", "parallel")),
        name="attn_latent" if latent else "attn_ctx",
    )(*args)


def _ssd_kernel(*refs, seq, has_init):
    if has_init:
        (act_ref, z_ref, small_ref, init_ref, dtbias_ref, alog_ref, dskip_ref, gout_ref,
         tril_ref, e64_ref, out_ref, fin_ref, bt_s, y_s, xwb_s, eg_s, sf_s, sb_s) = refs
    else:
        (act_ref, z_ref, small_ref, dtbias_ref, alog_ref, dskip_ref, gout_ref,
         tril_ref, e64_ref, out_ref, fin_ref, bt_s, y_s, xwb_s, eg_s, sf_s, sb_s) = refs
    L = CHUNK
    nchunk = seq // L
    if has_init:
        sf_s[...] = jnp.transpose(init_ref[0, 0])
        sb_s[...] = jnp.transpose(init_ref[0, 1])
    else:
        sf_s[...] = jnp.zeros_like(sf_s)
        sb_s[...] = jnp.zeros_like(sb_s)

    col = lax.broadcasted_iota(jnp.int32, (L, LANES), 1)
    dt_cols = (col >= DT_OFF) & (col < DT_OFF + N_DT)
    fwd_cols = col < DT_OFF + SSM_HEADS
    row_t = lax.broadcasted_iota(jnp.int32, (L, L), 0)
    col_s = lax.broadcasted_iota(jnp.int32, (L, L), 1)
    lower = col_s <= row_t
    upper = col_s >= row_t
    lane_lo = col < SSM_HEAD_DIM
    a_row = -jnp.exp(alog_ref[...])
    neg_inf = jnp.float32(-jnp.inf)
    b_off = SSM_WIDTH
    c_off = SSM_WIDTH + SSM_GROUPS * D_STATE

    def fwd(j, carry):
        r0 = pl.multiple_of(j * L, L)
        xs_b = act_ref[0, pl.ds(r0, L), 0:SSM_WIDTH]
        xs = xs_b.astype(F32)

        dtc = jax.nn.softplus(small_ref[0, pl.ds(r0, L), :] + dtbias_ref[...])
        da = jnp.where(dt_cols, dtc * a_row, 0.0)
        cum = _dot3_rhs(tril_ref[...], da)
        tot = cum[L - 1:L, :]
        suf = tot - cum + da
        fg = jnp.where(fwd_cols, cum, suf)
        w_small = jnp.where(dt_cols, dtc * jnp.exp(tot - fg), 0.0)
        e_small = jnp.where(dt_cols, jnp.exp(fg), 0.0)
        w_hi = w_small.astype(BF16)
        w_lo = (w_small - w_hi.astype(F32)).astype(BF16)
        ex = _dot(jnp.concatenate([w_hi, w_lo, e_small.astype(BF16)], axis=0), e64_ref[...])
        w_x = ex[0:L] + ex[L:2 * L]
        e_x = ex[2 * L:3 * L]
        r_t = jnp.transpose(fg - jnp.log(dtc))

        wf_x = w_x[:, :SSM_WIDTH]
        wb_x = w_x[:, SSM_WIDTH:]
        ef_x = e_x[:, :SSM_WIDTH]
        eg_x = e_x[:, SSM_WIDTH:]
        eg_s[pl.ds(r0, L), :] = eg_x
        xwf_b = (xs * wf_x).astype(BF16)
        xwb_s[pl.ds(r0, L), :] = (xs * wb_x).astype(BF16)

        y_parts = []
        new_states = []
        heads_per_group = SSM_HEADS // SSM_GROUPS
        for g in range(SSM_GROUPS):
            gs = slice(g * GROUP_W, (g + 1) * GROUP_W)
            bm_g = act_ref[0, pl.ds(r0, L), b_off + g * D_STATE:b_off + (g + 1) * D_STATE]
            cm_g = act_ref[0, pl.ds(r0, L), c_off + g * D_STATE:c_off + (g + 1) * D_STATE]
            bt_g = jnp.transpose(bm_g.astype(F32)).astype(BF16)
            bt_s[pl.ds(pl.multiple_of((j * SSM_GROUPS + g) * D_STATE, D_STATE), D_STATE), :] = bt_g
            cbm = _dot_nt(cm_g, bm_g)
            for pair in range(heads_per_group // 2):
                ws = []
                for k in range(2):
                    cf = DT_OFF + g * heads_per_group + pair * 2 + k
                    seg_f = fg[:, cf:cf + 1] - r_t[cf:cf + 1, :]
                    lf = jnp.exp(jnp.where(lower, seg_f, neg_inf))
                    cg = cf + SSM_HEADS
                    seg_b = fg[:, cg:cg + 1] - r_t[cg:cg + 1, :]
                    ub = jnp.exp(jnp.where(upper, seg_b, neg_inf))
                    ws.append((cbm * (lf + ub)).astype(BF16))
                p0 = (g * heads_per_group + pair * 2) * SSM_HEAD_DIM
                xpair = xs_b[:, p0:p0 + LANES]
                zeros = jnp.zeros_like(xpair)
                rhs = jnp.concatenate([jnp.where(lane_lo, xpair, zeros),
                                       jnp.where(lane_lo, zeros, xpair)], axis=0)
                y_parts.append(_dot(jnp.concatenate(ws, axis=1), rhs))
            s_in = sf_s[:, gs]
            y_off = _dot(cm_g, s_in.astype(BF16)) * ef_x[:, gs]
            y_parts[-2] = y_parts[-2] + y_off[:, :LANES]
            y_parts[-1] = y_parts[-1] + y_off[:, LANES:]
            new_states.append(s_in * ef_x[L - 1:L, gs] + _dot(bt_g, xwf_b[:, gs]))
        for g in range(SSM_GROUPS):
            sf_s[:, g * GROUP_W:(g + 1) * GROUP_W] = new_states[g]
        for i, yp in enumerate(y_parts):
            y_s[pl.ds(r0, L), i * LANES:(i + 1) * LANES] = yp
        return carry

    lax.fori_loop(0, nchunk, fwd, 0, unroll=2)

    def bwd(jj, carry):
        j = nchunk - 1 - jj
        r0 = pl.multiple_of(j * L, L)
        xs = act_ref[0, pl.ds(r0, L), 0:SSM_WIDTH].astype(F32)
        eg_x = eg_s[pl.ds(r0, L), :]
        xwb_b = xwb_s[pl.ds(r0, L), :]
        y_off = []
        for g in range(SSM_GROUPS):
            gs = slice(g * GROUP_W, (g + 1) * GROUP_W)
            s_in = sb_s[:, gs]
            cm_g = act_ref[0, pl.ds(r0, L), c_off + g * D_STATE:c_off + (g + 1) * D_STATE]
            y_off.append(_dot(cm_g, s_in.astype(BF16)) * eg_x[:, gs])
            bt_g = bt_s[pl.ds(pl.multiple_of((j * SSM_GROUPS + g) * D_STATE, D_STATE), D_STATE), :]
            sb_s[:, gs] = s_in * eg_x[0:1, gs] + _dot(bt_g, xwb_b[:, gs])
        y = y_s[pl.ds(r0, L), :] + jnp.concatenate(y_off, axis=1) + dskip_ref[...] * xs
        y = y * _silu(z_ref[0, pl.ds(r0, L), :].astype(F32))
        out_ref[0, pl.ds(r0, L), :] = _rms(y, gout_ref[...]).astype(BF16)
        return carry

    lax.fori_loop(0, nchunk, bwd, 0, unroll=2)
    fin_ref[0, 0] = jnp.transpose(sf_s[...])
    fin_ref[0, 1] = jnp.transpose(sb_s[...])


def _ssd(act, z, small, init, consts):
    b, s, _ = act.shape
    has_init = init is not None
    nchunk = s // CHUNK
    blk = lambda w: pl.BlockSpec((1, s, w), lambda i: (i, 0, 0))
    st_spec = pl.BlockSpec((1, 2, SSM_WIDTH, D_STATE), lambda i: (i, 0, 0, 0))
    in_specs = [blk(CONV_DIM), blk(SSM_WIDTH), blk(LANES)]
    args = [act, z, small]
    if has_init:
        in_specs.append(st_spec)
        args.append(init)
    in_specs += [_const_spec(c.shape) for c in consts]
    args += list(consts)
    scratch = [pltpu.VMEM((nchunk * SSM_GROUPS * D_STATE, CHUNK), BF16),
               pltpu.VMEM((s, SSM_WIDTH), F32),
               pltpu.VMEM((s, SSM_WIDTH), BF16),
               pltpu.VMEM((s, SSM_WIDTH), F32),
               pltpu.VMEM((D_STATE, SSM_WIDTH), F32),
               pltpu.VMEM((D_STATE, SSM_WIDTH), F32)]
    return pl.pallas_call(
        functools.partial(_ssd_kernel, seq=s, has_init=has_init),
        grid=(b,), in_specs=in_specs,
        out_specs=[blk(SSM_WIDTH), st_spec],
        out_shape=[jax.ShapeDtypeStruct((b, s, SSM_WIDTH), BF16),
                   jax.ShapeDtypeStruct((b, 2, SSM_WIDTH, D_STATE), F32)],
        scratch_shapes=scratch,
        compiler_params=_params(("parallel",)),
        name="ssd_latent" if has_init else "ssd_ctx",
    )(*args)


SSD_GROUP = LANES // N_DT


def _ssd2_kernel(*refs, nseq, cps, has_init, n_cast):
    refs = list(refs)
    n_in = 3 + int(has_init) + 6
    cast_in = refs[n_in:n_in + n_cast]
    cast_out = refs[n_in + n_cast + 2:n_in + 2 * n_cast + 2]
    for src, dst in zip(cast_in, cast_out):
        dst[...] = src[...].astype(BF16)
    refs = refs[:n_in] + refs[n_in + n_cast:n_in + n_cast + 2] + refs[n_in + 2 * n_cast + 2:]
    if has_init:
        (act_ref, z_ref, small_ref, init_ref, dtbias_ref, alog_ref, dskip_ref, gout_ref,
         tril_ref, e64_ref, out_ref, fin_ref,
         fg_s, wsm_s, rt_s, bt_s, y_s, xwb_s, eg_s, sf_s, sb_s) = refs
    else:
        (act_ref, z_ref, small_ref, dtbias_ref, alog_ref, dskip_ref, gout_ref,
         tril_ref, e64_ref, out_ref, fin_ref,
         fg_s, wsm_s, rt_s, bt_s, y_s, xwb_s, eg_s, sf_s, sb_s) = refs
    L = CHUNK
    G = nseq * cps
    heads_per_group = SSM_HEADS // SSM_GROUPS
    b_off = SSM_WIDTH
    c_off = SSM_WIDTH + SSM_GROUPS * D_STATE

    lane = lax.broadcasted_iota(jnp.int32, (L, LANES), 1)
    packed = jnp.zeros((L, LANES), F32)
    for c in range(G):
        raw = small_ref[c // cps, (c % cps) * L:(c % cps + 1) * L, :]
        shifted = pltpu.roll(raw, (c * N_DT - DT_OFF) % LANES, axis=1)
        packed = jnp.where((lane >= c * N_DT) & (lane < (c + 1) * N_DT), shifted, packed)
    dtc = jax.nn.softplus(packed + dtbias_ref[...])
    da = dtc * (-jnp.exp(alog_ref[...]))
    cum = _dot3_rhs(tril_ref[...], da)
    tot = cum[L - 1:L, :]
    suf = tot - cum + da
    fg = jnp.where(jnp.bitwise_and(lane, SSM_HEADS) == 0, cum, suf)
    w_small = dtc * jnp.exp(tot - fg)
    e_small = jnp.exp(fg)
    rt_s[...] = jnp.transpose(fg - jnp.log(dtc))
    for c in range(G):
        back = (LANES - c * N_DT) % LANES
        unroll = lambda t: t if back == 0 else pltpu.roll(t, back, axis=1)
        fg_s[c] = unroll(fg)
        wsm_s[c, 0:L, :] = unroll(w_small).astype(BF16)
        wsm_s[c, L:2 * L, :] = unroll(e_small).astype(BF16)

    for s in range(nseq):
        if has_init:
            sf_s[s] = jnp.transpose(init_ref[s, 0])
            sb_s[s] = jnp.transpose(init_ref[s, 1])
        else:
            sf_s[s] = jnp.zeros((D_STATE, SSM_WIDTH), F32)
            sb_s[s] = jnp.zeros((D_STATE, SSM_WIDTH), F32)

    row_t = lax.broadcasted_iota(jnp.int32, (L, L), 0)
    col_s = lax.broadcasted_iota(jnp.int32, (L, L), 1)
    lower = col_s <= row_t
    upper = col_s >= row_t
    lane_lo = lane < SSM_HEAD_DIM
    neg_inf = jnp.float32(-jnp.inf)

    def locate(c):
        sq = c // cps
        r0 = pl.multiple_of((c - sq * cps) * L, L)
        return sq, r0

    def fwd(c, carry):
        sq, r0 = locate(c)
        f0 = pl.multiple_of(c * L, L)
        xs_b = act_ref[sq, pl.ds(r0, L), 0:SSM_WIDTH]
        fgc = fg_s[c]
        rtc = rt_s[pl.ds(pl.multiple_of(c * N_DT, N_DT), N_DT), :]
        ex = _dot(wsm_s[c], e64_ref[...])
        w_b = ex[0:L, :].astype(BF16)
        ef_x = ex[L:2 * L, :SSM_WIDTH]
        eg_x = ex[L:2 * L, SSM_WIDTH:]
        eg_s[pl.ds(f0, L), :] = eg_x
        xwf_b = xs_b * w_b[:, :SSM_WIDTH]
        xwb_s[pl.ds(f0, L), :] = xs_b * w_b[:, SSM_WIDTH:]

        y_parts = []
        new_states = []
        for g in range(SSM_GROUPS):
            gs = slice(g * GROUP_W, (g + 1) * GROUP_W)
            bm_g = act_ref[sq, pl.ds(r0, L), b_off + g * D_STATE:b_off + (g + 1) * D_STATE]
            cm_g = act_ref[sq, pl.ds(r0, L), c_off + g * D_STATE:c_off + (g + 1) * D_STATE]
            bt_g = jnp.transpose(bm_g.astype(F32)).astype(BF16)
            bt_s[pl.ds(pl.multiple_of((c * SSM_GROUPS + g) * D_STATE, D_STATE), D_STATE), :] = bt_g
            cbm = _dot_nt(cm_g, bm_g)
            for pair in range(heads_per_group // 2):
                ws = []
                for k in range(2):
                    cf = g * heads_per_group + pair * 2 + k
                    seg_f = fgc[:, cf:cf + 1] - rtc[cf:cf + 1, :]
                    lf = jnp.exp(jnp.where(lower, seg_f, neg_inf))
                    cg = cf + SSM_HEADS
                    seg_b = fgc[:, cg:cg + 1] - rtc[cg:cg + 1, :]
                    ub = jnp.exp(jnp.where(upper, seg_b, neg_inf))
                    ws.append((cbm * (lf + ub)).astype(BF16))
                p0 = (g * heads_per_group + pair * 2) * SSM_HEAD_DIM
                xpair = xs_b[:, p0:p0 + LANES]
                zeros = jnp.zeros_like(xpair)
                rhs = jnp.concatenate([jnp.where(lane_lo, xpair, zeros),
                                       jnp.where(lane_lo, zeros, xpair)], axis=0)
                y_parts.append(_dot(jnp.concatenate(ws, axis=1), rhs))
            s_in = sf_s[sq, :, gs]
            y_off = _dot(cm_g, s_in.astype(BF16)) * ef_x[:, gs]
            y_parts[-2] = y_parts[-2] + y_off[:, :LANES]
            y_parts[-1] = y_parts[-1] + y_off[:, LANES:]
            new_states.append(s_in * ef_x[L - 1:L, gs] + _dot(bt_g, xwf_b[:, gs]))
        for g in range(SSM_GROUPS):
            sf_s[sq, :, g * GROUP_W:(g + 1) * GROUP_W] = new_states[g]
        for i, yp in enumerate(y_parts):
            y_s[pl.ds(f0, L), i * LANES:(i + 1) * LANES] = yp
        return carry

    lax.fori_loop(0, G, fwd, 0, unroll=True)

    def bwd(cc, carry):
        c = G - 1 - cc
        sq, r0 = locate(c)
        f0 = pl.multiple_of(c * L, L)
        xs = act_ref[sq, pl.ds(r0, L), 0:SSM_WIDTH].astype(F32)
        eg_x = eg_s[pl.ds(f0, L), :]
        xwb_b = xwb_s[pl.ds(f0, L), :]
        y_off = []
        for g in range(SSM_GROUPS):
            gs = slice(g * GROUP_W, (g + 1) * GROUP_W)
            s_in = sb_s[sq, :, gs]
            cm_g = act_ref[sq, pl.ds(r0, L), c_off + g * D_STATE:c_off + (g + 1) * D_STATE]
            y_off.append(_dot(cm_g, s_in.astype(BF16)) * eg_x[:, gs])
            bt_g = bt_s[pl.ds(pl.multiple_of((c * SSM_GROUPS + g) * D_STATE, D_STATE), D_STATE), :]
            sb_s[sq, :, gs] = s_in * eg_x[0:1, gs] + _dot(bt_g, xwb_b[:, gs])
        y = y_s[pl.ds(f0, L), :] + jnp.concatenate(y_off, axis=1) + dskip_ref[...] * xs
        y = y * _silu(z_ref[sq, pl.ds(r0, L), :].astype(F32))
        out_ref[sq, pl.ds(r0, L), :] = _rms(y, gout_ref[...]).astype(BF16)
        return carry

    lax.fori_loop(0, G, bwd, 0, unroll=True)
    for s in range(nseq):
        fin_ref[s, 0] = jnp.transpose(sf_s[s])
        fin_ref[s, 1] = jnp.transpose(sb_s[s])


def _ssd2(act, z, small, init, consts, cast=()):
    b, s, _ = act.shape
    has_init = init is not None
    cps = s // CHUNK
    assert SSD_GROUP % cps == 0
    nseq = SSD_GROUP // cps
    assert b % nseq == 0
    steps = b // nseq
    blk = lambda w: pl.BlockSpec((nseq, s, w), lambda i: (i, 0, 0))
    st_spec = pl.BlockSpec((nseq, 2, SSM_WIDTH, D_STATE), lambda i: (i, 0, 0, 0))
    in_specs = [blk(CONV_DIM), blk(SSM_WIDTH), blk(LANES)]
    args = [act, z, small]
    if has_init:
        in_specs.append(st_spec)
        args.append(init)
    in_specs += [_const_spec(c.shape) for c in consts]
    args += list(consts)
    cast_specs = [pl.BlockSpec((w.shape[0] // steps, w.shape[1]), lambda i: (i, 0)) for w in cast]
    in_specs += cast_specs
    args += list(cast)
    rows = SSD_GROUP * CHUNK
    scratch = [pltpu.VMEM((SSD_GROUP, CHUNK, LANES), F32),
               pltpu.VMEM((SSD_GROUP, 2 * CHUNK, LANES), BF16),
               pltpu.VMEM((LANES, CHUNK), F32),
               pltpu.VMEM((SSD_GROUP * SSM_GROUPS * D_STATE, CHUNK), BF16),
               pltpu.VMEM((rows, SSM_WIDTH), F32),
               pltpu.VMEM((rows, SSM_WIDTH), BF16),
               pltpu.VMEM((rows, SSM_WIDTH), F32),
               pltpu.VMEM((nseq, D_STATE, SSM_WIDTH), F32),
               pltpu.VMEM((nseq, D_STATE, SSM_WIDTH), F32)]
    return pl.pallas_call(
        functools.partial(_ssd2_kernel, nseq=nseq, cps=cps, has_init=has_init, n_cast=len(cast)),
        grid=(steps,), in_specs=in_specs,
        out_specs=[blk(SSM_WIDTH), st_spec] + cast_specs,
        out_shape=[jax.ShapeDtypeStruct((b, s, SSM_WIDTH), BF16),
                   jax.ShapeDtypeStruct((b, 2, SSM_WIDTH, D_STATE), F32)]
                  + [jax.ShapeDtypeStruct(w.shape, BF16) for w in cast],
        scratch_shapes=scratch,
        compiler_params=_params(("parallel",)),
        name="ssd_latent" if has_init else "ssd_ctx",
    )(*args)


FF_CHUNK = 256
ROW_SPLIT = 1


def _outffn_kernel(x_ref, attn_ref, ssm_ref, mod_ref, wout_ref, gpost_ref, gpre_ref, gpostf_ref,
                   wg_ref, wu_ref, wd_ref, o_ref, mix_s, y_s, h_s):
    m = mod_ref[0]
    half = MLA_HEADS * V_HEAD
    rows = x_ref.shape[1] // 2
    nslice = 8
    srows = rows // nslice
    nchunks = D_FF // FF_CHUNK
    g1 = gpost_ref[...] * m[2:3]
    g2 = gpre_ref[...] * (1.0 + m[4:5])
    sh2 = m[3:4]
    g3 = gpostf_ref[...] * m[5:6]

    def out_proj(rs):
        return _dot(attn_ref[0, rs, :], wout_ref[0:half, :]) + _dot(ssm_ref[0, rs, :], wout_ref[half:, :])

    def pre_ffn(mix, xr):
        y = xr + _rms(mix, g1)
        return y, (_rms(y, g2) + sh2).astype(BF16)

    def ffn_chunk(h, c, acc):
        cs = slice(c * FF_CHUNK, (c + 1) * FF_CHUNK)
        part = _dot((_silu(_dot(h, wg_ref[:, cs])) * _dot(h, wu_ref[:, cs])).astype(BF16), wd_ref[cs, :])
        return part if acc is None else acc + part

    y_a, h_a = pre_ffn(out_proj(slice(0, rows)), x_ref[0, 0:rows, :])
    mix_s[...] = out_proj(slice(rows, 2 * rows))
    acc_a = None
    for c in range(nchunks):
        acc_a = ffn_chunk(h_a, c, acc_a)
        if c < nslice:
            rs = slice(c * srows, (c + 1) * srows)
            y_b, h_b = pre_ffn(mix_s[rs, :], x_ref[0, rows + c * srows:rows + (c + 1) * srows, :])
            y_s[rs, :] = y_b
            h_s[rs, :] = h_b
    h_bb = h_s[...]
    acc_b = None
    for c in range(nchunks):
        acc_b = ffn_chunk(h_bb, c, acc_b)
        if c < nslice:
            rs = slice(c * srows, (c + 1) * srows)
            o_ref[0, rs, :] = y_a[rs] + _rms(acc_a[rs], g3)
    o_ref[0, rows:2 * rows, :] = y_s[...] + _rms(acc_b, g3)


def _outffn_both_kernel(*refs, n_ctx):
    (xc, ac, sc, xl, al, sl, mod_ref, wout, gpost, gpre, gpostf, wg, wu, wd, oc, ol,
     mix_s, y_s, h_s) = refs
    i = pl.program_id(0)

    @pl.when(i < n_ctx)
    def _():
        _outffn_kernel(xc, ac, sc, mod_ref, wout, gpost, gpre, gpostf, wg, wu, wd, oc, mix_s, y_s, h_s)

    @pl.when(i >= n_ctx)
    def _():
        _outffn_kernel(xl, al, sl, mod_ref, wout, gpost, gpre, gpostf, wg, wu, wd, ol, mix_s, y_s, h_s)


def _outffn_both(ctx, lat, mod, wout, gpost, gpre, gpostf, wg, wu, wd):
    n_ctx, tm, d = ctx[0].shape
    n_lat = lat[0].shape[0]
    ctx_tok = lambda w: pl.BlockSpec((1, tm, w), lambda i: (jnp.minimum(i, n_ctx - 1), 0, 0))
    lat_tok = lambda w: pl.BlockSpec((1, tm, w), lambda i: (jnp.maximum(i - n_ctx, 0), 0, 0))
    widths = (d, MLA_HEADS * V_HEAD, SSM_WIDTH)
    consts = [wout, gpost, gpre, gpostf, wg, wu, wd]
    return pl.pallas_call(
        functools.partial(_outffn_both_kernel, n_ctx=n_ctx),
        grid=(n_ctx + n_lat,),
        in_specs=[ctx_tok(w) for w in widths] + [lat_tok(w) for w in widths]
                 + [pl.BlockSpec((1, 6, d), lambda i: (jnp.maximum(i - n_ctx + 1, 0), 0, 0))]
                 + [_const_spec(c.shape) for c in consts],
        out_specs=[ctx_tok(d), lat_tok(d)],
        out_shape=[jax.ShapeDtypeStruct((n_ctx, tm, d), F32), jax.ShapeDtypeStruct((n_lat, tm, d), F32)],
        scratch_shapes=[pltpu.VMEM((tm // 2, d), F32),
                        pltpu.VMEM((tm // 2, d), F32),
                        pltpu.VMEM((tm // 2, d), BF16)],
        compiler_params=_params(("arbitrary",)),
        name="out_ffn",
    )(*ctx, *lat, mod, *consts)


def _outffn(x, attn, ssm, mod, mod_off, wout, gpost, gpre, gpostf, wg, wu, wd, tm):
    b, s, d = x.shape
    tok = lambda w: pl.BlockSpec((1, tm, w), lambda i, j: (i, j, 0))
    consts = [wout, gpost, gpre, gpostf, wg, wu, wd]
    return pl.pallas_call(
        _outffn_kernel,
        grid=(b, s // tm),
        in_specs=[tok(d), tok(MLA_HEADS * V_HEAD), tok(SSM_WIDTH),
                  pl.BlockSpec((1, 6, d), lambda i, j: (i + mod_off, 0, 0))]
                 + [_const_spec(c.shape) for c in consts],
        out_specs=tok(d),
        out_shape=jax.ShapeDtypeStruct((b, s, d), F32),
        scratch_shapes=[pltpu.VMEM((tm // 2, d), F32),
                        pltpu.VMEM((tm // 2, d), F32),
                        pltpu.VMEM((tm // 2, d), BF16)],
        compiler_params=_params(("parallel", "parallel")),
        name="out_ffn",
    )(x, attn, ssm, mod, *consts)


def _rope_tables(length):
    quarter = QK_ROPE // 4
    pos = np.arange(length)
    inv_freq = ROPE_THETA ** (-np.arange(quarter, dtype=np.float64) / quarter)
    ang_r = (pos // GRID_W)[:, None] * inv_freq[None, :]
    ang_c = (pos % GRID_W)[:, None] * inv_freq[None, :]
    cos = np.concatenate([np.cos(ang_r)] * 2 + [np.cos(ang_c)] * 2, axis=1)
    sin = np.concatenate([-np.sin(ang_r), np.sin(ang_r), -np.sin(ang_c), np.sin(ang_c)], axis=1)
    return jnp.asarray(np.tile(cos, (1, 2)), F32), jnp.asarray(np.tile(sin, (1, 2)), F32)


def _pad_cols(w, width, left=0):
    return jnp.pad(w, ((0, 0), (left, width - left - w.shape[-1])))


def _ssd_constants():
    idx = np.arange(CHUNK)
    tril = (idx[None, :] <= idx[:, None]).astype(np.float32)
    e64 = np.zeros((LANES, N_DT * SSM_HEAD_DIM), np.float32)
    for q in range(N_DT):
        e64[q, q * SSM_HEAD_DIM:(q + 1) * SSM_HEAD_DIM] = 1.0
    return [jnp.asarray(a, BF16) for a in (tril, e64)]


def kernel(x_prompt, x_sample, cache_ckv, cache_krope, state_ssm, c, c_ctx, w_mod, b_mod,
           g_pre_mix, g_post_mix, w_in, g_q, w_uq, g_kv, w_ukv, conv_w, conv_b, dt_bias,
           a_log, d_skip, g_ssm_out, w_out, g_pre_ffn, g_post_ffn, w_gate, w_up, w_down):
    depth = w_in.shape[0]
    assert depth == 1
    nb, seq, d = x_prompt.shape
    db, dseq, _ = x_sample.shape
    l = 0

    rows = 16
    cond = jnp.concatenate([c_ctx[None, :], c, jnp.zeros((rows - 1 - db, d), F32)], axis=0)
    mod = _modulation(cond, w_mod[l], b_mod[l][None, :]).reshape(rows, 6, d)

    wi = w_in[l]
    o_q, o_kv, o_kr = Q_LORA, Q_LORA + KV_LORA, Q_LORA + KV_LORA + QK_ROPE
    o_z, o_xbc = o_kr + SSM_WIDTH, o_kr + SSM_WIDTH + CONV_DIM
    kr_dt = _pad_cols(jnp.concatenate([wi[:, o_kv:o_kr], wi[:, o_xbc:]], axis=1), LANES)
    win = jnp.concatenate([wi[:, :o_q], wi[:, o_q:o_kv], wi[:, o_kr:o_z], wi[:, o_z:o_xbc], kr_dt],
                          axis=1).astype(BF16)
    wq = w_uq[l].reshape(Q_LORA, MLA_HEADS, QK_NOPE + QK_ROPE)
    wuq = jnp.concatenate([wq[:, :, :QK_NOPE].reshape(Q_LORA, MLA_HEADS * QK_NOPE),
                           wq[:, :, QK_NOPE:].reshape(Q_LORA, MLA_HEADS * QK_ROPE)], axis=1).astype(BF16)
    wukv = w_ukv[l].astype(BF16)
    wout = w_out[l].astype(BF16)
    row = lambda v: v.reshape(1, -1)
    gpre, gq, gkv = row(g_pre_mix[l]), row(g_q[l]), row(g_kv[l])
    gpost, gpre_f, gpost_f = row(g_post_mix[l]), row(g_pre_ffn[l]), row(g_post_ffn[l])
    cw, cb = conv_w[l], row(conv_b[l])

    small = lambda v: jnp.tile(v.reshape(1, -1), (1, SSD_GROUP))
    ssd_consts = [small(dt_bias[l]), small(a_log[l]),
                  row(jnp.repeat(d_skip[l], SSM_HEAD_DIM)), row(g_ssm_out[l])] + _ssd_constants()

    xp = x_prompt.reshape(1, nb * seq, d)
    q, k, v, z, act, sm, ckv, krope = _inproj(xp, mod, 0, gpre, win, gq, wuq, gkv, wukv, cw, cb,
                                              None, tm=1024, seq_rows=seq)
    per_seq = lambda a: a.reshape(nb, seq, a.shape[-1])
    attn = _attention(per_seq(q), per_seq(k), per_seq(v), None, tq=seq, bs=4)
    ssm, fin, wg, wu, wd = _ssd2(per_seq(act), per_seq(z), per_seq(sm), None, ssd_consts,
                                 cast=(w_gate[l], w_up[l], w_down[l]))
    y_p = _outffn(xp, attn.reshape(1, nb * seq, -1), ssm.reshape(1, nb * seq, -1), mod, 0,
                  wout, gpost, gpre_f, gpost_f, wg, wu, wd, tm=1024).reshape(nb, seq, d)
    new_ckv = ckv.reshape(nb, 1, seq, KV_LORA)
    new_krope = krope.reshape(nb, 1, seq, QK_ROPE)
    new_ssm = fin.reshape(nb, 1, 2, SSM_HEADS, SSM_HEAD_DIM, D_STATE)

    kr_cache = jnp.pad(cache_krope[:, l], ((0, 0), (0, 0), (0, LANES - QK_ROPE)))
    kc, vc = _ctxkv(cache_ckv[:, l], kr_cache, wukv)
    q, k, v, z, act, sm = _inproj(x_sample, mod, 1, gpre, win, gq, wuq, gkv, wukv, cw, cb,
                                  _rope_tables(dseq), tm=1024, seq_rows=dseq)
    attn = _attention(q, k, v, (kc, vc), tq=1024, bs=1)
    init = state_ssm[:, l].reshape(db, 2, SSM_WIDTH, D_STATE)
    ssm, _ = _ssd2(act, z, sm, init, ssd_consts)
    y_s = _outffn(x_sample, attn, ssm, mod, 1, wout, gpost, gpre_f, gpost_f, wg, wu, wd, tm=1024)

    return (y_p, y_s, new_ckv, new_krope, new_ssm)
```

```python
import functools

import numpy as np
import jax
import jax.numpy as jnp
from jax import lax
from jax.experimental import pallas as pl
from jax.experimental.pallas import tpu as pltpu

F32 = jnp.float32
BF16 = jnp.bfloat16

D_MODEL = 1024
GRID_W = 64
ROPE_THETA = 10000.0
NORM_EPS = 1e-6
MLA_HEADS = 4
QK_NOPE = 128
QK_ROPE = 64
V_HEAD = 128
Q_LORA = 384
KV_LORA = 256
SSM_HEADS = 8
SSM_HEAD_DIM = 64
SSM_WIDTH = SSM_HEADS * SSM_HEAD_DIM
SSM_GROUPS = 2
D_STATE = 128
CHUNK = 128
CONV_DIM = SSM_WIDTH + 2 * SSM_GROUPS * D_STATE
D_FF = 2816

LOG2E = 1.4426950408889634
LANES = 128
SUBLANES = 8
BF16_ROWS = 16
QK_PAD = 256
GROUP_W = SSM_WIDTH // SSM_GROUPS
VMEM_LIMIT = 56 * 1024 * 1024

_SEG_Q = (0, 384)
_SEG_KV = (384, 640)
_SEG_Z = (640, 1152)
_SEG_XBC = (1152, 2176)
_SEG_KR = (2176, 2304)
DT_OFF = QK_ROPE
N_DT = 2 * SSM_HEADS


def _rms(x, g):
    return x * lax.rsqrt(jnp.mean(x * x, axis=-1, keepdims=True) + NORM_EPS) * g


def _silu(x):
    u = 0.5 * x
    return u * jnp.tanh(u) + u


def _dot(a, b):
    return jnp.dot(a, b, preferred_element_type=F32)


def _dot_nt(a, b):
    return lax.dot_general(a, b, (((1,), (1,)), ((), ())), preferred_element_type=F32)


def _split3(x):
    hi = x.astype(BF16)
    r = x - hi.astype(F32)
    mid = r.astype(BF16)
    lo = (r - mid.astype(F32)).astype(BF16)
    return hi, mid, lo


def _dot3_rhs(m, x):
    hi, mid, lo = _split3(x)
    return _dot(m, hi) + _dot(m, mid) + _dot(m, lo)


def _const_spec(shape):
    nd = len(shape)
    return pl.BlockSpec(shape, lambda *_: (0,) * nd, pipeline_mode=pl.Buffered(1))


def _params(sem):
    return pltpu.CompilerParams(dimension_semantics=sem, vmem_limit_bytes=VMEM_LIMIT)


def _mod_kernel(c_ref, w_ref, b_ref, o_ref):
    c = c_ref[...]
    s = _silu(c)
    w = w_ref[...]
    s_hi = s.astype(BF16)
    s_lo = (s - s_hi.astype(F32)).astype(BF16)
    w_hi = w.astype(BF16)
    w_lo = (w - w_hi.astype(F32)).astype(BF16)
    o_ref[...] = _dot(s_hi, w_hi) + _dot(s_lo, w_hi) + _dot(s_hi, w_lo) + b_ref[...]


def _modulation(cond, w_mod, b_mod):
    rows, d = cond.shape
    n = w_mod.shape[1]
    tn = 1536
    return pl.pallas_call(
        _mod_kernel,
        grid=(n // tn,),
        in_specs=[pl.BlockSpec((rows, d), lambda j: (0, 0)),
                  pl.BlockSpec((d, tn), lambda j: (0, j)),
                  pl.BlockSpec((1, tn), lambda j: (0, j))],
        out_specs=pl.BlockSpec((rows, tn), lambda j: (0, j)),
        out_shape=jax.ShapeDtypeStruct((rows, n), F32),
        compiler_params=_params(("arbitrary",)),
        name="modulation",
    )(cond, w_mod, b_mod)


def _conv_silu(xb, prev_rows, next_rows, cw_ref, cb_ref):
    n = xb.shape[0]
    win = jnp.concatenate([prev_rows, xb, next_rows], axis=0)
    total = n + 2 * SUBLANES
    prev = pltpu.roll(win, 1, axis=0)[SUBLANES:SUBLANES + n]
    nxt = pltpu.roll(win, total - 1, axis=0)[SUBLANES:SUBLANES + n]
    conv = cb_ref[...] + prev * cw_ref[0:1, :] + xb * cw_ref[1:2, :] + nxt * cw_ref[2:3, :]
    return _silu(conv)


def _inproj_kernel(*refs, latent, seq_rows):
    if latent:
        (x_ref, mod_ref, gpre_ref, win_ref, gq_ref, wuq_ref, gkv_ref, wukv_ref, cw_ref, cb_ref,
         cos_ref, sin_ref, q_ref, k_ref, v_ref, z_ref, act_ref, small_ref) = refs
    else:
        (x_ref, mod_ref, gpre_ref, win_ref, gq_ref, wuq_ref, gkv_ref, wukv_ref, cw_ref, cb_ref,
         attn_ref, z_ref, act_ref, small_ref, ckv_ref, krope_ref) = refs
    scale = (QK_NOPE + QK_ROPE) ** -0.5 * LOG2E
    m = mod_ref[0]

    g_mod = gpre_ref[...] * (1.0 + m[1:2])

    def pre(xv):
        return (_rms(xv, g_mod) + m[0:1]).astype(BF16)

    tm = x_ref.shape[1]
    rows = tm // 2
    halves = (slice(0, rows), slice(rows, tm))
    assert seq_rows == tm or rows % seq_rows == 0
    zero_rows = jnp.zeros((SUBLANES, CONV_DIM), F32)
    projs = [_dot(pre(x_ref[0, halves[0], :]), win_ref[...])]
    if seq_rows == tm:
        ph = _dot(pre(x_ref[0, rows - SUBLANES:rows + SUBLANES, :]), win_ref[:, _SEG_XBC[0]:_SEG_XBC[1]])
        halo = ((zero_rows, ph[SUBLANES:]), (ph[:SUBLANES], zero_rows))
    projs.append(_dot(pre(x_ref[0, halves[1], :]), win_ref[...]))

    lane = lax.broadcasted_iota(jnp.int32, (rows, LANES), 1)
    rope_lanes = lane < QK_ROPE
    first_quarter = jnp.bitwise_and(lane, QK_ROPE // 4) == 0
    nw = MLA_HEADS * QK_NOPE
    for hf, rs in enumerate(halves):
        proj = projs[hf]
        seg = lambda s: proj[:, s[0]:s[1]]
        xbc = seg(_SEG_XBC)
        if seq_rows == tm:
            act_ref[0, rs, :] = _conv_silu(xbc, halo[hf][0], halo[hf][1], cw_ref, cb_ref).astype(BF16)
        else:
            for s in range(rows // seq_rows):
                sub = slice(s * seq_rows, (s + 1) * seq_rows)
                dst = slice(rs.start + sub.start, rs.start + sub.stop)
                act_ref[0, dst, :] = _conv_silu(xbc[sub], zero_rows, zero_rows, cw_ref, cb_ref).astype(BF16)
        krb = seg(_SEG_KR)
        small_ref[0, rs, :] = krb
        ckv = _rms(seg(_SEG_KV), gkv_ref[...])
        qall = _dot(_rms(seg(_SEG_Q), gq_ref[...]).astype(BF16), wuq_ref[...])
        kv = _dot(ckv.astype(BF16), wukv_ref[...])
        z_ref[0, rs, :] = seg(_SEG_Z).astype(BF16)
        if latent:
            cos = cos_ref[rs, :]
            sin = sin_ref[rs, :]

            def rot(t):
                swapped = jnp.where(first_quarter, pltpu.roll(t, LANES - QK_ROPE // 4, axis=1),
                                    pltpu.roll(t, QK_ROPE // 4, axis=1))
                return t * cos + swapped * sin
        else:
            rot = lambda t: t
            ckv_ref[0, rs, :] = ckv
            for s in range(rows // seq_rows):
                krope_ref[rs.start // seq_rows + s, 0] = krb[s * seq_rows:(s + 1) * seq_rows, :QK_ROPE]
        kr_b = jnp.where(rope_lanes, rot(krb), 0.0).astype(BF16)
        for pr in range(MLA_HEADS // 2):
            pair = rot(qall[:, nw + pr * LANES:nw + (pr + 1) * LANES]) * scale
            for k in range(2):
                hd = 2 * pr + k
                base = hd * QK_PAD
                qr = pair if k == 0 else pltpu.roll(pair, QK_ROPE, axis=1)
                q_n = (qall[:, hd * LANES:(hd + 1) * LANES] * scale).astype(BF16)
                q_r = jnp.where(rope_lanes, qr, 0.0).astype(BF16)
                k_n = kv[:, base:base + LANES].astype(BF16)
                v_h = kv[:, base + LANES:base + QK_PAD].astype(BF16)
                if latent:
                    q_ref[0, rs, base:base + LANES] = q_n
                    q_ref[0, rs, base + LANES:base + QK_PAD] = q_r
                    k_ref[0, rs, base:base + LANES] = k_n
                    k_ref[0, rs, base + LANES:base + QK_PAD] = kr_b
                    v_ref[0, rs, hd * LANES:(hd + 1) * LANES] = v_h
                else:
                    q_h = jnp.concatenate([q_n, q_r], axis=1)
                    k_h = jnp.concatenate([k_n, kr_b], axis=1)
                    for s in range(rows // seq_rows):
                        sub = slice(s * seq_rows, (s + 1) * seq_rows)
                        sc = _dot_nt(q_h[sub], k_h[sub])
                        p = jnp.exp2(sc - jnp.max(sc, axis=-1, keepdims=True))
                        den = jnp.sum(p, axis=-1, keepdims=True)
                        dst = slice(rs.start + sub.start, rs.start + sub.stop)
                        attn_ref[0, dst, hd * V_HEAD:(hd + 1) * V_HEAD] = (
                            _dot(p.astype(BF16), v_h[sub]) / den).astype(BF16)


def _inproj(x, mod, mod_off, gpre, win, gq, wuq, gkv, wukv, cw, cb, rope, tm, seq_rows):
    b, s, d = x.shape
    latent = rope is not None
    grid = (b, s // tm)
    tok = lambda w: pl.BlockSpec((1, tm, w), lambda i, j: (i, j, 0))
    mod_spec = pl.BlockSpec((1, 6, d), lambda i, j: (i + mod_off, 0, 0))
    consts = [gpre, win, gq, wuq, gkv, wukv, cw, cb]
    in_specs = [tok(d), mod_spec]
    args = [x, mod]
    in_specs += [_const_spec(c.shape) for c in consts]
    args += consts
    widths = [MLA_HEADS * QK_PAD, MLA_HEADS * QK_PAD, MLA_HEADS * V_HEAD] if latent else [MLA_HEADS * V_HEAD]
    widths += [SSM_WIDTH, CONV_DIM]
    out_shape = [jax.ShapeDtypeStruct((b, s, w), BF16) for w in widths]
    out_shape.append(jax.ShapeDtypeStruct((b, s, LANES), F32))
    out_specs = [tok(w) for w in widths] + [tok(LANES)]
    if latent:
        in_specs += [pl.BlockSpec((tm, LANES), lambda i, j: (j, 0))] * 2
        args += list(rope)
    else:
        assert b == 1
        out_shape += [jax.ShapeDtypeStruct((b, s, KV_LORA), F32),
                      jax.ShapeDtypeStruct((s // seq_rows, 1, seq_rows, QK_ROPE), F32)]
        out_specs += [tok(KV_LORA),
                      pl.BlockSpec((tm // seq_rows, 1, seq_rows, QK_ROPE), lambda i, j: (j, 0, 0, 0))]
    return pl.pallas_call(
        functools.partial(_inproj_kernel, latent=latent, seq_rows=seq_rows),
        grid=grid, in_specs=in_specs, out_specs=out_specs, out_shape=out_shape,
        compiler_params=_params(("parallel", "parallel")),
        name="inproj_latent" if latent else "inproj_ctx",
    )(*args)


def _ctxkv_kernel(ckv_ref, kr_ref, wukv_ref, k_ref, v_ref):
    kv = _dot(ckv_ref[0].astype(BF16), wukv_ref[...])
    kr_b = kr_ref[0].astype(BF16)
    for hd in range(MLA_HEADS):
        base = hd * QK_PAD
        k_ref[0, :, base:base + LANES] = kv[:, base:base + LANES].astype(BF16)
        k_ref[0, :, base + LANES:base + QK_PAD] = kr_b
        v_ref[0, :, hd * LANES:(hd + 1) * LANES] = kv[:, base + LANES:base + QK_PAD].astype(BF16)


def _ctxkv(ckv, kr_pad, wukv):
    b, s, _ = ckv.shape
    n = b * s
    tm = min(n, 1024)
    blk = lambda w: pl.BlockSpec((1, tm, w), lambda i: (0, i, 0))
    k, v = pl.pallas_call(
        _ctxkv_kernel,
        grid=(n // tm,),
        in_specs=[blk(KV_LORA), blk(LANES), _const_spec(wukv.shape)],
        out_specs=[blk(MLA_HEADS * QK_PAD), blk(MLA_HEADS * V_HEAD)],
        out_shape=[jax.ShapeDtypeStruct((1, n, MLA_HEADS * QK_PAD), BF16),
                   jax.ShapeDtypeStruct((1, n, MLA_HEADS * V_HEAD), BF16)],
        compiler_params=_params(("parallel",)),
        name="ctx_kv",
    )(ckv.reshape(1, n, KV_LORA), kr_pad.reshape(1, n, LANES), wukv)
    return k.reshape(b, s, -1), v.reshape(b, s, -1)


def _attn_kernel(*refs, latent):
    if latent:
        q_ref, k_ref, v_ref, kc_ref, vc_ref, o_ref = refs
    else:
        q_ref, k_ref, v_ref, o_ref = refs
    items = [(sq, hd) for sq in range(q_ref.shape[0]) for hd in range(MLA_HEADS)]

    def scores(item):
        sq, hd = item
        qs = slice(hd * QK_PAD, (hd + 1) * QK_PAD)
        q = q_ref[sq, :, qs]
        if latent:
            return _dot_nt(k_ref[sq, :, qs], q), _dot_nt(kc_ref[sq, :, qs], q)
        return (_dot_nt(q, k_ref[sq, :, qs]),)

    nxt = scores(items[0])
    for n, (sq, hd) in enumerate(items):
        cur = nxt
        if n + 1 < len(items):
            nxt = scores(items[n + 1])
        vs = slice(hd * V_HEAD, (hd + 1) * V_HEAD)
        if latent:
            s, sc = cur
            mx = jnp.maximum(jnp.max(s, axis=0, keepdims=True), jnp.max(sc, axis=0, keepdims=True))
            p = jnp.exp2(s - mx)
            pc = jnp.exp2(sc - mx)
            ones = lambda rows: jnp.ones((BF16_ROWS, rows), BF16)
            v_t = jnp.concatenate([jnp.transpose(v_ref[sq, :, vs].astype(F32)).astype(BF16),
                                   ones(v_ref.shape[1])], axis=0)
            vc_t = jnp.concatenate([jnp.transpose(vc_ref[sq, :, vs].astype(F32)).astype(BF16),
                                    ones(vc_ref.shape[1])], axis=0)
            acc = _dot(v_t, p.astype(BF16)) + _dot(vc_t, pc.astype(BF16))
            out = acc[:V_HEAD] / acc[V_HEAD:V_HEAD + 1]
            o_ref[sq, :, vs] = jnp.transpose(out).astype(BF16)
        else:
            s, = cur
            p = jnp.exp2(s - jnp.max(s, axis=-1, keepdims=True))
            den = jnp.sum(p, axis=-1, keepdims=True)
            o_ref[sq, :, vs] = (_dot(p.astype(BF16), v_ref[sq, :, vs]) / den).astype(BF16)


def _attention(q, k, v, ctx, tq, bs):
    b, s, _ = q.shape
    latent = ctx is not None
    qw, vw = MLA_HEADS * QK_PAD, MLA_HEADS * V_HEAD
    in_specs = [pl.BlockSpec((bs, tq, qw), lambda i, j: (i, j, 0)),
                pl.BlockSpec((bs, s, qw), lambda i, j: (i, 0, 0)),
                pl.BlockSpec((bs, s, vw), lambda i, j: (i, 0, 0))]
    args = [q, k, v]
    if latent:
        sc = ctx[0].shape[1]
        in_specs += [pl.BlockSpec((bs, sc, qw), lambda i, j: (i, 0, 0)),
                     pl.BlockSpec((bs, sc, vw), lambda i, j: (i, 0, 0))]
        args += list(ctx)
    return pl.pallas_call(
        functools.partial(_attn_kernel, latent=latent),
        grid=(b // bs, s // tq), in_specs=in_specs,
        out_specs=pl.BlockSpec((bs, tq, vw), lambda i, j: (i, j, 0)),
        out_shape=jax.ShapeDtypeStruct((b, s, vw), BF16),
        compiler_params=_params(("parallel", "parallel")),
        name="attn_latent" if latent else "attn_ctx",
    )(*args)


SSD_GROUP = LANES // N_DT


def _ssd_kernel(*refs, nseq, cps, has_init, n_cast):
    refs = list(refs)
    n_in = 3 + int(has_init) + 6
    cast_in = refs[n_in:n_in + n_cast]
    cast_out = refs[n_in + n_cast + 2:n_in + 2 * n_cast + 2]
    for src, dst in zip(cast_in, cast_out):
        dst[...] = src[...].astype(BF16)
    refs = refs[:n_in] + refs[n_in + n_cast:n_in + n_cast + 2] + refs[n_in + 2 * n_cast + 2:]
    if has_init:
        (act_ref, z_ref, small_ref, init_ref, dtbias_ref, alog_ref, dskip_ref, gout_ref,
         tril_ref, e64_ref, out_ref, fin_ref,
         fg_s, wsm_s, rt_s, bt_s, y_s, xwb_s, eg_s, sf_s, sb_s) = refs
    else:
        (act_ref, z_ref, small_ref, dtbias_ref, alog_ref, dskip_ref, gout_ref,
         tril_ref, e64_ref, out_ref, fin_ref,
         fg_s, wsm_s, rt_s, bt_s, y_s, xwb_s, eg_s, sf_s, sb_s) = refs
    L = CHUNK
    G = nseq * cps
    heads_per_group = SSM_HEADS // SSM_GROUPS
    b_off = SSM_WIDTH
    c_off = SSM_WIDTH + SSM_GROUPS * D_STATE

    lane = lax.broadcasted_iota(jnp.int32, (L, LANES), 1)
    packed = jnp.zeros((L, LANES), F32)
    for c in range(G):
        raw = small_ref[c // cps, (c % cps) * L:(c % cps + 1) * L, :]
        shifted = pltpu.roll(raw, (c * N_DT - DT_OFF) % LANES, axis=1)
        packed = jnp.where((lane >= c * N_DT) & (lane < (c + 1) * N_DT), shifted, packed)
    dtc = jax.nn.softplus(packed + dtbias_ref[...])
    da = dtc * (-jnp.exp(alog_ref[...]))
    cum = _dot3_rhs(tril_ref[...], da)
    tot = cum[L - 1:L, :]
    suf = tot - cum + da
    fg = jnp.where(jnp.bitwise_and(lane, SSM_HEADS) == 0, cum, suf)
    w_small = dtc * jnp.exp(tot - fg)
    e_small = jnp.exp(fg)
    rt_s[...] = jnp.transpose(fg - jnp.log(dtc))
    for c in range(G):
        back = (LANES - c * N_DT) % LANES
        unroll = lambda t: t if back == 0 else pltpu.roll(t, back, axis=1)
        fg_s[c] = unroll(fg)
        wsm_s[c, 0:L, :] = unroll(w_small).astype(BF16)
        wsm_s[c, L:2 * L, :] = unroll(e_small).astype(BF16)

    for s in range(nseq):
        if has_init:
            sf_s[s] = jnp.transpose(init_ref[s, 0])
            sb_s[s] = jnp.transpose(init_ref[s, 1])
        else:
            sf_s[s] = jnp.zeros((D_STATE, SSM_WIDTH), F32)
            sb_s[s] = jnp.zeros((D_STATE, SSM_WIDTH), F32)

    row_t = lax.broadcasted_iota(jnp.int32, (L, L), 0)
    col_s = lax.broadcasted_iota(jnp.int32, (L, L), 1)
    lower = col_s <= row_t
    upper = col_s >= row_t
    lane_lo = lane < SSM_HEAD_DIM
    neg_inf = jnp.float32(-jnp.inf)

    def locate(c):
        sq, j = divmod(c, cps)
        return sq, slice(j * L, (j + 1) * L), slice(c * L, (c + 1) * L)

    def bt_rows(c, g):
        r = (c * SSM_GROUPS + g) * D_STATE
        return slice(r, r + D_STATE)

    for c in range(G):
        sq, rs, fs = locate(c)
        xs_b = act_ref[sq, rs, 0:SSM_WIDTH]
        fgc = fg_s[c]
        rtc = rt_s[c * N_DT:(c + 1) * N_DT, :]
        ex = _dot(wsm_s[c], e64_ref[...])
        w_b = ex[0:L, :].astype(BF16)
        ef_x = ex[L:2 * L, :SSM_WIDTH]
        eg_x = ex[L:2 * L, SSM_WIDTH:]
        eg_s[fs, :] = eg_x
        xwf_b = xs_b * w_b[:, :SSM_WIDTH]
        xwb_s[fs, :] = xs_b * w_b[:, SSM_WIDTH:]

        y_parts = []
        new_states = []
        for g in range(SSM_GROUPS):
            gs = slice(g * GROUP_W, (g + 1) * GROUP_W)
            bm_g = act_ref[sq, rs, b_off + g * D_STATE:b_off + (g + 1) * D_STATE]
            cm_g = act_ref[sq, rs, c_off + g * D_STATE:c_off + (g + 1) * D_STATE]
            bt_g = jnp.transpose(bm_g.astype(F32)).astype(BF16)
            bt_s[bt_rows(c, g), :] = bt_g
            cbm = _dot_nt(cm_g, bm_g)
            for pair in range(heads_per_group // 2):
                ws = []
                for k in range(2):
                    cf = g * heads_per_group + pair * 2 + k
                    seg_f = fgc[:, cf:cf + 1] - rtc[cf:cf + 1, :]
                    lf = jnp.exp(jnp.where(lower, seg_f, neg_inf))
                    cg = cf + SSM_HEADS
                    seg_b = fgc[:, cg:cg + 1] - rtc[cg:cg + 1, :]
                    ub = jnp.exp(jnp.where(upper, seg_b, neg_inf))
                    ws.append((cbm * (lf + ub)).astype(BF16))
                p0 = (g * heads_per_group + pair * 2) * SSM_HEAD_DIM
                xpair = xs_b[:, p0:p0 + LANES]
                zeros = jnp.zeros_like(xpair)
                rhs = jnp.concatenate([jnp.where(lane_lo, xpair, zeros),
                                       jnp.where(lane_lo, zeros, xpair)], axis=0)
                y_parts.append(_dot(jnp.concatenate(ws, axis=1), rhs))
            s_in = sf_s[sq, :, gs]
            y_off = _dot(cm_g, s_in.astype(BF16)) * ef_x[:, gs]
            y_parts[-2] = y_parts[-2] + y_off[:, :LANES]
            y_parts[-1] = y_parts[-1] + y_off[:, LANES:]
            new_states.append(s_in * ef_x[L - 1:L, gs] + _dot(bt_g, xwf_b[:, gs]))
        for g in range(SSM_GROUPS):
            sf_s[sq, :, g * GROUP_W:(g + 1) * GROUP_W] = new_states[g]
        for i, yp in enumerate(y_parts):
            y_s[fs, i * LANES:(i + 1) * LANES] = yp

    for c in reversed(range(G)):
        sq, rs, fs = locate(c)
        xs = act_ref[sq, rs, 0:SSM_WIDTH].astype(F32)
        eg_x = eg_s[fs, :]
        xwb_b = xwb_s[fs, :]
        y_off = []
        for g in range(SSM_GROUPS):
            gs = slice(g * GROUP_W, (g + 1) * GROUP_W)
            s_in = sb_s[sq, :, gs]
            cm_g = act_ref[sq, rs, c_off + g * D_STATE:c_off + (g + 1) * D_STATE]
            y_off.append(_dot(cm_g, s_in.astype(BF16)) * eg_x[:, gs])
            sb_s[sq, :, gs] = s_in * eg_x[0:1, gs] + _dot(bt_s[bt_rows(c, g), :], xwb_b[:, gs])
        y = y_s[fs, :] + jnp.concatenate(y_off, axis=1) + dskip_ref[...] * xs
        y = y * _silu(z_ref[sq, rs, :].astype(F32))
        out_ref[sq, rs, :] = _rms(y, gout_ref[...]).astype(BF16)

    for s in range(nseq):
        fin_ref[s, 0] = jnp.transpose(sf_s[s])
        fin_ref[s, 1] = jnp.transpose(sb_s[s])


def _ssd(act, z, small, init, consts, cast=()):
    b, s, _ = act.shape
    has_init = init is not None
    cps = s // CHUNK
    assert SSD_GROUP % cps == 0
    nseq = SSD_GROUP // cps
    assert b % nseq == 0
    steps = b // nseq
    blk = lambda w: pl.BlockSpec((nseq, s, w), lambda i: (i, 0, 0))
    st_spec = pl.BlockSpec((nseq, 2, SSM_WIDTH, D_STATE), lambda i: (i, 0, 0, 0))
    in_specs = [blk(CONV_DIM), blk(SSM_WIDTH), blk(LANES)]
    args = [act, z, small]
    if has_init:
        in_specs.append(st_spec)
        args.append(init)
    in_specs += [_const_spec(c.shape) for c in consts]
    args += list(consts)
    cast_specs = [pl.BlockSpec((w.shape[0] // steps, w.shape[1]), lambda i: (i, 0)) for w in cast]
    in_specs += cast_specs
    args += list(cast)
    rows = SSD_GROUP * CHUNK
    scratch = [pltpu.VMEM((SSD_GROUP, CHUNK, LANES), F32),
               pltpu.VMEM((SSD_GROUP, 2 * CHUNK, LANES), BF16),
               pltpu.VMEM((LANES, CHUNK), F32),
               pltpu.VMEM((SSD_GROUP * SSM_GROUPS * D_STATE, CHUNK), BF16),
               pltpu.VMEM((rows, SSM_WIDTH), F32),
               pltpu.VMEM((rows, SSM_WIDTH), BF16),
               pltpu.VMEM((rows, SSM_WIDTH), F32),
               pltpu.VMEM((nseq, D_STATE, SSM_WIDTH), F32),
               pltpu.VMEM((nseq, D_STATE, SSM_WIDTH), F32)]
    return pl.pallas_call(
        functools.partial(_ssd_kernel, nseq=nseq, cps=cps, has_init=has_init, n_cast=len(cast)),
        grid=(steps,), in_specs=in_specs,
        out_specs=[blk(SSM_WIDTH), st_spec] + cast_specs,
        out_shape=[jax.ShapeDtypeStruct((b, s, SSM_WIDTH), BF16),
                   jax.ShapeDtypeStruct((b, 2, SSM_WIDTH, D_STATE), F32)]
                  + [jax.ShapeDtypeStruct(w.shape, BF16) for w in cast],
        scratch_shapes=scratch,
        compiler_params=_params(("parallel",)),
        name="ssd_latent" if has_init else "ssd_ctx",
    )(*args)


FF_CHUNK = 256


def _outffn_kernel(x_ref, attn_ref, ssm_ref, mod_ref, wout_ref, gpost_ref, gpre_ref, gpostf_ref,
                   wg_ref, wu_ref, wd_ref, o_ref, mix_s, y_s, h_s):
    m = mod_ref[0]
    half = MLA_HEADS * V_HEAD
    rows = x_ref.shape[1] // 2
    nslice = 8
    srows = rows // nslice
    nchunks = D_FF // FF_CHUNK
    g1 = gpost_ref[...] * m[2:3]
    g2 = gpre_ref[...] * (1.0 + m[4:5])
    sh2 = m[3:4]
    g3 = gpostf_ref[...] * m[5:6]

    def out_proj(rs):
        return _dot(attn_ref[0, rs, :], wout_ref[0:half, :]) + _dot(ssm_ref[0, rs, :], wout_ref[half:, :])

    def pre_ffn(mix, xr):
        y = xr + _rms(mix, g1)
        return y, (_rms(y, g2) + sh2).astype(BF16)

    def ffn_chunk(h, c, acc):
        cs = slice(c * FF_CHUNK, (c + 1) * FF_CHUNK)
        part = _dot((_silu(_dot(h, wg_ref[:, cs])) * _dot(h, wu_ref[:, cs])).astype(BF16), wd_ref[cs, :])
        return part if acc is None else acc + part

    y_a, h_a = pre_ffn(out_proj(slice(0, rows)), x_ref[0, 0:rows, :])
    mix_s[...] = out_proj(slice(rows, 2 * rows))
    acc_a = None
    for c in range(nchunks):
        acc_a = ffn_chunk(h_a, c, acc_a)
        if c < nslice:
            rs = slice(c * srows, (c + 1) * srows)
            y_b, h_b = pre_ffn(mix_s[rs, :], x_ref[0, rows + c * srows:rows + (c + 1) * srows, :])
            y_s[rs, :] = y_b
            h_s[rs, :] = h_b
    h_bb = h_s[...]
    acc_b = None
    for c in range(nchunks):
        acc_b = ffn_chunk(h_bb, c, acc_b)
        if c < nslice:
            rs = slice(c * srows, (c + 1) * srows)
            o_ref[0, rs, :] = y_a[rs] + _rms(acc_a[rs], g3)
    o_ref[0, rows:2 * rows, :] = y_s[...] + _rms(acc_b, g3)


def _outffn(x, attn, ssm, mod, mod_off, wout, gpost, gpre, gpostf, wg, wu, wd, tm):
    b, s, d = x.shape
    tok = lambda w: pl.BlockSpec((1, tm, w), lambda i, j: (i, j, 0))
    consts = [wout, gpost, gpre, gpostf, wg, wu, wd]
    return pl.pallas_call(
        _outffn_kernel,
        grid=(b, s // tm),
        in_specs=[tok(d), tok(MLA_HEADS * V_HEAD), tok(SSM_WIDTH),
                  pl.BlockSpec((1, 6, d), lambda i, j: (i + mod_off, 0, 0))]
                 + [_const_spec(c.shape) for c in consts],
        out_specs=tok(d),
        out_shape=jax.ShapeDtypeStruct((b, s, d), F32),
        scratch_shapes=[pltpu.VMEM((tm // 2, d), F32),
                        pltpu.VMEM((tm // 2, d), F32),
                        pltpu.VMEM((tm // 2, d), BF16)],
        compiler_params=_params(("parallel", "parallel")),
        name="out_ffn",
    )(x, attn, ssm, mod, *consts)


def _rope_tables(length):
    quarter = QK_ROPE // 4
    pos = np.arange(length)
    inv_freq = ROPE_THETA ** (-np.arange(quarter, dtype=np.float64) / quarter)
    ang_r = (pos // GRID_W)[:, None] * inv_freq[None, :]
    ang_c = (pos % GRID_W)[:, None] * inv_freq[None, :]
    cos = np.concatenate([np.cos(ang_r)] * 2 + [np.cos(ang_c)] * 2, axis=1)
    sin = np.concatenate([-np.sin(ang_r), np.sin(ang_r), -np.sin(ang_c), np.sin(ang_c)], axis=1)
    return jnp.asarray(np.tile(cos, (1, 2)), F32), jnp.asarray(np.tile(sin, (1, 2)), F32)


def _pad_cols(w, width, left=0):
    return jnp.pad(w, ((0, 0), (left, width - left - w.shape[-1])))


def _ssd_constants():
    idx = np.arange(CHUNK)
    tril = (idx[None, :] <= idx[:, None]).astype(np.float32)
    e64 = np.zeros((LANES, N_DT * SSM_HEAD_DIM), np.float32)
    for q in range(N_DT):
        e64[q, q * SSM_HEAD_DIM:(q + 1) * SSM_HEAD_DIM] = 1.0
    return [jnp.asarray(a, BF16) for a in (tril, e64)]


def kernel(x_prompt, x_sample, cache_ckv, cache_krope, state_ssm, c, c_ctx, w_mod, b_mod,
           g_pre_mix, g_post_mix, w_in, g_q, w_uq, g_kv, w_ukv, conv_w, conv_b, dt_bias,
           a_log, d_skip, g_ssm_out, w_out, g_pre_ffn, g_post_ffn, w_gate, w_up, w_down):
    depth = w_in.shape[0]
    assert depth == 1
    nb, seq, d = x_prompt.shape
    db, dseq, _ = x_sample.shape
    l = 0

    rows = 16
    cond = jnp.concatenate([c_ctx[None, :], c, jnp.zeros((rows - 1 - db, d), F32)], axis=0)
    mod = _modulation(cond, w_mod[l], b_mod[l][None, :]).reshape(rows, 6, d)

    wi = w_in[l]
    o_q, o_kv, o_kr = Q_LORA, Q_LORA + KV_LORA, Q_LORA + KV_LORA + QK_ROPE
    o_z, o_xbc = o_kr + SSM_WIDTH, o_kr + SSM_WIDTH + CONV_DIM
    kr_dt = _pad_cols(jnp.concatenate([wi[:, o_kv:o_kr], wi[:, o_xbc:]], axis=1), LANES)
    win = jnp.concatenate([wi[:, :o_q], wi[:, o_q:o_kv], wi[:, o_kr:o_z], wi[:, o_z:o_xbc], kr_dt],
                          axis=1).astype(BF16)
    wq = w_uq[l].reshape(Q_LORA, MLA_HEADS, QK_NOPE + QK_ROPE)
    wuq = jnp.concatenate([wq[:, :, :QK_NOPE].reshape(Q_LORA, MLA_HEADS * QK_NOPE),
                           wq[:, :, QK_NOPE:].reshape(Q_LORA, MLA_HEADS * QK_ROPE)], axis=1).astype(BF16)
    wukv = w_ukv[l].astype(BF16)
    wout = w_out[l].astype(BF16)
    row = lambda v: v.reshape(1, -1)
    gpre, gq, gkv = row(g_pre_mix[l]), row(g_q[l]), row(g_kv[l])
    gpost, gpre_f, gpost_f = row(g_post_mix[l]), row(g_pre_ffn[l]), row(g_post_ffn[l])
    cw, cb = conv_w[l], row(conv_b[l])

    small = lambda v: jnp.tile(v.reshape(1, -1), (1, SSD_GROUP))
    ssd_consts = [small(dt_bias[l]), small(a_log[l]),
                  row(jnp.repeat(d_skip[l], SSM_HEAD_DIM)), row(g_ssm_out[l])] + _ssd_constants()

    xp = x_prompt.reshape(1, nb * seq, d)
    attn, z, act, sm, ckv, krope = _inproj(xp, mod, 0, gpre, win, gq, wuq, gkv, wukv, cw, cb,
                                           None, tm=1024, seq_rows=seq)
    per_seq = lambda a: a.reshape(nb, seq, a.shape[-1])
    ssm, fin, wg, wu, wd = _ssd(per_seq(act), per_seq(z), per_seq(sm), None, ssd_consts,
                                cast=(w_gate[l], w_up[l], w_down[l]))
    y_p = _outffn(xp, attn, ssm.reshape(1, nb * seq, -1), mod, 0,
                  wout, gpost, gpre_f, gpost_f, wg, wu, wd, tm=1024).reshape(nb, seq, d)
    new_ckv = ckv.reshape(nb, 1, seq, KV_LORA)
    new_krope = krope
    new_ssm = fin.reshape(nb, 1, 2, SSM_HEADS, SSM_HEAD_DIM, D_STATE)

    kr_cache = jnp.pad(cache_krope[:, l], ((0, 0), (0, 0), (0, LANES - QK_ROPE)))
    kc, vc = _ctxkv(cache_ckv[:, l], kr_cache, wukv)
    q, k, v, z, act, sm = _inproj(x_sample, mod, 1, gpre, win, gq, wuq, gkv, wukv, cw, cb,
                                  _rope_tables(dseq), tm=1024, seq_rows=dseq)
    attn = _attention(q, k, v, (kc, vc), tq=1024, bs=1)
    init = state_ssm[:, l].reshape(db, 2, SSM_WIDTH, D_STATE)
    ssm, _ = _ssd(act, z, sm, init, ssd_consts)
    y_s = _outffn(x_sample, attn, ssm, mod, 1, wout, gpost, gpre_f, gpost_f, wg, wu, wd, tm=1024)

    return (y_p, y_s, new_ckv, new_krope, new_ssm)
```

```python
import functools

import numpy as np
import jax
import jax.numpy as jnp
from jax import lax
from jax.experimental import pallas as pl
from jax.experimental.pallas import tpu as pltpu

F32 = jnp.float32
BF16 = jnp.bfloat16

D_MODEL = 1024
GRID_W = 64
ROPE_THETA = 10000.0
NORM_EPS = 1e-6
MLA_HEADS = 4
QK_NOPE = 128
QK_ROPE = 64
V_HEAD = 128
Q_LORA = 384
KV_LORA = 256
SSM_HEADS = 8
SSM_HEAD_DIM = 64
SSM_WIDTH = SSM_HEADS * SSM_HEAD_DIM
SSM_GROUPS = 2
D_STATE = 128
CHUNK = 128
CONV_DIM = SSM_WIDTH + 2 * SSM_GROUPS * D_STATE
D_FF = 2816

LOG2E = 1.4426950408889634
LANES = 128
SUBLANES = 8
BF16_ROWS = 16
QK_PAD = 256
GROUP_W = SSM_WIDTH // SSM_GROUPS
VMEM_LIMIT = 56 * 1024 * 1024

_SEG_Q = (0, 384)
_SEG_KV = (384, 640)
_SEG_Z = (640, 1152)
_SEG_XBC = (1152, 2176)
_SEG_KR = (2176, 2304)
DT_OFF = QK_ROPE
N_DT = 2 * SSM_HEADS


def _rms(x, g):
    return x * lax.rsqrt(jnp.mean(x * x, axis=-1, keepdims=True) + NORM_EPS) * g


def _silu(x):
    u = 0.5 * x
    return u * jnp.tanh(u) + u


def _dot(a, b):
    return jnp.dot(a, b, preferred_element_type=F32)


def _dot_nt(a, b):
    return lax.dot_general(a, b, (((1,), (1,)), ((), ())), preferred_element_type=F32)


def _split3(x):
    hi = x.astype(BF16)
    r = x - hi.astype(F32)
    mid = r.astype(BF16)
    lo = (r - mid.astype(F32)).astype(BF16)
    return hi, mid, lo


def _dot3_rhs(m, x):
    hi, mid, lo = _split3(x)
    return _dot(m, hi) + _dot(m, mid) + _dot(m, lo)


def _const_spec(shape):
    nd = len(shape)
    return pl.BlockSpec(shape, lambda *_: (0,) * nd, pipeline_mode=pl.Buffered(1))


def _params(sem):
    return pltpu.CompilerParams(dimension_semantics=sem, vmem_limit_bytes=VMEM_LIMIT)


def _mod_kernel(c_ref, w_ref, b_ref, o_ref):
    s = _silu(c_ref[...])
    rows = s.shape[0]
    s_hi = s.astype(BF16)
    s_lo = (s - s_hi.astype(F32)).astype(BF16)
    both = _dot(jnp.concatenate([s_hi, s_lo], axis=0), w_ref[...].astype(BF16))
    o_ref[...] = both[:rows] + both[rows:] + b_ref[...]


def _modulation(cond, w_mod, b_mod):
    rows, d = cond.shape
    n = w_mod.shape[1]
    tn = 1536
    return pl.pallas_call(
        _mod_kernel,
        grid=(n // tn,),
        in_specs=[pl.BlockSpec((rows, d), lambda j: (0, 0)),
                  pl.BlockSpec((d, tn), lambda j: (0, j)),
                  pl.BlockSpec((1, tn), lambda j: (0, j))],
        out_specs=pl.BlockSpec((rows, tn), lambda j: (0, j)),
        out_shape=jax.ShapeDtypeStruct((rows, n), F32),
        compiler_params=_params(("arbitrary",)),
        name="modulation",
    )(cond, w_mod, b_mod)


def _conv_silu(xb, prev_rows, next_rows, cw_ref, cb_ref):
    n = xb.shape[0]
    win = jnp.concatenate([prev_rows, xb, next_rows], axis=0)
    total = n + 2 * SUBLANES
    prev = pltpu.roll(win, 1, axis=0)[SUBLANES:SUBLANES + n]
    nxt = pltpu.roll(win, total - 1, axis=0)[SUBLANES:SUBLANES + n]
    conv = cb_ref[...] + prev * cw_ref[0:1, :] + xb * cw_ref[1:2, :] + nxt * cw_ref[2:3, :]
    return _silu(conv)


def _inproj_kernel(*refs, latent, seq_rows):
    if latent:
        (x_ref, mod_ref, gpre_ref, win_ref, gq_ref, wuq_ref, gkv_ref, wukv_ref, cw_ref, cb_ref,
         cos_ref, sin_ref, q_ref, k_ref, v_ref, z_ref, act_ref, small_ref) = refs
    else:
        (x_ref, mod_ref, gpre_ref, win_ref, gq_ref, wuq_ref, gkv_ref, wukv_ref, cw_ref, cb_ref,
         attn_ref, z_ref, act_ref, small_ref, ckv_ref, krope_ref) = refs
    scale = (QK_NOPE + QK_ROPE) ** -0.5 * LOG2E
    m = mod_ref[0]

    g_mod = gpre_ref[...] * (1.0 + m[1:2])

    def pre(xv):
        return (_rms(xv, g_mod) + m[0:1]).astype(BF16)

    tm = x_ref.shape[1]
    rows = tm // 2
    halves = (slice(0, rows), slice(rows, tm))
    assert seq_rows == tm or rows % seq_rows == 0
    zero_rows = jnp.zeros((SUBLANES, CONV_DIM), F32)
    projs = [_dot(pre(x_ref[0, halves[0], :]), win_ref[...])]
    if seq_rows == tm:
        ph = _dot(pre(x_ref[0, rows - SUBLANES:rows + SUBLANES, :]), win_ref[:, _SEG_XBC[0]:_SEG_XBC[1]])
        halo = ((zero_rows, ph[SUBLANES:]), (ph[:SUBLANES], zero_rows))
    projs.append(_dot(pre(x_ref[0, halves[1], :]), win_ref[...]))

    lane = lax.broadcasted_iota(jnp.int32, (rows, LANES), 1)
    rope_lanes = lane < QK_ROPE
    first_quarter = jnp.bitwise_and(lane, QK_ROPE // 4) == 0
    nw = MLA_HEADS * QK_NOPE
    for hf, rs in enumerate(halves):
        proj = projs[hf]
        seg = lambda s: proj[:, s[0]:s[1]]
        xbc = seg(_SEG_XBC)
        if seq_rows == tm:
            act_ref[0, rs, :] = _conv_silu(xbc, halo[hf][0], halo[hf][1], cw_ref, cb_ref).astype(BF16)
        else:
            for s in range(rows // seq_rows):
                sub = slice(s * seq_rows, (s + 1) * seq_rows)
                dst = slice(rs.start + sub.start, rs.start + sub.stop)
                act_ref[0, dst, :] = _conv_silu(xbc[sub], zero_rows, zero_rows, cw_ref, cb_ref).astype(BF16)
        krb = seg(_SEG_KR)
        small_ref[0, rs, :] = krb
        ckv = _rms(seg(_SEG_KV), gkv_ref[...])
        qall = _dot(_rms(seg(_SEG_Q), gq_ref[...]).astype(BF16), wuq_ref[...])
        kv = _dot(ckv.astype(BF16), wukv_ref[...])
        z_ref[0, rs, :] = seg(_SEG_Z).astype(BF16)
        if latent:
            cos = cos_ref[rs, :]
            sin = sin_ref[rs, :]

            def rot(t):
                swapped = jnp.where(first_quarter, pltpu.roll(t, LANES - QK_ROPE // 4, axis=1),
                                    pltpu.roll(t, QK_ROPE // 4, axis=1))
                return t * cos + swapped * sin
        else:
            rot = lambda t: t
            ckv_ref[0, rs, :] = ckv
            for s in range(rows // seq_rows):
                krope_ref[rs.start // seq_rows + s, 0] = krb[s * seq_rows:(s + 1) * seq_rows, :QK_ROPE]
        kr_b = jnp.where(rope_lanes, rot(krb), 0.0).astype(BF16)
        for pr in range(MLA_HEADS // 2):
            pair = rot(qall[:, nw + pr * LANES:nw + (pr + 1) * LANES]) * scale
            for k in range(2):
                hd = 2 * pr + k
                base = hd * QK_PAD
                qr = pair if k == 0 else pltpu.roll(pair, QK_ROPE, axis=1)
                q_n = (qall[:, hd * LANES:(hd + 1) * LANES] * scale).astype(BF16)
                q_r = jnp.where(rope_lanes, qr, 0.0).astype(BF16)
                k_n = kv[:, base:base + LANES].astype(BF16)
                v_h = kv[:, base + LANES:base + QK_PAD].astype(BF16)
                if latent:
                    q_ref[0, rs, base:base + LANES] = q_n
                    q_ref[0, rs, base + LANES:base + QK_PAD] = q_r
                    k_ref[0, rs, base:base + LANES] = k_n
                    k_ref[0, rs, base + LANES:base + QK_PAD] = kr_b
                    v_ref[0, rs, hd * LANES:(hd + 1) * LANES] = v_h
                else:
                    q_h = jnp.concatenate([q_n, q_r], axis=1)
                    k_h = jnp.concatenate([k_n, kr_b], axis=1)
                    for s in range(rows // seq_rows):
                        sub = slice(s * seq_rows, (s + 1) * seq_rows)
                        sc = _dot_nt(q_h[sub], k_h[sub])
                        p = jnp.exp2(sc - jnp.max(sc, axis=-1, keepdims=True))
                        den = jnp.sum(p, axis=-1, keepdims=True)
                        dst = slice(rs.start + sub.start, rs.start + sub.stop)
                        attn_ref[0, dst, hd * V_HEAD:(hd + 1) * V_HEAD] = (
                            _dot(p.astype(BF16), v_h[sub]) / den).astype(BF16)


def _inproj(x, mod, mod_off, gpre, win, gq, wuq, gkv, wukv, cw, cb, rope, tm, seq_rows):
    b, s, d = x.shape
    latent = rope is not None
    grid = (b, s // tm)
    tok = lambda w: pl.BlockSpec((1, tm, w), lambda i, j: (i, j, 0))
    mod_spec = pl.BlockSpec((1, 6, d), lambda i, j: (i + mod_off, 0, 0))
    consts = [gpre, win, gq, wuq, gkv, wukv, cw, cb]
    in_specs = [tok(d), mod_spec]
    args = [x, mod]
    in_specs += [_const_spec(c.shape) for c in consts]
    args += consts
    widths = [MLA_HEADS * QK_PAD, MLA_HEADS * QK_PAD, MLA_HEADS * V_HEAD] if latent else [MLA_HEADS * V_HEAD]
    widths += [SSM_WIDTH, CONV_DIM]
    out_shape = [jax.ShapeDtypeStruct((b, s, w), BF16) for w in widths]
    out_shape.append(jax.ShapeDtypeStruct((b, s, LANES), F32))
    out_specs = [tok(w) for w in widths] + [tok(LANES)]
    if latent:
        in_specs += [pl.BlockSpec((tm, LANES), lambda i, j: (j, 0))] * 2
        args += list(rope)
    else:
        assert b == 1
        out_shape += [jax.ShapeDtypeStruct((b, s, KV_LORA), F32),
                      jax.ShapeDtypeStruct((s // seq_rows, 1, seq_rows, QK_ROPE), F32)]
        out_specs += [tok(KV_LORA),
                      pl.BlockSpec((tm // seq_rows, 1, seq_rows, QK_ROPE), lambda i, j: (j, 0, 0, 0))]
    return pl.pallas_call(
        functools.partial(_inproj_kernel, latent=latent, seq_rows=seq_rows),
        grid=grid, in_specs=in_specs, out_specs=out_specs, out_shape=out_shape,
        compiler_params=_params(("parallel", "parallel")),
        name="inproj_latent" if latent else "inproj_ctx",
    )(*args)


def _attn_kernel(q_ref, k_ref, v_ref, cckv_ref, ckr_ref, wukv_ref, o_ref):
    kv_c = _dot(cckv_ref[0].astype(BF16), wukv_ref[...])
    kr_c = ckr_ref[0].astype(BF16)

    def scores(hd):
        qs = slice(hd * QK_PAD, (hd + 1) * QK_PAD)
        q = q_ref[0, :, qs]
        kc = jnp.concatenate([kv_c[:, hd * QK_PAD:hd * QK_PAD + LANES].astype(BF16), kr_c], axis=1)
        return _dot_nt(k_ref[0, :, qs], q), _dot_nt(kc, q)

    nxt = scores(0)
    for hd in range(MLA_HEADS):
        s, sc = nxt
        if hd + 1 < MLA_HEADS:
            nxt = scores(hd + 1)
        vs = slice(hd * V_HEAD, (hd + 1) * V_HEAD)
        mx = jnp.maximum(jnp.max(s, axis=0, keepdims=True), jnp.max(sc, axis=0, keepdims=True))
        p = jnp.exp2(s - mx)
        pc = jnp.exp2(sc - mx)
        ones = lambda rows: jnp.ones((BF16_ROWS, rows), BF16)
        v_t = jnp.concatenate([jnp.transpose(v_ref[0, :, vs].astype(F32)).astype(BF16),
                               ones(v_ref.shape[1])], axis=0)
        vc = kv_c[:, hd * QK_PAD + LANES:(hd + 1) * QK_PAD]
        vc_t = jnp.concatenate([jnp.transpose(vc).astype(BF16), ones(vc.shape[0])], axis=0)
        acc = _dot(v_t, p.astype(BF16)) + _dot(vc_t, pc.astype(BF16))
        out = acc[:V_HEAD] / acc[V_HEAD:V_HEAD + 1]
        o_ref[0, :, vs] = jnp.transpose(out).astype(BF16)


def _attention(q, k, v, cache_ckv, cache_kr, wukv):
    b, s, _ = q.shape
    past = cache_ckv.shape[1]
    qw, vw = MLA_HEADS * QK_PAD, MLA_HEADS * V_HEAD
    seq_blk = lambda rows, w: pl.BlockSpec((1, rows, w), lambda i: (i, 0, 0))
    return pl.pallas_call(
        _attn_kernel,
        grid=(b,),
        in_specs=[seq_blk(s, qw), seq_blk(s, qw), seq_blk(s, vw),
                  seq_blk(past, KV_LORA), seq_blk(past, LANES), _const_spec(wukv.shape)],
        out_specs=seq_blk(s, vw),
        out_shape=jax.ShapeDtypeStruct((b, s, vw), BF16),
        compiler_params=_params(("parallel",)),
        name="attn_latent",
    )(q, k, v, cache_ckv, cache_kr, wukv)


SSD_GROUP = LANES // N_DT


def _ssd_kernel(*refs, nseq, cps, has_init, n_cast):
    refs = list(refs)
    n_in = 3 + int(has_init) + 6
    cast_in = refs[n_in:n_in + n_cast]
    cast_out = refs[n_in + n_cast + 2:n_in + 2 * n_cast + 2]
    for src, dst in zip(cast_in, cast_out):
        dst[...] = src[...].astype(BF16)
    refs = refs[:n_in] + refs[n_in + n_cast:n_in + n_cast + 2] + refs[n_in + 2 * n_cast + 2:]
    if has_init:
        (act_ref, z_ref, small_ref, init_ref, dtbias_ref, alog_ref, dskip_ref, gout_ref,
         tril_ref, e64_ref, out_ref, fin_ref,
         fg_s, wsm_s, rt_s, bt_s, y_s, xwb_s, eg_s, sf_s, sb_s) = refs
    else:
        (act_ref, z_ref, small_ref, dtbias_ref, alog_ref, dskip_ref, gout_ref,
         tril_ref, e64_ref, out_ref, fin_ref,
         fg_s, wsm_s, rt_s, bt_s, y_s, xwb_s, eg_s, sf_s, sb_s) = refs
    L = CHUNK
    G = nseq * cps
    heads_per_group = SSM_HEADS // SSM_GROUPS
    b_off = SSM_WIDTH
    c_off = SSM_WIDTH + SSM_GROUPS * D_STATE

    lane = lax.broadcasted_iota(jnp.int32, (L, LANES), 1)
    packed = jnp.zeros((L, LANES), F32)
    for c in range(G):
        raw = small_ref[c // cps, (c % cps) * L:(c % cps + 1) * L, :]
        shifted = pltpu.roll(raw, (c * N_DT - DT_OFF) % LANES, axis=1)
        packed = jnp.where((lane >= c * N_DT) & (lane < (c + 1) * N_DT), shifted, packed)
    dtc = jax.nn.softplus(packed + dtbias_ref[...])
    da = dtc * (-jnp.exp(alog_ref[...]))
    cum = _dot3_rhs(tril_ref[...], da)
    tot = cum[L - 1:L, :]
    suf = tot - cum + da
    fg = jnp.where(jnp.bitwise_and(lane, SSM_HEADS) == 0, cum, suf)
    w_small = dtc * jnp.exp(tot - fg)
    e_small = jnp.exp(fg)
    rt_s[...] = jnp.transpose(fg - jnp.log(dtc))
    for c in range(G):
        back = (LANES - c * N_DT) % LANES
        unroll = lambda t: t if back == 0 else pltpu.roll(t, back, axis=1)
        fg_s[c] = unroll(fg)
        wsm_s[c, 0:L, :] = unroll(w_small).astype(BF16)
        wsm_s[c, L:2 * L, :] = unroll(e_small).astype(BF16)

    for s in range(nseq):
        if has_init:
            sf_s[s] = jnp.transpose(init_ref[s, 0])
            sb_s[s] = jnp.transpose(init_ref[s, 1])
        else:
            sf_s[s] = jnp.zeros((D_STATE, SSM_WIDTH), F32)
            sb_s[s] = jnp.zeros((D_STATE, SSM_WIDTH), F32)

    row_t = lax.broadcasted_iota(jnp.int32, (L, L), 0)
    col_s = lax.broadcasted_iota(jnp.int32, (L, L), 1)
    lower = col_s <= row_t
    upper = col_s >= row_t
    lane_lo = lane < SSM_HEAD_DIM
    neg_inf = jnp.float32(-jnp.inf)

    def locate(c):
        sq, j = divmod(c, cps)
        return sq, slice(j * L, (j + 1) * L), slice(c * L, (c + 1) * L)

    def bt_rows(c, g):
        r = (c * SSM_GROUPS + g) * D_STATE
        return slice(r, r + D_STATE)

    for c in range(G):
        sq, rs, fs = locate(c)
        xs_b = act_ref[sq, rs, 0:SSM_WIDTH]
        fgc = fg_s[c]
        rtc = rt_s[c * N_DT:(c + 1) * N_DT, :]
        ex = _dot(wsm_s[c], e64_ref[...])
        w_b = ex[0:L, :].astype(BF16)
        ef_x = ex[L:2 * L, :SSM_WIDTH]
        eg_x = ex[L:2 * L, SSM_WIDTH:]
        eg_s[fs, :] = eg_x
        xwf_b = xs_b * w_b[:, :SSM_WIDTH]
        xwb_s[fs, :] = xs_b * w_b[:, SSM_WIDTH:]

        y_parts = []
        new_states = []
        for g in range(SSM_GROUPS):
            gs = slice(g * GROUP_W, (g + 1) * GROUP_W)
            bm_g = act_ref[sq, rs, b_off + g * D_STATE:b_off + (g + 1) * D_STATE]
            cm_g = act_ref[sq, rs, c_off + g * D_STATE:c_off + (g + 1) * D_STATE]
            bt_g = jnp.transpose(bm_g.astype(F32)).astype(BF16)
            bt_s[bt_rows(c, g), :] = bt_g
            cbm = _dot_nt(cm_g, bm_g)
            for pair in range(heads_per_group // 2):
                ws = []
                for k in range(2):
                    cf = g * heads_per_group + pair * 2 + k
                    seg_f = fgc[:, cf:cf + 1] - rtc[cf:cf + 1, :]
                    lf = jnp.exp(jnp.where(lower, seg_f, neg_inf))
                    cg = cf + SSM_HEADS
                    seg_b = fgc[:, cg:cg + 1] - rtc[cg:cg + 1, :]
                    ub = jnp.exp(jnp.where(upper, seg_b, neg_inf))
                    ws.append((cbm * (lf + ub)).astype(BF16))
                p0 = (g * heads_per_group + pair * 2) * SSM_HEAD_DIM
                xpair = xs_b[:, p0:p0 + LANES]
                zeros = jnp.zeros_like(xpair)
                rhs = jnp.concatenate([jnp.where(lane_lo, xpair, zeros),
                                       jnp.where(lane_lo, zeros, xpair)], axis=0)
                y_parts.append(_dot(jnp.concatenate(ws, axis=1), rhs))
            s_in = sf_s[sq, :, gs]
            y_off = _dot(cm_g, s_in.astype(BF16)) * ef_x[:, gs]
            y_parts[-2] = y_parts[-2] + y_off[:, :LANES]
            y_parts[-1] = y_parts[-1] + y_off[:, LANES:]
            new_states.append(s_in * ef_x[L - 1:L, gs] + _dot(bt_g, xwf_b[:, gs]))
        for g in range(SSM_GROUPS):
            sf_s[sq, :, g * GROUP_W:(g + 1) * GROUP_W] = new_states[g]
        for i, yp in enumerate(y_parts):
            y_s[fs, i * LANES:(i + 1) * LANES] = yp

    for c in reversed(range(G)):
        sq, rs, fs = locate(c)
        xs = act_ref[sq, rs, 0:SSM_WIDTH].astype(F32)
        eg_x = eg_s[fs, :]
        xwb_b = xwb_s[fs, :]
        y_off = []
        for g in range(SSM_GROUPS):
            gs = slice(g * GROUP_W, (g + 1) * GROUP_W)
            s_in = sb_s[sq, :, gs]
            cm_g = act_ref[sq, rs, c_off + g * D_STATE:c_off + (g + 1) * D_STATE]
            y_off.append(_dot(cm_g, s_in.astype(BF16)) * eg_x[:, gs])
            sb_s[sq, :, gs] = s_in * eg_x[0:1, gs] + _dot(bt_s[bt_rows(c, g), :], xwb_b[:, gs])
        y = y_s[fs, :] + jnp.concatenate(y_off, axis=1) + dskip_ref[...] * xs
        y = y * _silu(z_ref[sq, rs, :].astype(F32))
        out_ref[sq, rs, :] = _rms(y, gout_ref[...]).astype(BF16)

    for s in range(nseq):
        fin_ref[s, 0] = jnp.transpose(sf_s[s])
        fin_ref[s, 1] = jnp.transpose(sb_s[s])


def _ssd(act, z, small, init, consts, cast=()):
    b, s, _ = act.shape
    has_init = init is not None
    cps = s // CHUNK
    assert SSD_GROUP % cps == 0
    nseq = SSD_GROUP // cps
    assert b % nseq == 0
    steps = b // nseq
    blk = lambda w: pl.BlockSpec((nseq, s, w), lambda i: (i, 0, 0))
    st_spec = pl.BlockSpec((nseq, 2, SSM_WIDTH, D_STATE), lambda i: (i, 0, 0, 0))
    in_specs = [blk(CONV_DIM), blk(SSM_WIDTH), blk(LANES)]
    args = [act, z, small]
    if has_init:
        in_specs.append(st_spec)
        args.append(init)
    in_specs += [_const_spec(c.shape) for c in consts]
    args += list(consts)
    cast_specs = [pl.BlockSpec((w.shape[0] // steps, w.shape[1]), lambda i: (i, 0)) for w in cast]
    in_specs += cast_specs
    args += list(cast)
    rows = SSD_GROUP * CHUNK
    scratch = [pltpu.VMEM((SSD_GROUP, CHUNK, LANES), F32),
               pltpu.VMEM((SSD_GROUP, 2 * CHUNK, LANES), BF16),
               pltpu.VMEM((LANES, CHUNK), F32),
               pltpu.VMEM((SSD_GROUP * SSM_GROUPS * D_STATE, CHUNK), BF16),
               pltpu.VMEM((rows, SSM_WIDTH), F32),
               pltpu.VMEM((rows, SSM_WIDTH), BF16),
               pltpu.VMEM((rows, SSM_WIDTH), F32),
               pltpu.VMEM((nseq, D_STATE, SSM_WIDTH), F32),
               pltpu.VMEM((nseq, D_STATE, SSM_WIDTH), F32)]
    return pl.pallas_call(
        functools.partial(_ssd_kernel, nseq=nseq, cps=cps, has_init=has_init, n_cast=len(cast)),
        grid=(steps,), in_specs=in_specs,
        out_specs=[blk(SSM_WIDTH), st_spec] + cast_specs,
        out_shape=[jax.ShapeDtypeStruct((b, s, SSM_WIDTH), BF16),
                   jax.ShapeDtypeStruct((b, 2, SSM_WIDTH, D_STATE), F32)]
                  + [jax.ShapeDtypeStruct(w.shape, BF16) for w in cast],
        scratch_shapes=scratch,
        compiler_params=_params(("parallel",)),
        name="ssd_latent" if has_init else "ssd_ctx",
    )(*args)


FF_CHUNK = 256


def _outffn_kernel(x_ref, attn_ref, ssm_ref, mod_ref, wout_ref, gpost_ref, gpre_ref, gpostf_ref,
                   wg_ref, wu_ref, wd_ref, o_ref, mix_s, y_s, h_s):
    m = mod_ref[0]
    half = MLA_HEADS * V_HEAD
    rows = x_ref.shape[1] // 2
    nslice = 8
    srows = rows // nslice
    nchunks = D_FF // FF_CHUNK
    g1 = gpost_ref[...] * m[2:3]
    g2 = gpre_ref[...] * (1.0 + m[4:5])
    sh2 = m[3:4]
    g3 = gpostf_ref[...] * m[5:6]

    def out_proj(rs):
        return _dot(attn_ref[0, rs, :], wout_ref[0:half, :]) + _dot(ssm_ref[0, rs, :], wout_ref[half:, :])

    def pre_ffn(mix, xr):
        y = xr + _rms(mix, g1)
        return y, (_rms(y, g2) + sh2).astype(BF16)

    def ffn_chunk(h, c, acc):
        cs = slice(c * FF_CHUNK, (c + 1) * FF_CHUNK)
        part = _dot((_silu(_dot(h, wg_ref[:, cs])) * _dot(h, wu_ref[:, cs])).astype(BF16), wd_ref[cs, :])
        return part if acc is None else acc + part

    y_a, h_a = pre_ffn(out_proj(slice(0, rows)), x_ref[0, 0:rows, :])
    mix_s[...] = out_proj(slice(rows, 2 * rows))
    acc_a = None
    for c in range(nchunks):
        acc_a = ffn_chunk(h_a, c, acc_a)
        if c < nslice:
            rs = slice(c * srows, (c + 1) * srows)
            y_b, h_b = pre_ffn(mix_s[rs, :], x_ref[0, rows + c * srows:rows + (c + 1) * srows, :])
            y_s[rs, :] = y_b
            h_s[rs, :] = h_b
    h_bb = h_s[...]
    acc_b = None
    for c in range(nchunks):
        acc_b = ffn_chunk(h_bb, c, acc_b)
        if c < nslice:
            rs = slice(c * srows, (c + 1) * srows)
            o_ref[0, rs, :] = y_a[rs] + _rms(acc_a[rs], g3)
    o_ref[0, rows:2 * rows, :] = y_s[...] + _rms(acc_b, g3)


def _outffn(x, attn, ssm, mod, mod_off, wout, gpost, gpre, gpostf, wg, wu, wd, tm):
    b, s, d = x.shape
    tok = lambda w: pl.BlockSpec((1, tm, w), lambda i, j: (i, j, 0))
    consts = [wout, gpost, gpre, gpostf, wg, wu, wd]
    return pl.pallas_call(
        _outffn_kernel,
        grid=(b, s // tm),
        in_specs=[tok(d), tok(MLA_HEADS * V_HEAD), tok(SSM_WIDTH),
                  pl.BlockSpec((1, 6, d), lambda i, j: (i + mod_off, 0, 0))]
                 + [_const_spec(c.shape) for c in consts],
        out_specs=tok(d),
        out_shape=jax.ShapeDtypeStruct((b, s, d), F32),
        scratch_shapes=[pltpu.VMEM((tm // 2, d), F32),
                        pltpu.VMEM((tm // 2, d), F32),
                        pltpu.VMEM((tm // 2, d), BF16)],
        compiler_params=_params(("parallel", "parallel")),
        name="out_ffn",
    )(x, attn, ssm, mod, *consts)


def _rope_tables(length):
    quarter = QK_ROPE // 4
    pos = np.arange(length)
    inv_freq = ROPE_THETA ** (-np.arange(quarter, dtype=np.float64) / quarter)
    ang_r = (pos // GRID_W)[:, None] * inv_freq[None, :]
    ang_c = (pos % GRID_W)[:, None] * inv_freq[None, :]
    cos = np.concatenate([np.cos(ang_r)] * 2 + [np.cos(ang_c)] * 2, axis=1)
    sin = np.concatenate([-np.sin(ang_r), np.sin(ang_r), -np.sin(ang_c), np.sin(ang_c)], axis=1)
    return jnp.asarray(np.tile(cos, (1, 2)), F32), jnp.asarray(np.tile(sin, (1, 2)), F32)


def _pad_cols(w, width, left=0):
    return jnp.pad(w, ((0, 0), (left, width - left - w.shape[-1])))


def _ssd_constants():
    idx = np.arange(CHUNK)
    tril = (idx[None, :] <= idx[:, None]).astype(np.float32)
    e64 = np.zeros((LANES, N_DT * SSM_HEAD_DIM), np.float32)
    for q in range(N_DT):
        e64[q, q * SSM_HEAD_DIM:(q + 1) * SSM_HEAD_DIM] = 1.0
    return [jnp.asarray(a, BF16) for a in (tril, e64)]


def kernel(x_prompt, x_sample, cache_ckv, cache_krope, state_ssm, c, c_ctx, w_mod, b_mod,
           g_pre_mix, g_post_mix, w_in, g_q, w_uq, g_kv, w_ukv, conv_w, conv_b, dt_bias,
           a_log, d_skip, g_ssm_out, w_out, g_pre_ffn, g_post_ffn, w_gate, w_up, w_down):
    depth = w_in.shape[0]
    assert depth == 1
    nb, seq, d = x_prompt.shape
    db, dseq, _ = x_sample.shape
    l = 0

    rows = 16
    cond = jnp.concatenate([c_ctx[None, :], c, jnp.zeros((rows - 1 - db, d), F32)], axis=0)
    mod = _modulation(cond, w_mod[l], b_mod[l][None, :]).reshape(rows, 6, d)

    wi = w_in[l]
    o_q, o_kv, o_kr = Q_LORA, Q_LORA + KV_LORA, Q_LORA + KV_LORA + QK_ROPE
    o_z, o_xbc = o_kr + SSM_WIDTH, o_kr + SSM_WIDTH + CONV_DIM
    kr_dt = _pad_cols(jnp.concatenate([wi[:, o_kv:o_kr], wi[:, o_xbc:]], axis=1), LANES)
    win = jnp.concatenate([wi[:, :o_q], wi[:, o_q:o_kv], wi[:, o_kr:o_z], wi[:, o_z:o_xbc], kr_dt],
                          axis=1).astype(BF16)
    wq = w_uq[l].reshape(Q_LORA, MLA_HEADS, QK_NOPE + QK_ROPE)
    wuq = jnp.concatenate([wq[:, :, :QK_NOPE].reshape(Q_LORA, MLA_HEADS * QK_NOPE),
                           wq[:, :, QK_NOPE:].reshape(Q_LORA, MLA_HEADS * QK_ROPE)], axis=1).astype(BF16)
    wukv = w_ukv[l].astype(BF16)
    wout = w_out[l].astype(BF16)
    row = lambda v: v.reshape(1, -1)
    gpre, gq, gkv = row(g_pre_mix[l]), row(g_q[l]), row(g_kv[l])
    gpost, gpre_f, gpost_f = row(g_post_mix[l]), row(g_pre_ffn[l]), row(g_post_ffn[l])
    cw, cb = conv_w[l], row(conv_b[l])

    small = lambda v: jnp.tile(v.reshape(1, -1), (1, SSD_GROUP))
    ssd_consts = [small(dt_bias[l]), small(a_log[l]),
                  row(jnp.repeat(d_skip[l], SSM_HEAD_DIM)), row(g_ssm_out[l])] + _ssd_constants()

    xp = x_prompt.reshape(1, nb * seq, d)
    attn, z, act, sm, ckv, krope = _inproj(xp, mod, 0, gpre, win, gq, wuq, gkv, wukv, cw, cb,
                                           None, tm=1024, seq_rows=seq)
    per_seq = lambda a: a.reshape(nb, seq, a.shape[-1])
    ssm, fin, wg, wu, wd = _ssd(per_seq(act), per_seq(z), per_seq(sm), None, ssd_consts,
                                cast=(w_gate[l], w_up[l], w_down[l]))
    y_p = _outffn(xp, attn, ssm.reshape(1, nb * seq, -1), mod, 0,
                  wout, gpost, gpre_f, gpost_f, wg, wu, wd, tm=1024).reshape(nb, seq, d)
    new_ckv = ckv.reshape(nb, 1, seq, KV_LORA)
    new_krope = krope
    new_ssm = fin.reshape(nb, 1, 2, SSM_HEADS, SSM_HEAD_DIM, D_STATE)

    kr_cache = jnp.pad(cache_krope[:, l], ((0, 0), (0, 0), (0, LANES - QK_ROPE)))
    q, k, v, z, act, sm = _inproj(x_sample, mod, 1, gpre, win, gq, wuq, gkv, wukv, cw, cb,
                                  _rope_tables(dseq), tm=1024, seq_rows=dseq)
    attn = _attention(q, k, v, cache_ckv[:, l], kr_cache, wukv)
    init = state_ssm[:, l].reshape(db, 2, SSM_WIDTH, D_STATE)
    ssm, _ = _ssd(act, z, sm, init, ssd_consts)
    y_s = _outffn(x_sample, attn, ssm, mod, 1, wout, gpost, gpre_f, gpost_f, wg, wu, wd, tm=1024)

    return (y_p, y_s, new_ckv, new_krope, new_ssm)
```

```python
import functools

import numpy as np
import jax
import jax.numpy as jnp
from jax import lax
from jax.experimental import pallas as pl
from jax.experimental.pallas import tpu as pltpu

F32 = jnp.float32
BF16 = jnp.bfloat16

D_MODEL = 1024
GRID_W = 64
ROPE_THETA = 10000.0
NORM_EPS = 1e-6
MLA_HEADS = 4
QK_NOPE = 128
QK_ROPE = 64
V_HEAD = 128
Q_LORA = 384
KV_LORA = 256
SSM_HEADS = 8
SSM_HEAD_DIM = 64
SSM_WIDTH = SSM_HEADS * SSM_HEAD_DIM
SSM_GROUPS = 2
D_STATE = 128
CHUNK = 128
CONV_DIM = SSM_WIDTH + 2 * SSM_GROUPS * D_STATE
D_FF = 2816

LOG2E = 1.4426950408889634
LANES = 128
SUBLANES = 8
BF16_ROWS = 16
QK_PAD = 256
GROUP_W = SSM_WIDTH // SSM_GROUPS
VMEM_LIMIT = 56 * 1024 * 1024

_SEG_Q = (0, 384)
_SEG_KV = (384, 640)
_SEG_Z = (640, 1152)
_SEG_XBC = (1152, 2176)
_SEG_KR = (2176, 2304)
DT_OFF = QK_ROPE
N_DT = 2 * SSM_HEADS


def _rms(x, g):
    return x * lax.rsqrt(jnp.mean(x * x, axis=-1, keepdims=True) + NORM_EPS) * g


def _silu(x):
    u = 0.5 * x
    return u * jnp.tanh(u) + u


def _dot(a, b):
    return jnp.dot(a, b, preferred_element_type=F32)


def _dot_nt(a, b):
    return lax.dot_general(a, b, (((1,), (1,)), ((), ())), preferred_element_type=F32)


def _split3(x):
    hi = x.astype(BF16)
    r = x - hi.astype(F32)
    mid = r.astype(BF16)
    lo = (r - mid.astype(F32)).astype(BF16)
    return hi, mid, lo


def _dot3_rhs(m, x):
    hi, mid, lo = _split3(x)
    return _dot(m, hi) + _dot(m, mid) + _dot(m, lo)


def _mod_spec(d, mod_off):
    return pl.BlockSpec((SUBLANES, 6 * d), lambda i, j: ((i + mod_off) // SUBLANES, 0))


def _mod_terms(mod_ref, mod_off):
    row = lax.rem(pl.program_id(0) + mod_off, SUBLANES)
    r = mod_ref[pl.ds(row, 1), :]
    d = r.shape[1] // 6
    return [r[:, k * d:(k + 1) * d] for k in range(6)]


def _const_spec(shape):
    nd = len(shape)
    return pl.BlockSpec(shape, lambda *_: (0,) * nd, pipeline_mode=pl.Buffered(1))


def _params(sem):
    return pltpu.CompilerParams(dimension_semantics=sem, vmem_limit_bytes=VMEM_LIMIT)


def _mod_kernel(c_ref, w_ref, b_ref, wi_ref, dt_ref, wq_ref, wkv_ref, wo_ref,
                o_ref, win_ref, wuq_ref, wukv_ref, wout_ref):
    s = _silu(c_ref[...])
    rows = s.shape[0]
    s_hi = s.astype(BF16)
    s_lo = (s - s_hi.astype(F32)).astype(BF16)
    both = _dot(jnp.concatenate([s_hi, s_lo], axis=0), w_ref[...].astype(BF16))
    o_ref[...] = both[:rows] + both[rows:] + b_ref[...]

    o_kv, o_kr = Q_LORA + KV_LORA, Q_LORA + KV_LORA + QK_ROPE
    wide = _SEG_XBC[1] - _SEG_Z[0]
    win_ref[:, 0:o_kv] = wi_ref[:, 0:o_kv].astype(BF16)
    win_ref[:, _SEG_Z[0]:_SEG_XBC[1]] = wi_ref[:, o_kr:o_kr + wide].astype(BF16)
    lane = lax.broadcasted_iota(jnp.int32, (wi_ref.shape[0], LANES), 1)
    kr_blk = wi_ref[:, o_kv:o_kv + LANES]
    win_ref[:, _SEG_KR[0]:_SEG_KR[1]] = jnp.where(lane < QK_ROPE, kr_blk, dt_ref[...]).astype(BF16)
    qk = QK_NOPE + QK_ROPE
    for hd in range(MLA_HEADS):
        wuq_ref[:, hd * QK_NOPE:(hd + 1) * QK_NOPE] = wq_ref[:, hd * qk:hd * qk + QK_NOPE].astype(BF16)
        r0 = MLA_HEADS * QK_NOPE + hd * QK_ROPE
        wuq_ref[:, r0:r0 + QK_ROPE] = wq_ref[:, hd * qk + QK_NOPE:(hd + 1) * qk].astype(BF16)
    wukv_ref[...] = wkv_ref[...].astype(BF16)
    wout_ref[...] = wo_ref[...].astype(BF16)


def _modulation(cond, w_mod, b_mod, w_in, dt_cols, w_uq, w_ukv, w_out):
    rows, d = cond.shape
    n = w_mod.shape[1]
    tn = 1536
    steps = n // tn
    row_blk = lambda w, cols=None: pl.BlockSpec((w.shape[0] // steps, cols or w.shape[1]), lambda j: (j, 0))
    extra = [w_in, dt_cols, w_uq, w_ukv, w_out]
    return pl.pallas_call(
        _mod_kernel,
        grid=(steps,),
        in_specs=[pl.BlockSpec((rows, d), lambda j: (0, 0)),
                  pl.BlockSpec((d, tn), lambda j: (0, j)),
                  pl.BlockSpec((1, tn), lambda j: (0, j))] + [row_blk(w) for w in extra],
        out_specs=[pl.BlockSpec((rows, tn), lambda j: (0, j)),
                   row_blk(w_in, _SEG_KR[1]), row_blk(w_uq), row_blk(w_ukv), row_blk(w_out)],
        out_shape=[jax.ShapeDtypeStruct((rows, n), F32),
                   jax.ShapeDtypeStruct((w_in.shape[0], _SEG_KR[1]), BF16)]
                  + [jax.ShapeDtypeStruct(w.shape, BF16) for w in (w_uq, w_ukv, w_out)],
        compiler_params=_params(("arbitrary",)),
        name="modulation",
    )(cond, w_mod, b_mod, *extra)


def _conv_silu(xb, prev_rows, next_rows, cw_ref, cb_ref):
    n = xb.shape[0]
    win = jnp.concatenate([prev_rows, xb, next_rows], axis=0)
    total = n + 2 * SUBLANES
    prev = pltpu.roll(win, 1, axis=0)[SUBLANES:SUBLANES + n]
    nxt = pltpu.roll(win, total - 1, axis=0)[SUBLANES:SUBLANES + n]
    conv = cb_ref[...] + prev * cw_ref[0:1, :] + xb * cw_ref[1:2, :] + nxt * cw_ref[2:3, :]
    return _silu(conv)


def _inproj_kernel(*refs, latent, seq_rows, mod_off):
    if latent:
        (x_ref, mod_ref, gpre_ref, win_ref, gq_ref, wuq_ref, gkv_ref, wukv_ref, cw_ref, cb_ref,
         cos_ref, sin_ref, cckv_ref, ckr_ref, attn_ref, z_ref, act_ref, small_ref,
         q_ref, k_ref, v_ref) = refs
    else:
        (x_ref, mod_ref, gpre_ref, win_ref, gq_ref, wuq_ref, gkv_ref, wukv_ref, cw_ref, cb_ref,
         attn_ref, z_ref, act_ref, small_ref, ckv_ref, krope_ref) = refs
    scale = (QK_NOPE + QK_ROPE) ** -0.5 * LOG2E
    shift, mscale = _mod_terms(mod_ref, mod_off)[:2]

    g_mod = gpre_ref[...] * (1.0 + mscale)

    def pre(xv):
        return (_rms(xv, g_mod) + shift).astype(BF16)

    tm = x_ref.shape[1]
    rows = tm // 2
    halves = (slice(0, rows), slice(rows, tm))
    assert seq_rows == tm or rows % seq_rows == 0
    zero_rows = jnp.zeros((SUBLANES, CONV_DIM), F32)
    projs = [_dot(pre(x_ref[0, halves[0], :]), win_ref[...])]
    if seq_rows == tm:
        ph = _dot(pre(x_ref[0, rows - SUBLANES:rows + SUBLANES, :]), win_ref[:, _SEG_XBC[0]:_SEG_XBC[1]])
        halo = ((zero_rows, ph[SUBLANES:]), (ph[:SUBLANES], zero_rows))
    projs.append(_dot(pre(x_ref[0, halves[1], :]), win_ref[...]))

    lane = lax.broadcasted_iota(jnp.int32, (rows, LANES), 1)
    rope_lanes = lane < QK_ROPE
    first_quarter = jnp.bitwise_and(lane, QK_ROPE // 4) == 0
    nw = MLA_HEADS * QK_NOPE
    for hf, rs in enumerate(halves):
        proj = projs[hf]
        seg = lambda s: proj[:, s[0]:s[1]]
        xbc = seg(_SEG_XBC)
        if seq_rows == tm:
            act_ref[0, rs, :] = _conv_silu(xbc, halo[hf][0], halo[hf][1], cw_ref, cb_ref).astype(BF16)
        else:
            for s in range(rows // seq_rows):
                sub = slice(s * seq_rows, (s + 1) * seq_rows)
                dst = slice(rs.start + sub.start, rs.start + sub.stop)
                act_ref[0, dst, :] = _conv_silu(xbc[sub], zero_rows, zero_rows, cw_ref, cb_ref).astype(BF16)
        krb = seg(_SEG_KR)
        small_ref[0, rs, :] = krb
        ckv = _rms(seg(_SEG_KV), gkv_ref[...])
        qall = _dot(_rms(seg(_SEG_Q), gq_ref[...]).astype(BF16), wuq_ref[...])
        kv = _dot(ckv.astype(BF16), wukv_ref[...])
        z_ref[0, rs, :] = seg(_SEG_Z).astype(BF16)
        if latent:
            cos = cos_ref[rs, :]
            sin = sin_ref[rs, :]

            def rot(t):
                swapped = jnp.where(first_quarter, pltpu.roll(t, LANES - QK_ROPE // 4, axis=1),
                                    pltpu.roll(t, QK_ROPE // 4, axis=1))
                return t * cos + swapped * sin
        else:
            rot = lambda t: t
            ckv_ref[0, rs, :] = ckv
            for s in range(rows // seq_rows):
                krope_ref[rs.start // seq_rows + s, 0] = krb[s * seq_rows:(s + 1) * seq_rows, :QK_ROPE]
        kr_b = jnp.where(rope_lanes, rot(krb), 0.0).astype(BF16)
        for pr in range(MLA_HEADS // 2):
            pair = rot(qall[:, nw + pr * LANES:nw + (pr + 1) * LANES]) * scale
            for k in range(2):
                hd = 2 * pr + k
                base = hd * QK_PAD
                qr = pair if k == 0 else pltpu.roll(pair, QK_ROPE, axis=1)
                q_n = (qall[:, hd * LANES:(hd + 1) * LANES] * scale).astype(BF16)
                q_r = jnp.where(rope_lanes, qr, 0.0).astype(BF16)
                k_n = kv[:, base:base + LANES].astype(BF16)
                v_h = kv[:, base + LANES:base + QK_PAD].astype(BF16)
                if latent:
                    q_ref[0, rs, base:base + LANES] = q_n
                    q_ref[0, rs, base + LANES:base + QK_PAD] = q_r
                    k_ref[0, rs, base:base + LANES] = k_n
                    k_ref[0, rs, base + LANES:base + QK_PAD] = kr_b
                    v_ref[0, rs, hd * LANES:(hd + 1) * LANES] = v_h
                else:
                    q_h = jnp.concatenate([q_n, q_r], axis=1)
                    k_h = jnp.concatenate([k_n, kr_b], axis=1)
                    for s in range(rows // seq_rows):
                        sub = slice(s * seq_rows, (s + 1) * seq_rows)
                        sc = _dot_nt(q_h[sub], k_h[sub])
                        p = jnp.exp2(sc - jnp.max(sc, axis=-1, keepdims=True))
                        den = jnp.sum(p, axis=-1, keepdims=True)
                        dst = slice(rs.start + sub.start, rs.start + sub.stop)
                        attn_ref[0, dst, hd * V_HEAD:(hd + 1) * V_HEAD] = (
                            _dot(p.astype(BF16), v_h[sub]) / den).astype(BF16)
    if latent:
        _attn_body(q_ref, k_ref, v_ref, cckv_ref, ckr_ref, wukv_ref, attn_ref)


def _inproj(x, mod, mod_off, gpre, win, gq, wuq, gkv, wukv, cw, cb, rope, tm, seq_rows):
    b, s, d = x.shape
    latent = rope is not None
    grid = (b, s // tm)
    tok = lambda w: pl.BlockSpec((1, tm, w), lambda i, j: (i, j, 0))
    mod_spec = _mod_spec(d, mod_off)
    consts = [gpre, win, gq, wuq, gkv, wukv, cw, cb]
    in_specs = [tok(d), mod_spec]
    args = [x, mod]
    in_specs += [_const_spec(c.shape) for c in consts]
    args += consts
    widths = [MLA_HEADS * V_HEAD, SSM_WIDTH, CONV_DIM]
    out_shape = [jax.ShapeDtypeStruct((b, s, w), BF16) for w in widths]
    out_shape.append(jax.ShapeDtypeStruct((b, s, LANES), F32))
    out_specs = [tok(w) for w in widths] + [tok(LANES)]
    scratch = []
    if latent:
        cos, sin, cache_ckv, cache_kr = rope
        assert tm == s
        in_specs += [pl.BlockSpec((tm, LANES), lambda i, j: (j, 0))] * 2
        in_specs += [pl.BlockSpec((1,) + a.shape[1:], lambda i, j: (i, 0, 0)) for a in (cache_ckv, cache_kr)]
        args += [cos, sin, cache_ckv, cache_kr]
        scratch = [pltpu.VMEM((1, tm, w), BF16)
                   for w in (MLA_HEADS * QK_PAD, MLA_HEADS * QK_PAD, MLA_HEADS * V_HEAD)]
    else:
        assert b == 1
        out_shape += [jax.ShapeDtypeStruct((b, s, KV_LORA), F32),
                      jax.ShapeDtypeStruct((s // seq_rows, 1, seq_rows, QK_ROPE), F32)]
        out_specs += [tok(KV_LORA),
                      pl.BlockSpec((tm // seq_rows, 1, seq_rows, QK_ROPE), lambda i, j: (j, 0, 0, 0))]
    return pl.pallas_call(
        functools.partial(_inproj_kernel, latent=latent, seq_rows=seq_rows, mod_off=mod_off),
        grid=grid, in_specs=in_specs, out_specs=out_specs, out_shape=out_shape,
        scratch_shapes=scratch,
        compiler_params=_params(("parallel", "parallel")),
        name="inproj_latent" if latent else "inproj_ctx",
    )(*args)


def _attn_body(q_ref, k_ref, v_ref, cckv_ref, ckr_ref, wukv_ref, o_ref):
    kv_c = _dot(cckv_ref[0].astype(BF16), wukv_ref[...])
    kr_c = ckr_ref[0].astype(BF16)

    def scores(hd):
        qs = slice(hd * QK_PAD, (hd + 1) * QK_PAD)
        q = q_ref[0, :, qs]
        kc = jnp.concatenate([kv_c[:, hd * QK_PAD:hd * QK_PAD + LANES].astype(BF16), kr_c], axis=1)
        return _dot_nt(k_ref[0, :, qs], q), _dot_nt(kc, q)

    nxt = scores(0)
    for hd in range(MLA_HEADS):
        s, sc = nxt
        if hd + 1 < MLA_HEADS:
            nxt = scores(hd + 1)
        vs = slice(hd * V_HEAD, (hd + 1) * V_HEAD)
        mx = jnp.maximum(jnp.max(s, axis=0, keepdims=True), jnp.max(sc, axis=0, keepdims=True))
        p = jnp.exp2(s - mx)
        pc = jnp.exp2(sc - mx)
        ones = lambda rows: jnp.ones((BF16_ROWS, rows), BF16)
        v_t = jnp.concatenate([jnp.transpose(v_ref[0, :, vs].astype(F32)).astype(BF16),
                               ones(v_ref.shape[1])], axis=0)
        vc = kv_c[:, hd * QK_PAD + LANES:(hd + 1) * QK_PAD]
        vc_t = jnp.concatenate([jnp.transpose(vc).astype(BF16), ones(vc.shape[0])], axis=0)
        acc = _dot(v_t, p.astype(BF16)) + _dot(vc_t, pc.astype(BF16))
        out = acc[:V_HEAD] / acc[V_HEAD:V_HEAD + 1]
        o_ref[0, :, vs] = jnp.transpose(out).astype(BF16)


SSD_PACK = LANES // N_DT
SSD_GROUP = SSD_PACK


def _ssd_kernel(*refs, nseq, cps, has_init, n_cast):
    refs = list(refs)
    n_in = 3 + int(has_init) + 6
    cast_in = refs[n_in:n_in + n_cast]
    cast_out = refs[n_in + n_cast + 2:n_in + 2 * n_cast + 2]
    for src, dst in zip(cast_in, cast_out):
        dst[...] = src[...].astype(BF16)
    refs = refs[:n_in] + refs[n_in + n_cast:n_in + n_cast + 2] + refs[n_in + 2 * n_cast + 2:]
    if has_init:
        (act_ref, z_ref, small_ref, init_ref, dtbias_ref, alog_ref, dskip_ref, gout_ref,
         tril_ref, e64_ref, out_ref, fin_ref,
         fg_s, wsm_s, rt_s, bt_s, y_s, xwb_s, eg_s, sf_s, sb_s) = refs
    else:
        (act_ref, z_ref, small_ref, dtbias_ref, alog_ref, dskip_ref, gout_ref,
         tril_ref, e64_ref, out_ref, fin_ref,
         fg_s, wsm_s, rt_s, bt_s, y_s, xwb_s, eg_s, sf_s, sb_s) = refs
    L = CHUNK
    G = nseq * cps
    heads_per_group = SSM_HEADS // SSM_GROUPS
    b_off = SSM_WIDTH
    c_off = SSM_WIDTH + SSM_GROUPS * D_STATE

    lane = lax.broadcasted_iota(jnp.int32, (L, LANES), 1)
    for pk in range(G // SSD_PACK):
        chunks = range(pk * SSD_PACK, (pk + 1) * SSD_PACK)
        packed = jnp.zeros((L, LANES), F32)
        for k, c in enumerate(chunks):
            raw = small_ref[c // cps, (c % cps) * L:(c % cps + 1) * L, :]
            shifted = pltpu.roll(raw, (k * N_DT - DT_OFF) % LANES, axis=1)
            packed = jnp.where((lane >= k * N_DT) & (lane < (k + 1) * N_DT), shifted, packed)
        dtc = jax.nn.softplus(packed + dtbias_ref[...])
        da = dtc * (-jnp.exp(alog_ref[...]))
        cum = _dot3_rhs(tril_ref[...], da)
        tot = cum[L - 1:L, :]
        suf = tot - cum + da
        fg = jnp.where(jnp.bitwise_and(lane, SSM_HEADS) == 0, cum, suf)
        w_small = dtc * jnp.exp(tot - fg)
        e_small = jnp.exp(fg)
        rt_s[pk * LANES:(pk + 1) * LANES, :] = jnp.transpose(fg - jnp.log(dtc))
        for k, c in enumerate(chunks):
            back = (LANES - k * N_DT) % LANES
            unroll = lambda t: t if back == 0 else pltpu.roll(t, back, axis=1)
            fg_s[c] = unroll(fg)
            wsm_s[c, 0:L, :] = unroll(w_small).astype(BF16)
            wsm_s[c, L:2 * L, :] = unroll(e_small).astype(BF16)

    for s in range(nseq):
        if has_init:
            sf_s[s] = jnp.transpose(init_ref[s, 0])
            sb_s[s] = jnp.transpose(init_ref[s, 1])
        else:
            sf_s[s] = jnp.zeros((D_STATE, SSM_WIDTH), F32)
            sb_s[s] = jnp.zeros((D_STATE, SSM_WIDTH), F32)

    row_t = lax.broadcasted_iota(jnp.int32, (L, L), 0)
    col_s = lax.broadcasted_iota(jnp.int32, (L, L), 1)
    lower = col_s <= row_t
    upper = col_s >= row_t
    lane_lo = lane < SSM_HEAD_DIM
    neg_inf = jnp.float32(-jnp.inf)

    def locate(c):
        sq, j = divmod(c, cps)
        return sq, slice(j * L, (j + 1) * L), slice(c * L, (c + 1) * L)

    def bt_rows(c, g):
        r = (c * SSM_GROUPS + g) * D_STATE
        return slice(r, r + D_STATE)

    for c in range(G):
        sq, rs, fs = locate(c)
        xs_b = act_ref[sq, rs, 0:SSM_WIDTH]
        fgc = fg_s[c]
        rtc = rt_s[c * N_DT:(c + 1) * N_DT, :]
        ex = _dot(wsm_s[c], e64_ref[...])
        w_b = ex[0:L, :].astype(BF16)
        ef_x = ex[L:2 * L, :SSM_WIDTH]
        eg_x = ex[L:2 * L, SSM_WIDTH:]
        eg_s[fs, :] = eg_x
        xwf_b = xs_b * w_b[:, :SSM_WIDTH]
        xwb_s[fs, :] = xs_b * w_b[:, SSM_WIDTH:]

        y_parts = []
        new_states = []
        for g in range(SSM_GROUPS):
            gs = slice(g * GROUP_W, (g + 1) * GROUP_W)
            bm_g = act_ref[sq, rs, b_off + g * D_STATE:b_off + (g + 1) * D_STATE]
            cm_g = act_ref[sq, rs, c_off + g * D_STATE:c_off + (g + 1) * D_STATE]
            bt_g = jnp.transpose(bm_g.astype(F32)).astype(BF16)
            bt_s[bt_rows(c, g), :] = bt_g
            cbm = _dot_nt(cm_g, bm_g)
            for pair in range(heads_per_group // 2):
                ws = []
                for k in range(2):
                    cf = g * heads_per_group + pair * 2 + k
                    seg_f = fgc[:, cf:cf + 1] - rtc[cf:cf + 1, :]
                    lf = jnp.exp(jnp.where(lower, seg_f, neg_inf))
                    cg = cf + SSM_HEADS
                    seg_b = fgc[:, cg:cg + 1] - rtc[cg:cg + 1, :]
                    ub = jnp.exp(jnp.where(upper, seg_b, neg_inf))
                    ws.append((cbm * (lf + ub)).astype(BF16))
                p0 = (g * heads_per_group + pair * 2) * SSM_HEAD_DIM
                xpair = xs_b[:, p0:p0 + LANES]
                zeros = jnp.zeros_like(xpair)
                rhs = jnp.concatenate([jnp.where(lane_lo, xpair, zeros),
                                       jnp.where(lane_lo, zeros, xpair)], axis=0)
                y_parts.append(_dot(jnp.concatenate(ws, axis=1), rhs))
            s_in = sf_s[sq, :, gs]
            y_off = _dot(cm_g, s_in.astype(BF16)) * ef_x[:, gs]
            y_parts[-2] = y_parts[-2] + y_off[:, :LANES]
            y_parts[-1] = y_parts[-1] + y_off[:, LANES:]
            new_states.append(s_in * ef_x[L - 1:L, gs] + _dot(bt_g, xwf_b[:, gs]))
        for g in range(SSM_GROUPS):
            sf_s[sq, :, g * GROUP_W:(g + 1) * GROUP_W] = new_states[g]
        for i, yp in enumerate(y_parts):
            y_s[fs, i * LANES:(i + 1) * LANES] = yp

    for c in reversed(range(G)):
        sq, rs, fs = locate(c)
        xs = act_ref[sq, rs, 0:SSM_WIDTH].astype(F32)
        eg_x = eg_s[fs, :]
        xwb_b = xwb_s[fs, :]
        y_off = []
        for g in range(SSM_GROUPS):
            gs = slice(g * GROUP_W, (g + 1) * GROUP_W)
            s_in = sb_s[sq, :, gs]
            cm_g = act_ref[sq, rs, c_off + g * D_STATE:c_off + (g + 1) * D_STATE]
            y_off.append(_dot(cm_g, s_in.astype(BF16)) * eg_x[:, gs])
            sb_s[sq, :, gs] = s_in * eg_x[0:1, gs] + _dot(bt_s[bt_rows(c, g), :], xwb_b[:, gs])
        y = y_s[fs, :] + jnp.concatenate(y_off, axis=1) + dskip_ref[...] * xs
        y = y * _silu(z_ref[sq, rs, :].astype(F32))
        out_ref[sq, rs, :] = _rms(y, gout_ref[...]).astype(BF16)

    for s in range(nseq):
        fin_ref[s, 0] = jnp.transpose(sf_s[s])
        fin_ref[s, 1] = jnp.transpose(sb_s[s])


def _ssd(act, z, small, init, consts, cast=()):
    b, s, _ = act.shape
    has_init = init is not None
    cps = s // CHUNK
    assert SSD_GROUP % cps == 0
    nseq = SSD_GROUP // cps
    assert b % nseq == 0
    steps = b // nseq
    blk = lambda w: pl.BlockSpec((nseq, s, w), lambda i: (i, 0, 0))
    st_spec = pl.BlockSpec((nseq, 2, SSM_WIDTH, D_STATE), lambda i: (i, 0, 0, 0))
    in_specs = [blk(CONV_DIM), blk(SSM_WIDTH), blk(LANES)]
    args = [act, z, small]
    if has_init:
        in_specs.append(st_spec)
        args.append(init)
    in_specs += [_const_spec(c.shape) for c in consts]
    args += list(consts)
    cast_specs = [pl.BlockSpec((w.shape[0] // steps, w.shape[1]), lambda i: (i, 0)) for w in cast]
    in_specs += cast_specs
    args += list(cast)
    rows = SSD_GROUP * CHUNK
    scratch = [pltpu.VMEM((SSD_GROUP, CHUNK, LANES), F32),
               pltpu.VMEM((SSD_GROUP, 2 * CHUNK, LANES), BF16),
               pltpu.VMEM((SSD_GROUP * N_DT, CHUNK), F32),
               pltpu.VMEM((SSD_GROUP * SSM_GROUPS * D_STATE, CHUNK), BF16),
               pltpu.VMEM((rows, SSM_WIDTH), F32),
               pltpu.VMEM((rows, SSM_WIDTH), BF16),
               pltpu.VMEM((rows, SSM_WIDTH), F32),
               pltpu.VMEM((nseq, D_STATE, SSM_WIDTH), F32),
               pltpu.VMEM((nseq, D_STATE, SSM_WIDTH), F32)]
    return pl.pallas_call(
        functools.partial(_ssd_kernel, nseq=nseq, cps=cps, has_init=has_init, n_cast=len(cast)),
        grid=(steps,), in_specs=in_specs,
        out_specs=[blk(SSM_WIDTH), st_spec] + cast_specs,
        out_shape=[jax.ShapeDtypeStruct((b, s, SSM_WIDTH), BF16),
                   jax.ShapeDtypeStruct((b, 2, SSM_WIDTH, D_STATE), F32)]
                  + [jax.ShapeDtypeStruct(w.shape, BF16) for w in cast],
        scratch_shapes=scratch,
        compiler_params=_params(("parallel",)),
        name="ssd_latent" if has_init else "ssd_ctx",
    )(*args)


FF_CHUNK = 256


def _outffn_kernel(x_ref, attn_ref, ssm_ref, mod_ref, wout_ref, gpost_ref, gpre_ref, gpostf_ref,
                   wg_ref, wu_ref, wd_ref, o_ref, mix_s, y_s, h_s, *, mod_off):
    m = _mod_terms(mod_ref, mod_off)
    half = MLA_HEADS * V_HEAD
    rows = x_ref.shape[1] // 2
    nslice = 8
    srows = rows // nslice
    nchunks = D_FF // FF_CHUNK
    g1 = gpost_ref[...] * m[2]
    g2 = gpre_ref[...] * (1.0 + m[4])
    sh2 = m[3]
    g3 = gpostf_ref[...] * m[5]

    def out_proj(rs):
        return _dot(attn_ref[0, rs, :], wout_ref[0:half, :]) + _dot(ssm_ref[0, rs, :], wout_ref[half:, :])

    def pre_ffn(mix, xr):
        y = xr + _rms(mix, g1)
        return y, (_rms(y, g2) + sh2).astype(BF16)

    def ffn_chunk(h, c, acc):
        cs = slice(c * FF_CHUNK, (c + 1) * FF_CHUNK)
        part = _dot((_silu(_dot(h, wg_ref[:, cs])) * _dot(h, wu_ref[:, cs])).astype(BF16), wd_ref[cs, :])
        return part if acc is None else acc + part

    y_a, h_a = pre_ffn(out_proj(slice(0, rows)), x_ref[0, 0:rows, :])
    mix_s[...] = out_proj(slice(rows, 2 * rows))
    acc_a = None
    for c in range(nchunks):
        acc_a = ffn_chunk(h_a, c, acc_a)
        if c < nslice:
            rs = slice(c * srows, (c + 1) * srows)
            y_b, h_b = pre_ffn(mix_s[rs, :], x_ref[0, rows + c * srows:rows + (c + 1) * srows, :])
            y_s[rs, :] = y_b
            h_s[rs, :] = h_b
    h_bb = h_s[...]
    acc_b = None
    for c in range(nchunks):
        acc_b = ffn_chunk(h_bb, c, acc_b)
        if c < nslice:
            rs = slice(c * srows, (c + 1) * srows)
            o_ref[0, rs, :] = y_a[rs] + _rms(acc_a[rs], g3)
    o_ref[0, rows:2 * rows, :] = y_s[...] + _rms(acc_b, g3)


def _outffn(x, attn, ssm, mod, mod_off, wout, gpost, gpre, gpostf, wg, wu, wd, tm):
    b, s, d = x.shape
    tok = lambda w: pl.BlockSpec((1, tm, w), lambda i, j: (i, j, 0))
    consts = [wout, gpost, gpre, gpostf, wg, wu, wd]
    return pl.pallas_call(
        functools.partial(_outffn_kernel, mod_off=mod_off),
        grid=(b, s // tm),
        in_specs=[tok(d), tok(MLA_HEADS * V_HEAD), tok(SSM_WIDTH), _mod_spec(d, mod_off)]
                 + [_const_spec(c.shape) for c in consts],
        out_specs=tok(d),
        out_shape=jax.ShapeDtypeStruct((b, s, d), F32),
        scratch_shapes=[pltpu.VMEM((tm // 2, d), F32),
                        pltpu.VMEM((tm // 2, d), F32),
                        pltpu.VMEM((tm // 2, d), BF16)],
        compiler_params=_params(("parallel", "parallel")),
        name="out_ffn",
    )(x, attn, ssm, mod, *consts)


def _rope_tables(length):
    quarter = QK_ROPE // 4
    pos = np.arange(length)
    inv_freq = ROPE_THETA ** (-np.arange(quarter, dtype=np.float64) / quarter)
    ang_r = (pos // GRID_W)[:, None] * inv_freq[None, :]
    ang_c = (pos % GRID_W)[:, None] * inv_freq[None, :]
    cos = np.concatenate([np.cos(ang_r)] * 2 + [np.cos(ang_c)] * 2, axis=1)
    sin = np.concatenate([-np.sin(ang_r), np.sin(ang_r), -np.sin(ang_c), np.sin(ang_c)], axis=1)
    return jnp.asarray(np.tile(cos, (1, 2)), F32), jnp.asarray(np.tile(sin, (1, 2)), F32)


def _pad_cols(w, width, left=0):
    return jnp.pad(w, ((0, 0), (left, width - left - w.shape[-1])))


def _ssd_constants():
    idx = np.arange(CHUNK)
    tril = (idx[None, :] <= idx[:, None]).astype(np.float32)
    e64 = np.zeros((LANES, N_DT * SSM_HEAD_DIM), np.float32)
    for q in range(N_DT):
        e64[q, q * SSM_HEAD_DIM:(q + 1) * SSM_HEAD_DIM] = 1.0
    return [jnp.asarray(a, BF16) for a in (tril, e64)]


def kernel(x_prompt, x_sample, cache_ckv, cache_krope, state_ssm, c, c_ctx, w_mod, b_mod,
           g_pre_mix, g_post_mix, w_in, g_q, w_uq, g_kv, w_ukv, conv_w, conv_b, dt_bias,
           a_log, d_skip, g_ssm_out, w_out, g_pre_ffn, g_post_ffn, w_gate, w_up, w_down):
    depth = w_in.shape[0]
    assert depth == 1
    nb, seq, d = x_prompt.shape
    db, dseq, _ = x_sample.shape
    l = 0

    rows = 16
    cond = jnp.concatenate([c_ctx[None, :], c, jnp.zeros((rows - 1 - db, d), F32)], axis=0)
    wi = w_in[l]
    dt_cols = _pad_cols(wi[:, wi.shape[1] - N_DT:], LANES, left=DT_OFF)
    mod, win, wuq, wukv, wout = _modulation(cond, w_mod[l], b_mod[l][None, :], wi, dt_cols,
                                            w_uq[l], w_ukv[l], w_out[l])

    row = lambda v: v.reshape(1, -1)
    gpre, gq, gkv = row(g_pre_mix[l]), row(g_q[l]), row(g_kv[l])
    gpost, gpre_f, gpost_f = row(g_post_mix[l]), row(g_pre_ffn[l]), row(g_post_ffn[l])
    cw, cb = conv_w[l], row(conv_b[l])

    small = lambda v: jnp.tile(v.reshape(1, -1), (1, SSD_PACK))
    ssd_consts = [small(dt_bias[l]), small(a_log[l]),
                  row(jnp.repeat(d_skip[l], SSM_HEAD_DIM)), row(g_ssm_out[l])] + _ssd_constants()

    xp = x_prompt.reshape(1, nb * seq, d)
    attn, z, act, sm, ckv, krope = _inproj(xp, mod, 0, gpre, win, gq, wuq, gkv, wukv, cw, cb,
                                           None, tm=1024, seq_rows=seq)
    per_seq = lambda a: a.reshape(nb, seq, a.shape[-1])
    ssm, fin, wg, wu, wd = _ssd(per_seq(act), per_seq(z), per_seq(sm), None, ssd_consts,
                                cast=(w_gate[l], w_up[l], w_down[l]))
    y_p = _outffn(xp, attn, ssm.reshape(1, nb * seq, -1), mod, 0,
                  wout, gpost, gpre_f, gpost_f, wg, wu, wd, tm=1024).reshape(nb, seq, d)
    new_ckv = ckv.reshape(nb, 1, seq, KV_LORA)
    new_krope = krope
    new_ssm = fin.reshape(nb, 1, 2, SSM_HEADS, SSM_HEAD_DIM, D_STATE)

    kr_cache = jnp.pad(cache_krope[:, l], ((0, 0), (0, 0), (0, LANES - QK_ROPE)))
    attn, z, act, sm = _inproj(x_sample, mod, 1, gpre, win, gq, wuq, gkv, wukv, cw, cb,
                               _rope_tables(dseq) + (cache_ckv[:, l], kr_cache), tm=1024, seq_rows=dseq)
    init = state_ssm[:, l].reshape(db, 2, SSM_WIDTH, D_STATE)
    ssm, _ = _ssd(act, z, sm, init, ssd_consts)
    y_s = _outffn(x_sample, attn, ssm, mod, 1, wout, gpost, gpre_f, gpost_f, wg, wu, wd, tm=1024)

    return (y_p, y_s, new_ckv, new_krope, new_ssm)
```

```python
import functools

import numpy as np
import jax
import jax.numpy as jnp
from jax import lax
from jax.experimental import pallas as pl
from jax.experimental.pallas import tpu as pltpu

F32 = jnp.float32
BF16 = jnp.bfloat16

D_MODEL = 1024
GRID_W = 64
ROPE_THETA = 10000.0
NORM_EPS = 1e-6
MLA_HEADS = 4
QK_NOPE = 128
QK_ROPE = 64
V_HEAD = 128
Q_LORA = 384
KV_LORA = 256
SSM_HEADS = 8
SSM_HEAD_DIM = 64
SSM_WIDTH = SSM_HEADS * SSM_HEAD_DIM
SSM_GROUPS = 2
D_STATE = 128
CHUNK = 128
CONV_DIM = SSM_WIDTH + 2 * SSM_GROUPS * D_STATE
D_FF = 2816

LOG2E = 1.4426950408889634
LANES = 128
SUBLANES = 8
BF16_ROWS = 16
QK_PAD = 256
GROUP_W = SSM_WIDTH // SSM_GROUPS
VMEM_LIMIT = 56 * 1024 * 1024

_SEG_Q = (0, 384)
_SEG_KV = (384, 640)
_SEG_Z = (640, 1152)
_SEG_XBC = (1152, 2176)
_SEG_KR = (2176, 2304)
DT_OFF = QK_ROPE
N_DT = 2 * SSM_HEADS


def _rms(x, g):
    return x * lax.rsqrt(jnp.mean(x * x, axis=-1, keepdims=True) + NORM_EPS) * g


def _silu(x):
    u = 0.5 * x
    return u * jnp.tanh(u) + u


def _dot(a, b):
    return jnp.dot(a, b, preferred_element_type=F32)


def _dot_nt(a, b):
    return lax.dot_general(a, b, (((1,), (1,)), ((), ())), preferred_element_type=F32)


def _split3(x):
    hi = x.astype(BF16)
    r = x - hi.astype(F32)
    mid = r.astype(BF16)
    lo = (r - mid.astype(F32)).astype(BF16)
    return hi, mid, lo


def _dot3_rhs(m, x):
    hi, mid, lo = _split3(x)
    return _dot(m, hi) + _dot(m, mid) + _dot(m, lo)


def _mod_spec(d, mod_off):
    return pl.BlockSpec((SUBLANES, 6 * d), lambda i, j: ((i + mod_off) // SUBLANES, 0))


def _mod_terms(mod_ref, mod_off):
    row = lax.rem(pl.program_id(0) + mod_off, SUBLANES)
    r = mod_ref[pl.ds(row, 1), :]
    d = r.shape[1] // 6
    return [r[:, k * d:(k + 1) * d] for k in range(6)]


def _const_spec(shape):
    nd = len(shape)
    return pl.BlockSpec(shape, lambda *_: (0,) * nd, pipeline_mode=pl.Buffered(1))


def _params(sem):
    return pltpu.CompilerParams(dimension_semantics=sem, vmem_limit_bytes=VMEM_LIMIT)


def _mod_kernel(c_ref, w_ref, b_ref, o_ref):
    s = _silu(c_ref[...])
    rows = s.shape[0]
    s_hi = s.astype(BF16)
    s_lo = (s - s_hi.astype(F32)).astype(BF16)
    both = _dot(jnp.concatenate([s_hi, s_lo], axis=0), w_ref[...].astype(BF16))
    o_ref[...] = both[:rows] + both[rows:] + b_ref[...]


def _modulation(cond, w_mod, b_mod):
    rows, d = cond.shape
    n = w_mod.shape[1]
    tn = 1536
    return pl.pallas_call(
        _mod_kernel,
        grid=(n // tn,),
        in_specs=[pl.BlockSpec((rows, d), lambda j: (0, 0)),
                  pl.BlockSpec((d, tn), lambda j: (0, j)),
                  pl.BlockSpec((1, tn), lambda j: (0, j))],
        out_specs=pl.BlockSpec((rows, tn), lambda j: (0, j)),
        out_shape=jax.ShapeDtypeStruct((rows, n), F32),
        compiler_params=_params(("arbitrary",)),
        name="modulation",
    )(cond, w_mod, b_mod)


def _conv_silu(xb, prev_rows, next_rows, cw_ref, cb_ref):
    n = xb.shape[0]
    win = jnp.concatenate([prev_rows, xb, next_rows], axis=0)
    total = n + 2 * SUBLANES
    prev = pltpu.roll(win, 1, axis=0)[SUBLANES:SUBLANES + n]
    nxt = pltpu.roll(win, total - 1, axis=0)[SUBLANES:SUBLANES + n]
    conv = cb_ref[...] + prev * cw_ref[0:1, :] + xb * cw_ref[1:2, :] + nxt * cw_ref[2:3, :]
    return _silu(conv)


def _inproj_kernel(*refs, latent, seq_rows, mod_off):
    if latent:
        (x_ref, mod_ref, gpre_ref, win_ref, gq_ref, wuq_ref, gkv_ref, wukv_ref, cw_ref, cb_ref,
         cos_ref, sin_ref, cckv_ref, ckr_ref, attn_ref, z_ref, act_ref, small_ref,
         q_ref, k_ref, v_ref) = refs
    else:
        (x_ref, mod_ref, gpre_ref, win_ref, gq_ref, wuq_ref, gkv_ref, wukv_ref, cw_ref, cb_ref,
         attn_ref, z_ref, act_ref, small_ref, ckv_ref, krope_ref) = refs
    scale = (QK_NOPE + QK_ROPE) ** -0.5 * LOG2E
    shift, mscale = _mod_terms(mod_ref, mod_off)[:2]

    g_mod = gpre_ref[...] * (1.0 + mscale)

    def pre(xv):
        return (_rms(xv, g_mod) + shift).astype(BF16)

    tm = x_ref.shape[1]
    rows = tm // 2
    halves = (slice(0, rows), slice(rows, tm))
    assert seq_rows == tm or rows % seq_rows == 0
    zero_rows = jnp.zeros((SUBLANES, CONV_DIM), F32)
    projs = [_dot_nt(pre(x_ref[0, halves[0], :]), win_ref[...])]
    if seq_rows == tm:
        ph = _dot_nt(pre(x_ref[0, rows - SUBLANES:rows + SUBLANES, :]), win_ref[_SEG_XBC[0]:_SEG_XBC[1], :])
        halo = ((zero_rows, ph[SUBLANES:]), (ph[:SUBLANES], zero_rows))
    projs.append(_dot_nt(pre(x_ref[0, halves[1], :]), win_ref[...]))

    lane = lax.broadcasted_iota(jnp.int32, (rows, LANES), 1)
    rope_lanes = lane < QK_ROPE
    first_quarter = jnp.bitwise_and(lane, QK_ROPE // 4) == 0
    nw = MLA_HEADS * QK_NOPE
    for hf, rs in enumerate(halves):
        proj = projs[hf]
        seg = lambda s: proj[:, s[0]:s[1]]
        xbc = seg(_SEG_XBC)
        if seq_rows == tm:
            act_ref[0, rs, :] = _conv_silu(xbc, halo[hf][0], halo[hf][1], cw_ref, cb_ref).astype(BF16)
        else:
            for s in range(rows // seq_rows):
                sub = slice(s * seq_rows, (s + 1) * seq_rows)
                dst = slice(rs.start + sub.start, rs.start + sub.stop)
                act_ref[0, dst, :] = _conv_silu(xbc[sub], zero_rows, zero_rows, cw_ref, cb_ref).astype(BF16)
        krb = seg(_SEG_KR)
        small_ref[0, rs, :] = krb
        ckv = _rms(seg(_SEG_KV), gkv_ref[...])
        qall = _dot(_rms(seg(_SEG_Q), gq_ref[...]).astype(BF16), wuq_ref[...])
        kv = _dot(ckv.astype(BF16), wukv_ref[...])
        z_ref[0, rs, :] = seg(_SEG_Z).astype(BF16)
        if latent:
            cos = cos_ref[rs, :]
            sin = sin_ref[rs, :]

            def rot(t):
                swapped = jnp.where(first_quarter, pltpu.roll(t, LANES - QK_ROPE // 4, axis=1),
                                    pltpu.roll(t, QK_ROPE // 4, axis=1))
                return t * cos + swapped * sin
        else:
            rot = lambda t: t
            ckv_ref[0, rs, :] = ckv
            for s in range(rows // seq_rows):
                krope_ref[rs.start // seq_rows + s, 0] = krb[s * seq_rows:(s + 1) * seq_rows, :QK_ROPE]
        kr_b = jnp.where(rope_lanes, rot(krb), 0.0).astype(BF16)
        for pr in range(MLA_HEADS // 2):
            pair = rot(qall[:, nw + pr * LANES:nw + (pr + 1) * LANES]) * scale
            for k in range(2):
                hd = 2 * pr + k
                base = hd * QK_PAD
                qr = pair if k == 0 else pltpu.roll(pair, QK_ROPE, axis=1)
                q_n = (qall[:, hd * LANES:(hd + 1) * LANES] * scale).astype(BF16)
                q_r = jnp.where(rope_lanes, qr, 0.0).astype(BF16)
                k_n = kv[:, base:base + LANES].astype(BF16)
                v_h = kv[:, base + LANES:base + QK_PAD].astype(BF16)
                if latent:
                    q_ref[0, rs, base:base + LANES] = q_n
                    q_ref[0, rs, base + LANES:base + QK_PAD] = q_r
                    k_ref[0, rs, base:base + LANES] = k_n
                    k_ref[0, rs, base + LANES:base + QK_PAD] = kr_b
                    v_ref[0, rs, hd * LANES:(hd + 1) * LANES] = v_h
                else:
                    q_h = jnp.concatenate([q_n, q_r], axis=1)
                    k_h = jnp.concatenate([k_n, kr_b], axis=1)
                    for s in range(rows // seq_rows):
                        sub = slice(s * seq_rows, (s + 1) * seq_rows)
                        sc = _dot_nt(q_h[sub], k_h[sub])
                        p = jnp.exp2(sc - jnp.max(sc, axis=-1, keepdims=True))
                        den = jnp.sum(p, axis=-1, keepdims=True)
                        dst = slice(rs.start + sub.start, rs.start + sub.stop)
                        attn_ref[0, dst, hd * V_HEAD:(hd + 1) * V_HEAD] = (
                            _dot(p.astype(BF16), v_h[sub]) / den).astype(BF16)
    if latent:
        _attn_body(q_ref, k_ref, v_ref, cckv_ref, ckr_ref, wukv_ref, attn_ref)


def _inproj(x, mod, mod_off, gpre, win, gq, wuq, gkv, wukv, cw, cb, rope, tm, seq_rows):
    b, s, d = x.shape
    latent = rope is not None
    grid = (b, s // tm)
    tok = lambda w: pl.BlockSpec((1, tm, w), lambda i, j: (i, j, 0))
    mod_spec = _mod_spec(d, mod_off)
    consts = [gpre, win, gq, wuq, gkv, wukv, cw, cb]
    in_specs = [tok(d), mod_spec]
    args = [x, mod]
    in_specs += [_const_spec(c.shape) for c in consts]
    args += consts
    widths = [MLA_HEADS * V_HEAD, SSM_WIDTH, CONV_DIM]
    out_shape = [jax.ShapeDtypeStruct((b, s, w), BF16) for w in widths]
    out_shape.append(jax.ShapeDtypeStruct((b, s, LANES), F32))
    out_specs = [tok(w) for w in widths] + [tok(LANES)]
    scratch = []
    if latent:
        cos, sin, cache_ckv, cache_kr = rope
        assert tm == s
        in_specs += [pl.BlockSpec((tm, LANES), lambda i, j: (j, 0))] * 2
        in_specs += [pl.BlockSpec((1,) + a.shape[1:], lambda i, j: (i, 0, 0)) for a in (cache_ckv, cache_kr)]
        args += [cos, sin, cache_ckv, cache_kr]
        scratch = [pltpu.VMEM((1, tm, w), BF16)
                   for w in (MLA_HEADS * QK_PAD, MLA_HEADS * QK_PAD, MLA_HEADS * V_HEAD)]
    else:
        assert b == 1
        out_shape += [jax.ShapeDtypeStruct((b, s, KV_LORA), F32),
                      jax.ShapeDtypeStruct((s // seq_rows, 1, seq_rows, QK_ROPE), F32)]
        out_specs += [tok(KV_LORA),
                      pl.BlockSpec((tm // seq_rows, 1, seq_rows, QK_ROPE), lambda i, j: (j, 0, 0, 0))]
    return pl.pallas_call(
        functools.partial(_inproj_kernel, latent=latent, seq_rows=seq_rows, mod_off=mod_off),
        grid=grid, in_specs=in_specs, out_specs=out_specs, out_shape=out_shape,
        scratch_shapes=scratch,
        compiler_params=_params(("parallel", "parallel")),
        name="inproj_latent" if latent else "inproj_ctx",
    )(*args)


def _attn_body(q_ref, k_ref, v_ref, cckv_ref, ckr_ref, wukv_ref, o_ref):
    kv_c = _dot(cckv_ref[0].astype(BF16), wukv_ref[...])
    kr_c = ckr_ref[0].astype(BF16)

    def scores(hd):
        qs = slice(hd * QK_PAD, (hd + 1) * QK_PAD)
        q = q_ref[0, :, qs]
        kc = jnp.concatenate([kv_c[:, hd * QK_PAD:hd * QK_PAD + LANES].astype(BF16), kr_c], axis=1)
        return _dot_nt(k_ref[0, :, qs], q), _dot_nt(kc, q)

    nxt = scores(0)
    for hd in range(MLA_HEADS):
        s, sc = nxt
        if hd + 1 < MLA_HEADS:
            nxt = scores(hd + 1)
        vs = slice(hd * V_HEAD, (hd + 1) * V_HEAD)
        mx = jnp.maximum(jnp.max(s, axis=0, keepdims=True), jnp.max(sc, axis=0, keepdims=True))
        p = jnp.exp2(s - mx)
        pc = jnp.exp2(sc - mx)
        ones = lambda rows: jnp.ones((BF16_ROWS, rows), BF16)
        v_t = jnp.concatenate([jnp.transpose(v_ref[0, :, vs].astype(F32)).astype(BF16),
                               ones(v_ref.shape[1])], axis=0)
        vc = kv_c[:, hd * QK_PAD + LANES:(hd + 1) * QK_PAD]
        vc_t = jnp.concatenate([jnp.transpose(vc).astype(BF16), ones(vc.shape[0])], axis=0)
        acc = _dot(v_t, p.astype(BF16)) + _dot(vc_t, pc.astype(BF16))
        out = acc[:V_HEAD] / acc[V_HEAD:V_HEAD + 1]
        o_ref[0, :, vs] = jnp.transpose(out).astype(BF16)


SSD_PACK = LANES // N_DT
SSD_GROUP = SSD_PACK


def _ssd_kernel(*refs, nseq, cps, has_init, n_cast):
    refs = list(refs)
    n_in = 3 + int(has_init) + 6
    cast_in = refs[n_in:n_in + n_cast]
    cast_out = refs[n_in + n_cast + 2:n_in + 2 * n_cast + 2]
    for src, dst in zip(cast_in, cast_out):
        dst[...] = src[...].astype(BF16)
    refs = refs[:n_in] + refs[n_in + n_cast:n_in + n_cast + 2] + refs[n_in + 2 * n_cast + 2:]
    if has_init:
        (act_ref, z_ref, small_ref, init_ref, dtbias_ref, alog_ref, dskip_ref, gout_ref,
         tril_ref, e64_ref, out_ref, fin_ref,
         fg_s, wsm_s, rt_s, bt_s, y_s, xwb_s, eg_s, sf_s, sb_s) = refs
    else:
        (act_ref, z_ref, small_ref, dtbias_ref, alog_ref, dskip_ref, gout_ref,
         tril_ref, e64_ref, out_ref, fin_ref,
         fg_s, wsm_s, rt_s, bt_s, y_s, xwb_s, eg_s, sf_s, sb_s) = refs
    L = CHUNK
    G = nseq * cps
    heads_per_group = SSM_HEADS // SSM_GROUPS
    b_off = SSM_WIDTH
    c_off = SSM_WIDTH + SSM_GROUPS * D_STATE

    lane = lax.broadcasted_iota(jnp.int32, (L, LANES), 1)
    for pk in range(G // SSD_PACK):
        chunks = range(pk * SSD_PACK, (pk + 1) * SSD_PACK)
        packed = jnp.zeros((L, LANES), F32)
        for k, c in enumerate(chunks):
            raw = small_ref[c // cps, (c % cps) * L:(c % cps + 1) * L, :]
            shifted = pltpu.roll(raw, (k * N_DT - DT_OFF) % LANES, axis=1)
            packed = jnp.where((lane >= k * N_DT) & (lane < (k + 1) * N_DT), shifted, packed)
        dtc = jax.nn.softplus(packed + dtbias_ref[...])
        da = dtc * (-jnp.exp(alog_ref[...]))
        cum = _dot3_rhs(tril_ref[...], da)
        tot = cum[L - 1:L, :]
        suf = tot - cum + da
        fg = jnp.where(jnp.bitwise_and(lane, SSM_HEADS) == 0, cum, suf)
        w_small = dtc * jnp.exp(tot - fg)
        e_small = jnp.exp(fg)
        rt_s[pk * LANES:(pk + 1) * LANES, :] = jnp.transpose(fg - jnp.log(dtc))
        for k, c in enumerate(chunks):
            back = (LANES - k * N_DT) % LANES
            unroll = lambda t: t if back == 0 else pltpu.roll(t, back, axis=1)
            fg_s[c] = unroll(fg)
            wsm_s[c, 0:L, :] = unroll(w_small).astype(BF16)
            wsm_s[c, L:2 * L, :] = unroll(e_small).astype(BF16)

    for s in range(nseq):
        if has_init:
            sf_s[s] = jnp.transpose(init_ref[s, 0])
            sb_s[s] = jnp.transpose(init_ref[s, 1])
        else:
            sf_s[s] = jnp.zeros((D_STATE, SSM_WIDTH), F32)
            sb_s[s] = jnp.zeros((D_STATE, SSM_WIDTH), F32)

    row_t = lax.broadcasted_iota(jnp.int32, (L, L), 0)
    col_s = lax.broadcasted_iota(jnp.int32, (L, L), 1)
    lower = col_s <= row_t
    upper = col_s >= row_t
    lane_lo = lane < SSM_HEAD_DIM
    neg_inf = jnp.float32(-jnp.inf)

    def locate(c):
        sq, j = divmod(c, cps)
        return sq, slice(j * L, (j + 1) * L), slice(c * L, (c + 1) * L)

    def bt_rows(c, g):
        r = (c * SSM_GROUPS + g) * D_STATE
        return slice(r, r + D_STATE)

    for c in range(G):
        sq, rs, fs = locate(c)
        xs_b = act_ref[sq, rs, 0:SSM_WIDTH]
        fgc = fg_s[c]
        rtc = rt_s[c * N_DT:(c + 1) * N_DT, :]
        ex = _dot(wsm_s[c], e64_ref[...])
        w_b = ex[0:L, :].astype(BF16)
        ef_x = ex[L:2 * L, :SSM_WIDTH]
        eg_x = ex[L:2 * L, SSM_WIDTH:]
        eg_s[fs, :] = eg_x
        xwf_b = xs_b * w_b[:, :SSM_WIDTH]
        xwb_s[fs, :] = xs_b * w_b[:, SSM_WIDTH:]

        y_parts = []
        new_states = []
        for g in range(SSM_GROUPS):
            gs = slice(g * GROUP_W, (g + 1) * GROUP_W)
            bm_g = act_ref[sq, rs, b_off + g * D_STATE:b_off + (g + 1) * D_STATE]
            cm_g = act_ref[sq, rs, c_off + g * D_STATE:c_off + (g + 1) * D_STATE]
            bt_g = jnp.transpose(bm_g.astype(F32)).astype(BF16)
            bt_s[bt_rows(c, g), :] = bt_g
            cbm = _dot_nt(cm_g, bm_g)
            for pair in range(heads_per_group // 2):
                ws = []
                for k in range(2):
                    cf = g * heads_per_group + pair * 2 + k
                    seg_f = fgc[:, cf:cf + 1] - rtc[cf:cf + 1, :]
                    lf = jnp.exp(jnp.where(lower, seg_f, neg_inf))
                    cg = cf + SSM_HEADS
                    seg_b = fgc[:, cg:cg + 1] - rtc[cg:cg + 1, :]
                    ub = jnp.exp(jnp.where(upper, seg_b, neg_inf))
                    ws.append((cbm * (lf + ub)).astype(BF16))
                p0 = (g * heads_per_group + pair * 2) * SSM_HEAD_DIM
                xpair = xs_b[:, p0:p0 + LANES]
                zeros = jnp.zeros_like(xpair)
                rhs = jnp.concatenate([jnp.where(lane_lo, xpair, zeros),
                                       jnp.where(lane_lo, zeros, xpair)], axis=0)
                y_parts.append(_dot(jnp.concatenate(ws, axis=1), rhs))
            s_in = sf_s[sq, :, gs]
            y_off = _dot(cm_g, s_in.astype(BF16)) * ef_x[:, gs]
            y_parts[-2] = y_parts[-2] + y_off[:, :LANES]
            y_parts[-1] = y_parts[-1] + y_off[:, LANES:]
            new_states.append(s_in * ef_x[L - 1:L, gs] + _dot(bt_g, xwf_b[:, gs]))
        for g in range(SSM_GROUPS):
            sf_s[sq, :, g * GROUP_W:(g + 1) * GROUP_W] = new_states[g]
        for i, yp in enumerate(y_parts):
            y_s[fs, i * LANES:(i + 1) * LANES] = yp

    for c in reversed(range(G)):
        sq, rs, fs = locate(c)
        xs = act_ref[sq, rs, 0:SSM_WIDTH].astype(F32)
        eg_x = eg_s[fs, :]
        xwb_b = xwb_s[fs, :]
        y_off = []
        for g in range(SSM_GROUPS):
            gs = slice(g * GROUP_W, (g + 1) * GROUP_W)
            s_in = sb_s[sq, :, gs]
            cm_g = act_ref[sq, rs, c_off + g * D_STATE:c_off + (g + 1) * D_STATE]
            y_off.append(_dot(cm_g, s_in.astype(BF16)) * eg_x[:, gs])
            sb_s[sq, :, gs] = s_in * eg_x[0:1, gs] + _dot(bt_s[bt_rows(c, g), :], xwb_b[:, gs])
        y = y_s[fs, :] + jnp.concatenate(y_off, axis=1) + dskip_ref[...] * xs
        y = y * _silu(z_ref[sq, rs, :].astype(F32))
        out_ref[sq, rs, :] = _rms(y, gout_ref[...]).astype(BF16)

    for s in range(nseq):
        fin_ref[s, 0] = jnp.transpose(sf_s[s])
        fin_ref[s, 1] = jnp.transpose(sb_s[s])


def _ssd(act, z, small, init, consts, cast=()):
    b, s, _ = act.shape
    has_init = init is not None
    cps = s // CHUNK
    assert SSD_GROUP % cps == 0
    nseq = SSD_GROUP // cps
    assert b % nseq == 0
    steps = b // nseq
    blk = lambda w: pl.BlockSpec((nseq, s, w), lambda i: (i, 0, 0))
    st_spec = pl.BlockSpec((nseq, 2, SSM_WIDTH, D_STATE), lambda i: (i, 0, 0, 0))
    in_specs = [blk(CONV_DIM), blk(SSM_WIDTH), blk(LANES)]
    args = [act, z, small]
    if has_init:
        in_specs.append(st_spec)
        args.append(init)
    in_specs += [_const_spec(c.shape) for c in consts]
    args += list(consts)
    cast_specs = [pl.BlockSpec((w.shape[0] // steps, w.shape[1]), lambda i: (i, 0)) for w in cast]
    in_specs += cast_specs
    args += list(cast)
    rows = SSD_GROUP * CHUNK
    scratch = [pltpu.VMEM((SSD_GROUP, CHUNK, LANES), F32),
               pltpu.VMEM((SSD_GROUP, 2 * CHUNK, LANES), BF16),
               pltpu.VMEM((SSD_GROUP * N_DT, CHUNK), F32),
               pltpu.VMEM((SSD_GROUP * SSM_GROUPS * D_STATE, CHUNK), BF16),
               pltpu.VMEM((rows, SSM_WIDTH), F32),
               pltpu.VMEM((rows, SSM_WIDTH), BF16),
               pltpu.VMEM((rows, SSM_WIDTH), F32),
               pltpu.VMEM((nseq, D_STATE, SSM_WIDTH), F32),
               pltpu.VMEM((nseq, D_STATE, SSM_WIDTH), F32)]
    return pl.pallas_call(
        functools.partial(_ssd_kernel, nseq=nseq, cps=cps, has_init=has_init, n_cast=len(cast)),
        grid=(steps,), in_specs=in_specs,
        out_specs=[blk(SSM_WIDTH), st_spec] + cast_specs,
        out_shape=[jax.ShapeDtypeStruct((b, s, SSM_WIDTH), BF16),
                   jax.ShapeDtypeStruct((b, 2, SSM_WIDTH, D_STATE), F32)]
                  + [jax.ShapeDtypeStruct(w.shape, BF16) for w in cast],
        scratch_shapes=scratch,
        compiler_params=_params(("parallel",)),
        name="ssd_latent" if has_init else "ssd_ctx",
    )(*args)


FF_CHUNK = 256


def _outffn_kernel(x_ref, attn_ref, ssm_ref, mod_ref, wout_ref, gpost_ref, gpre_ref, gpostf_ref,
                   wg_ref, wu_ref, wd_ref, o_ref, mix_s, y_s, h_s, *, mod_off):
    m = _mod_terms(mod_ref, mod_off)
    half = MLA_HEADS * V_HEAD
    rows = x_ref.shape[1] // 2
    nslice = 8
    srows = rows // nslice
    nchunks = D_FF // FF_CHUNK
    g1 = gpost_ref[...] * m[2]
    g2 = gpre_ref[...] * (1.0 + m[4])
    sh2 = m[3]
    g3 = gpostf_ref[...] * m[5]

    def out_proj(rs):
        return _dot(attn_ref[0, rs, :], wout_ref[0:half, :]) + _dot(ssm_ref[0, rs, :], wout_ref[half:, :])

    def pre_ffn(mix, xr):
        y = xr + _rms(mix, g1)
        return y, (_rms(y, g2) + sh2).astype(BF16)

    def ffn_chunk(h, c, acc):
        cs = slice(c * FF_CHUNK, (c + 1) * FF_CHUNK)
        part = _dot((_silu(_dot(h, wg_ref[:, cs])) * _dot(h, wu_ref[:, cs])).astype(BF16), wd_ref[cs, :])
        return part if acc is None else acc + part

    y_a, h_a = pre_ffn(out_proj(slice(0, rows)), x_ref[0, 0:rows, :])
    mix_s[...] = out_proj(slice(rows, 2 * rows))
    acc_a = None
    for c in range(nchunks):
        acc_a = ffn_chunk(h_a, c, acc_a)
        if c < nslice:
            rs = slice(c * srows, (c + 1) * srows)
            y_b, h_b = pre_ffn(mix_s[rs, :], x_ref[0, rows + c * srows:rows + (c + 1) * srows, :])
            y_s[rs, :] = y_b
            h_s[rs, :] = h_b
    h_bb = h_s[...]
    acc_b = None
    for c in range(nchunks):
        acc_b = ffn_chunk(h_bb, c, acc_b)
        if c < nslice:
            rs = slice(c * srows, (c + 1) * srows)
            o_ref[0, rs, :] = y_a[rs] + _rms(acc_a[rs], g3)
    o_ref[0, rows:2 * rows, :] = y_s[...] + _rms(acc_b, g3)


def _outffn(x, attn, ssm, mod, mod_off, wout, gpost, gpre, gpostf, wg, wu, wd, tm):
    b, s, d = x.shape
    tok = lambda w: pl.BlockSpec((1, tm, w), lambda i, j: (i, j, 0))
    consts = [wout, gpost, gpre, gpostf, wg, wu, wd]
    return pl.pallas_call(
        functools.partial(_outffn_kernel, mod_off=mod_off),
        grid=(b, s // tm),
        in_specs=[tok(d), tok(MLA_HEADS * V_HEAD), tok(SSM_WIDTH), _mod_spec(d, mod_off)]
                 + [_const_spec(c.shape) for c in consts],
        out_specs=tok(d),
        out_shape=jax.ShapeDtypeStruct((b, s, d), F32),
        scratch_shapes=[pltpu.VMEM((tm // 2, d), F32),
                        pltpu.VMEM((tm // 2, d), F32),
                        pltpu.VMEM((tm // 2, d), BF16)],
        compiler_params=_params(("parallel", "parallel")),
        name="out_ffn",
    )(x, attn, ssm, mod, *consts)


def _rope_tables(length):
    quarter = QK_ROPE // 4
    pos = np.arange(length)
    inv_freq = ROPE_THETA ** (-np.arange(quarter, dtype=np.float64) / quarter)
    ang_r = (pos // GRID_W)[:, None] * inv_freq[None, :]
    ang_c = (pos % GRID_W)[:, None] * inv_freq[None, :]
    cos = np.concatenate([np.cos(ang_r)] * 2 + [np.cos(ang_c)] * 2, axis=1)
    sin = np.concatenate([-np.sin(ang_r), np.sin(ang_r), -np.sin(ang_c), np.sin(ang_c)], axis=1)
    return jnp.asarray(np.tile(cos, (1, 2)), F32), jnp.asarray(np.tile(sin, (1, 2)), F32)


def _pad_cols(w, width, left=0):
    return jnp.pad(w, ((0, 0), (left, width - left - w.shape[-1])))


def _ssd_constants():
    idx = np.arange(CHUNK)
    tril = (idx[None, :] <= idx[:, None]).astype(np.float32)
    e64 = np.zeros((LANES, N_DT * SSM_HEAD_DIM), np.float32)
    for q in range(N_DT):
        e64[q, q * SSM_HEAD_DIM:(q + 1) * SSM_HEAD_DIM] = 1.0
    return [jnp.asarray(a, BF16) for a in (tril, e64)]


def kernel(x_prompt, x_sample, cache_ckv, cache_krope, state_ssm, c, c_ctx, w_mod, b_mod,
           g_pre_mix, g_post_mix, w_in, g_q, w_uq, g_kv, w_ukv, conv_w, conv_b, dt_bias,
           a_log, d_skip, g_ssm_out, w_out, g_pre_ffn, g_post_ffn, w_gate, w_up, w_down):
    depth = w_in.shape[0]
    assert depth == 1
    nb, seq, d = x_prompt.shape
    db, dseq, _ = x_sample.shape
    l = 0

    rows = 16
    cond = jnp.concatenate([c_ctx[None, :], c, jnp.zeros((rows - 1 - db, d), F32)], axis=0)
    mod = _modulation(cond, w_mod[l], b_mod[l][None, :])

    wt = jnp.swapaxes(w_in[l], 0, 1)
    o_q, o_kv, o_kr = Q_LORA, Q_LORA + KV_LORA, Q_LORA + KV_LORA + QK_ROPE
    o_z, o_xbc = o_kr + SSM_WIDTH, o_kr + SSM_WIDTH + CONV_DIM
    win = jnp.concatenate([wt[:o_kv], wt[o_kr:o_xbc], wt[o_kv:o_kr], wt[o_xbc:],
                           jnp.zeros((_SEG_KR[1] - wt.shape[0], d), F32)], axis=0).astype(BF16)
    wq = w_uq[l].reshape(Q_LORA, MLA_HEADS, QK_NOPE + QK_ROPE)
    wuq = jnp.concatenate([wq[:, :, :QK_NOPE].reshape(Q_LORA, MLA_HEADS * QK_NOPE),
                           wq[:, :, QK_NOPE:].reshape(Q_LORA, MLA_HEADS * QK_ROPE)], axis=1).astype(BF16)
    wukv = w_ukv[l].astype(BF16)
    wout = w_out[l].astype(BF16)
    row = lambda v: v.reshape(1, -1)
    gpre, gq, gkv = row(g_pre_mix[l]), row(g_q[l]), row(g_kv[l])
    gpost, gpre_f, gpost_f = row(g_post_mix[l]), row(g_pre_ffn[l]), row(g_post_ffn[l])
    cw, cb = conv_w[l], row(conv_b[l])

    small = lambda v: jnp.tile(v.reshape(1, -1), (1, SSD_PACK))
    ssd_consts = [small(dt_bias[l]), small(a_log[l]),
                  row(jnp.repeat(d_skip[l], SSM_HEAD_DIM)), row(g_ssm_out[l])] + _ssd_constants()

    xp = x_prompt.reshape(1, nb * seq, d)
    attn, z, act, sm, ckv, krope = _inproj(xp, mod, 0, gpre, win, gq, wuq, gkv, wukv, cw, cb,
                                           None, tm=1024, seq_rows=seq)
    per_seq = lambda a: a.reshape(nb, seq, a.shape[-1])
    ssm, fin, wg, wu, wd = _ssd(per_seq(act), per_seq(z), per_seq(sm), None, ssd_consts,
                                cast=(w_gate[l], w_up[l], w_down[l]))
    y_p = _outffn(xp, attn, ssm.reshape(1, nb * seq, -1), mod, 0,
                  wout, gpost, gpre_f, gpost_f, wg, wu, wd, tm=1024).reshape(nb, seq, d)
    new_ckv = ckv.reshape(nb, 1, seq, KV_LORA)
    new_krope = krope
    new_ssm = fin.reshape(nb, 1, 2, SSM_HEADS, SSM_HEAD_DIM, D_STATE)

    kr_cache = jnp.pad(cache_krope[:, l], ((0, 0), (0, 0), (0, LANES - QK_ROPE)))
    attn, z, act, sm = _inproj(x_sample, mod, 1, gpre, win, gq, wuq, gkv, wukv, cw, cb,
                               _rope_tables(dseq) + (cache_ckv[:, l], kr_cache), tm=1024, seq_rows=dseq)
    init = state_ssm[:, l].reshape(db, 2, SSM_WIDTH, D_STATE)
    ssm, _ = _ssd(act, z, sm, init, ssd_consts)
    y_s = _outffn(x_sample, attn, ssm, mod, 1, wout, gpost, gpre_f, gpost_f, wg, wu, wd, tm=1024)

    return (y_p, y_s, new_ckv, new_krope, new_ssm)
```

```python
import functools

import numpy as np
import jax
import jax.numpy as jnp
from jax import lax
from jax.experimental import pallas as pl
from jax.experimental.pallas import tpu as pltpu

F32 = jnp.float32
BF16 = jnp.bfloat16

D_MODEL = 1024
GRID_W = 64
ROPE_THETA = 10000.0
NORM_EPS = 1e-6
MLA_HEADS = 4
QK_NOPE = 128
QK_ROPE = 64
V_HEAD = 128
Q_LORA = 384
KV_LORA = 256
SSM_HEADS = 8
SSM_HEAD_DIM = 64
SSM_WIDTH = SSM_HEADS * SSM_HEAD_DIM
SSM_GROUPS = 2
D_STATE = 128
CHUNK = 128
CONV_DIM = SSM_WIDTH + 2 * SSM_GROUPS * D_STATE
D_FF = 2816

LOG2E = 1.4426950408889634
LANES = 128
SUBLANES = 8
BF16_ROWS = 16
QK_PAD = 256
GROUP_W = SSM_WIDTH // SSM_GROUPS
VMEM_LIMIT = 56 * 1024 * 1024

_SEG_Q = (0, 384)
_SEG_KV = (384, 640)
_SEG_Z = (640, 1152)
_SEG_XBC = (1152, 2176)
_SEG_KR = (2176, 2304)
DT_OFF = QK_ROPE
N_DT = 2 * SSM_HEADS


def _rms(x, g):
    return x * lax.rsqrt(jnp.mean(x * x, axis=-1, keepdims=True) + NORM_EPS) * g


def _silu(x):
    u = 0.5 * x
    return u * jnp.tanh(u) + u


def _dot(a, b):
    return jnp.dot(a, b, preferred_element_type=F32)


def _dot_nt(a, b):
    return lax.dot_general(a, b, (((1,), (1,)), ((), ())), preferred_element_type=F32)


def _split3(x):
    hi = x.astype(BF16)
    r = x - hi.astype(F32)
    mid = r.astype(BF16)
    lo = (r - mid.astype(F32)).astype(BF16)
    return hi, mid, lo


def _dot3_rhs(m, x):
    hi, mid, lo = _split3(x)
    return _dot(m, hi) + _dot(m, mid) + _dot(m, lo)


def _mod_spec(d, mod_off):
    return pl.BlockSpec((SUBLANES, 6 * d), lambda i, j: ((i + mod_off) // SUBLANES, 0))


def _mod_terms(mod_ref, mod_off):
    row = lax.rem(pl.program_id(0) + mod_off, SUBLANES)
    r = mod_ref[pl.ds(row, 1), :]
    d = r.shape[1] // 6
    return [r[:, k * d:(k + 1) * d] for k in range(6)]


def _const_spec(shape):
    nd = len(shape)
    return pl.BlockSpec(shape, lambda *_: (0,) * nd, pipeline_mode=pl.Buffered(1))


def _params(sem):
    return pltpu.CompilerParams(dimension_semantics=sem, vmem_limit_bytes=VMEM_LIMIT)


def _mod_kernel(c_ref, w_ref, b_ref, o_ref):
    s = _silu(c_ref[...])
    rows = s.shape[0]
    s_hi = s.astype(BF16)
    s_lo = (s - s_hi.astype(F32)).astype(BF16)
    both = _dot(jnp.concatenate([s_hi, s_lo], axis=0), w_ref[...].astype(BF16))
    o_ref[...] = both[:rows] + both[rows:] + b_ref[...]


def _modulation(cond, w_mod, b_mod):
    rows, d = cond.shape
    n = w_mod.shape[1]
    tn = 1536
    return pl.pallas_call(
        _mod_kernel,
        grid=(n // tn,),
        in_specs=[pl.BlockSpec((rows, d), lambda j: (0, 0)),
                  pl.BlockSpec((d, tn), lambda j: (0, j)),
                  pl.BlockSpec((1, tn), lambda j: (0, j))],
        out_specs=pl.BlockSpec((rows, tn), lambda j: (0, j)),
        out_shape=jax.ShapeDtypeStruct((rows, n), F32),
        compiler_params=_params(("arbitrary",)),
        name="modulation",
    )(cond, w_mod, b_mod)


def _conv_silu(xb, prev_rows, next_rows, cw_ref, cb_ref):
    n = xb.shape[0]
    win = jnp.concatenate([prev_rows, xb, next_rows], axis=0)
    total = n + 2 * SUBLANES
    prev = pltpu.roll(win, 1, axis=0)[SUBLANES:SUBLANES + n]
    nxt = pltpu.roll(win, total - 1, axis=0)[SUBLANES:SUBLANES + n]
    conv = cb_ref[...] + prev * cw_ref[0:1, :] + xb * cw_ref[1:2, :] + nxt * cw_ref[2:3, :]
    return _silu(conv)


def _inproj_kernel(*refs, latent, seq_rows, mod_off):
    if latent:
        (x_ref, mod_ref, gpre_ref, win_ref, gq_ref, wuq_ref, gkv_ref, wukv_ref, cw_ref, cb_ref,
         cos_ref, sin_ref, cckv_ref, ckr_ref, attn_ref, z_ref, act_ref, small_ref,
         q_ref, k_ref, v_ref) = refs
    else:
        (x_ref, mod_ref, gpre_ref, win_ref, gq_ref, wuq_ref, gkv_ref, wukv_ref, cw_ref, cb_ref,
         attn_ref, z_ref, act_ref, small_ref, ckv_ref, krope_ref) = refs
    scale = (QK_NOPE + QK_ROPE) ** -0.5 * LOG2E
    shift, mscale = _mod_terms(mod_ref, mod_off)[:2]

    g_mod = gpre_ref[...] * (1.0 + mscale)

    def pre(xv):
        return (_rms(xv, g_mod) + shift).astype(BF16)

    tm = x_ref.shape[1]
    rows = tm // 2
    halves = (slice(0, rows), slice(rows, tm))
    assert seq_rows == tm or rows % seq_rows == 0
    zero_rows = jnp.zeros((SUBLANES, CONV_DIM), F32)
    projs = [_dot(pre(x_ref[0, halves[0], :]), win_ref[...])]
    if seq_rows == tm:
        ph = _dot(pre(x_ref[0, rows - SUBLANES:rows + SUBLANES, :]), win_ref[:, _SEG_XBC[0]:_SEG_XBC[1]])
        halo = ((zero_rows, ph[SUBLANES:]), (ph[:SUBLANES], zero_rows))
    projs.append(_dot(pre(x_ref[0, halves[1], :]), win_ref[...]))

    lane = lax.broadcasted_iota(jnp.int32, (rows, LANES), 1)
    rope_lanes = lane < QK_ROPE
    first_quarter = jnp.bitwise_and(lane, QK_ROPE // 4) == 0
    nw = MLA_HEADS * QK_NOPE
    for hf, rs in enumerate(halves):
        proj = projs[hf]
        seg = lambda s: proj[:, s[0]:s[1]]
        xbc = seg(_SEG_XBC)
        if seq_rows == tm:
            act_ref[0, rs, :] = _conv_silu(xbc, halo[hf][0], halo[hf][1], cw_ref, cb_ref).astype(BF16)
        else:
            for s in range(rows // seq_rows):
                sub = slice(s * seq_rows, (s + 1) * seq_rows)
                dst = slice(rs.start + sub.start, rs.start + sub.stop)
                act_ref[0, dst, :] = _conv_silu(xbc[sub], zero_rows, zero_rows, cw_ref, cb_ref).astype(BF16)
        krb = seg(_SEG_KR)
        small_ref[0, rs, :] = krb
        ckv = _rms(seg(_SEG_KV), gkv_ref[...])
        qall = _dot(_rms(seg(_SEG_Q), gq_ref[...]).astype(BF16), wuq_ref[...])
        kv = _dot(ckv.astype(BF16), wukv_ref[...])
        z_ref[0, rs, :] = seg(_SEG_Z).astype(BF16)
        if latent:
            cos = cos_ref[rs, :]
            sin = sin_ref[rs, :]

            def rot(t):
                swapped = jnp.where(first_quarter, pltpu.roll(t, LANES - QK_ROPE // 4, axis=1),
                                    pltpu.roll(t, QK_ROPE // 4, axis=1))
                return t * cos + swapped * sin
        else:
            rot = lambda t: t
            ckv_ref[0, rs, :] = ckv
            for s in range(rows // seq_rows):
                krope_ref[rs.start // seq_rows + s, 0] = krb[s * seq_rows:(s + 1) * seq_rows, :QK_ROPE]
        kr_b = jnp.where(rope_lanes, rot(krb), 0.0).astype(BF16)
        for pr in range(MLA_HEADS // 2):
            pair = rot(qall[:, nw + pr * LANES:nw + (pr + 1) * LANES]) * scale
            for k in range(2):
                hd = 2 * pr + k
                base = hd * QK_PAD
                qr = pair if k == 0 else pltpu.roll(pair, QK_ROPE, axis=1)
                q_n = (qall[:, hd * LANES:(hd + 1) * LANES] * scale).astype(BF16)
                q_r = jnp.where(rope_lanes, qr, 0.0).astype(BF16)
                k_n = kv[:, base:base + LANES].astype(BF16)
                v_h = kv[:, base + LANES:base + QK_PAD].astype(BF16)
                if latent:
                    q_ref[0, rs, base:base + LANES] = q_n
                    q_ref[0, rs, base + LANES:base + QK_PAD] = q_r
                    k_ref[0, rs, base:base + LANES] = k_n
                    k_ref[0, rs, base + LANES:base + QK_PAD] = kr_b
                    v_ref[0, rs, hd * LANES:(hd + 1) * LANES] = v_h
                else:
                    q_h = jnp.concatenate([q_n, q_r], axis=1)
                    k_h = jnp.concatenate([k_n, kr_b], axis=1)
                    for s in range(rows // seq_rows):
                        sub = slice(s * seq_rows, (s + 1) * seq_rows)
                        sc = _dot_nt(q_h[sub], k_h[sub])
                        p = jnp.exp2(sc - jnp.max(sc, axis=-1, keepdims=True))
                        den = jnp.sum(p, axis=-1, keepdims=True)
                        dst = slice(rs.start + sub.start, rs.start + sub.stop)
                        attn_ref[0, dst, hd * V_HEAD:(hd + 1) * V_HEAD] = (
                            _dot(p.astype(BF16), v_h[sub]) / den).astype(BF16)
    if latent:
        _attn_body(q_ref, k_ref, v_ref, cckv_ref, ckr_ref, wukv_ref, attn_ref)


def _inproj(x, mod, mod_off, gpre, win, gq, wuq, gkv, wukv, cw, cb, rope, tm, seq_rows):
    b, s, d = x.shape
    latent = rope is not None
    grid = (b, s // tm)
    tok = lambda w: pl.BlockSpec((1, tm, w), lambda i, j: (i, j, 0))
    mod_spec = _mod_spec(d, mod_off)
    consts = [gpre, win, gq, wuq, gkv, wukv, cw, cb]
    in_specs = [tok(d), mod_spec]
    args = [x, mod]
    in_specs += [_const_spec(c.shape) for c in consts]
    args += consts
    widths = [MLA_HEADS * V_HEAD, SSM_WIDTH, CONV_DIM]
    out_shape = [jax.ShapeDtypeStruct((b, s, w), BF16) for w in widths]
    out_shape.append(jax.ShapeDtypeStruct((b, s, LANES), F32))
    out_specs = [tok(w) for w in widths] + [tok(LANES)]
    scratch = []
    if latent:
        cos, sin, cache_ckv, cache_kr = rope
        assert tm == s
        in_specs += [pl.BlockSpec((tm, LANES), lambda i, j: (j, 0))] * 2
        in_specs += [pl.BlockSpec((1,) + a.shape[1:], lambda i, j: (i, 0, 0)) for a in (cache_ckv, cache_kr)]
        args += [cos, sin, cache_ckv, cache_kr]
        scratch = [pltpu.VMEM((1, tm, w), BF16)
                   for w in (MLA_HEADS * QK_PAD, MLA_HEADS * QK_PAD, MLA_HEADS * V_HEAD)]
    else:
        assert b == 1
        out_shape += [jax.ShapeDtypeStruct((b, s, KV_LORA), F32),
                      jax.ShapeDtypeStruct((s // seq_rows, 1, seq_rows, QK_ROPE), F32)]
        out_specs += [tok(KV_LORA),
                      pl.BlockSpec((tm // seq_rows, 1, seq_rows, QK_ROPE), lambda i, j: (j, 0, 0, 0))]
    return pl.pallas_call(
        functools.partial(_inproj_kernel, latent=latent, seq_rows=seq_rows, mod_off=mod_off),
        grid=grid, in_specs=in_specs, out_specs=out_specs, out_shape=out_shape,
        scratch_shapes=scratch,
        compiler_params=_params(("parallel", "parallel")),
        name="inproj_latent" if latent else "inproj_ctx",
    )(*args)


def _attn_body(q_ref, k_ref, v_ref, cckv_ref, ckr_ref, wukv_ref, o_ref):
    kv_c = _dot(cckv_ref[0].astype(BF16), wukv_ref[...])
    kr_c = ckr_ref[0].astype(BF16)

    def scores(hd):
        qs = slice(hd * QK_PAD, (hd + 1) * QK_PAD)
        q = q_ref[0, :, qs]
        kc = jnp.concatenate([kv_c[:, hd * QK_PAD:hd * QK_PAD + LANES].astype(BF16), kr_c], axis=1)
        return _dot_nt(k_ref[0, :, qs], q), _dot_nt(kc, q)

    nxt = scores(0)
    for hd in range(MLA_HEADS):
        s, sc = nxt
        if hd + 1 < MLA_HEADS:
            nxt = scores(hd + 1)
        vs = slice(hd * V_HEAD, (hd + 1) * V_HEAD)
        mx = jnp.maximum(jnp.max(s, axis=0, keepdims=True), jnp.max(sc, axis=0, keepdims=True))
        p = jnp.exp2(s - mx)
        pc = jnp.exp2(sc - mx)
        ones = lambda rows: jnp.ones((BF16_ROWS, rows), BF16)
        v_t = jnp.concatenate([jnp.transpose(v_ref[0, :, vs].astype(F32)).astype(BF16),
                               ones(v_ref.shape[1])], axis=0)
        vc = kv_c[:, hd * QK_PAD + LANES:(hd + 1) * QK_PAD]
        vc_t = jnp.concatenate([jnp.transpose(vc).astype(BF16), ones(vc.shape[0])], axis=0)
        acc = _dot(v_t, p.astype(BF16)) + _dot(vc_t, pc.astype(BF16))
        out = acc[:V_HEAD] / acc[V_HEAD:V_HEAD + 1]
        o_ref[0, :, vs] = jnp.transpose(out).astype(BF16)


SSD_PACK = LANES // N_DT
SSD_GROUP = SSD_PACK


def _ssd_kernel(*refs, nseq, cps, has_init, n_cast):
    refs = list(refs)
    n_in = 3 + int(has_init) + 6
    cast_in = refs[n_in:n_in + n_cast]
    cast_out = refs[n_in + n_cast + 2:n_in + 2 * n_cast + 2]
    for src, dst in zip(cast_in, cast_out):
        dst[...] = src[...].astype(BF16)
    refs = refs[:n_in] + refs[n_in + n_cast:n_in + n_cast + 2] + refs[n_in + 2 * n_cast + 2:]
    if has_init:
        (act_ref, z_ref, small_ref, init_ref, dtbias_ref, alog_ref, dskip_ref, gout_ref,
         tril_ref, e64_ref, out_ref, fin_ref,
         fg_s, wsm_s, rt_s, bt_s, y_s, xwb_s, eg_s, sf_s, sb_s) = refs
    else:
        (act_ref, z_ref, small_ref, dtbias_ref, alog_ref, dskip_ref, gout_ref,
         tril_ref, e64_ref, out_ref, fin_ref,
         fg_s, wsm_s, rt_s, bt_s, y_s, xwb_s, eg_s, sf_s, sb_s) = refs
    L = CHUNK
    G = nseq * cps
    heads_per_group = SSM_HEADS // SSM_GROUPS
    b_off = SSM_WIDTH
    c_off = SSM_WIDTH + SSM_GROUPS * D_STATE

    lane = lax.broadcasted_iota(jnp.int32, (L, LANES), 1)
    for pk in range(G // SSD_PACK):
        chunks = range(pk * SSD_PACK, (pk + 1) * SSD_PACK)
        packed = jnp.zeros((L, LANES), F32)
        for k, c in enumerate(chunks):
            raw = small_ref[c // cps, (c % cps) * L:(c % cps + 1) * L, :]
            shifted = pltpu.roll(raw, (k * N_DT - DT_OFF) % LANES, axis=1)
            packed = jnp.where((lane >= k * N_DT) & (lane < (k + 1) * N_DT), shifted, packed)
        dtc = jax.nn.softplus(packed + dtbias_ref[...])
        da = dtc * (-jnp.exp(alog_ref[...]))
        cum = _dot3_rhs(tril_ref[...], da)
        tot = cum[L - 1:L, :]
        suf = tot - cum + da
        fg = jnp.where(jnp.bitwise_and(lane, SSM_HEADS) == 0, cum, suf)
        w_small = dtc * jnp.exp(tot - fg)
        e_small = jnp.exp(fg)
        rt_s[pk * LANES:(pk + 1) * LANES, :] = jnp.transpose(fg - jnp.log(dtc))
        for k, c in enumerate(chunks):
            back = (LANES - k * N_DT) % LANES
            unroll = lambda t: t if back == 0 else pltpu.roll(t, back, axis=1)
            fg_s[c] = unroll(fg)
            wsm_s[c, 0:L, :] = unroll(w_small).astype(BF16)
            wsm_s[c, L:2 * L, :] = unroll(e_small).astype(BF16)

    for s in range(nseq):
        if has_init:
            sf_s[s] = jnp.transpose(init_ref[s, 0])
            sb_s[s] = jnp.transpose(init_ref[s, 1])
        else:
            sf_s[s] = jnp.zeros((D_STATE, SSM_WIDTH), F32)
            sb_s[s] = jnp.zeros((D_STATE, SSM_WIDTH), F32)

    row_t = lax.broadcasted_iota(jnp.int32, (L, L), 0)
    col_s = lax.broadcasted_iota(jnp.int32, (L, L), 1)
    lower = col_s <= row_t
    upper = col_s >= row_t
    lane_lo = lane < SSM_HEAD_DIM
    neg_inf = jnp.float32(-jnp.inf)

    def locate(c):
        sq, j = divmod(c, cps)
        return sq, slice(j * L, (j + 1) * L), slice(c * L, (c + 1) * L)

    def bt_rows(c, g):
        r = (c * SSM_GROUPS + g) * D_STATE
        return slice(r, r + D_STATE)

    for c in range(G):
        sq, rs, fs = locate(c)
        xs_b = act_ref[sq, rs, 0:SSM_WIDTH]
        fgc = fg_s[c]
        rtc = rt_s[c * N_DT:(c + 1) * N_DT, :]
        ex = _dot(wsm_s[c], e64_ref[...])
        w_b = ex[0:L, :].astype(BF16)
        ef_x = ex[L:2 * L, :SSM_WIDTH]
        eg_x = ex[L:2 * L, SSM_WIDTH:]
        eg_s[fs, :] = eg_x
        xwf_b = xs_b * w_b[:, :SSM_WIDTH]
        xwb_s[fs, :] = xs_b * w_b[:, SSM_WIDTH:]

        y_parts = []
        new_states = []
        for g in range(SSM_GROUPS):
            gs = slice(g * GROUP_W, (g + 1) * GROUP_W)
            bm_g = act_ref[sq, rs, b_off + g * D_STATE:b_off + (g + 1) * D_STATE]
            cm_g = act_ref[sq, rs, c_off + g * D_STATE:c_off + (g + 1) * D_STATE]
            bt_g = jnp.transpose(bm_g.astype(F32)).astype(BF16)
            bt_s[bt_rows(c, g), :] = bt_g
            cbm = _dot_nt(cm_g, bm_g)
            for pair in range(heads_per_group // 2):
                ws = []
                for k in range(2):
                    cf = g * heads_per_group + pair * 2 + k
                    seg_f = fgc[:, cf:cf + 1] - rtc[cf:cf + 1, :]
                    lf = jnp.exp(jnp.where(lower, seg_f, neg_inf))
                    cg = cf + SSM_HEADS
                    seg_b = fgc[:, cg:cg + 1] - rtc[cg:cg + 1, :]
                    ub = jnp.exp(jnp.where(upper, seg_b, neg_inf))
                    ws.append((cbm * (lf + ub)).astype(BF16))
                p0 = (g * heads_per_group + pair * 2) * SSM_HEAD_DIM
                xpair = xs_b[:, p0:p0 + LANES]
                zeros = jnp.zeros_like(xpair)
                rhs = jnp.concatenate([jnp.where(lane_lo, xpair, zeros),
                                       jnp.where(lane_lo, zeros, xpair)], axis=0)
                y_parts.append(_dot(jnp.concatenate(ws, axis=1), rhs))
            s_in = sf_s[sq, :, gs]
            y_off = _dot(cm_g, s_in.astype(BF16)) * ef_x[:, gs]
            y_parts[-2] = y_parts[-2] + y_off[:, :LANES]
            y_parts[-1] = y_parts[-1] + y_off[:, LANES:]
            new_states.append(s_in * ef_x[L - 1:L, gs] + _dot(bt_g, xwf_b[:, gs]))
        for g in range(SSM_GROUPS):
            sf_s[sq, :, g * GROUP_W:(g + 1) * GROUP_W] = new_states[g]
        for i, yp in enumerate(y_parts):
            y_s[fs, i * LANES:(i + 1) * LANES] = yp

    for c in reversed(range(G)):
        sq, rs, fs = locate(c)
        xs = act_ref[sq, rs, 0:SSM_WIDTH].astype(F32)
        eg_x = eg_s[fs, :]
        xwb_b = xwb_s[fs, :]
        y_off = []
        for g in range(SSM_GROUPS):
            gs = slice(g * GROUP_W, (g + 1) * GROUP_W)
            s_in = sb_s[sq, :, gs]
            cm_g = act_ref[sq, rs, c_off + g * D_STATE:c_off + (g + 1) * D_STATE]
            y_off.append(_dot(cm_g, s_in.astype(BF16)) * eg_x[:, gs])
            sb_s[sq, :, gs] = s_in * eg_x[0:1, gs] + _dot(bt_s[bt_rows(c, g), :], xwb_b[:, gs])
        y = y_s[fs, :] + jnp.concatenate(y_off, axis=1) + dskip_ref[...] * xs
        y = y * _silu(z_ref[sq, rs, :].astype(F32))
        out_ref[sq, rs, :] = _rms(y, gout_ref[...]).astype(BF16)

    for s in range(nseq):
        fin_ref[s, 0] = jnp.transpose(sf_s[s])
        fin_ref[s, 1] = jnp.transpose(sb_s[s])


def _ssd(act, z, small, init, consts, cast=()):
    b, s, _ = act.shape
    has_init = init is not None
    cps = s // CHUNK
    assert SSD_GROUP % cps == 0
    nseq = SSD_GROUP // cps
    assert b % nseq == 0
    steps = b // nseq
    blk = lambda w: pl.BlockSpec((nseq, s, w), lambda i: (i, 0, 0))
    st_spec = pl.BlockSpec((nseq, 2, SSM_WIDTH, D_STATE), lambda i: (i, 0, 0, 0))
    in_specs = [blk(CONV_DIM), blk(SSM_WIDTH), blk(LANES)]
    args = [act, z, small]
    if has_init:
        in_specs.append(st_spec)
        args.append(init)
    in_specs += [_const_spec(c.shape) for c in consts]
    args += list(consts)
    cast_specs = [pl.BlockSpec((w.shape[0] // steps, w.shape[1]), lambda i: (i, 0)) for w in cast]
    in_specs += cast_specs
    args += list(cast)
    rows = SSD_GROUP * CHUNK
    scratch = [pltpu.VMEM((SSD_GROUP, CHUNK, LANES), F32),
               pltpu.VMEM((SSD_GROUP, 2 * CHUNK, LANES), BF16),
               pltpu.VMEM((SSD_GROUP * N_DT, CHUNK), F32),
               pltpu.VMEM((SSD_GROUP * SSM_GROUPS * D_STATE, CHUNK), BF16),
               pltpu.VMEM((rows, SSM_WIDTH), F32),
               pltpu.VMEM((rows, SSM_WIDTH), BF16),
               pltpu.VMEM((rows, SSM_WIDTH), F32),
               pltpu.VMEM((nseq, D_STATE, SSM_WIDTH), F32),
               pltpu.VMEM((nseq, D_STATE, SSM_WIDTH), F32)]
    return pl.pallas_call(
        functools.partial(_ssd_kernel, nseq=nseq, cps=cps, has_init=has_init, n_cast=len(cast)),
        grid=(steps,), in_specs=in_specs,
        out_specs=[blk(SSM_WIDTH), st_spec] + cast_specs,
        out_shape=[jax.ShapeDtypeStruct((b, s, SSM_WIDTH), BF16),
                   jax.ShapeDtypeStruct((b, 2, SSM_WIDTH, D_STATE), F32)]
                  + [jax.ShapeDtypeStruct(w.shape, BF16) for w in cast],
        scratch_shapes=scratch,
        compiler_params=_params(("parallel",)),
        name="ssd_latent" if has_init else "ssd_ctx",
    )(*args)


FF_CHUNK = 256


def _outffn_kernel(x_ref, attn_ref, ssm_ref, mod_ref, wout_ref, gpost_ref, gpre_ref, gpostf_ref,
                   wg_ref, wu_ref, wd_ref, o_ref, mix_s, y_s, h_s, *, mod_off):
    m = _mod_terms(mod_ref, mod_off)
    half = MLA_HEADS * V_HEAD
    rows = x_ref.shape[1] // 2
    nslice = 8
    srows = rows // nslice
    nchunks = D_FF // FF_CHUNK
    g1 = gpost_ref[...] * m[2]
    g2 = gpre_ref[...] * (1.0 + m[4])
    sh2 = m[3]
    g3 = gpostf_ref[...] * m[5]

    def out_proj(rs):
        return _dot(attn_ref[0, rs, :], wout_ref[0:half, :]) + _dot(ssm_ref[0, rs, :], wout_ref[half:, :])

    def pre_ffn(mix, xr):
        y = xr + _rms(mix, g1)
        return y, (_rms(y, g2) + sh2).astype(BF16)

    def ffn_chunk(h, c, acc):
        cs = slice(c * FF_CHUNK, (c + 1) * FF_CHUNK)
        part = _dot((_silu(_dot(h, wg_ref[:, cs])) * _dot(h, wu_ref[:, cs])).astype(BF16), wd_ref[cs, :])
        return part if acc is None else acc + part

    y_a, h_a = pre_ffn(out_proj(slice(0, rows)), x_ref[0, 0:rows, :])
    mix_s[...] = out_proj(slice(rows, 2 * rows))
    acc_a = None
    for c in range(nchunks):
        acc_a = ffn_chunk(h_a, c, acc_a)
        if c < nslice:
            rs = slice(c * srows, (c + 1) * srows)
            y_b, h_b = pre_ffn(mix_s[rs, :], x_ref[0, rows + c * srows:rows + (c + 1) * srows, :])
            y_s[rs, :] = y_b
            h_s[rs, :] = h_b
    h_bb = h_s[...]
    acc_b = None
    for c in range(nchunks):
        acc_b = ffn_chunk(h_bb, c, acc_b)
        if c < nslice:
            rs = slice(c * srows, (c + 1) * srows)
            o_ref[0, rs, :] = y_a[rs] + _rms(acc_a[rs], g3)
    o_ref[0, rows:2 * rows, :] = y_s[...] + _rms(acc_b, g3)


def _outffn(x, attn, ssm, mod, mod_off, wout, gpost, gpre, gpostf, wg, wu, wd, tm):
    b, s, d = x.shape
    tok = lambda w: pl.BlockSpec((1, tm, w), lambda i, j: (i, j, 0))
    consts = [wout, gpost, gpre, gpostf, wg, wu, wd]
    return pl.pallas_call(
        functools.partial(_outffn_kernel, mod_off=mod_off),
        grid=(b, s // tm),
        in_specs=[tok(d), tok(MLA_HEADS * V_HEAD), tok(SSM_WIDTH), _mod_spec(d, mod_off)]
                 + [_const_spec(c.shape) for c in consts],
        out_specs=tok(d),
        out_shape=jax.ShapeDtypeStruct((b, s, d), F32),
        scratch_shapes=[pltpu.VMEM((tm // 2, d), F32),
                        pltpu.VMEM((tm // 2, d), F32),
                        pltpu.VMEM((tm // 2, d), BF16)],
        compiler_params=_params(("parallel", "parallel")),
        name="out_ffn",
    )(x, attn, ssm, mod, *consts)


def _rope_tables(length):
    quarter = QK_ROPE // 4
    pos = np.arange(length)
    inv_freq = ROPE_THETA ** (-np.arange(quarter, dtype=np.float64) / quarter)
    ang_r = (pos // GRID_W)[:, None] * inv_freq[None, :]
    ang_c = (pos % GRID_W)[:, None] * inv_freq[None, :]
    cos = np.concatenate([np.cos(ang_r)] * 2 + [np.cos(ang_c)] * 2, axis=1)
    sin = np.concatenate([-np.sin(ang_r), np.sin(ang_r), -np.sin(ang_c), np.sin(ang_c)], axis=1)
    return jnp.asarray(np.tile(cos, (1, 2)), F32), jnp.asarray(np.tile(sin, (1, 2)), F32)


def _pad_cols(w, width, left=0):
    return jnp.pad(w, ((0, 0), (left, width - left - w.shape[-1])))


def _ssd_constants():
    idx = np.arange(CHUNK)
    tril = (idx[None, :] <= idx[:, None]).astype(np.float32)
    e64 = np.zeros((LANES, N_DT * SSM_HEAD_DIM), np.float32)
    for q in range(N_DT):
        e64[q, q * SSM_HEAD_DIM:(q + 1) * SSM_HEAD_DIM] = 1.0
    return [jnp.asarray(a, BF16) for a in (tril, e64)]


def kernel(x_prompt, x_sample, cache_ckv, cache_krope, state_ssm, c, c_ctx, w_mod, b_mod,
           g_pre_mix, g_post_mix, w_in, g_q, w_uq, g_kv, w_ukv, conv_w, conv_b, dt_bias,
           a_log, d_skip, g_ssm_out, w_out, g_pre_ffn, g_post_ffn, w_gate, w_up, w_down):
    depth = w_in.shape[0]
    assert depth == 1
    nb, seq, d = x_prompt.shape
    db, dseq, _ = x_sample.shape
    l = 0

    rows = 16
    cond = jnp.concatenate([c_ctx[None, :], c, jnp.zeros((rows - 1 - db, d), F32)], axis=0)
    mod = _modulation(cond, w_mod[l], b_mod[l][None, :])

    wi = w_in[l]
    o_q, o_kv, o_kr = Q_LORA, Q_LORA + KV_LORA, Q_LORA + KV_LORA + QK_ROPE
    o_z, o_xbc = o_kr + SSM_WIDTH, o_kr + SSM_WIDTH + CONV_DIM
    kr_dt = _pad_cols(jnp.concatenate([wi[:, o_kv:o_kr], wi[:, o_xbc:]], axis=1), LANES)
    win = jnp.concatenate([wi[:, :o_q], wi[:, o_q:o_kv], wi[:, o_kr:o_z], wi[:, o_z:o_xbc], kr_dt],
                          axis=1).astype(BF16)
    wq = w_uq[l].reshape(Q_LORA, MLA_HEADS, QK_NOPE + QK_ROPE)
    wuq = jnp.concatenate([wq[:, :, :QK_NOPE].reshape(Q_LORA, MLA_HEADS * QK_NOPE),
                           wq[:, :, QK_NOPE:].reshape(Q_LORA, MLA_HEADS * QK_ROPE)], axis=1).astype(BF16)
    wukv = w_ukv[l].astype(BF16)
    wout = w_out[l].astype(BF16)
    row = lambda v: v.reshape(1, -1)
    gpre, gq, gkv = row(g_pre_mix[l]), row(g_q[l]), row(g_kv[l])
    gpost, gpre_f, gpost_f = row(g_post_mix[l]), row(g_pre_ffn[l]), row(g_post_ffn[l])
    cw, cb = conv_w[l], row(conv_b[l])

    small = lambda v: jnp.tile(v.reshape(1, -1), (1, SSD_PACK))
    ssd_consts = [small(dt_bias[l]), small(a_log[l]),
                  row(jnp.repeat(d_skip[l], SSM_HEAD_DIM)), row(g_ssm_out[l])] + _ssd_constants()

    xp = x_prompt.reshape(1, nb * seq, d)
    attn, z, act, sm, ckv, krope = _inproj(xp, mod, 0, gpre, win, gq, wuq, gkv, wukv, cw, cb,
                                           None, tm=1024, seq_rows=seq)
    per_seq = lambda a: a.reshape(nb, seq, a.shape[-1])
    ssm, fin, wg, wu, wd = _ssd(per_seq(act), per_seq(z), per_seq(sm), None, ssd_consts,
                                cast=(w_gate[l], w_up[l], w_down[l]))
    y_p = _outffn(xp, attn, ssm.reshape(1, nb * seq, -1), mod, 0,
                  wout, gpost, gpre_f, gpost_f, wg, wu, wd, tm=1024).reshape(nb, seq, d)
    new_ckv = ckv.reshape(nb, 1, seq, KV_LORA)
    new_krope = krope
    new_ssm = fin.reshape(nb, 1, 2, SSM_HEADS, SSM_HEAD_DIM, D_STATE)

    kr_cache = jnp.pad(cache_krope[:, l], ((0, 0), (0, 0), (0, LANES - QK_ROPE)))
    attn, z, act, sm = _inproj(x_sample, mod, 1, gpre, win, gq, wuq, gkv, wukv, cw, cb,
                               _rope_tables(dseq) + (cache_ckv[:, l], kr_cache), tm=1024, seq_rows=dseq)
    init = state_ssm[:, l].reshape(db, 2, SSM_WIDTH, D_STATE)
    ssm, _ = _ssd(act, z, sm, init, ssd_consts)
    y_s = _outffn(x_sample, attn, ssm, mod, 1, wout, gpost, gpre_f, gpost_f, wg, wu, wd, tm=1024)

    return (y_p, y_s, new_ckv, new_krope, new_ssm)
```

```python
import functools

import numpy as np
import jax
import jax.numpy as jnp
from jax import lax
from jax.experimental import pallas as pl
from jax.experimental.pallas import tpu as pltpu

F32 = jnp.float32
BF16 = jnp.bfloat16

GRID_W = 64
ROPE_THETA = 10000.0
NORM_EPS = 1e-6
MLA_HEADS = 4
QK_NOPE = 128
QK_ROPE = 64
V_HEAD = 128
Q_LORA = 384
KV_LORA = 256
SSM_HEADS = 8
SSM_HEAD_DIM = 64
SSM_WIDTH = SSM_HEADS * SSM_HEAD_DIM
SSM_GROUPS = 2
D_STATE = 128
CHUNK = 128
CONV_DIM = SSM_WIDTH + 2 * SSM_GROUPS * D_STATE
D_FF = 2816

LOG2E = 1.4426950408889634
LANES = 128
SUBLANES = 8
BF16_ROWS = 16
QK_PAD = 256
GROUP_W = SSM_WIDTH // SSM_GROUPS
VMEM_LIMIT = 56 * 1024 * 1024

_SEG_Q = (0, 384)
_SEG_KV = (384, 640)
_SEG_Z = (640, 1152)
_SEG_XBC = (1152, 2176)
_SEG_KR = (2176, 2304)
DT_OFF = QK_ROPE
N_DT = 2 * SSM_HEADS


def _rms(x, g):
    return x * lax.rsqrt(jnp.mean(x * x, axis=-1, keepdims=True) + NORM_EPS) * g


def _silu(x):
    u = 0.5 * x
    return u * jnp.tanh(u) + u


def _dot(a, b):
    return jnp.dot(a, b, preferred_element_type=F32)


def _dot_nt(a, b):
    return lax.dot_general(a, b, (((1,), (1,)), ((), ())), preferred_element_type=F32)


def _split3(x):
    hi = x.astype(BF16)
    r = x - hi.astype(F32)
    mid = r.astype(BF16)
    lo = (r - mid.astype(F32)).astype(BF16)
    return hi, mid, lo


def _dot3_rhs(m, x):
    hi, mid, lo = _split3(x)
    return _dot(m, hi) + _dot(m, mid) + _dot(m, lo)


def _mod_spec(d, mod_off):
    return pl.BlockSpec((SUBLANES, 6 * d), lambda i, j: ((i + mod_off) // SUBLANES, 0))


def _mod_terms(mod_ref, mod_off):
    row = lax.rem(pl.program_id(0) + mod_off, SUBLANES)
    r = mod_ref[pl.ds(row, 1), :]
    d = r.shape[1] // 6
    return [r[:, k * d:(k + 1) * d] for k in range(6)]


def _const_spec(shape):
    nd = len(shape)
    return pl.BlockSpec(shape, lambda *_: (0,) * nd, pipeline_mode=pl.Buffered(1))


def _params(sem):
    return pltpu.CompilerParams(dimension_semantics=sem, vmem_limit_bytes=VMEM_LIMIT)


def _mod_kernel(c_ref, w_ref, b_ref, o_ref):
    s = _silu(c_ref[...])
    rows = s.shape[0]
    s_hi = s.astype(BF16)
    s_lo = (s - s_hi.astype(F32)).astype(BF16)
    both = _dot(jnp.concatenate([s_hi, s_lo], axis=0), w_ref[...].astype(BF16))
    o_ref[...] = both[:rows] + both[rows:] + b_ref[...]


def _modulation(cond, w_mod, b_mod):
    rows, d = cond.shape
    n = w_mod.shape[1]
    tn = 1536
    return pl.pallas_call(
        _mod_kernel,
        grid=(n // tn,),
        in_specs=[pl.BlockSpec((rows, d), lambda j: (0, 0)),
                  pl.BlockSpec((d, tn), lambda j: (0, j)),
                  pl.BlockSpec((1, tn), lambda j: (0, j))],
        out_specs=pl.BlockSpec((rows, tn), lambda j: (0, j)),
        out_shape=jax.ShapeDtypeStruct((rows, n), F32),
        compiler_params=_params(("arbitrary",)),
        name="modulation",
    )(cond, w_mod, b_mod)


def _conv_silu(xb, prev_rows, next_rows, cw_ref, cb_ref):
    n = xb.shape[0]
    win = jnp.concatenate([prev_rows, xb, next_rows], axis=0)
    total = n + 2 * SUBLANES
    prev = pltpu.roll(win, 1, axis=0)[SUBLANES:SUBLANES + n]
    nxt = pltpu.roll(win, total - 1, axis=0)[SUBLANES:SUBLANES + n]
    conv = cb_ref[...] + prev * cw_ref[0:1, :] + xb * cw_ref[1:2, :] + nxt * cw_ref[2:3, :]
    return _silu(conv)


def _inproj_kernel(*refs, latent, seq_rows, mod_off):
    if latent:
        (x_ref, mod_ref, gpre_ref, win_ref, gq_ref, wuq_ref, gkv_ref, wukv_ref, cw_ref, cb_ref,
         cos_ref, sin_ref, cckv_ref, ckr_ref, attn_ref, z_ref, act_ref, small_ref,
         q_ref, k_ref, v_ref) = refs
    else:
        (x_ref, mod_ref, gpre_ref, win_ref, gq_ref, wuq_ref, gkv_ref, wukv_ref, cw_ref, cb_ref,
         attn_ref, z_ref, act_ref, small_ref, ckv_ref, krope_ref) = refs
    scale = (QK_NOPE + QK_ROPE) ** -0.5 * LOG2E
    shift, mscale = _mod_terms(mod_ref, mod_off)[:2]

    g_mod = gpre_ref[...] * (1.0 + mscale)

    def pre(xv):
        return (_rms(xv, g_mod) + shift).astype(BF16)

    tm = x_ref.shape[1]
    rows = tm // 2
    halves = (slice(0, rows), slice(rows, tm))
    assert seq_rows == tm or rows % seq_rows == 0
    zero_rows = jnp.zeros((SUBLANES, CONV_DIM), F32)
    projs = [_dot(pre(x_ref[0, halves[0], :]), win_ref[...])]
    if seq_rows == tm:
        ph = _dot(pre(x_ref[0, rows - SUBLANES:rows + SUBLANES, :]), win_ref[:, _SEG_XBC[0]:_SEG_XBC[1]])
        halo = ((zero_rows, ph[SUBLANES:]), (ph[:SUBLANES], zero_rows))
    projs.append(_dot(pre(x_ref[0, halves[1], :]), win_ref[...]))

    lane = lax.broadcasted_iota(jnp.int32, (rows, LANES), 1)
    rope_lanes = lane < QK_ROPE
    first_quarter = jnp.bitwise_and(lane, QK_ROPE // 4) == 0
    nw = MLA_HEADS * QK_NOPE
    for hf, rs in enumerate(halves):
        proj = projs[hf]
        seg = lambda s: proj[:, s[0]:s[1]]
        xbc = seg(_SEG_XBC)
        if seq_rows == tm:
            act_ref[0, rs, :] = _conv_silu(xbc, halo[hf][0], halo[hf][1], cw_ref, cb_ref).astype(BF16)
        else:
            for s in range(rows // seq_rows):
                sub = slice(s * seq_rows, (s + 1) * seq_rows)
                dst = slice(rs.start + sub.start, rs.start + sub.stop)
                act_ref[0, dst, :] = _conv_silu(xbc[sub], zero_rows, zero_rows, cw_ref, cb_ref).astype(BF16)
        krb = seg(_SEG_KR)
        small_ref[0, rs, :] = krb
        ckv = _rms(seg(_SEG_KV), gkv_ref[...])
        qall = _dot(_rms(seg(_SEG_Q), gq_ref[...]).astype(BF16), wuq_ref[...])
        kv = _dot(ckv.astype(BF16), wukv_ref[...])
        z_ref[0, rs, :] = seg(_SEG_Z).astype(BF16)
        if latent:
            cos = cos_ref[rs, :]
            sin = sin_ref[rs, :]

            def rot(t):
                swapped = jnp.where(first_quarter, pltpu.roll(t, LANES - QK_ROPE // 4, axis=1),
                                    pltpu.roll(t, QK_ROPE // 4, axis=1))
                return t * cos + swapped * sin
        else:
            rot = lambda t: t
            ckv_ref[0, rs, :] = ckv
            for s in range(rows // seq_rows):
                krope_ref[rs.start // seq_rows + s, 0] = krb[s * seq_rows:(s + 1) * seq_rows, :QK_ROPE]
        kr_b = jnp.where(rope_lanes, rot(krb), 0.0).astype(BF16)
        for pr in range(MLA_HEADS // 2):
            pair = rot(qall[:, nw + pr * LANES:nw + (pr + 1) * LANES]) * scale
            for k in range(2):
                hd = 2 * pr + k
                base = hd * QK_PAD
                qr = pair if k == 0 else pltpu.roll(pair, QK_ROPE, axis=1)
                q_n = (qall[:, hd * LANES:(hd + 1) * LANES] * scale).astype(BF16)
                q_r = jnp.where(rope_lanes, qr, 0.0).astype(BF16)
                k_n = kv[:, base:base + LANES].astype(BF16)
                v_h = kv[:, base + LANES:base + QK_PAD].astype(BF16)
                if latent:
                    q_ref[0, rs, base:base + LANES] = q_n
                    q_ref[0, rs, base + LANES:base + QK_PAD] = q_r
                    k_ref[0, rs, base:base + LANES] = k_n
                    k_ref[0, rs, base + LANES:base + QK_PAD] = kr_b
                    v_ref[0, rs, hd * LANES:(hd + 1) * LANES] = v_h
                else:
                    q_h = jnp.concatenate([q_n, q_r], axis=1)
                    k_h = jnp.concatenate([k_n, kr_b], axis=1)
                    for s in range(rows // seq_rows):
                        sub = slice(s * seq_rows, (s + 1) * seq_rows)
                        sc = _dot_nt(q_h[sub], k_h[sub])
                        p = jnp.exp2(sc - jnp.max(sc, axis=-1, keepdims=True))
                        den = jnp.sum(p, axis=-1, keepdims=True)
                        dst = slice(rs.start + sub.start, rs.start + sub.stop)
                        attn_ref[0, dst, hd * V_HEAD:(hd + 1) * V_HEAD] = (
                            _dot(p.astype(BF16), v_h[sub]) / den).astype(BF16)
    if latent:
        _attn_body(q_ref, k_ref, v_ref, cckv_ref, ckr_ref, wukv_ref, attn_ref)


def _inproj(x, mod, mod_off, gpre, win, gq, wuq, gkv, wukv, cw, cb, rope, tm, seq_rows):
    b, s, d = x.shape
    latent = rope is not None
    grid = (b, s // tm)
    tok = lambda w: pl.BlockSpec((1, tm, w), lambda i, j: (i, j, 0))
    mod_spec = _mod_spec(d, mod_off)
    consts = [gpre, win, gq, wuq, gkv, wukv, cw, cb]
    in_specs = [tok(d), mod_spec]
    args = [x, mod]
    in_specs += [_const_spec(c.shape) for c in consts]
    args += consts
    widths = [MLA_HEADS * V_HEAD, SSM_WIDTH, CONV_DIM]
    out_shape = [jax.ShapeDtypeStruct((b, s, w), BF16) for w in widths]
    out_shape.append(jax.ShapeDtypeStruct((b, s, LANES), F32))
    out_specs = [tok(w) for w in widths] + [tok(LANES)]
    scratch = []
    if latent:
        cos, sin, cache_ckv, cache_kr = rope
        assert tm == s
        in_specs += [pl.BlockSpec((tm, LANES), lambda i, j: (j, 0))] * 2
        in_specs += [pl.BlockSpec((1,) + a.shape[1:], lambda i, j: (i, 0, 0)) for a in (cache_ckv, cache_kr)]
        args += [cos, sin, cache_ckv, cache_kr]
        scratch = [pltpu.VMEM((1, tm, w), BF16)
                   for w in (MLA_HEADS * QK_PAD, MLA_HEADS * QK_PAD, MLA_HEADS * V_HEAD)]
    else:
        assert b == 1
        out_shape += [jax.ShapeDtypeStruct((b, s, KV_LORA), F32),
                      jax.ShapeDtypeStruct((s // seq_rows, 1, seq_rows, QK_ROPE), F32)]
        out_specs += [tok(KV_LORA),
                      pl.BlockSpec((tm // seq_rows, 1, seq_rows, QK_ROPE), lambda i, j: (j, 0, 0, 0))]
    return pl.pallas_call(
        functools.partial(_inproj_kernel, latent=latent, seq_rows=seq_rows, mod_off=mod_off),
        grid=grid, in_specs=in_specs, out_specs=out_specs, out_shape=out_shape,
        scratch_shapes=scratch,
        compiler_params=_params(("parallel", "parallel")),
        name="inproj_latent" if latent else "inproj_ctx",
    )(*args)


def _attn_body(q_ref, k_ref, v_ref, cckv_ref, ckr_ref, wukv_ref, o_ref):
    kv_c = _dot(cckv_ref[0].astype(BF16), wukv_ref[...])
    kr_c = ckr_ref[0].astype(BF16)

    def scores(hd):
        qs = slice(hd * QK_PAD, (hd + 1) * QK_PAD)
        q = q_ref[0, :, qs]
        kc = jnp.concatenate([kv_c[:, hd * QK_PAD:hd * QK_PAD + LANES].astype(BF16), kr_c], axis=1)
        return _dot_nt(k_ref[0, :, qs], q), _dot_nt(kc, q)

    nxt = scores(0)
    for hd in range(MLA_HEADS):
        s, sc = nxt
        if hd + 1 < MLA_HEADS:
            nxt = scores(hd + 1)
        vs = slice(hd * V_HEAD, (hd + 1) * V_HEAD)
        mx = jnp.maximum(jnp.max(s, axis=0, keepdims=True), jnp.max(sc, axis=0, keepdims=True))
        p = jnp.exp2(s - mx)
        pc = jnp.exp2(sc - mx)
        ones = lambda rows: jnp.ones((BF16_ROWS, rows), BF16)
        v_t = jnp.concatenate([jnp.transpose(v_ref[0, :, vs].astype(F32)).astype(BF16),
                               ones(v_ref.shape[1])], axis=0)
        vc = kv_c[:, hd * QK_PAD + LANES:(hd + 1) * QK_PAD]
        vc_t = jnp.concatenate([jnp.transpose(vc).astype(BF16), ones(vc.shape[0])], axis=0)
        acc = _dot(v_t, p.astype(BF16)) + _dot(vc_t, pc.astype(BF16))
        out = acc[:V_HEAD] / acc[V_HEAD:V_HEAD + 1]
        o_ref[0, :, vs] = jnp.transpose(out).astype(BF16)


SSD_PACK = LANES // N_DT
SSD_GROUP = SSD_PACK


def _ssd_kernel(*refs, nseq, cps, has_init, n_cast):
    refs = list(refs)
    n_in = 3 + int(has_init) + 6
    cast_in = refs[n_in:n_in + n_cast]
    cast_out = refs[n_in + n_cast + 2:n_in + 2 * n_cast + 2]
    for src, dst in zip(cast_in, cast_out):
        dst[...] = src[...].astype(BF16)
    refs = refs[:n_in] + refs[n_in + n_cast:n_in + n_cast + 2] + refs[n_in + 2 * n_cast + 2:]
    if has_init:
        (act_ref, z_ref, small_ref, init_ref, dtbias_ref, alog_ref, dskip_ref, gout_ref,
         tril_ref, e64_ref, out_ref, fin_ref,
         fg_s, wsm_s, rt_s, bt_s, y_s, xwb_s, eg_s, sf_s, sb_s) = refs
    else:
        (act_ref, z_ref, small_ref, dtbias_ref, alog_ref, dskip_ref, gout_ref,
         tril_ref, e64_ref, out_ref, fin_ref,
         fg_s, wsm_s, rt_s, bt_s, y_s, xwb_s, eg_s, sf_s, sb_s) = refs
    L = CHUNK
    G = nseq * cps
    heads_per_group = SSM_HEADS // SSM_GROUPS
    b_off = SSM_WIDTH
    c_off = SSM_WIDTH + SSM_GROUPS * D_STATE

    lane = lax.broadcasted_iota(jnp.int32, (L, LANES), 1)
    for pk in range(G // SSD_PACK):
        chunks = range(pk * SSD_PACK, (pk + 1) * SSD_PACK)
        packed = jnp.zeros((L, LANES), F32)
        for k, c in enumerate(chunks):
            raw = small_ref[c // cps, (c % cps) * L:(c % cps + 1) * L, :]
            shifted = pltpu.roll(raw, (k * N_DT - DT_OFF) % LANES, axis=1)
            packed = jnp.where((lane >= k * N_DT) & (lane < (k + 1) * N_DT), shifted, packed)
        dtc = jax.nn.softplus(packed + dtbias_ref[...])
        da = dtc * (-jnp.exp(alog_ref[...]))
        cum = _dot3_rhs(tril_ref[...], da)
        tot = cum[L - 1:L, :]
        suf = tot - cum + da
        fg = jnp.where(jnp.bitwise_and(lane, SSM_HEADS) == 0, cum, suf)
        w_small = dtc * jnp.exp(tot - fg)
        e_small = jnp.exp(fg)
        rt_s[pk * LANES:(pk + 1) * LANES, :] = jnp.transpose(fg - jnp.log(dtc))
        for k, c in enumerate(chunks):
            back = (LANES - k * N_DT) % LANES
            unroll = lambda t: t if back == 0 else pltpu.roll(t, back, axis=1)
            fg_s[c] = unroll(fg)
            wsm_s[c, 0:L, :] = unroll(w_small).astype(BF16)
            wsm_s[c, L:2 * L, :] = unroll(e_small).astype(BF16)

    if has_init:
        for s in range(nseq):
            sf_s[s] = jnp.transpose(init_ref[s, 0])
            sb_s[s] = jnp.transpose(init_ref[s, 1])

    row_t = lax.broadcasted_iota(jnp.int32, (L, L), 0)
    col_s = lax.broadcasted_iota(jnp.int32, (L, L), 1)
    lower = col_s <= row_t
    upper = col_s >= row_t
    lane_lo = lane < SSM_HEAD_DIM
    neg_inf = jnp.float32(-jnp.inf)

    def locate(c):
        sq, j = divmod(c, cps)
        return sq, slice(j * L, (j + 1) * L), slice(c * L, (c + 1) * L)

    def bt_rows(c, g):
        r = (c * SSM_GROUPS + g) * D_STATE
        return slice(r, r + D_STATE)

    for c in range(G):
        sq, rs, fs = locate(c)
        xs_b = act_ref[sq, rs, 0:SSM_WIDTH]
        fgc = fg_s[c]
        rtc = rt_s[c * N_DT:(c + 1) * N_DT, :]
        ex = _dot(wsm_s[c], e64_ref[...])
        w_b = ex[0:L, :].astype(BF16)
        ef_x = ex[L:2 * L, :SSM_WIDTH]
        eg_x = ex[L:2 * L, SSM_WIDTH:]
        eg_s[fs, :] = eg_x
        xwf_b = xs_b * w_b[:, :SSM_WIDTH]
        xwb_s[fs, :] = xs_b * w_b[:, SSM_WIDTH:]

        y_parts = []
        new_states = []
        for g in range(SSM_GROUPS):
            gs = slice(g * GROUP_W, (g + 1) * GROUP_W)
            bm_g = act_ref[sq, rs, b_off + g * D_STATE:b_off + (g + 1) * D_STATE]
            cm_g = act_ref[sq, rs, c_off + g * D_STATE:c_off + (g + 1) * D_STATE]
            bt_g = jnp.transpose(bm_g.astype(F32)).astype(BF16)
            bt_s[bt_rows(c, g), :] = bt_g
            cbm = _dot_nt(cm_g, bm_g)
            for pair in range(heads_per_group // 2):
                ws = []
                for k in range(2):
                    cf = g * heads_per_group + pair * 2 + k
                    seg_f = fgc[:, cf:cf + 1] - rtc[cf:cf + 1, :]
                    lf = jnp.exp(jnp.where(lower, seg_f, neg_inf))
                    cg = cf + SSM_HEADS
                    seg_b = fgc[:, cg:cg + 1] - rtc[cg:cg + 1, :]
                    ub = jnp.exp(jnp.where(upper, seg_b, neg_inf))
                    ws.append((cbm * (lf + ub)).astype(BF16))
                p0 = (g * heads_per_group + pair * 2) * SSM_HEAD_DIM
                xpair = xs_b[:, p0:p0 + LANES]
                zeros = jnp.zeros_like(xpair)
                rhs = jnp.concatenate([jnp.where(lane_lo, xpair, zeros),
                                       jnp.where(lane_lo, zeros, xpair)], axis=0)
                y_parts.append(_dot(jnp.concatenate(ws, axis=1), rhs))
            chunk_state = _dot(bt_g, xwf_b[:, gs])
            if not has_init and c % cps == 0:
                new_states.append(chunk_state)
                continue
            s_in = sf_s[sq, :, gs]
            y_off = _dot(cm_g, s_in.astype(BF16)) * ef_x[:, gs]
            y_parts[-2] = y_parts[-2] + y_off[:, :LANES]
            y_parts[-1] = y_parts[-1] + y_off[:, LANES:]
            new_states.append(s_in * ef_x[L - 1:L, gs] + chunk_state)
        for g in range(SSM_GROUPS):
            sf_s[sq, :, g * GROUP_W:(g + 1) * GROUP_W] = new_states[g]
        for i, yp in enumerate(y_parts):
            y_s[fs, i * LANES:(i + 1) * LANES] = yp

    for c in reversed(range(G)):
        sq, rs, fs = locate(c)
        xs = act_ref[sq, rs, 0:SSM_WIDTH].astype(F32)
        eg_x = eg_s[fs, :]
        xwb_b = xwb_s[fs, :]
        y = y_s[fs, :] + dskip_ref[...] * xs
        from_zero = not has_init and c % cps == cps - 1
        y_off = []
        for g in range(SSM_GROUPS):
            gs = slice(g * GROUP_W, (g + 1) * GROUP_W)
            chunk_state = _dot(bt_s[bt_rows(c, g), :], xwb_b[:, gs])
            if from_zero:
                sb_s[sq, :, gs] = chunk_state
                continue
            s_in = sb_s[sq, :, gs]
            cm_g = act_ref[sq, rs, c_off + g * D_STATE:c_off + (g + 1) * D_STATE]
            y_off.append(_dot(cm_g, s_in.astype(BF16)) * eg_x[:, gs])
            sb_s[sq, :, gs] = s_in * eg_x[0:1, gs] + chunk_state
        if y_off:
            y = y + jnp.concatenate(y_off, axis=1)
        y = y * _silu(z_ref[sq, rs, :].astype(F32))
        out_ref[sq, rs, :] = _rms(y, gout_ref[...]).astype(BF16)

    for s in range(nseq):
        fin_ref[s, 0] = jnp.transpose(sf_s[s])
        fin_ref[s, 1] = jnp.transpose(sb_s[s])


def _ssd(act, z, small, init, consts, cast=()):
    b, s, _ = act.shape
    has_init = init is not None
    cps = s // CHUNK
    assert SSD_GROUP % cps == 0
    nseq = SSD_GROUP // cps
    assert b % nseq == 0
    steps = b // nseq
    blk = lambda w: pl.BlockSpec((nseq, s, w), lambda i: (i, 0, 0))
    st_spec = pl.BlockSpec((nseq, 2, SSM_WIDTH, D_STATE), lambda i: (i, 0, 0, 0))
    in_specs = [blk(CONV_DIM), blk(SSM_WIDTH), blk(LANES)]
    args = [act, z, small]
    if has_init:
        in_specs.append(st_spec)
        args.append(init)
    in_specs += [_const_spec(c.shape) for c in consts]
    args += list(consts)
    cast_specs = [pl.BlockSpec((w.shape[0] // steps, w.shape[1]), lambda i: (i, 0)) for w in cast]
    in_specs += cast_specs
    args += list(cast)
    rows = SSD_GROUP * CHUNK
    scratch = [pltpu.VMEM((SSD_GROUP, CHUNK, LANES), F32),
               pltpu.VMEM((SSD_GROUP, 2 * CHUNK, LANES), BF16),
               pltpu.VMEM((SSD_GROUP * N_DT, CHUNK), F32),
               pltpu.VMEM((SSD_GROUP * SSM_GROUPS * D_STATE, CHUNK), BF16),
               pltpu.VMEM((rows, SSM_WIDTH), F32),
               pltpu.VMEM((rows, SSM_WIDTH), BF16),
               pltpu.VMEM((rows, SSM_WIDTH), F32),
               pltpu.VMEM((nseq, D_STATE, SSM_WIDTH), F32),
               pltpu.VMEM((nseq, D_STATE, SSM_WIDTH), F32)]
    return pl.pallas_call(
        functools.partial(_ssd_kernel, nseq=nseq, cps=cps, has_init=has_init, n_cast=len(cast)),
        grid=(steps,), in_specs=in_specs,
        out_specs=[blk(SSM_WIDTH), st_spec] + cast_specs,
        out_shape=[jax.ShapeDtypeStruct((b, s, SSM_WIDTH), BF16),
                   jax.ShapeDtypeStruct((b, 2, SSM_WIDTH, D_STATE), F32)]
                  + [jax.ShapeDtypeStruct(w.shape, BF16) for w in cast],
        scratch_shapes=scratch,
        compiler_params=_params(("parallel",)),
        name="ssd_latent" if has_init else "ssd_ctx",
    )(*args)


FF_CHUNK = 256


def _outffn_kernel(x_ref, attn_ref, ssm_ref, mod_ref, wout_ref, gpost_ref, gpre_ref, gpostf_ref,
                   wg_ref, wu_ref, wd_ref, o_ref, mix_s, y_s, h_s, *, mod_off):
    m = _mod_terms(mod_ref, mod_off)
    half = MLA_HEADS * V_HEAD
    rows = x_ref.shape[1] // 2
    nslice = 8
    srows = rows // nslice
    nchunks = D_FF // FF_CHUNK
    g1 = gpost_ref[...] * m[2]
    g2 = gpre_ref[...] * (1.0 + m[4])
    sh2 = m[3]
    g3 = gpostf_ref[...] * m[5]

    def out_proj(rs):
        return _dot(attn_ref[0, rs, :], wout_ref[0:half, :]) + _dot(ssm_ref[0, rs, :], wout_ref[half:, :])

    def pre_ffn(mix, xr):
        y = xr + _rms(mix, g1)
        return y, (_rms(y, g2) + sh2).astype(BF16)

    def ffn_chunk(h, c, acc):
        cs = slice(c * FF_CHUNK, (c + 1) * FF_CHUNK)
        part = _dot((_silu(_dot(h, wg_ref[:, cs])) * _dot(h, wu_ref[:, cs])).astype(BF16), wd_ref[cs, :])
        return part if acc is None else acc + part

    y_a, h_a = pre_ffn(out_proj(slice(0, rows)), x_ref[0, 0:rows, :])
    mix_s[...] = out_proj(slice(rows, 2 * rows))
    acc_a = None
    for c in range(nchunks):
        acc_a = ffn_chunk(h_a, c, acc_a)
        if c < nslice:
            rs = slice(c * srows, (c + 1) * srows)
            y_b, h_b = pre_ffn(mix_s[rs, :], x_ref[0, rows + c * srows:rows + (c + 1) * srows, :])
            y_s[rs, :] = y_b
            h_s[rs, :] = h_b
    h_bb = h_s[...]
    acc_b = None
    for c in range(nchunks):
        acc_b = ffn_chunk(h_bb, c, acc_b)
        if c < nslice:
            rs = slice(c * srows, (c + 1) * srows)
            o_ref[0, rs, :] = y_a[rs] + _rms(acc_a[rs], g3)
    o_ref[0, rows:2 * rows, :] = y_s[...] + _rms(acc_b, g3)


def _outffn(x, attn, ssm, mod, mod_off, wout, gpost, gpre, gpostf, wg, wu, wd, tm):
    b, s, d = x.shape
    tok = lambda w: pl.BlockSpec((1, tm, w), lambda i, j: (i, j, 0))
    consts = [wout, gpost, gpre, gpostf, wg, wu, wd]
    return pl.pallas_call(
        functools.partial(_outffn_kernel, mod_off=mod_off),
        grid=(b, s // tm),
        in_specs=[tok(d), tok(MLA_HEADS * V_HEAD), tok(SSM_WIDTH), _mod_spec(d, mod_off)]
                 + [_const_spec(c.shape) for c in consts],
        out_specs=tok(d),
        out_shape=jax.ShapeDtypeStruct((b, s, d), F32),
        scratch_shapes=[pltpu.VMEM((tm // 2, d), F32),
                        pltpu.VMEM((tm // 2, d), F32),
                        pltpu.VMEM((tm // 2, d), BF16)],
        compiler_params=_params(("parallel", "parallel")),
        name="out_ffn",
    )(x, attn, ssm, mod, *consts)


def _rope_tables(length):
    quarter = QK_ROPE // 4
    pos = np.arange(length)
    inv_freq = ROPE_THETA ** (-np.arange(quarter, dtype=np.float64) / quarter)
    ang_r = (pos // GRID_W)[:, None] * inv_freq[None, :]
    ang_c = (pos % GRID_W)[:, None] * inv_freq[None, :]
    cos = np.concatenate([np.cos(ang_r)] * 2 + [np.cos(ang_c)] * 2, axis=1)
    sin = np.concatenate([-np.sin(ang_r), np.sin(ang_r), -np.sin(ang_c), np.sin(ang_c)], axis=1)
    return jnp.asarray(np.tile(cos, (1, 2)), F32), jnp.asarray(np.tile(sin, (1, 2)), F32)


def _pad_cols(w, width):
    return jnp.pad(w, ((0, 0), (0, width - w.shape[-1])))


def _ssd_constants():
    idx = np.arange(CHUNK)
    tril = (idx[None, :] <= idx[:, None]).astype(np.float32)
    e64 = np.zeros((LANES, N_DT * SSM_HEAD_DIM), np.float32)
    for q in range(N_DT):
        e64[q, q * SSM_HEAD_DIM:(q + 1) * SSM_HEAD_DIM] = 1.0
    return [jnp.asarray(a, BF16) for a in (tril, e64)]


def kernel(x_prompt, x_sample, cache_ckv, cache_krope, state_ssm, c, c_ctx, w_mod, b_mod,
           g_pre_mix, g_post_mix, w_in, g_q, w_uq, g_kv, w_ukv, conv_w, conv_b, dt_bias,
           a_log, d_skip, g_ssm_out, w_out, g_pre_ffn, g_post_ffn, w_gate, w_up, w_down):
    depth = w_in.shape[0]
    assert depth == 1
    nb, seq, d = x_prompt.shape
    db, dseq, _ = x_sample.shape
    l = 0

    rows = 16
    cond = jnp.concatenate([c_ctx[None, :], c, jnp.zeros((rows - 1 - db, d), F32)], axis=0)
    mod = _modulation(cond, w_mod[l], b_mod[l][None, :])

    wi = w_in[l]
    o_q, o_kv, o_kr = Q_LORA, Q_LORA + KV_LORA, Q_LORA + KV_LORA + QK_ROPE
    o_z, o_xbc = o_kr + SSM_WIDTH, o_kr + SSM_WIDTH + CONV_DIM
    kr_dt = _pad_cols(jnp.concatenate([wi[:, o_kv:o_kr], wi[:, o_xbc:]], axis=1), LANES)
    win = jnp.concatenate([wi[:, :o_q], wi[:, o_q:o_kv], wi[:, o_kr:o_z], wi[:, o_z:o_xbc], kr_dt],
                          axis=1).astype(BF16)
    wq = w_uq[l].reshape(Q_LORA, MLA_HEADS, QK_NOPE + QK_ROPE)
    wuq = jnp.concatenate([wq[:, :, :QK_NOPE].reshape(Q_LORA, MLA_HEADS * QK_NOPE),
                           wq[:, :, QK_NOPE:].reshape(Q_LORA, MLA_HEADS * QK_ROPE)], axis=1).astype(BF16)
    wukv = w_ukv[l].astype(BF16)
    wout = w_out[l].astype(BF16)
    row = lambda v: v.reshape(1, -1)
    gpre, gq, gkv = row(g_pre_mix[l]), row(g_q[l]), row(g_kv[l])
    gpost, gpre_f, gpost_f = row(g_post_mix[l]), row(g_pre_ffn[l]), row(g_post_ffn[l])
    cw, cb = conv_w[l], row(conv_b[l])

    small = lambda v: jnp.tile(v.reshape(1, -1), (1, SSD_PACK))
    ssd_consts = [small(dt_bias[l]), small(a_log[l]),
                  row(jnp.repeat(d_skip[l], SSM_HEAD_DIM)), row(g_ssm_out[l])] + _ssd_constants()

    xp = x_prompt.reshape(1, nb * seq, d)
    attn, z, act, sm, ckv, krope = _inproj(xp, mod, 0, gpre, win, gq, wuq, gkv, wukv, cw, cb,
                                           None, tm=1024, seq_rows=seq)
    per_seq = lambda a: a.reshape(nb, seq, a.shape[-1])
    ssm, fin, wg, wu, wd = _ssd(per_seq(act), per_seq(z), per_seq(sm), None, ssd_consts,
                                cast=(w_gate[l], w_up[l], w_down[l]))
    y_p = _outffn(xp, attn, ssm.reshape(1, nb * seq, -1), mod, 0,
                  wout, gpost, gpre_f, gpost_f, wg, wu, wd, tm=1024).reshape(nb, seq, d)
    new_ckv = ckv.reshape(nb, 1, seq, KV_LORA)
    new_krope = krope
    new_ssm = fin.reshape(nb, 1, 2, SSM_HEADS, SSM_HEAD_DIM, D_STATE)

    kr_cache = jnp.pad(cache_krope[:, l], ((0, 0), (0, 0), (0, LANES - QK_ROPE)))
    attn, z, act, sm = _inproj(x_sample, mod, 1, gpre, win, gq, wuq, gkv, wukv, cw, cb,
                               _rope_tables(dseq) + (cache_ckv[:, l], kr_cache), tm=1024, seq_rows=dseq)
    init = state_ssm[:, l].reshape(db, 2, SSM_WIDTH, D_STATE)
    ssm, _ = _ssd(act, z, sm, init, ssd_consts)
    y_s = _outffn(x_sample, attn, ssm, mod, 1, wout, gpost, gpre_f, gpost_f, wg, wu, wd, tm=1024)

    return (y_p, y_s, new_ckv, new_krope, new_ssm)
```

```python
import functools

import numpy as np
import jax
import jax.numpy as jnp
from jax import lax
from jax.experimental import pallas as pl
from jax.experimental.pallas import tpu as pltpu

F32 = jnp.float32
BF16 = jnp.bfloat16

GRID_W = 64
ROPE_THETA = 10000.0
NORM_EPS = 1e-6
MLA_HEADS = 4
QK_NOPE = 128
QK_ROPE = 64
V_HEAD = 128
Q_LORA = 384
KV_LORA = 256
SSM_HEADS = 8
SSM_HEAD_DIM = 64
SSM_WIDTH = SSM_HEADS * SSM_HEAD_DIM
SSM_GROUPS = 2
D_STATE = 128
CHUNK = 128
CONV_DIM = SSM_WIDTH + 2 * SSM_GROUPS * D_STATE
D_FF = 2816

LOG2E = 1.4426950408889634
LANES = 128
SUBLANES = 8
BF16_ROWS = 16
QK_PAD = 256
GROUP_W = SSM_WIDTH // SSM_GROUPS
VMEM_LIMIT = 56 * 1024 * 1024

_SEG_Q = (0, 384)
_SEG_KV = (384, 640)
_SEG_Z = (640, 1152)
_SEG_XBC = (1152, 2176)
_SEG_KR = (2176, 2304)
DT_OFF = QK_ROPE
N_DT = 2 * SSM_HEADS


def _rms(x, g):
    return x * lax.rsqrt(jnp.mean(x * x, axis=-1, keepdims=True) + NORM_EPS) * g


def _silu(x):
    u = 0.5 * x
    return u * jnp.tanh(u) + u


def _dot(a, b):
    return jnp.dot(a, b, preferred_element_type=F32)


def _dot_nt(a, b):
    return lax.dot_general(a, b, (((1,), (1,)), ((), ())), preferred_element_type=F32)


def _split3(x):
    hi = x.astype(BF16)
    r = x - hi.astype(F32)
    mid = r.astype(BF16)
    lo = (r - mid.astype(F32)).astype(BF16)
    return hi, mid, lo


def _dot3_rhs(m, x):
    hi, mid, lo = _split3(x)
    return _dot(m, hi) + _dot(m, mid) + _dot(m, lo)


def _mod_spec(d, mod_off):
    return pl.BlockSpec((SUBLANES, 6 * d), lambda i, j: ((i + mod_off) // SUBLANES, 0))


def _mod_terms(mod_ref, mod_off):
    row = lax.rem(pl.program_id(0) + mod_off, SUBLANES)
    r = mod_ref[pl.ds(row, 1), :]
    d = r.shape[1] // 6
    return [r[:, k * d:(k + 1) * d] for k in range(6)]


def _const_spec(shape):
    nd = len(shape)
    return pl.BlockSpec(shape, lambda *_: (0,) * nd, pipeline_mode=pl.Buffered(1))


def _params(sem):
    return pltpu.CompilerParams(dimension_semantics=sem, vmem_limit_bytes=VMEM_LIMIT)


def _mod_kernel(c_ref, w_ref, b_ref, o_ref):
    s = _silu(c_ref[...])
    rows = s.shape[0]
    s_hi = s.astype(BF16)
    s_lo = (s - s_hi.astype(F32)).astype(BF16)
    both = _dot(jnp.concatenate([s_hi, s_lo], axis=0), w_ref[...].astype(BF16))
    o_ref[...] = both[:rows] + both[rows:] + b_ref[...]


def _modulation(cond, w_mod, b_mod):
    rows, d = cond.shape
    n = w_mod.shape[1]
    tn = 1536
    return pl.pallas_call(
        _mod_kernel,
        grid=(n // tn,),
        in_specs=[pl.BlockSpec((rows, d), lambda j: (0, 0)),
                  pl.BlockSpec((d, tn), lambda j: (0, j)),
                  pl.BlockSpec((1, tn), lambda j: (0, j))],
        out_specs=pl.BlockSpec((rows, tn), lambda j: (0, j)),
        out_shape=jax.ShapeDtypeStruct((rows, n), F32),
        compiler_params=_params(("arbitrary",)),
        name="modulation",
    )(cond, w_mod, b_mod)


def _conv_silu(xb, prev_rows, next_rows, cw_ref, cb_ref):
    n = xb.shape[0]
    win = jnp.concatenate([prev_rows, xb, next_rows], axis=0)
    total = n + 2 * SUBLANES
    prev = pltpu.roll(win, 1, axis=0)[SUBLANES:SUBLANES + n]
    nxt = pltpu.roll(win, total - 1, axis=0)[SUBLANES:SUBLANES + n]
    conv = cb_ref[...] + prev * cw_ref[0:1, :] + xb * cw_ref[1:2, :] + nxt * cw_ref[2:3, :]
    return _silu(conv)


def _inproj_kernel(*refs, latent, seq_rows, mod_off):
    if latent:
        (x_ref, mod_ref, gpre_ref, win_ref, gq_ref, wuq_ref, gkv_ref, wukv_ref, cw_ref, cb_ref,
         cos_ref, sin_ref, cckv_ref, ckr_ref, attn_ref, z_ref, act_ref, small_ref,
         q_ref, k_ref, v_ref) = refs
    else:
        (x_ref, mod_ref, gpre_ref, win_ref, gq_ref, wuq_ref, gkv_ref, wukv_ref, cw_ref, cb_ref,
         attn_ref, z_ref, act_ref, small_ref, ckv_ref, krope_ref) = refs
    scale = (QK_NOPE + QK_ROPE) ** -0.5 * LOG2E
    shift, mscale = _mod_terms(mod_ref, mod_off)[:2]

    g_mod = gpre_ref[...] * (1.0 + mscale)

    def pre(xv):
        return (_rms(xv, g_mod) + shift).astype(BF16)

    tm = x_ref.shape[1]
    rows = tm // 2
    halves = (slice(0, rows), slice(rows, tm))
    assert seq_rows == tm or rows % seq_rows == 0
    zero_rows = jnp.zeros((SUBLANES, CONV_DIM), F32)
    projs = [_dot(pre(x_ref[0, halves[0], :]), win_ref[...])]
    if seq_rows == tm:
        ph = _dot(pre(x_ref[0, rows - SUBLANES:rows + SUBLANES, :]), win_ref[:, _SEG_XBC[0]:_SEG_XBC[1]])
        halo = ((zero_rows, ph[SUBLANES:]), (ph[:SUBLANES], zero_rows))
    projs.append(_dot(pre(x_ref[0, halves[1], :]), win_ref[...]))

    lane = lax.broadcasted_iota(jnp.int32, (rows, LANES), 1)
    rope_lanes = lane < QK_ROPE
    first_quarter = jnp.bitwise_and(lane, QK_ROPE // 4) == 0
    nw = MLA_HEADS * QK_NOPE
    for hf, rs in enumerate(halves):
        proj = projs[hf]
        seg = lambda s: proj[:, s[0]:s[1]]
        xbc = seg(_SEG_XBC)
        if seq_rows == tm:
            act_ref[0, rs, :] = _conv_silu(xbc, halo[hf][0], halo[hf][1], cw_ref, cb_ref).astype(BF16)
        else:
            for s in range(rows // seq_rows):
                sub = slice(s * seq_rows, (s + 1) * seq_rows)
                dst = slice(rs.start + sub.start, rs.start + sub.stop)
                act_ref[0, dst, :] = _conv_silu(xbc[sub], zero_rows, zero_rows, cw_ref, cb_ref).astype(BF16)
        krb = seg(_SEG_KR)
        small_ref[0, rs, :] = krb
        ckv = _rms(seg(_SEG_KV), gkv_ref[...])
        qall = _dot(_rms(seg(_SEG_Q), gq_ref[...]).astype(BF16), wuq_ref[...])
        kv = _dot(ckv.astype(BF16), wukv_ref[...])
        z_ref[0, rs, :] = seg(_SEG_Z).astype(BF16)
        if latent:
            cos = cos_ref[rs, :]
            sin = sin_ref[rs, :]

            def rot(t):
                swapped = jnp.where(first_quarter, pltpu.roll(t, LANES - QK_ROPE // 4, axis=1),
                                    pltpu.roll(t, QK_ROPE // 4, axis=1))
                return t * cos + swapped * sin
        else:
            rot = lambda t: t
            ckv_ref[0, rs, :] = ckv
            for s in range(rows // seq_rows):
                krope_ref[rs.start // seq_rows + s, 0] = krb[s * seq_rows:(s + 1) * seq_rows, :QK_ROPE]
        kr_b = jnp.where(rope_lanes, rot(krb), 0.0).astype(BF16)
        for pr in range(MLA_HEADS // 2):
            pair = rot(qall[:, nw + pr * LANES:nw + (pr + 1) * LANES]) * scale
            for k in range(2):
                hd = 2 * pr + k
                base = hd * QK_PAD
                qr = pair if k == 0 else pltpu.roll(pair, QK_ROPE, axis=1)
                q_n = (qall[:, hd * LANES:(hd + 1) * LANES] * scale).astype(BF16)
                q_r = jnp.where(rope_lanes, qr, 0.0).astype(BF16)
                k_n = kv[:, base:base + LANES].astype(BF16)
                v_h = kv[:, base + LANES:base + QK_PAD].astype(BF16)
                if latent:
                    q_ref[0, rs, base:base + LANES] = q_n
                    q_ref[0, rs, base + LANES:base + QK_PAD] = q_r
                    k_ref[0, rs, base:base + LANES] = k_n
                    k_ref[0, rs, base + LANES:base + QK_PAD] = kr_b
                    v_ref[0, rs, hd * LANES:(hd + 1) * LANES] = v_h
                else:
                    q_h = jnp.concatenate([q_n, q_r], axis=1)
                    k_h = jnp.concatenate([k_n, kr_b], axis=1)
                    for s in range(rows // seq_rows):
                        sub = slice(s * seq_rows, (s + 1) * seq_rows)
                        sc = _dot_nt(q_h[sub], k_h[sub])
                        p = jnp.exp2(sc - jnp.max(sc, axis=-1, keepdims=True))
                        den = jnp.sum(p, axis=-1, keepdims=True)
                        dst = slice(rs.start + sub.start, rs.start + sub.stop)
                        attn_ref[0, dst, hd * V_HEAD:(hd + 1) * V_HEAD] = (
                            _dot(p.astype(BF16), v_h[sub]) / den).astype(BF16)
    if latent:
        _attn_body(q_ref, k_ref, v_ref, cckv_ref, ckr_ref, wukv_ref, attn_ref)


def _inproj(x, mod, mod_off, gpre, win, gq, wuq, gkv, wukv, cw, cb, rope, tm, seq_rows):
    b, s, d = x.shape
    latent = rope is not None
    grid = (b, s // tm)
    tok = lambda w: pl.BlockSpec((1, tm, w), lambda i, j: (i, j, 0))
    mod_spec = _mod_spec(d, mod_off)
    consts = [gpre, win, gq, wuq, gkv, wukv, cw, cb]
    in_specs = [tok(d), mod_spec]
    args = [x, mod]
    in_specs += [_const_spec(c.shape) for c in consts]
    args += consts
    widths = [MLA_HEADS * V_HEAD, SSM_WIDTH, CONV_DIM]
    out_shape = [jax.ShapeDtypeStruct((b, s, w), BF16) for w in widths]
    out_shape.append(jax.ShapeDtypeStruct((b, s, LANES), F32))
    out_specs = [tok(w) for w in widths] + [tok(LANES)]
    scratch = []
    if latent:
        cos, sin, cache_ckv, cache_kr = rope
        assert tm == s
        in_specs += [pl.BlockSpec((tm, LANES), lambda i, j: (j, 0))] * 2
        in_specs += [pl.BlockSpec((1,) + a.shape[1:], lambda i, j: (i, 0, 0)) for a in (cache_ckv, cache_kr)]
        args += [cos, sin, cache_ckv, cache_kr]
        scratch = [pltpu.VMEM((1, tm, w), BF16)
                   for w in (MLA_HEADS * QK_PAD, MLA_HEADS * QK_PAD, MLA_HEADS * V_HEAD)]
    else:
        assert b == 1
        out_shape += [jax.ShapeDtypeStruct((b, s, KV_LORA), F32),
                      jax.ShapeDtypeStruct((s // seq_rows, 1, seq_rows, QK_ROPE), F32)]
        out_specs += [tok(KV_LORA),
                      pl.BlockSpec((tm // seq_rows, 1, seq_rows, QK_ROPE), lambda i, j: (j, 0, 0, 0))]
    return pl.pallas_call(
        functools.partial(_inproj_kernel, latent=latent, seq_rows=seq_rows, mod_off=mod_off),
        grid=grid, in_specs=in_specs, out_specs=out_specs, out_shape=out_shape,
        scratch_shapes=scratch,
        compiler_params=_params(("parallel", "parallel")),
        name="inproj_latent" if latent else "inproj_ctx",
    )(*args)


def _attn_body(q_ref, k_ref, v_ref, cckv_ref, ckr_ref, wukv_ref, o_ref):
    kv_c = _dot(cckv_ref[0].astype(BF16), wukv_ref[...])
    kr_c = ckr_ref[0].astype(BF16)

    def scores(hd):
        qs = slice(hd * QK_PAD, (hd + 1) * QK_PAD)
        q = q_ref[0, :, qs]
        kc = jnp.concatenate([kv_c[:, hd * QK_PAD:hd * QK_PAD + LANES].astype(BF16), kr_c], axis=1)
        return _dot_nt(k_ref[0, :, qs], q), _dot_nt(kc, q)

    nxt = scores(0)
    for hd in range(MLA_HEADS):
        s, sc = nxt
        if hd + 1 < MLA_HEADS:
            nxt = scores(hd + 1)
        vs = slice(hd * V_HEAD, (hd + 1) * V_HEAD)
        mx = jnp.maximum(jnp.max(s, axis=0, keepdims=True), jnp.max(sc, axis=0, keepdims=True))
        p = jnp.exp2(s - mx)
        pc = jnp.exp2(sc - mx)
        ones = lambda rows: jnp.ones((BF16_ROWS, rows), BF16)
        v_t = jnp.concatenate([jnp.transpose(v_ref[0, :, vs]),
                               ones(v_ref.shape[1])], axis=0)
        vc = kv_c[:, hd * QK_PAD + LANES:(hd + 1) * QK_PAD]
        vc_t = jnp.concatenate([jnp.transpose(vc.astype(BF16)), ones(vc.shape[0])], axis=0)
        acc = _dot(v_t, p.astype(BF16)) + _dot(vc_t, pc.astype(BF16))
        out = acc[:V_HEAD] / acc[V_HEAD:V_HEAD + 1]
        o_ref[0, :, vs] = jnp.transpose(out.astype(BF16))


SSD_PACK = LANES // N_DT
SSD_GROUP = SSD_PACK


def _ssd_kernel(*refs, nseq, cps, has_init, n_cast):
    refs = list(refs)
    n_in = 3 + int(has_init) + 6
    cast_in = refs[n_in:n_in + n_cast]
    cast_out = refs[n_in + n_cast + 2:n_in + 2 * n_cast + 2]
    for src, dst in zip(cast_in, cast_out):
        dst[...] = src[...].astype(BF16)
    refs = refs[:n_in] + refs[n_in + n_cast:n_in + n_cast + 2] + refs[n_in + 2 * n_cast + 2:]
    if has_init:
        (act_ref, z_ref, small_ref, init_ref, dtbias_ref, alog_ref, dskip_ref, gout_ref,
         tril_ref, e64_ref, out_ref, fin_ref,
         fg_s, wsm_s, rt_s, bt_s, y_s, xwb_s, eg_s, sf_s, sb_s) = refs
    else:
        (act_ref, z_ref, small_ref, dtbias_ref, alog_ref, dskip_ref, gout_ref,
         tril_ref, e64_ref, out_ref, fin_ref,
         fg_s, wsm_s, rt_s, bt_s, y_s, xwb_s, eg_s, sf_s, sb_s) = refs
    L = CHUNK
    G = nseq * cps
    heads_per_group = SSM_HEADS // SSM_GROUPS
    b_off = SSM_WIDTH
    c_off = SSM_WIDTH + SSM_GROUPS * D_STATE

    lane = lax.broadcasted_iota(jnp.int32, (L, LANES), 1)
    for pk in range(G // SSD_PACK):
        chunks = range(pk * SSD_PACK, (pk + 1) * SSD_PACK)
        packed = jnp.zeros((L, LANES), F32)
        for k, c in enumerate(chunks):
            raw = small_ref[c // cps, (c % cps) * L:(c % cps + 1) * L, :]
            shifted = pltpu.roll(raw, (k * N_DT - DT_OFF) % LANES, axis=1)
            packed = jnp.where((lane >= k * N_DT) & (lane < (k + 1) * N_DT), shifted, packed)
        dtc = jax.nn.softplus(packed + dtbias_ref[...])
        da = dtc * (-jnp.exp(alog_ref[...]))
        cum = _dot3_rhs(tril_ref[...], da)
        tot = cum[L - 1:L, :]
        suf = tot - cum + da
        fg = jnp.where(jnp.bitwise_and(lane, SSM_HEADS) == 0, cum, suf)
        w_small = dtc * jnp.exp(tot - fg)
        e_small = jnp.exp(fg)
        rt_s[pk * LANES:(pk + 1) * LANES, :] = jnp.transpose(fg - jnp.log(dtc))
        for k, c in enumerate(chunks):
            back = (LANES - k * N_DT) % LANES
            unroll = lambda t: t if back == 0 else pltpu.roll(t, back, axis=1)
            fg_s[c] = unroll(fg)
            wsm_s[c, 0:L, :] = unroll(w_small).astype(BF16)
            wsm_s[c, L:2 * L, :] = unroll(e_small).astype(BF16)

    if has_init:
        for s in range(nseq):
            sf_s[s] = jnp.transpose(init_ref[s, 0])
            sb_s[s] = jnp.transpose(init_ref[s, 1])

    row_t = lax.broadcasted_iota(jnp.int32, (L, L), 0)
    col_s = lax.broadcasted_iota(jnp.int32, (L, L), 1)
    lower = col_s <= row_t
    upper = col_s >= row_t
    lane_lo = lane < SSM_HEAD_DIM
    neg_inf = jnp.float32(-jnp.inf)

    def locate(c):
        sq, j = divmod(c, cps)
        return sq, slice(j * L, (j + 1) * L), slice(c * L, (c + 1) * L)

    def bt_rows(c, g):
        r = (c * SSM_GROUPS + g) * D_STATE
        return slice(r, r + D_STATE)

    for c in range(G):
        sq, rs, fs = locate(c)
        xs_b = act_ref[sq, rs, 0:SSM_WIDTH]
        fgc = fg_s[c]
        rtc = rt_s[c * N_DT:(c + 1) * N_DT, :]
        ex = _dot(wsm_s[c], e64_ref[...])
        w_b = ex[0:L, :].astype(BF16)
        ef_x = ex[L:2 * L, :SSM_WIDTH]
        eg_x = ex[L:2 * L, SSM_WIDTH:]
        eg_s[fs, :] = eg_x
        xwf_b = xs_b * w_b[:, :SSM_WIDTH]
        xwb_s[fs, :] = xs_b * w_b[:, SSM_WIDTH:]

        y_parts = []
        new_states = []
        for g in range(SSM_GROUPS):
            gs = slice(g * GROUP_W, (g + 1) * GROUP_W)
            bm_g = act_ref[sq, rs, b_off + g * D_STATE:b_off + (g + 1) * D_STATE]
            cm_g = act_ref[sq, rs, c_off + g * D_STATE:c_off + (g + 1) * D_STATE]
            bt_g = jnp.transpose(bm_g)
            bt_s[bt_rows(c, g), :] = bt_g
            cbm = _dot_nt(cm_g, bm_g)
            for pair in range(heads_per_group // 2):
                ws = []
                for k in range(2):
                    cf = g * heads_per_group + pair * 2 + k
                    seg_f = fgc[:, cf:cf + 1] - rtc[cf:cf + 1, :]
                    lf = jnp.exp(jnp.where(lower, seg_f, neg_inf))
                    cg = cf + SSM_HEADS
                    seg_b = fgc[:, cg:cg + 1] - rtc[cg:cg + 1, :]
                    ub = jnp.exp(jnp.where(upper, seg_b, neg_inf))
                    ws.append((cbm * (lf + ub)).astype(BF16))
                p0 = (g * heads_per_group + pair * 2) * SSM_HEAD_DIM
                xpair = xs_b[:, p0:p0 + LANES]
                zeros = jnp.zeros_like(xpair)
                rhs = jnp.concatenate([jnp.where(lane_lo, xpair, zeros),
                                       jnp.where(lane_lo, zeros, xpair)], axis=0)
                y_parts.append(_dot(jnp.concatenate(ws, axis=1), rhs))
            chunk_state = _dot(bt_g, xwf_b[:, gs])
            if not has_init and c % cps == 0:
                new_states.append(chunk_state)
                continue
            s_in = sf_s[sq, :, gs]
            y_off = _dot(cm_g, s_in.astype(BF16)) * ef_x[:, gs]
            y_parts[-2] = y_parts[-2] + y_off[:, :LANES]
            y_parts[-1] = y_parts[-1] + y_off[:, LANES:]
            new_states.append(s_in * ef_x[L - 1:L, gs] + chunk_state)
        for g in range(SSM_GROUPS):
            sf_s[sq, :, g * GROUP_W:(g + 1) * GROUP_W] = new_states[g]
        for i, yp in enumerate(y_parts):
            y_s[fs, i * LANES:(i + 1) * LANES] = yp

    for c in reversed(range(G)):
        sq, rs, fs = locate(c)
        xs = act_ref[sq, rs, 0:SSM_WIDTH].astype(F32)
        eg_x = eg_s[fs, :]
        xwb_b = xwb_s[fs, :]
        y = y_s[fs, :] + dskip_ref[...] * xs
        from_zero = not has_init and c % cps == cps - 1
        y_off = []
        for g in range(SSM_GROUPS):
            gs = slice(g * GROUP_W, (g + 1) * GROUP_W)
            chunk_state = _dot(bt_s[bt_rows(c, g), :], xwb_b[:, gs])
            if from_zero:
                sb_s[sq, :, gs] = chunk_state
                continue
            s_in = sb_s[sq, :, gs]
            cm_g = act_ref[sq, rs, c_off + g * D_STATE:c_off + (g + 1) * D_STATE]
            y_off.append(_dot(cm_g, s_in.astype(BF16)) * eg_x[:, gs])
            sb_s[sq, :, gs] = s_in * eg_x[0:1, gs] + chunk_state
        if y_off:
            y = y + jnp.concatenate(y_off, axis=1)
        y = y * _silu(z_ref[sq, rs, :].astype(F32))
        out_ref[sq, rs, :] = _rms(y, gout_ref[...]).astype(BF16)

    for s in range(nseq):
        fin_ref[s, 0] = jnp.transpose(sf_s[s])
        fin_ref[s, 1] = jnp.transpose(sb_s[s])


def _ssd(act, z, small, init, consts, cast=()):
    b, s, _ = act.shape
    has_init = init is not None
    cps = s // CHUNK
    assert SSD_GROUP % cps == 0
    nseq = SSD_GROUP // cps
    assert b % nseq == 0
    steps = b // nseq
    blk = lambda w: pl.BlockSpec((nseq, s, w), lambda i: (i, 0, 0))
    st_spec = pl.BlockSpec((nseq, 2, SSM_WIDTH, D_STATE), lambda i: (i, 0, 0, 0))
    in_specs = [blk(CONV_DIM), blk(SSM_WIDTH), blk(LANES)]
    args = [act, z, small]
    if has_init:
        in_specs.append(st_spec)
        args.append(init)
    in_specs += [_const_spec(c.shape) for c in consts]
    args += list(consts)
    cast_specs = [pl.BlockSpec((w.shape[0] // steps, w.shape[1]), lambda i: (i, 0)) for w in cast]
    in_specs += cast_specs
    args += list(cast)
    rows = SSD_GROUP * CHUNK
    scratch = [pltpu.VMEM((SSD_GROUP, CHUNK, LANES), F32),
               pltpu.VMEM((SSD_GROUP, 2 * CHUNK, LANES), BF16),
               pltpu.VMEM((SSD_GROUP * N_DT, CHUNK), F32),
               pltpu.VMEM((SSD_GROUP * SSM_GROUPS * D_STATE, CHUNK), BF16),
               pltpu.VMEM((rows, SSM_WIDTH), F32),
               pltpu.VMEM((rows, SSM_WIDTH), BF16),
               pltpu.VMEM((rows, SSM_WIDTH), F32),
               pltpu.VMEM((nseq, D_STATE, SSM_WIDTH), F32),
               pltpu.VMEM((nseq, D_STATE, SSM_WIDTH), F32)]
    return pl.pallas_call(
        functools.partial(_ssd_kernel, nseq=nseq, cps=cps, has_init=has_init, n_cast=len(cast)),
        grid=(steps,), in_specs=in_specs,
        out_specs=[blk(SSM_WIDTH), st_spec] + cast_specs,
        out_shape=[jax.ShapeDtypeStruct((b, s, SSM_WIDTH), BF16),
                   jax.ShapeDtypeStruct((b, 2, SSM_WIDTH, D_STATE), F32)]
                  + [jax.ShapeDtypeStruct(w.shape, BF16) for w in cast],
        scratch_shapes=scratch,
        compiler_params=_params(("parallel",)),
        name="ssd_latent" if has_init else "ssd_ctx",
    )(*args)


FF_CHUNK = 256


def _outffn_kernel(x_ref, attn_ref, ssm_ref, mod_ref, wout_ref, gpost_ref, gpre_ref, gpostf_ref,
                   wg_ref, wu_ref, wd_ref, o_ref, mix_s, y_s, h_s, *, mod_off):
    m = _mod_terms(mod_ref, mod_off)
    half = MLA_HEADS * V_HEAD
    rows = x_ref.shape[1] // 2
    nslice = 8
    srows = rows // nslice
    nchunks = D_FF // FF_CHUNK
    g1 = gpost_ref[...] * m[2]
    g2 = gpre_ref[...] * (1.0 + m[4])
    sh2 = m[3]
    g3 = gpostf_ref[...] * m[5]

    def out_proj(rs):
        return _dot(attn_ref[0, rs, :], wout_ref[0:half, :]) + _dot(ssm_ref[0, rs, :], wout_ref[half:, :])

    def pre_ffn(mix, xr):
        y = xr + _rms(mix, g1)
        return y, (_rms(y, g2) + sh2).astype(BF16)

    def ffn_chunk(h, c, acc):
        cs = slice(c * FF_CHUNK, (c + 1) * FF_CHUNK)
        part = _dot((_silu(_dot(h, wg_ref[:, cs])) * _dot(h, wu_ref[:, cs])).astype(BF16), wd_ref[cs, :])
        return part if acc is None else acc + part

    y_a, h_a = pre_ffn(out_proj(slice(0, rows)), x_ref[0, 0:rows, :])
    mix_s[...] = out_proj(slice(rows, 2 * rows))
    acc_a = None
    for c in range(nchunks):
        acc_a = ffn_chunk(h_a, c, acc_a)
        if c < nslice:
            rs = slice(c * srows, (c + 1) * srows)
            y_b, h_b = pre_ffn(mix_s[rs, :], x_ref[0, rows + c * srows:rows + (c + 1) * srows, :])
            y_s[rs, :] = y_b
            h_s[rs, :] = h_b
    h_bb = h_s[...]
    acc_b = None
    for c in range(nchunks):
        acc_b = ffn_chunk(h_bb, c, acc_b)
        if c < nslice:
            rs = slice(c * srows, (c + 1) * srows)
            o_ref[0, rs, :] = y_a[rs] + _rms(acc_a[rs], g3)
    o_ref[0, rows:2 * rows, :] = y_s[...] + _rms(acc_b, g3)


def _outffn(x, attn, ssm, mod, mod_off, wout, gpost, gpre, gpostf, wg, wu, wd, tm):
    b, s, d = x.shape
    tok = lambda w: pl.BlockSpec((1, tm, w), lambda i, j: (i, j, 0))
    consts = [wout, gpost, gpre, gpostf, wg, wu, wd]
    return pl.pallas_call(
        functools.partial(_outffn_kernel, mod_off=mod_off),
        grid=(b, s // tm),
        in_specs=[tok(d), tok(MLA_HEADS * V_HEAD), tok(SSM_WIDTH), _mod_spec(d, mod_off)]
                 + [_const_spec(c.shape) for c in consts],
        out_specs=tok(d),
        out_shape=jax.ShapeDtypeStruct((b, s, d), F32),
        scratch_shapes=[pltpu.VMEM((tm // 2, d), F32),
                        pltpu.VMEM((tm // 2, d), F32),
                        pltpu.VMEM((tm // 2, d), BF16)],
        compiler_params=_params(("parallel", "parallel")),
        name="out_ffn",
    )(x, attn, ssm, mod, *consts)


def _rope_tables(length):
    quarter = QK_ROPE // 4
    pos = np.arange(length)
    inv_freq = ROPE_THETA ** (-np.arange(quarter, dtype=np.float64) / quarter)
    ang_r = (pos // GRID_W)[:, None] * inv_freq[None, :]
    ang_c = (pos % GRID_W)[:, None] * inv_freq[None, :]
    cos = np.concatenate([np.cos(ang_r)] * 2 + [np.cos(ang_c)] * 2, axis=1)
    sin = np.concatenate([-np.sin(ang_r), np.sin(ang_r), -np.sin(ang_c), np.sin(ang_c)], axis=1)
    return jnp.asarray(np.tile(cos, (1, 2)), F32), jnp.asarray(np.tile(sin, (1, 2)), F32)


def _pad_cols(w, width):
    return jnp.pad(w, ((0, 0), (0, width - w.shape[-1])))


def _ssd_constants():
    idx = np.arange(CHUNK)
    tril = (idx[None, :] <= idx[:, None]).astype(np.float32)
    e64 = np.zeros((LANES, N_DT * SSM_HEAD_DIM), np.float32)
    for q in range(N_DT):
        e64[q, q * SSM_HEAD_DIM:(q + 1) * SSM_HEAD_DIM] = 1.0
    return [jnp.asarray(a, BF16) for a in (tril, e64)]


def kernel(x_prompt, x_sample, cache_ckv, cache_krope, state_ssm, c, c_ctx, w_mod, b_mod,
           g_pre_mix, g_post_mix, w_in, g_q, w_uq, g_kv, w_ukv, conv_w, conv_b, dt_bias,
           a_log, d_skip, g_ssm_out, w_out, g_pre_ffn, g_post_ffn, w_gate, w_up, w_down):
    depth = w_in.shape[0]
    assert depth == 1
    nb, seq, d = x_prompt.shape
    db, dseq, _ = x_sample.shape
    l = 0

    rows = 16
    cond = jnp.concatenate([c_ctx[None, :], c, jnp.zeros((rows - 1 - db, d), F32)], axis=0)
    mod = _modulation(cond, w_mod[l], b_mod[l][None, :])

    wi = w_in[l]
    o_q, o_kv, o_kr = Q_LORA, Q_LORA + KV_LORA, Q_LORA + KV_LORA + QK_ROPE
    o_z, o_xbc = o_kr + SSM_WIDTH, o_kr + SSM_WIDTH + CONV_DIM
    kr_dt = _pad_cols(jnp.concatenate([wi[:, o_kv:o_kr], wi[:, o_xbc:]], axis=1), LANES)
    win = jnp.concatenate([wi[:, :o_q], wi[:, o_q:o_kv], wi[:, o_kr:o_z], wi[:, o_z:o_xbc], kr_dt],
                          axis=1).astype(BF16)
    wq = w_uq[l].reshape(Q_LORA, MLA_HEADS, QK_NOPE + QK_ROPE)
    wuq = jnp.concatenate([wq[:, :, :QK_NOPE].reshape(Q_LORA, MLA_HEADS * QK_NOPE),
                           wq[:, :, QK_NOPE:].reshape(Q_LORA, MLA_HEADS * QK_ROPE)], axis=1).astype(BF16)
    wukv = w_ukv[l].astype(BF16)
    wout = w_out[l].astype(BF16)
    row = lambda v: v.reshape(1, -1)
    gpre, gq, gkv = row(g_pre_mix[l]), row(g_q[l]), row(g_kv[l])
    gpost, gpre_f, gpost_f = row(g_post_mix[l]), row(g_pre_ffn[l]), row(g_post_ffn[l])
    cw, cb = conv_w[l], row(conv_b[l])

    small = lambda v: jnp.tile(v.reshape(1, -1), (1, SSD_PACK))
    ssd_consts = [small(dt_bias[l]), small(a_log[l]),
                  row(jnp.repeat(d_skip[l], SSM_HEAD_DIM)), row(g_ssm_out[l])] + _ssd_constants()

    xp = x_prompt.reshape(1, nb * seq, d)
    attn, z, act, sm, ckv, krope = _inproj(xp, mod, 0, gpre, win, gq, wuq, gkv, wukv, cw, cb,
                                           None, tm=1024, seq_rows=seq)
    per_seq = lambda a: a.reshape(nb, seq, a.shape[-1])
    ssm, fin, wg, wu, wd = _ssd(per_seq(act), per_seq(z), per_seq(sm), None, ssd_consts,
                                cast=(w_gate[l], w_up[l], w_down[l]))
    y_p = _outffn(xp, attn, ssm.reshape(1, nb * seq, -1), mod, 0,
                  wout, gpost, gpre_f, gpost_f, wg, wu, wd, tm=1024).reshape(nb, seq, d)
    new_ckv = ckv.reshape(nb, 1, seq, KV_LORA)
    new_krope = krope
    new_ssm = fin.reshape(nb, 1, 2, SSM_HEADS, SSM_HEAD_DIM, D_STATE)

    kr_cache = jnp.pad(cache_krope[:, l], ((0, 0), (0, 0), (0, LANES - QK_ROPE)))
    attn, z, act, sm = _inproj(x_sample, mod, 1, gpre, win, gq, wuq, gkv, wukv, cw, cb,
                               _rope_tables(dseq) + (cache_ckv[:, l], kr_cache), tm=1024, seq_rows=dseq)
    init = state_ssm[:, l].reshape(db, 2, SSM_WIDTH, D_STATE)
    ssm, _ = _ssd(act, z, sm, init, ssd_consts)
    y_s = _outffn(x_sample, attn, ssm, mod, 1, wout, gpost, gpre_f, gpost_f, wg, wu, wd, tm=1024)

    return (y_p, y_s, new_ckv, new_krope, new_ssm)
```

```python
import functools

import numpy as np
import jax
import jax.numpy as jnp
from jax import lax
from jax.experimental import pallas as pl
from jax.experimental.pallas import tpu as pltpu

F32 = jnp.float32
BF16 = jnp.bfloat16

GRID_W = 64
ROPE_THETA = 10000.0
NORM_EPS = 1e-6
MLA_HEADS = 4
QK_NOPE = 128
QK_ROPE = 64
V_HEAD = 128
Q_LORA = 384
KV_LORA = 256
SSM_HEADS = 8
SSM_HEAD_DIM = 64
SSM_WIDTH = SSM_HEADS * SSM_HEAD_DIM
SSM_GROUPS = 2
D_STATE = 128
CHUNK = 128
CONV_DIM = SSM_WIDTH + 2 * SSM_GROUPS * D_STATE
D_FF = 2816

LOG2E = 1.4426950408889634
LANES = 128
SUBLANES = 8
BF16_ROWS = 16
QK_PAD = 256
GROUP_W = SSM_WIDTH // SSM_GROUPS
VMEM_LIMIT = 56 * 1024 * 1024

_SEG_Q = (0, 384)
_SEG_KV = (384, 640)
_SEG_Z = (640, 1152)
_SEG_XBC = (1152, 2176)
_SEG_KR = (2176, 2304)
DT_OFF = QK_ROPE
N_DT = 2 * SSM_HEADS


def _rms(x, g):
    return x * lax.rsqrt(jnp.mean(x * x, axis=-1, keepdims=True) + NORM_EPS) * g


def _silu(x):
    u = 0.5 * x
    return u * jnp.tanh(u) + u


def _dot(a, b):
    return jnp.dot(a, b, preferred_element_type=F32)


def _dot_nt(a, b):
    return lax.dot_general(a, b, (((1,), (1,)), ((), ())), preferred_element_type=F32)


def _split3(x):
    hi = x.astype(BF16)
    r = x - hi.astype(F32)
    mid = r.astype(BF16)
    lo = (r - mid.astype(F32)).astype(BF16)
    return hi, mid, lo


def _dot3_rhs(m, x):
    hi, mid, lo = _split3(x)
    return _dot(m, hi) + _dot(m, mid) + _dot(m, lo)


def _mod_spec(d, mod_off):
    return pl.BlockSpec((SUBLANES, 6 * d), lambda i, j: ((i + mod_off) // SUBLANES, 0))


def _mod_terms(mod_ref, mod_off):
    row = lax.rem(pl.program_id(0) + mod_off, SUBLANES)
    r = mod_ref[pl.ds(row, 1), :]
    d = r.shape[1] // 6
    return [r[:, k * d:(k + 1) * d] for k in range(6)]


def _const_spec(shape):
    nd = len(shape)
    return pl.BlockSpec(shape, lambda *_: (0,) * nd, pipeline_mode=pl.Buffered(1))


def _params(sem):
    return pltpu.CompilerParams(dimension_semantics=sem, vmem_limit_bytes=VMEM_LIMIT)


def _mod_kernel(c_ref, w_ref, b_ref, o_ref):
    s = _silu(c_ref[...])
    rows = s.shape[0]
    s_hi = s.astype(BF16)
    s_lo = (s - s_hi.astype(F32)).astype(BF16)
    both = _dot(jnp.concatenate([s_hi, s_lo], axis=0), w_ref[...].astype(BF16))
    o_ref[...] = both[:rows] + both[rows:] + b_ref[...]


def _modulation(cond, w_mod, b_mod):
    rows, d = cond.shape
    n = w_mod.shape[1]
    tn = 1536
    return pl.pallas_call(
        _mod_kernel,
        grid=(n // tn,),
        in_specs=[pl.BlockSpec((rows, d), lambda j: (0, 0)),
                  pl.BlockSpec((d, tn), lambda j: (0, j)),
                  pl.BlockSpec((1, tn), lambda j: (0, j))],
        out_specs=pl.BlockSpec((rows, tn), lambda j: (0, j)),
        out_shape=jax.ShapeDtypeStruct((rows, n), F32),
        compiler_params=_params(("arbitrary",)),
        name="modulation",
    )(cond, w_mod, b_mod)


def _conv_silu(xb, prev_rows, next_rows, cw_ref, cb_ref):
    n = xb.shape[0]
    win = jnp.concatenate([prev_rows, xb, next_rows], axis=0)
    total = n + 2 * SUBLANES
    prev = pltpu.roll(win, 1, axis=0)[SUBLANES:SUBLANES + n]
    nxt = pltpu.roll(win, total - 1, axis=0)[SUBLANES:SUBLANES + n]
    conv = cb_ref[...] + prev * cw_ref[0:1, :] + xb * cw_ref[1:2, :] + nxt * cw_ref[2:3, :]
    return _silu(conv)


def _inproj_kernel(*refs, latent, seq_rows, mod_off):
    if latent:
        (x_ref, mod_ref, gpre_ref, win_ref, gq_ref, wuq_ref, gkv_ref, wukv_ref, cw_ref, cb_ref,
         cos_ref, sin_ref, cckv_ref, ckr_ref, attn_ref, z_ref, act_ref, small_ref,
         q_ref, k_ref, v_ref) = refs
    else:
        (x_ref, mod_ref, gpre_ref, win_ref, gq_ref, wuq_ref, gkv_ref, wukv_ref, cw_ref, cb_ref,
         attn_ref, z_ref, act_ref, small_ref, ckv_ref, krope_ref) = refs
    scale = (QK_NOPE + QK_ROPE) ** -0.5 * LOG2E
    shift, mscale = _mod_terms(mod_ref, mod_off)[:2]

    g_mod = gpre_ref[...] * (1.0 + mscale)

    def pre(xv):
        return (_rms(xv, g_mod) + shift).astype(BF16)

    tm = x_ref.shape[1]
    rows = tm // 2
    halves = (slice(0, rows), slice(rows, tm))
    assert seq_rows == tm or rows % seq_rows == 0
    zero_rows = jnp.zeros((SUBLANES, CONV_DIM), F32)
    projs = [_dot(pre(x_ref[0, halves[0], :]), win_ref[...])]
    if seq_rows == tm:
        ph = _dot(pre(x_ref[0, rows - SUBLANES:rows + SUBLANES, :]), win_ref[:, _SEG_XBC[0]:_SEG_XBC[1]])
        halo = ((zero_rows, ph[SUBLANES:]), (ph[:SUBLANES], zero_rows))
    projs.append(_dot(pre(x_ref[0, halves[1], :]), win_ref[...]))

    lane = lax.broadcasted_iota(jnp.int32, (rows, LANES), 1)
    rope_lanes = lane < QK_ROPE
    first_quarter = jnp.bitwise_and(lane, QK_ROPE // 4) == 0
    nw = MLA_HEADS * QK_NOPE
    for hf, rs in enumerate(halves):
        proj = projs[hf]
        seg = lambda s: proj[:, s[0]:s[1]]
        xbc = seg(_SEG_XBC)
        if seq_rows == tm:
            act_ref[0, rs, :] = _conv_silu(xbc, halo[hf][0], halo[hf][1], cw_ref, cb_ref).astype(BF16)
        else:
            for s in range(rows // seq_rows):
                sub = slice(s * seq_rows, (s + 1) * seq_rows)
                dst = slice(rs.start + sub.start, rs.start + sub.stop)
                act_ref[0, dst, :] = _conv_silu(xbc[sub], zero_rows, zero_rows, cw_ref, cb_ref).astype(BF16)
        krb = seg(_SEG_KR)
        small_ref[0, rs, :] = krb
        ckv = _rms(seg(_SEG_KV), gkv_ref[...])
        qall = _dot(_rms(seg(_SEG_Q), gq_ref[...]).astype(BF16), wuq_ref[...])
        kv = _dot(ckv.astype(BF16), wukv_ref[...])
        z_ref[0, rs, :] = seg(_SEG_Z).astype(BF16)
        if latent:
            cos = cos_ref[rs, :]
            sin = sin_ref[rs, :]

            def rot(t):
                swapped = jnp.where(first_quarter, pltpu.roll(t, LANES - QK_ROPE // 4, axis=1),
                                    pltpu.roll(t, QK_ROPE // 4, axis=1))
                return t * cos + swapped * sin
        else:
            rot = lambda t: t
            ckv_ref[0, rs, :] = ckv
            for s in range(rows // seq_rows):
                krope_ref[rs.start // seq_rows + s, 0] = krb[s * seq_rows:(s + 1) * seq_rows, :QK_ROPE]
        kr_b = jnp.where(rope_lanes, rot(krb), 0.0).astype(BF16)
        for pr in range(MLA_HEADS // 2):
            pair = rot(qall[:, nw + pr * LANES:nw + (pr + 1) * LANES]) * scale
            for k in range(2):
                hd = 2 * pr + k
                base = hd * QK_PAD
                qr = pair if k == 0 else pltpu.roll(pair, QK_ROPE, axis=1)
                q_n = (qall[:, hd * LANES:(hd + 1) * LANES] * scale).astype(BF16)
                q_r = jnp.where(rope_lanes, qr, 0.0).astype(BF16)
                k_n = kv[:, base:base + LANES].astype(BF16)
                v_h = kv[:, base + LANES:base + QK_PAD].astype(BF16)
                if latent:
                    q_ref[0, rs, base:base + LANES] = q_n
                    q_ref[0, rs, base + LANES:base + QK_PAD] = q_r
                    k_ref[0, rs, base:base + LANES] = k_n
                    k_ref[0, rs, base + LANES:base + QK_PAD] = kr_b
                    v_ref[0, rs, hd * LANES:(hd + 1) * LANES] = v_h
                else:
                    q_h = jnp.concatenate([q_n, q_r], axis=1)
                    k_h = jnp.concatenate([k_n, kr_b], axis=1)
                    for s in range(rows // seq_rows):
                        sub = slice(s * seq_rows, (s + 1) * seq_rows)
                        sc = _dot_nt(q_h[sub], k_h[sub])
                        p = jnp.exp2(sc - jnp.max(sc, axis=-1, keepdims=True))
                        den = jnp.sum(p, axis=-1, keepdims=True)
                        dst = slice(rs.start + sub.start, rs.start + sub.stop)
                        attn_ref[0, dst, hd * V_HEAD:(hd + 1) * V_HEAD] = (
                            _dot(p.astype(BF16), v_h[sub]) / den).astype(BF16)
    if latent:
        _attn_body(q_ref, k_ref, v_ref, cckv_ref, ckr_ref, wukv_ref, attn_ref)


def _inproj(x, mod, mod_off, gpre, win, gq, wuq, gkv, wukv, cw, cb, rope, tm, seq_rows):
    b, s, d = x.shape
    latent = rope is not None
    grid = (b, s // tm)
    tok = lambda w: pl.BlockSpec((1, tm, w), lambda i, j: (i, j, 0))
    mod_spec = _mod_spec(d, mod_off)
    consts = [gpre, win, gq, wuq, gkv, wukv, cw, cb]
    in_specs = [tok(d), mod_spec]
    args = [x, mod]
    in_specs += [_const_spec(c.shape) for c in consts]
    args += consts
    widths = [MLA_HEADS * V_HEAD, SSM_WIDTH, CONV_DIM]
    out_shape = [jax.ShapeDtypeStruct((b, s, w), BF16) for w in widths]
    out_shape.append(jax.ShapeDtypeStruct((b, s, LANES), F32))
    out_specs = [tok(w) for w in widths] + [tok(LANES)]
    scratch = []
    if latent:
        cos, sin, cache_ckv, cache_kr = rope
        assert tm == s
        in_specs += [pl.BlockSpec((tm, LANES), lambda i, j: (j, 0))] * 2
        in_specs += [pl.BlockSpec((1,) + a.shape[1:], lambda i, j: (i, 0, 0)) for a in (cache_ckv, cache_kr)]
        args += [cos, sin, cache_ckv, cache_kr]
        scratch = [pltpu.VMEM((1, tm, w), BF16)
                   for w in (MLA_HEADS * QK_PAD, MLA_HEADS * QK_PAD, MLA_HEADS * V_HEAD)]
    else:
        assert b == 1
        out_shape += [jax.ShapeDtypeStruct((b, s, KV_LORA), F32),
                      jax.ShapeDtypeStruct((s // seq_rows, 1, seq_rows, QK_ROPE), F32)]
        out_specs += [tok(KV_LORA),
                      pl.BlockSpec((tm // seq_rows, 1, seq_rows, QK_ROPE), lambda i, j: (j, 0, 0, 0))]
    return pl.pallas_call(
        functools.partial(_inproj_kernel, latent=latent, seq_rows=seq_rows, mod_off=mod_off),
        grid=grid, in_specs=in_specs, out_specs=out_specs, out_shape=out_shape,
        scratch_shapes=scratch,
        compiler_params=_params(("parallel", "parallel")),
        name="inproj_latent" if latent else "inproj_ctx",
    )(*args)


def _attn_body(q_ref, k_ref, v_ref, cckv_ref, ckr_ref, wukv_ref, o_ref):
    kv_c = _dot(cckv_ref[0].astype(BF16), wukv_ref[...])
    kr_c = ckr_ref[0].astype(BF16)

    def scores(hd):
        qs = slice(hd * QK_PAD, (hd + 1) * QK_PAD)
        q = q_ref[0, :, qs]
        kc = jnp.concatenate([kv_c[:, hd * QK_PAD:hd * QK_PAD + LANES].astype(BF16), kr_c], axis=1)
        return _dot_nt(k_ref[0, :, qs], q), _dot_nt(kc, q)

    nxt = scores(0)
    for hd in range(MLA_HEADS):
        s, sc = nxt
        if hd + 1 < MLA_HEADS:
            nxt = scores(hd + 1)
        vs = slice(hd * V_HEAD, (hd + 1) * V_HEAD)
        mx = jnp.maximum(jnp.max(s, axis=0, keepdims=True), jnp.max(sc, axis=0, keepdims=True))
        p = jnp.exp2(s - mx)
        pc = jnp.exp2(sc - mx)
        ones = lambda rows: jnp.ones((BF16_ROWS, rows), BF16)
        v_t = jnp.concatenate([jnp.transpose(v_ref[0, :, vs]),
                               ones(v_ref.shape[1])], axis=0)
        vc = kv_c[:, hd * QK_PAD + LANES:(hd + 1) * QK_PAD]
        vc_t = jnp.concatenate([jnp.transpose(vc.astype(BF16)), ones(vc.shape[0])], axis=0)
        acc = _dot(v_t, p.astype(BF16)) + _dot(vc_t, pc.astype(BF16))
        out = acc[:V_HEAD] / acc[V_HEAD:V_HEAD + 1]
        o_ref[0, :, vs] = jnp.transpose(out.astype(BF16))


SSD_PACK = LANES // N_DT
SSD_GROUP = SSD_PACK


def _ssd_kernel(*refs, nseq, cps, has_init, n_cast):
    refs = list(refs)
    n_in = 3 + int(has_init) + 6
    cast_in = refs[n_in:n_in + n_cast]
    cast_out = refs[n_in + n_cast + 2:n_in + 2 * n_cast + 2]
    for src, dst in zip(cast_in, cast_out):
        dst[...] = src[...].astype(BF16)
    refs = refs[:n_in] + refs[n_in + n_cast:n_in + n_cast + 2] + refs[n_in + 2 * n_cast + 2:]
    if has_init:
        (act_ref, z_ref, small_ref, init_ref, dtbias_ref, alog_ref, dskip_ref, gout_ref,
         tril_ref, e64_ref, out_ref, fin_ref,
         fg_s, wsm_s, rt_s, bt_s, y_s, xwb_s, eg_s, sf_s, sb_s) = refs
    else:
        (act_ref, z_ref, small_ref, dtbias_ref, alog_ref, dskip_ref, gout_ref,
         tril_ref, e64_ref, out_ref, fin_ref,
         fg_s, wsm_s, rt_s, bt_s, y_s, xwb_s, eg_s, sf_s, sb_s) = refs
    L = CHUNK
    G = nseq * cps
    heads_per_group = SSM_HEADS // SSM_GROUPS
    b_off = SSM_WIDTH
    c_off = SSM_WIDTH + SSM_GROUPS * D_STATE

    lane = lax.broadcasted_iota(jnp.int32, (L, LANES), 1)
    for pk in range(G // SSD_PACK):
        chunks = range(pk * SSD_PACK, (pk + 1) * SSD_PACK)
        packed = jnp.zeros((L, LANES), F32)
        for k, c in enumerate(chunks):
            raw = small_ref[c // cps, (c % cps) * L:(c % cps + 1) * L, :]
            shifted = pltpu.roll(raw, (k * N_DT - DT_OFF) % LANES, axis=1)
            packed = jnp.where((lane >= k * N_DT) & (lane < (k + 1) * N_DT), shifted, packed)
        dtc = jax.nn.softplus(packed + dtbias_ref[...])
        da = dtc * (-jnp.exp(alog_ref[...]))
        cum = _dot3_rhs(tril_ref[...], da)
        tot = cum[L - 1:L, :]
        suf = tot - cum + da
        fg = jnp.where(jnp.bitwise_and(lane, SSM_HEADS) == 0, cum, suf)
        w_small = dtc * jnp.exp(tot - fg)
        e_small = jnp.exp(fg)
        rt_s[pk * LANES:(pk + 1) * LANES, :] = jnp.transpose(fg - jnp.log(dtc))
        for k, c in enumerate(chunks):
            back = (LANES - k * N_DT) % LANES
            unroll = lambda t: t if back == 0 else pltpu.roll(t, back, axis=1)
            fg_s[c] = unroll(fg)
            wsm_s[c, 0:L, :] = unroll(w_small).astype(BF16)
            wsm_s[c, L:2 * L, :] = unroll(e_small).astype(BF16)

    for s in range(nseq):
        if has_init:
            sf_s[s] = jnp.transpose(init_ref[s, 0])
            sb_s[s] = jnp.transpose(init_ref[s, 1])
        else:
            sf_s[s] = jnp.zeros((D_STATE, SSM_WIDTH), F32)
            sb_s[s] = jnp.zeros((D_STATE, SSM_WIDTH), F32)

    row_t = lax.broadcasted_iota(jnp.int32, (L, L), 0)
    col_s = lax.broadcasted_iota(jnp.int32, (L, L), 1)
    lower = col_s <= row_t
    upper = col_s >= row_t
    lane_lo = lane < SSM_HEAD_DIM
    neg_inf = jnp.float32(-jnp.inf)

    def locate(c):
        sq, j = divmod(c, cps)
        return sq, slice(j * L, (j + 1) * L), slice(c * L, (c + 1) * L)

    def bt_rows(c, g):
        r = (c * SSM_GROUPS + g) * D_STATE
        return slice(r, r + D_STATE)

    for c in range(G):
        sq, rs, fs = locate(c)
        xs_b = act_ref[sq, rs, 0:SSM_WIDTH]
        fgc = fg_s[c]
        rtc = rt_s[c * N_DT:(c + 1) * N_DT, :]
        ex = _dot(wsm_s[c], e64_ref[...])
        w_b = ex[0:L, :].astype(BF16)
        ef_x = ex[L:2 * L, :SSM_WIDTH]
        eg_x = ex[L:2 * L, SSM_WIDTH:]
        eg_s[fs, :] = eg_x
        xwf_b = xs_b * w_b[:, :SSM_WIDTH]
        xwb_s[fs, :] = xs_b * w_b[:, SSM_WIDTH:]

        y_parts = []
        new_states = []
        for g in range(SSM_GROUPS):
            gs = slice(g * GROUP_W, (g + 1) * GROUP_W)
            bm_g = act_ref[sq, rs, b_off + g * D_STATE:b_off + (g + 1) * D_STATE]
            cm_g = act_ref[sq, rs, c_off + g * D_STATE:c_off + (g + 1) * D_STATE]
            bt_g = jnp.transpose(bm_g)
            bt_s[bt_rows(c, g), :] = bt_g
            cbm = _dot_nt(cm_g, bm_g)
            for pair in range(heads_per_group // 2):
                ws = []
                for k in range(2):
                    cf = g * heads_per_group + pair * 2 + k
                    seg_f = fgc[:, cf:cf + 1] - rtc[cf:cf + 1, :]
                    lf = jnp.exp(jnp.where(lower, seg_f, neg_inf))
                    cg = cf + SSM_HEADS
                    seg_b = fgc[:, cg:cg + 1] - rtc[cg:cg + 1, :]
                    ub = jnp.exp(jnp.where(upper, seg_b, neg_inf))
                    ws.append((cbm * (lf + ub)).astype(BF16))
                p0 = (g * heads_per_group + pair * 2) * SSM_HEAD_DIM
                xpair = xs_b[:, p0:p0 + LANES]
                zeros = jnp.zeros_like(xpair)
                rhs = jnp.concatenate([jnp.where(lane_lo, xpair, zeros),
                                       jnp.where(lane_lo, zeros, xpair)], axis=0)
                y_parts.append(_dot(jnp.concatenate(ws, axis=1), rhs))
            s_in = sf_s[sq, :, gs]
            y_off = _dot(cm_g, s_in.astype(BF16)) * ef_x[:, gs]
            y_parts[-2] = y_parts[-2] + y_off[:, :LANES]
            y_parts[-1] = y_parts[-1] + y_off[:, LANES:]
            new_states.append(s_in * ef_x[L - 1:L, gs] + _dot(bt_g, xwf_b[:, gs]))
        for g in range(SSM_GROUPS):
            sf_s[sq, :, g * GROUP_W:(g + 1) * GROUP_W] = new_states[g]
        for i, yp in enumerate(y_parts):
            y_s[fs, i * LANES:(i + 1) * LANES] = yp

    for c in reversed(range(G)):
        sq, rs, fs = locate(c)
        xs = act_ref[sq, rs, 0:SSM_WIDTH].astype(F32)
        eg_x = eg_s[fs, :]
        xwb_b = xwb_s[fs, :]
        y_off = []
        for g in range(SSM_GROUPS):
            gs = slice(g * GROUP_W, (g + 1) * GROUP_W)
            s_in = sb_s[sq, :, gs]
            cm_g = act_ref[sq, rs, c_off + g * D_STATE:c_off + (g + 1) * D_STATE]
            y_off.append(_dot(cm_g, s_in.astype(BF16)) * eg_x[:, gs])
            sb_s[sq, :, gs] = s_in * eg_x[0:1, gs] + _dot(bt_s[bt_rows(c, g), :], xwb_b[:, gs])
        y = y_s[fs, :] + jnp.concatenate(y_off, axis=1) + dskip_ref[...] * xs
        y = y * _silu(z_ref[sq, rs, :].astype(F32))
        out_ref[sq, rs, :] = _rms(y, gout_ref[...]).astype(BF16)

    for s in range(nseq):
        fin_ref[s, 0] = jnp.transpose(sf_s[s])
        fin_ref[s, 1] = jnp.transpose(sb_s[s])


def _ssd(act, z, small, init, consts, cast=()):
    b, s, _ = act.shape
    has_init = init is not None
    cps = s // CHUNK
    assert SSD_GROUP % cps == 0
    nseq = SSD_GROUP // cps
    assert b % nseq == 0
    steps = b // nseq
    blk = lambda w: pl.BlockSpec((nseq, s, w), lambda i: (i, 0, 0))
    st_spec = pl.BlockSpec((nseq, 2, SSM_WIDTH, D_STATE), lambda i: (i, 0, 0, 0))
    in_specs = [blk(CONV_DIM), blk(SSM_WIDTH), blk(LANES)]
    args = [act, z, small]
    if has_init:
        in_specs.append(st_spec)
        args.append(init)
    in_specs += [_const_spec(c.shape) for c in consts]
    args += list(consts)
    cast_specs = [pl.BlockSpec((w.shape[0] // steps, w.shape[1]), lambda i: (i, 0)) for w in cast]
    in_specs += cast_specs
    args += list(cast)
    rows = SSD_GROUP * CHUNK
    scratch = [pltpu.VMEM((SSD_GROUP, CHUNK, LANES), F32),
               pltpu.VMEM((SSD_GROUP, 2 * CHUNK, LANES), BF16),
               pltpu.VMEM((SSD_GROUP * N_DT, CHUNK), F32),
               pltpu.VMEM((SSD_GROUP * SSM_GROUPS * D_STATE, CHUNK), BF16),
               pltpu.VMEM((rows, SSM_WIDTH), F32),
               pltpu.VMEM((rows, SSM_WIDTH), BF16),
               pltpu.VMEM((rows, SSM_WIDTH), F32),
               pltpu.VMEM((nseq, D_STATE, SSM_WIDTH), F32),
               pltpu.VMEM((nseq, D_STATE, SSM_WIDTH), F32)]
    return pl.pallas_call(
        functools.partial(_ssd_kernel, nseq=nseq, cps=cps, has_init=has_init, n_cast=len(cast)),
        grid=(steps,), in_specs=in_specs,
        out_specs=[blk(SSM_WIDTH), st_spec] + cast_specs,
        out_shape=[jax.ShapeDtypeStruct((b, s, SSM_WIDTH), BF16),
                   jax.ShapeDtypeStruct((b, 2, SSM_WIDTH, D_STATE), F32)]
                  + [jax.ShapeDtypeStruct(w.shape, BF16) for w in cast],
        scratch_shapes=scratch,
        compiler_params=_params(("parallel",)),
        name="ssd_latent" if has_init else "ssd_ctx",
    )(*args)


FF_CHUNK = 256


def _outffn_kernel(x_ref, attn_ref, ssm_ref, mod_ref, wout_ref, gpost_ref, gpre_ref, gpostf_ref,
                   wg_ref, wu_ref, wd_ref, o_ref, mix_s, y_s, h_s, *, mod_off):
    m = _mod_terms(mod_ref, mod_off)
    half = MLA_HEADS * V_HEAD
    rows = x_ref.shape[1] // 2
    nslice = 8
    srows = rows // nslice
    nchunks = D_FF // FF_CHUNK
    g1 = gpost_ref[...] * m[2]
    g2 = gpre_ref[...] * (1.0 + m[4])
    sh2 = m[3]
    g3 = gpostf_ref[...] * m[5]

    def out_proj(rs):
        return _dot(attn_ref[0, rs, :], wout_ref[0:half, :]) + _dot(ssm_ref[0, rs, :], wout_ref[half:, :])

    def pre_ffn(mix, xr):
        y = xr + _rms(mix, g1)
        return y, (_rms(y, g2) + sh2).astype(BF16)

    def ffn_chunk(h, c, acc):
        cs = slice(c * FF_CHUNK, (c + 1) * FF_CHUNK)
        part = _dot((_silu(_dot(h, wg_ref[:, cs])) * _dot(h, wu_ref[:, cs])).astype(BF16), wd_ref[cs, :])
        return part if acc is None else acc + part

    y_a, h_a = pre_ffn(out_proj(slice(0, rows)), x_ref[0, 0:rows, :])
    mix_s[...] = out_proj(slice(rows, 2 * rows))
    acc_a = None
    for c in range(nchunks):
        acc_a = ffn_chunk(h_a, c, acc_a)
        if c < nslice:
            rs = slice(c * srows, (c + 1) * srows)
            y_b, h_b = pre_ffn(mix_s[rs, :], x_ref[0, rows + c * srows:rows + (c + 1) * srows, :])
            y_s[rs, :] = y_b
            h_s[rs, :] = h_b
    h_bb = h_s[...]
    acc_b = None
    for c in range(nchunks):
        acc_b = ffn_chunk(h_bb, c, acc_b)
        if c < nslice:
            rs = slice(c * srows, (c + 1) * srows)
            o_ref[0, rs, :] = y_a[rs] + _rms(acc_a[rs], g3)
    o_ref[0, rows:2 * rows, :] = y_s[...] + _rms(acc_b, g3)


def _outffn(x, attn, ssm, mod, mod_off, wout, gpost, gpre, gpostf, wg, wu, wd, tm):
    b, s, d = x.shape
    tok = lambda w: pl.BlockSpec((1, tm, w), lambda i, j: (i, j, 0))
    consts = [wout, gpost, gpre, gpostf, wg, wu, wd]
    return pl.pallas_call(
        functools.partial(_outffn_kernel, mod_off=mod_off),
        grid=(b, s // tm),
        in_specs=[tok(d), tok(MLA_HEADS * V_HEAD), tok(SSM_WIDTH), _mod_spec(d, mod_off)]
                 + [_const_spec(c.shape) for c in consts],
        out_specs=tok(d),
        out_shape=jax.ShapeDtypeStruct((b, s, d), F32),
        scratch_shapes=[pltpu.VMEM((tm // 2, d), F32),
                        pltpu.VMEM((tm // 2, d), F32),
                        pltpu.VMEM((tm // 2, d), BF16)],
        compiler_params=_params(("parallel", "parallel")),
        name="out_ffn",
    )(x, attn, ssm, mod, *consts)


def _rope_tables(length):
    quarter = QK_ROPE // 4
    pos = np.arange(length)
    inv_freq = ROPE_THETA ** (-np.arange(quarter, dtype=np.float64) / quarter)
    ang_r = (pos // GRID_W)[:, None] * inv_freq[None, :]
    ang_c = (pos % GRID_W)[:, None] * inv_freq[None, :]
    cos = np.concatenate([np.cos(ang_r)] * 2 + [np.cos(ang_c)] * 2, axis=1)
    sin = np.concatenate([-np.sin(ang_r), np.sin(ang_r), -np.sin(ang_c), np.sin(ang_c)], axis=1)
    return jnp.asarray(np.tile(cos, (1, 2)), F32), jnp.asarray(np.tile(sin, (1, 2)), F32)


def _pad_cols(w, width):
    return jnp.pad(w, ((0, 0), (0, width - w.shape[-1])))


def _ssd_constants():
    idx = np.arange(CHUNK)
    tril = (idx[None, :] <= idx[:, None]).astype(np.float32)
    e64 = np.zeros((LANES, N_DT * SSM_HEAD_DIM), np.float32)
    for q in range(N_DT):
        e64[q, q * SSM_HEAD_DIM:(q + 1) * SSM_HEAD_DIM] = 1.0
    return [jnp.asarray(a, BF16) for a in (tril, e64)]


def kernel(x_prompt, x_sample, cache_ckv, cache_krope, state_ssm, c, c_ctx, w_mod, b_mod,
           g_pre_mix, g_post_mix, w_in, g_q, w_uq, g_kv, w_ukv, conv_w, conv_b, dt_bias,
           a_log, d_skip, g_ssm_out, w_out, g_pre_ffn, g_post_ffn, w_gate, w_up, w_down):
    depth = w_in.shape[0]
    assert depth == 1
    nb, seq, d = x_prompt.shape
    db, dseq, _ = x_sample.shape
    l = 0

    rows = 16
    cond = jnp.concatenate([c_ctx[None, :], c, jnp.zeros((rows - 1 - db, d), F32)], axis=0)
    mod = _modulation(cond, w_mod[l], b_mod[l][None, :])

    wi = w_in[l]
    o_q, o_kv, o_kr = Q_LORA, Q_LORA + KV_LORA, Q_LORA + KV_LORA + QK_ROPE
    o_z, o_xbc = o_kr + SSM_WIDTH, o_kr + SSM_WIDTH + CONV_DIM
    kr_dt = _pad_cols(jnp.concatenate([wi[:, o_kv:o_kr], wi[:, o_xbc:]], axis=1), LANES)
    win = jnp.concatenate([wi[:, :o_q], wi[:, o_q:o_kv], wi[:, o_kr:o_z], wi[:, o_z:o_xbc], kr_dt],
                          axis=1).astype(BF16)
    wq = w_uq[l].reshape(Q_LORA, MLA_HEADS, QK_NOPE + QK_ROPE)
    wuq = jnp.concatenate([wq[:, :, :QK_NOPE].reshape(Q_LORA, MLA_HEADS * QK_NOPE),
                           wq[:, :, QK_NOPE:].reshape(Q_LORA, MLA_HEADS * QK_ROPE)], axis=1).astype(BF16)
    wukv = w_ukv[l].astype(BF16)
    wout = w_out[l].astype(BF16)
    row = lambda v: v.reshape(1, -1)
    gpre, gq, gkv = row(g_pre_mix[l]), row(g_q[l]), row(g_kv[l])
    gpost, gpre_f, gpost_f = row(g_post_mix[l]), row(g_pre_ffn[l]), row(g_post_ffn[l])
    cw, cb = conv_w[l], row(conv_b[l])

    small = lambda v: jnp.tile(v.reshape(1, -1), (1, SSD_PACK))
    ssd_consts = [small(dt_bias[l]), small(a_log[l]),
                  row(jnp.repeat(d_skip[l], SSM_HEAD_DIM)), row(g_ssm_out[l])] + _ssd_constants()

    xp = x_prompt.reshape(1, nb * seq, d)
    attn, z, act, sm, ckv, krope = _inproj(xp, mod, 0, gpre, win, gq, wuq, gkv, wukv, cw, cb,
                                           None, tm=1024, seq_rows=seq)
    per_seq = lambda a: a.reshape(nb, seq, a.shape[-1])
    ssm, fin, wg, wu, wd = _ssd(per_seq(act), per_seq(z), per_seq(sm), None, ssd_consts,
                                cast=(w_gate[l], w_up[l], w_down[l]))
    y_p = _outffn(xp, attn, ssm.reshape(1, nb * seq, -1), mod, 0,
                  wout, gpost, gpre_f, gpost_f, wg, wu, wd, tm=1024).reshape(nb, seq, d)
    new_ckv = ckv.reshape(nb, 1, seq, KV_LORA)
    new_krope = krope
    new_ssm = fin.reshape(nb, 1, 2, SSM_HEADS, SSM_HEAD_DIM, D_STATE)

    kr_cache = jnp.pad(cache_krope[:, l], ((0, 0), (0, 0), (0, LANES - QK_ROPE)))
    attn, z, act, sm = _inproj(x_sample, mod, 1, gpre, win, gq, wuq, gkv, wukv, cw, cb,
                               _rope_tables(dseq) + (cache_ckv[:, l], kr_cache), tm=1024, seq_rows=dseq)
    init = state_ssm[:, l].reshape(db, 2, SSM_WIDTH, D_STATE)
    ssm, _ = _ssd(act, z, sm, init, ssd_consts)
    y_s = _outffn(x_sample, attn, ssm, mod, 1, wout, gpost, gpre_f, gpost_f, wg, wu, wd, tm=1024)

    return (y_p, y_s, new_ckv, new_krope, new_ssm)
```
